```python
import math
import jax
import jax.numpy as jnp
from jax import lax
import numpy as np

D_MODEL = 2048
BATCH = 2
SEQ = 4096
DEPTH = 4
DEC_BATCH = 8
DEC_SEQ = 4
PAST_LEN = 16384
PAGE_SIZE = 128

N_BRANCH = 4
BR_W = D_MODEL // N_BRANCH
A_GROUPS = 4
A_GW = BR_W // A_GROUPS
CHUNK = 128
B_HEADS = 4
B_HD = BR_W // B_HEADS
B_PATTERNS = ((128, 1), (512, 4), (2048, 16))
B_GROUPS = len(B_PATTERNS)
N_BUCKETS = 32
MAX_DIST = 2048
C_WINDOWS = (2, 4, 8, 16)
C_GROUPS = len(C_WINDOWS)
C_GW = BR_W // C_GROUPS
POOL_HIST = max(C_WINDOWS) - 1
CONV_W = 3
D_FF = -(-8 * D_MODEL // (3 * 256)) * 256
N_IN = 2 * BR_W + B_GROUPS * 3 * BR_W + BR_W + 3 * BR_W
EPS = 1e-6
NEG = -1e30

kernel_name = "hybrid_gated_branch_decoder_step"


def rmsnorm(x, g):
    xf = x.astype(jnp.float32)
    y = xf * lax.rsqrt(jnp.mean(xf * xf, axis=-1, keepdims=True) + EPS)
    return (y * g).astype(x.dtype)


def layernorm(x, g, b):
    xf = x.astype(jnp.float32)
    mu = jnp.mean(xf, axis=-1, keepdims=True)
    var = jnp.mean(jnp.square(xf - mu), axis=-1, keepdims=True)
    return ((xf - mu) * lax.rsqrt(var + 1e-5) * g + b).astype(x.dtype)


def t5_bucket(dist):
    max_exact = N_BUCKETS // 2
    n = jnp.maximum(dist, 0)
    nf = jnp.maximum(n, 1).astype(jnp.float32)
    large = max_exact + (jnp.log(nf / max_exact) / math.log(MAX_DIST / max_exact)
                         * (N_BUCKETS - max_exact)).astype(jnp.int32)
    large = jnp.minimum(large, N_BUCKETS - 1)
    return jnp.where(n < max_exact, n, large)


def chunk_gmlp(pa, ln_g, ln_b, ws, bs):
    z = jax.nn.gelu(pa)
    u, v = jnp.split(z, 2, axis=-1)
    v = layernorm(v, ln_g, ln_b)
    b, t, _ = v.shape
    nc = -(-t // CHUNK)
    vp = jnp.pad(v, ((0, 0), (0, nc * CHUNK - t), (0, 0))).reshape(b, nc, CHUNK, A_GROUPS, A_GW)
    tri = jnp.tril(jnp.ones((CHUNK, CHUNK), dtype=bool))
    wm = jnp.where(tri[None], ws, jnp.zeros((), ws.dtype))
    mix = jnp.einsum('gij,bnjgc->bnigc', wm, vp) + bs.T[None, None, :, :, None]
    mix = mix.reshape(b, nc * CHUNK, BR_W)[:, :t]
    return u * mix, v


def dilated_window_prompt(q, k, v, tab, window, dil):
    b, s, h, e = q.shape
    nk = window // dil
    L = s // dil
    nb = -(-L // nk)
    Lp = nb * nk

    def to_sub(a):
        a = a.reshape(b, L, dil, h, e).transpose(0, 2, 1, 3, 4)
        return jnp.pad(a, ((0, 0), (0, 0), (0, Lp - L), (0, 0), (0, 0)))

    def band(a):
        a = jnp.pad(a, ((0, 0), (0, 0), (nk, 0), (0, 0), (0, 0))).reshape(b, dil, nb + 1, nk, h, e)
        return jnp.concatenate([a[:, :, :-1], a[:, :, 1:]], axis=3)

    qb = to_sub(q).reshape(b, dil, nb, nk, h, e).astype(jnp.float32)
    kb = band(to_sub(k)).astype(jnp.float32)
    vb = band(to_sub(v)).astype(jnp.float32)
    i = jnp.arange(nk)[:, None]
    j = jnp.arange(2 * nk)[None, :]
    diff = nk + i - j
    valid = (diff >= 0) & (diff <= nk)
    blk = jnp.arange(nb)[:, None, None]
    mask = valid[None] & ((blk > 0) | (j >= nk)[None])
    bias = tab[t5_bucket(diff * dil)].transpose(2, 0, 1)
    logits = jnp.einsum('brnihe,brnjhe->brnhij', qb, kb) * (e ** -0.5) + bias.astype(jnp.float32)
    logits = jnp.where(mask[None, None, :, None], logits, NEG)
    lse = jax.nn.logsumexp(logits, axis=-1)
    p = jnp.exp(logits - lse[..., None])
    o = jnp.einsum('brnhij,brnjhe->brnihe', p, vb)
    o = o.reshape(b, dil, Lp, h, e)[:, :, :L].transpose(0, 2, 1, 3, 4).reshape(b, s, h, e)
    lse = lse.transpose(0, 1, 2, 4, 3).reshape(b, dil, Lp, h)[:, :, :L].transpose(0, 2, 1, 3).reshape(b, s, h)
    return o, lse


def dilated_window_sample(q, k_new, v_new, kv_buf, tab, window, dil):
    b, t, h, e = q.shape
    wb = kv_buf.shape[1]
    nk = window // dil
    kc = jnp.concatenate([kv_buf[:, :, 0], k_new], axis=1)
    vc = jnp.concatenate([kv_buf[:, :, 1], v_new], axis=1)
    steps = jnp.arange(nk + 1)
    idx = wb + jnp.arange(t)[:, None] - steps[None, :] * dil
    valid = idx >= 0
    idx = jnp.maximum(idx, 0)
    kg = kc[:, idx].astype(jnp.float32)
    vg = vc[:, idx].astype(jnp.float32)
    bias = tab[t5_bucket(steps * dil)].T.astype(jnp.float32)
    logits = jnp.einsum('bthe,btjhe->bthj', q.astype(jnp.float32), kg) * (e ** -0.5) + bias[None, None]
    logits = jnp.where(valid[None, :, None, :], logits, NEG)
    lse = jax.nn.logsumexp(logits, axis=-1)
    p = jnp.exp(logits - lse[..., None])
    o = jnp.einsum('bthj,btjhe->bthe', p, vg)
    return o, lse


def combine_dilations(outs, lses):
    w = jax.nn.softmax(jnp.stack(lses, axis=0), axis=0)
    return jnp.einsum('gbth,gbthe->bthe', w, jnp.stack(outs, axis=0))


def multiscale_pool(pc, hist, start, pool_w, pool_b, pool_scale):
    b, t, _ = pc.shape
    full = jnp.concatenate([hist, pc], axis=1).astype(jnp.float32)
    cs = jnp.concatenate([jnp.zeros((b, 1, BR_W), jnp.float32), jnp.cumsum(full, axis=1)], axis=1)
    end = POOL_HIST + 1 + jnp.arange(t)
    pos = start + jnp.arange(t)
    means = []
    for gi, win in enumerate(C_WINDOWS):
        sl = slice(gi * C_GW, (gi + 1) * C_GW)
        cnt = jnp.minimum(pos + 1, win).astype(jnp.float32)
        means.append((cs[:, end, sl] - cs[:, end - win, sl]) / cnt[None, :, None])
    pooled = (jnp.concatenate(means, axis=-1) - full[:, POOL_HIST:]).reshape(b, t, C_GROUPS, C_GW)
    y = jnp.einsum('btgc,gcd->btgd', pooled, pool_w.astype(jnp.float32)).reshape(b, t, BR_W) + pool_b
    return (y * pool_scale).astype(pc.dtype)


def short_conv(pd, hist, conv_w):
    bg, cg, hs = jnp.split(pd, 3, axis=-1)
    z = cg * hs
    zc = jnp.concatenate([hist, z], axis=1)
    t = z.shape[1]
    y = sum(zc[:, i:i + t] * conv_w[:, i] for i in range(CONV_W))
    return bg * y, z


def run_trunk(x, c, caches, params, start):
    (ln1_g, ln2_g, w_ada, b_ada, w_in, w_gate, b_gate, a_ln_g, a_ln_b, a_ws, a_bs,
     rel_bias, pool_w, pool_b, pool_scale, conv_w, w_branch, w_out, w1, w3, w2, final_g) = params
    bsz, t, _ = x.shape
    fresh = caches is None
    kv_new = [[] for _ in B_PATTERNS]
    pool_new, conv_new, chunk_new = [], [], []
    split_at = (2 * BR_W, 2 * BR_W + 3 * B_GROUPS * BR_W, 3 * BR_W + 3 * B_GROUPS * BR_W)
    for l in range(DEPTH):
        mod = jax.nn.silu(c) @ w_ada[l] + b_ada[l]
        sh1, sc1, g1, sh2, sc2, g2 = jnp.split(mod[:, None, :], 6, axis=-1)
        hn = rmsnorm(x, ln1_g[l]) * (1 + sc1) + sh1
        pa, pb, pc, pd = jnp.split(hn @ w_in[l], split_at, axis=-1)
        ya, va = chunk_gmlp(pa, a_ln_g[l], a_ln_b[l], a_ws[l], a_bs[l])
        qkv = pb.reshape(bsz, t, B_GROUPS, 3, B_HEADS, B_HD)
        outs, lses = [], []
        for gi, (win, dil) in enumerate(B_PATTERNS):
            q, k, v = qkv[:, :, gi, 0], qkv[:, :, gi, 1], qkv[:, :, gi, 2]
            tab = rel_bias[:, gi * B_HEADS:(gi + 1) * B_HEADS]
            if fresh:
                o, lse = dilated_window_prompt(q, k, v, tab, win, dil)
                keep = min(win, t)
                kv_new[gi].append(jnp.stack([k[:, t - keep:], v[:, t - keep:]], axis=2))
            else:
                o, lse = dilated_window_sample(q, k, v, caches[gi][l], tab, win, dil)
                kv_new[gi].append(jnp.stack([k, v], axis=2))
            outs.append(o)
            lses.append(lse)
        yb = combine_dilations(outs, lses).reshape(bsz, t, BR_W).astype(x.dtype)
        hist_c = jnp.zeros((bsz, POOL_HIST, BR_W), pc.dtype) if fresh else caches[3][l]
        yc = multiscale_pool(pc, hist_c, start, pool_w[l], pool_b[l], pool_scale[l])
        pool_new.append(pc[:, t - POOL_HIST:] if fresh else pc)
        hist_d = jnp.zeros((bsz, CONV_W - 1, BR_W), pd.dtype) if fresh else caches[4][l]
        yd, zd = short_conv(pd, hist_d, conv_w[l])
        conv_new.append(zd[:, t - (CONV_W - 1):] if fresh else zd)
        if not fresh:
            chunk_new.append(va)
        ys = jnp.stack([ya, yb, yc, yd], axis=2)
        br = jnp.einsum('btgc,gcd->btgd', ys, w_branch[l])
        gates = jax.nn.sigmoid(hn @ w_gate[l] + b_gate[l]).reshape(bsz, t, N_BRANCH, D_MODEL)
        x = x + g1 * (jnp.sum(gates * br, axis=2) @ w_out[l])
        hn2 = rmsnorm(x, ln2_g[l]) * (1 + sc2) + sh2
        x = x + g2 * ((jax.nn.silu(hn2 @ w1[l]) * (hn2 @ w3[l])) @ w2[l])
    y = rmsnorm(x, final_g)
    states = [jnp.stack(s, axis=0) for s in kv_new] + [jnp.stack(pool_new, axis=0), jnp.stack(conv_new, axis=0)]
    if not fresh:
        states.append(jnp.stack(chunk_new, axis=0))
    return y, states


def setup_inputs(seed: int = 0) -> dict:
    key = jax.random.key(seed)
    ks = iter(jax.random.split(key, 48))

    def nrm(shape, scale):
        return jax.random.normal(next(ks), shape, jnp.float32) * scale

    wb = [min(w, PAST_LEN) for w, _ in B_PATTERNS]
    return {
        "x_prompt": nrm((BATCH, SEQ, D_MODEL), 1.0),
        "x_sample": nrm((DEC_BATCH, DEC_SEQ, D_MODEL), 1.0),
        "c_prompt": nrm((BATCH, D_MODEL), 1.0),
        "c_sample": nrm((DEC_BATCH, D_MODEL), 1.0),
        "cache_b1": nrm((DEPTH, DEC_BATCH, wb[0], 2, B_HEADS, B_HD), 1.0),
        "cache_b2": nrm((DEPTH, DEC_BATCH, wb[1], 2, B_HEADS, B_HD), 1.0),
        "cache_b3": nrm((DEPTH, DEC_BATCH, wb[2], 2, B_HEADS, B_HD), 1.0),
        "cache_pool": nrm((DEPTH, DEC_BATCH, POOL_HIST, BR_W), 1.0),
        "cache_conv": nrm((DEPTH, DEC_BATCH, CONV_W - 1, BR_W), 0.5),
        "ln1_g": 1.0 + nrm((DEPTH, D_MODEL), 0.1),
        "ln2_g": 1.0 + nrm((DEPTH, D_MODEL), 0.1),
        "w_ada": nrm((DEPTH, D_MODEL, 6 * D_MODEL), 0.5 * D_MODEL ** -0.5),
        "b_ada": nrm((DEPTH, 6 * D_MODEL), 0.01),
        "w_in": nrm((DEPTH, D_MODEL, N_IN), D_MODEL ** -0.5),
        "w_gate": nrm((DEPTH, D_MODEL, N_BRANCH * D_MODEL), D_MODEL ** -0.5),
        "b_gate": nrm((DEPTH, N_BRANCH * D_MODEL), 0.01),
        "a_ln_g": 1.0 + nrm((DEPTH, BR_W), 0.1),
        "a_ln_b": nrm((DEPTH, BR_W), 0.01),
        "a_ws": nrm((DEPTH, A_GROUPS, CHUNK, CHUNK), CHUNK ** -0.5),
        "a_bs": 1.0 + nrm((DEPTH, A_GROUPS, CHUNK), 0.1),
        "rel_bias": nrm((N_BUCKETS, B_GROUPS * B_HEADS), 0.5),
        "pool_w": nrm((DEPTH, C_GROUPS, C_GW, C_GW), C_GW ** -0.5),
        "pool_b": nrm((DEPTH, BR_W), 0.01),
        "pool_scale": 1.0 + nrm((DEPTH, BR_W), 0.1),
        "conv_w": nrm((DEPTH, BR_W, CONV_W), CONV_W ** -0.5),
        "w_branch": nrm((DEPTH, N_BRANCH, BR_W, D_MODEL), BR_W ** -0.5),
        "w_out": nrm((DEPTH, D_MODEL, D_MODEL), 0.5 * D_MODEL ** -0.5),
        "w1": nrm((DEPTH, D_MODEL, D_FF), D_MODEL ** -0.5),
        "w3": nrm((DEPTH, D_MODEL, D_FF), D_MODEL ** -0.5),
        "w2": nrm((DEPTH, D_FF, D_MODEL), D_FF ** -0.5),
        "final_g": 1.0 + nrm((D_MODEL,), 0.1),
    }


def reference(x_prompt, x_sample, c_prompt, c_sample, cache_b1, cache_b2, cache_b3, cache_pool, cache_conv,
              ln1_g, ln2_g, w_ada, b_ada, w_in, w_gate, b_gate, a_ln_g, a_ln_b, a_ws, a_bs,
              rel_bias, pool_w, pool_b, pool_scale, conv_w, w_branch, w_out, w1, w3, w2, final_g):
    params = (ln1_g, ln2_g, w_ada, b_ada, w_in, w_gate, b_gate, a_ln_g, a_ln_b, a_ws, a_bs,
              rel_bias, pool_w, pool_b, pool_scale, conv_w, w_branch, w_out, w1, w3, w2, final_g)
    y_prompt, sp = run_trunk(x_prompt, c_prompt, None, params, 0)
    caches = (cache_b1, cache_b2, cache_b3, cache_pool, cache_conv)
    y_sample, ss = run_trunk(x_sample, c_sample, caches, params, PAST_LEN)
    return (y_prompt, y_sample, sp[0], sp[1], sp[2], sp[3], sp[4],
            ss[0], ss[1], ss[2], ss[3], ss[4], ss[5])
```

```python
import functools
import math

import jax
import jax.numpy as jnp
from jax import lax
from jax.experimental import pallas as pl
from jax.experimental.pallas import tpu as pltpu

F32 = jnp.float32
BF16 = jnp.bfloat16

D_MODEL = 2048
BR_W = 512
GW = 128
N_HEADS = 4
CHUNK = 128
PATTERNS = ((128, 1), (512, 4), (2048, 16))
NK = 128
POOL_WINDOWS = (2, 4, 8, 16)
POOL_HIST = 15
CONV_W = 3
N_BUCKETS = 32
MAX_DIST = 2048
D_FF = 5632
N_IN = 7680
COL_A = 0
COL_B = 2 * BR_W
COL_C = COL_B + 9 * BR_W
COL_D = COL_C + BR_W
EPS = 1e-6
NEG = -1e30
SAMPLE_ROWS = 8
PAST_LEN = 16384

VMEM_LIMIT = 56 * 1024 * 1024


def _params(n_axes):
    return pltpu.CompilerParams(dimension_semantics=("arbitrary",) * n_axes,
                                vmem_limit_bytes=VMEM_LIMIT)


def _ada_kernel(c_ref, w_ref, b_ref, o_ref):
    a = jax.nn.silu(c_ref[...]).astype(BF16)
    o_ref[0] = jnp.dot(a, w_ref[0].astype(BF16), preferred_element_type=F32) + b_ref[0]


def ada_modulation(c_all, w_ada, b_ada):
    depth, d, n = w_ada.shape
    r = c_all.shape[0]
    tn = 1024
    return pl.pallas_call(
        _ada_kernel,
        grid=(depth, n // tn),
        in_specs=[pl.BlockSpec((r, d), lambda l, j: (0, 0)),
                  pl.BlockSpec((1, d, tn), lambda l, j: (l, 0, j)),
                  pl.BlockSpec((1, 1, tn), lambda l, j: (l, 0, j))],
        out_specs=pl.BlockSpec((1, r, tn), lambda l, j: (l, 0, j)),
        out_shape=jax.ShapeDtypeStruct((depth, r, n), F32),
        compiler_params=_params(2),
        name="ada_modulation",
    )(c_all, w_ada, b_ada.reshape(depth, 1, n))


def _norm_kernel(x_ref, g_ref, sc_ref, sh_ref, o_ref):
    x = x_ref[...]
    y = x * lax.rsqrt(jnp.mean(x * x, axis=-1, keepdims=True) + EPS) * g_ref[...]
    o_ref[...] = (y * (1.0 + sc_ref[0]) + sh_ref[0]).astype(o_ref.dtype)


def norm_modulate(x, g, sc, sh, tm, out_dtype):
    m, d = x.shape
    nb, r, _ = sc.shape
    per = (m // tm) // nb
    mod_spec = pl.BlockSpec((1, r, d), lambda i: (i // per, 0, 0))
    return pl.pallas_call(
        _norm_kernel,
        grid=(m // tm,),
        in_specs=[pl.BlockSpec((tm, d), lambda i: (i, 0)),
                  pl.BlockSpec((1, d), lambda i: (0, 0)),
                  mod_spec, mod_spec],
        out_specs=pl.BlockSpec((tm, d), lambda i: (i, 0)),
        out_shape=jax.ShapeDtypeStruct((m, d), out_dtype),
        compiler_params=_params(1),
        name="norm_modulate",
    )(x, g, sc, sh)


def _mm_kernel(a_ref, w_ref, o_ref, wbf_ref):
    @pl.when(pl.program_id(1) == 0)
    def _():
        wbf_ref[...] = w_ref[0].astype(BF16)

    o_ref[...] = jnp.dot(a_ref[...], wbf_ref[...], preferred_element_type=F32).astype(o_ref.dtype)


def matmul(a, w, layer, tm, tn, out_dtype):
    m, k = a.shape
    n = w.shape[2]
    return pl.pallas_call(
        _mm_kernel,
        grid=(n // tn, m // tm),
        in_specs=[pl.BlockSpec((tm, k), lambda j, i: (i, 0)),
                  pl.BlockSpec((1, k, tn), lambda j, i: (layer, 0, j))],
        out_specs=pl.BlockSpec((tm, tn), lambda j, i: (i, j)),
        out_shape=jax.ShapeDtypeStruct((m, n), out_dtype),
        scratch_shapes=[pltpu.VMEM((k, tn), BF16)],
        compiler_params=_params(2),
        name="matmul",
    )(a, w)


def _mm_residual_kernel(a_ref, w_ref, x_ref, g_ref, o_ref, wbf_ref):
    @pl.when(pl.program_id(1) == 0)
    def _():
        wbf_ref[...] = w_ref[0].astype(BF16)

    y = jnp.dot(a_ref[...], wbf_ref[...], preferred_element_type=F32)
    o_ref[...] = x_ref[...] + g_ref[0] * y


def matmul_residual(a, w, layer, x, gate, tm, tn):
    m, k = a.shape
    n = w.shape[2]
    nb, r, _ = gate.shape
    per = (m // tm) // nb
    return pl.pallas_call(
        _mm_residual_kernel,
        grid=(n // tn, m // tm),
        in_specs=[pl.BlockSpec((tm, k), lambda j, i: (i, 0)),
                  pl.BlockSpec((1, k, tn), lambda j, i: (layer, 0, j)),
                  pl.BlockSpec((tm, tn), lambda j, i: (i, j)),
                  pl.BlockSpec((1, r, tn), lambda j, i: (i // per, 0, j))],
        out_specs=pl.BlockSpec((tm, tn), lambda j, i: (i, j)),
        out_shape=jax.ShapeDtypeStruct((m, n), F32),
        scratch_shapes=[pltpu.VMEM((k, tn), BF16)],
        compiler_params=_params(2),
        name="matmul_residual",
    )(a, w, x, gate)


def _swiglu_kernel(a_ref, w1_ref, w3_ref, o_ref, w1bf_ref, w3bf_ref):
    @pl.when(pl.program_id(1) == 0)
    def _():
        w1bf_ref[...] = w1_ref[0].astype(BF16)
        w3bf_ref[...] = w3_ref[0].astype(BF16)

    a = a_ref[...]
    h1 = jnp.dot(a, w1bf_ref[...], preferred_element_type=F32)
    h3 = jnp.dot(a, w3bf_ref[...], preferred_element_type=F32)
    o_ref[...] = (jax.nn.silu(h1) * h3).astype(o_ref.dtype)


def swiglu_up(a, w1, w3, layer, tm, tn):
    m, k = a.shape
    n = w1.shape[2]
    w_spec = pl.BlockSpec((1, k, tn), lambda j, i: (layer, 0, j))
    return pl.pallas_call(
        _swiglu_kernel,
        grid=(n // tn, m // tm),
        in_specs=[pl.BlockSpec((tm, k), lambda j, i: (i, 0)), w_spec, w_spec],
        out_specs=pl.BlockSpec((tm, tn), lambda j, i: (i, j)),
        out_shape=jax.ShapeDtypeStruct((m, n), BF16),
        scratch_shapes=[pltpu.VMEM((k, tn), BF16), pltpu.VMEM((k, tn), BF16)],
        compiler_params=_params(2),
        name="swiglu_up",
    )(a, w1, w3)


def _gate_merge_kernel(hn_ref, ya_ref, yb_ref, yc_ref, yd_ref, wg0_ref, wg1_ref, wg2_ref, wg3_ref,
                       wb_ref, bg_ref, o_ref, wgbf_ref, wbbf_ref):
    @pl.when(pl.program_id(1) == 0)
    def _():
        for g, wg_ref in enumerate((wg0_ref, wg1_ref, wg2_ref, wg3_ref)):
            wgbf_ref[g] = wg_ref[0].astype(BF16)
            wbbf_ref[g] = wb_ref[0, g].astype(BF16)

    hn = hn_ref[...]
    acc = None
    for g, y_ref in enumerate((ya_ref, yb_ref, yc_ref, yd_ref)):
        gate = jax.nn.sigmoid(jnp.dot(hn, wgbf_ref[g], preferred_element_type=F32) + bg_ref[0, g:g + 1, :])
        br = jnp.dot(y_ref[...], wbbf_ref[g], preferred_element_type=F32)
        acc = gate * br if acc is None else acc + gate * br
    o_ref[...] = acc.astype(o_ref.dtype)


def gate_merge(hn, ys, w_gate, b_gate, w_branch, layer, tm, tn):
    m, k = hn.shape
    depth = w_gate.shape[0]
    d = w_branch.shape[3]
    nj = d // tn
    wg_specs = [pl.BlockSpec((1, k, tn), functools.partial(lambda j, i, g: (layer, 0, g * nj + j), g=g))
                for g in range(4)]
    y_spec = pl.BlockSpec((tm, BR_W), lambda j, i: (i, 0))
    return pl.pallas_call(
        _gate_merge_kernel,
        grid=(nj, m // tm),
        in_specs=[pl.BlockSpec((tm, k), lambda j, i: (i, 0)), y_spec, y_spec, y_spec, y_spec,
                  *wg_specs,
                  pl.BlockSpec((1, 4, BR_W, tn), lambda j, i: (layer, 0, 0, j)),
                  pl.BlockSpec((1, 4, tn), lambda j, i: (layer, 0, j))],
        out_specs=pl.BlockSpec((tm, tn), lambda j, i: (i, j)),
        out_shape=jax.ShapeDtypeStruct((m, d), BF16),
        scratch_shapes=[pltpu.VMEM((4, k, tn), BF16), pltpu.VMEM((4, BR_W, tn), BF16)],
        compiler_params=_params(2),
        name="gate_merge",
    )(hn, *ys, w_gate, w_gate, w_gate, w_gate, w_branch, b_gate.reshape(depth, 4, d))


def _gmlp_kernel(pu_ref, pv_ref, lg_ref, lb_ref, ws_ref, bst_ref, ya_ref, va_ref, *, chunk):
    rows = pu_ref.shape[0]
    u = jax.nn.gelu(pu_ref[...])
    v = jax.nn.gelu(pv_ref[...])
    mu = jnp.mean(v, axis=-1, keepdims=True)
    var = jnp.mean(jnp.square(v - mu), axis=-1, keepdims=True)
    vn = (v - mu) * lax.rsqrt(var + 1e-5) * lg_ref[...] + lb_ref[...]
    va_ref[...] = vn
    causal = (lax.broadcasted_iota(jnp.int32, (CHUNK, CHUNK), 0)
              >= lax.broadcasted_iota(jnp.int32, (CHUNK, CHUNK), 1))
    for g in range(4):
        wm = jnp.where(causal, ws_ref[0, g], 0.0).astype(BF16)
        bias = bst_ref[0, :, g:g + 1]
        for c in range(rows // chunk):
            r0 = c * chunk
            vc = vn[r0:r0 + chunk, g * GW:(g + 1) * GW]
            if chunk < CHUNK:
                vc = jnp.concatenate([vc, jnp.zeros((CHUNK - chunk, GW), F32)], axis=0)
            mix = (jnp.dot(wm, vc.astype(BF16), preferred_element_type=F32) + bias)[:chunk]
            ya_ref[r0:r0 + chunk, g * GW:(g + 1) * GW] = (
                u[r0:r0 + chunk, g * GW:(g + 1) * GW] * mix).astype(ya_ref.dtype)


def gmlp(p, ln_g, ln_b, ws, bst, tr, chunk):
    m = p.shape[0]
    row_spec = lambda col: pl.BlockSpec((tr, BR_W), lambda i: (i, col))
    return pl.pallas_call(
        functools.partial(_gmlp_kernel, chunk=chunk),
        grid=(m // tr,),
        in_specs=[row_spec(0), row_spec(1),
                  pl.BlockSpec((1, BR_W), lambda i: (0, 0)),
                  pl.BlockSpec((1, BR_W), lambda i: (0, 0)),
                  pl.BlockSpec((1, 4, CHUNK, CHUNK), lambda i: (0, 0, 0, 0)),
                  pl.BlockSpec((1, CHUNK, 4), lambda i: (0, 0, 0))],
        out_specs=[row_spec(0), row_spec(0)],
        out_shape=[jax.ShapeDtypeStruct((m, BR_W), BF16), jax.ShapeDtypeStruct((m, BR_W), F32)],
        compiler_params=_params(1),
        name="gmlp",
    )(p, p, ln_g, ln_b, ws, bst)


def _attn_prompt_kernel(q_ref, k_ref, v_ref, bm_ref, o_ref, acc_ref, m_ref, l_ref):
    t = q_ref.shape[0]
    group = pl.program_id(2)
    scale = GW ** -0.5

    def rows(start, n, dil):
        return pl.ds(start, n) if dil == 1 else pl.ds(start, n, stride=dil)

    def block(dil, first_block, first_group, q0, k0):
        nkeys = NK if first_block else 2 * NK
        qi = rows(q0, NK, dil)
        ki = rows(k0, nkeys, dil)
        q = q_ref[qi, :].astype(BF16)
        kk = k_ref[ki, :].astype(BF16)
        vv = v_ref[ki, :].astype(BF16)
        bm = bm_ref[0, 0, :, NK:] if first_block else bm_ref[0, 0]
        s = lax.dot_general(q, kk, (((1,), (1,)), ((), ())), preferred_element_type=F32) * scale + bm
        m_blk = jnp.max(s, axis=-1, keepdims=True)
        if first_group:
            m_new = m_blk
            p = jnp.exp(s - m_new)
            l_new = jnp.sum(p, axis=-1, keepdims=True)
            acc_new = jnp.dot(p.astype(BF16), vv, preferred_element_type=F32)
        else:
            m_old = m_ref[qi, :][:, :1]
            l_old = l_ref[qi, :][:, :1]
            m_new = jnp.maximum(m_old, m_blk)
            alpha = jnp.exp(m_old - m_new)
            p = jnp.exp(s - m_new)
            l_new = alpha * l_old + jnp.sum(p, axis=-1, keepdims=True)
            acc_new = alpha * acc_ref[qi, :] + jnp.dot(p.astype(BF16), vv, preferred_element_type=F32)
        m_ref[qi, :] = jnp.broadcast_to(m_new, (NK, GW))
        l_ref[qi, :] = jnp.broadcast_to(l_new, (NK, GW))
        acc_ref[qi, :] = acc_new

    def run_group(gi):
        dil = PATTERNS[gi][1]
        nb = (t // dil) // NK

        def residue(r, carry):
            block(dil, True, gi == 0, r, r)

            def later(n, c):
                block(dil, False, gi == 0, n * (NK * dil) + r, (n - 1) * (NK * dil) + r)
                return c

            return lax.fori_loop(1, nb, later, carry)

        if dil == 1:
            residue(0, 0)
        else:
            lax.fori_loop(0, dil, residue, 0)

    for gi in range(len(PATTERNS)):
        pl.when(group == gi)(functools.partial(run_group, gi))

    @pl.when(group == len(PATTERNS) - 1)
    def _():
        step = 512

        def norm(c, carry):
            sl = pl.ds(pl.multiple_of(c * step, step), step)
            o_ref[sl, :] = (acc_ref[sl, :] / l_ref[sl, :]).astype(o_ref.dtype)
            return carry

        lax.fori_loop(0, t // step, norm, 0)


def attention_prompt(p, biasmask, nseq, t):
    def qkv_spec(which):
        return pl.BlockSpec((t, GW), lambda b, h, g: (b, COL_B // GW + g * 12 + which * 4 + h))

    return pl.pallas_call(
        _attn_prompt_kernel,
        grid=(nseq, N_HEADS, len(PATTERNS)),
        in_specs=[qkv_spec(0), qkv_spec(1), qkv_spec(2),
                  pl.BlockSpec((1, 1, NK, 2 * NK), lambda b, h, g: (g, h, 0, 0))],
        out_specs=pl.BlockSpec((t, GW), lambda b, h, g: (b, h)),
        out_shape=jax.ShapeDtypeStruct((nseq * t, BR_W), BF16),
        scratch_shapes=[pltpu.VMEM((t, GW), F32), pltpu.VMEM((t, GW), F32), pltpu.VMEM((t, GW), F32)],
        compiler_params=_params(3),
        name="attention_prompt",
    )(p, p, p, biasmask)


def _attn_sample_kernel(p_ref, c1_ref, c2_ref, c3_ref, bc1_ref, bn1_ref, bc_ref, bn_ref, o_ref):
    scale = GW ** -0.5
    rowid = lax.broadcasted_iota(jnp.int32, (SAMPLE_ROWS, NK), 0)
    nt = (((1,), (1,)), ((), ()))
    n_real = 4
    for h in range(N_HEADS):
        outs, lses = [], []
        for gi, (win, dil) in enumerate(PATTERNS):
            base = COL_B + gi * 3 * BR_W + h * GW
            q = p_ref[:, base:base + GW].astype(BF16)
            kn = p_ref[:, base + BR_W:base + BR_W + GW].astype(BF16)
            vn = p_ref[:, base + 2 * BR_W:base + 2 * BR_W + GW].astype(BF16)
            if dil == 1:
                kc = c1_ref[0, 0, :, h * GW:(h + 1) * GW].astype(BF16)
                vc = c1_ref[0, 0, :, BR_W + h * GW:BR_W + (h + 1) * GW].astype(BF16)
                fill = jnp.zeros((NK - SAMPLE_ROWS, GW), F32)
                kn = jnp.concatenate([p_ref[:, base + BR_W:base + BR_W + GW], fill], axis=0).astype(BF16)
                vn = jnp.concatenate([p_ref[:, base + 2 * BR_W:base + 2 * BR_W + GW], fill], axis=0).astype(BF16)
                s_c = lax.dot_general(q, kc, nt, preferred_element_type=F32) * scale + bc1_ref[h]
                s_n = lax.dot_general(q, kn, nt, preferred_element_type=F32) * scale + bn1_ref[h]
                m = jnp.maximum(jnp.max(s_c, axis=-1, keepdims=True), jnp.max(s_n, axis=-1, keepdims=True))
                p_c = jnp.exp(s_c - m)
                p_n = jnp.exp(s_n - m)
                l = jnp.sum(p_c, axis=-1, keepdims=True) + jnp.sum(p_n, axis=-1, keepdims=True)
                acc = (jnp.dot(p_c.astype(BF16), vc, preferred_element_type=F32)
                       + jnp.dot(p_n.astype(BF16), vn, preferred_element_type=F32))
            else:
                c_ref = c2_ref if gi == 1 else c3_ref
                s_c = jnp.zeros((SAMPLE_ROWS, NK), F32)
                for tq in range(n_real):
                    col = tq * 2 * BR_W + h * GW
                    kc = c_ref[0, 0, :, col:col + GW].astype(BF16)
                    s_t = lax.dot_general(q, kc, nt, preferred_element_type=F32)
                    s_c = jnp.where(rowid == tq, s_t, s_c)
                s_c = s_c * scale + bc_ref[gi - 1, h:h + 1, :]
                qk = jnp.sum(q.astype(F32) * kn.astype(F32), axis=-1, keepdims=True)
                s_n = qk * scale + bn_ref[gi - 1, h:h + 1, :][:, :1]
                m = jnp.maximum(jnp.max(s_c, axis=-1, keepdims=True), s_n)
                p_c = jnp.exp(s_c - m)
                p_n = jnp.exp(s_n - m)
                l = jnp.sum(p_c, axis=-1, keepdims=True) + p_n
                pcb = p_c.astype(BF16)
                acc = p_n.astype(BF16).astype(F32) * vn.astype(F32)
                for tq in range(n_real):
                    col = tq * 2 * BR_W + BR_W + h * GW
                    vc = c_ref[0, 0, :, col:col + GW].astype(BF16)
                    o_t = jnp.dot(pcb, vc, preferred_element_type=F32)
                    acc = acc + jnp.where(rowid == tq, o_t, 0.0)
            outs.append(acc / l)
            lses.append(m + jnp.log(l))
        top = jnp.maximum(jnp.maximum(lses[0], lses[1]), lses[2])
        ws = [jnp.exp(x - top) for x in lses]
        den = ws[0] + ws[1] + ws[2]
        y = (ws[0] * outs[0] + ws[1] * outs[1] + ws[2] * outs[2]) / den
        o_ref[:, h * GW:(h + 1) * GW] = y.astype(o_ref.dtype)


def attention_sample(p, c1, c2, c3, bias_tabs, layer, nseq):
    bc1, bn1, bc, bn = bias_tabs
    full = lambda a: pl.BlockSpec(a.shape, lambda b: (0,) * a.ndim)
    cache_spec = lambda width: pl.BlockSpec((1, 1, NK, width), lambda b: (layer, b, 0, 0))
    return pl.pallas_call(
        _attn_sample_kernel,
        grid=(nseq,),
        in_specs=[pl.BlockSpec((SAMPLE_ROWS, N_IN), lambda b: (b, 0)),
                  cache_spec(2 * BR_W), cache_spec(8 * BR_W), cache_spec(8 * BR_W),
                  full(bc1), full(bn1), full(bc), full(bn)],
        out_specs=pl.BlockSpec((SAMPLE_ROWS, BR_W), lambda b: (b, 0)),
        out_shape=jax.ShapeDtypeStruct((nseq * SAMPLE_ROWS, BR_W), BF16),
        compiler_params=_params(1),
        name="attention_sample",
    )(p, c1, c2, c3, bc1, bn1, bc, bn)


def _pool_conv_kernel(pc_ref, pbg_ref, pcg_ref, phs_ref, hc_ref, hd_ref, pw_ref, pb_ref, ps_ref, cw_ref,
                      yc_ref, yd_ref, zt_ref, cbuf_ref, zbuf_ref, *, start):
    tr = pc_ref.shape[0]
    j = pl.program_id(1)
    hc = 16
    hz = 8

    @pl.when(j == 0)
    def _():
        cbuf_ref[0:hc, :] = hc_ref[0]
        zbuf_ref[0:hz, :] = hd_ref[0]

    x = pc_ref[...]
    cbuf_ref[hc:hc + tr, :] = x
    pos = start + j * tr + lax.broadcasted_iota(jnp.int32, (tr, 1), 0)
    for gi, win in enumerate(POOL_WINDOWS):
        cols = slice(gi * GW, (gi + 1) * GW)
        total = x[:, cols]
        for back in range(1, win):
            total = total + cbuf_ref[hc - back:hc - back + tr, cols]
        cnt = jnp.minimum(pos + 1, win).astype(F32)
        pooled = total / cnt - x[:, cols]
        y = jnp.dot(pooled.astype(BF16), pw_ref[0, gi].astype(BF16), preferred_element_type=F32)
        yc_ref[:, cols] = ((y + pb_ref[:, cols]) * ps_ref[:, cols]).astype(yc_ref.dtype)
    cbuf_ref[0:hc, :] = cbuf_ref[tr:tr + hc, :]

    z = pcg_ref[...] * phs_ref[...]
    zbuf_ref[hz:hz + tr, :] = z
    y = (zbuf_ref[hz - 2:hz - 2 + tr, :] * cw_ref[0:1, :] + zbuf_ref[hz - 1:hz - 1 + tr, :] * cw_ref[1:2, :]
         + z * cw_ref[2:3, :])
    yd_ref[...] = (pbg_ref[...] * y).astype(yd_ref.dtype)
    zt_ref[0] = z[tr - hz:, :]
    zbuf_ref[0:hz, :] = zbuf_ref[tr:tr + hz, :]


def pool_conv(p, hist_c, hist_d, pool_w, pool_b, pool_scale, conv_wt, layer, nseq, t, tr, start):
    per = t // tr
    col = lambda c: pl.BlockSpec((tr, BR_W), lambda b, j: (b * per + j, c))
    vec = pl.BlockSpec((1, BR_W), lambda b, j: (0, 0))
    return pl.pallas_call(
        functools.partial(_pool_conv_kernel, start=start),
        grid=(nseq, per),
        in_specs=[col(COL_C // BR_W), col(COL_D // BR_W), col(COL_D // BR_W + 1), col(COL_D // BR_W + 2),
                  pl.BlockSpec((1, 16, BR_W), lambda b, j: (b, 0, 0)),
                  pl.BlockSpec((1, 8, BR_W), lambda b, j: (b, 0, 0)),
                  pl.BlockSpec((1, 4, GW, GW), lambda b, j: (layer, 0, 0, 0)),
                  vec, vec,
                  pl.BlockSpec((CONV_W, BR_W), lambda b, j: (0, 0))],
        out_specs=[col(0), col(0), pl.BlockSpec((1, 8, BR_W), lambda b, j: (b, 0, 0))],
        out_shape=[jax.ShapeDtypeStruct((nseq * t, BR_W), BF16),
                   jax.ShapeDtypeStruct((nseq * t, BR_W), BF16),
                   jax.ShapeDtypeStruct((nseq, 8, BR_W), F32)],
        scratch_shapes=[pltpu.VMEM((16 + tr, BR_W), F32), pltpu.VMEM((8 + tr, BR_W), F32)],
        compiler_params=_params(2),
        name="pool_conv",
    )(p, p, p, p, hist_c, hist_d, pool_w, pool_b, pool_scale, conv_wt)


def _t5_bucket(dist):
    max_exact = N_BUCKETS // 2
    n = jnp.maximum(dist, 0)
    nf = jnp.maximum(n, 1).astype(F32)
    large = max_exact + (jnp.log(nf / max_exact) / math.log(MAX_DIST / max_exact)
                         * (N_BUCKETS - max_exact)).astype(jnp.int32)
    large = jnp.minimum(large, N_BUCKETS - 1)
    return jnp.where(n < max_exact, n, large)


def _prompt_bias(rel_bias):
    i = jnp.arange(NK)[:, None]
    j = jnp.arange(2 * NK)[None, :]
    diff = NK + i - j
    valid = (diff >= 0) & (diff <= NK)
    tabs = []
    for gi, (_, dil) in enumerate(PATTERNS):
        tab = rel_bias[:, gi * N_HEADS:(gi + 1) * N_HEADS]
        bias = tab[_t5_bucket(diff * dil)].transpose(2, 0, 1)
        tabs.append(jnp.where(valid[None], bias, NEG))
    return jnp.stack(tabs, axis=0).astype(F32)


def _sample_bias(rel_bias):
    tq = jnp.arange(SAMPLE_ROWS)[:, None]
    tab1 = rel_bias[:, 0:N_HEADS]
    r = jnp.arange(NK)[None, :]
    steps = NK + tq - r
    bc1 = jnp.where((steps <= NK)[None], tab1[_t5_bucket(steps)].transpose(2, 0, 1), NEG)
    jn = jnp.arange(NK)[None, :]
    stepn = tq - jn
    okn = (stepn >= 0) & (jn < 4)
    bn1 = jnp.where(okn[None], tab1[_t5_bucket(stepn)].transpose(2, 0, 1), NEG)
    bcs, bns = [], []
    for gi in (1, 2):
        dil = PATTERNS[gi][1]
        tab = rel_bias[:, gi * N_HEADS:(gi + 1) * N_HEADS]
        bcs.append(tab[_t5_bucket((NK - jnp.arange(NK)) * dil)].T)
        bns.append(jnp.broadcast_to(tab[_t5_bucket(jnp.zeros((), jnp.int32))][:, None], (N_HEADS, NK)))
    return (bc1.astype(F32), bn1.astype(F32), jnp.stack(bcs).astype(F32), jnp.stack(bns).astype(F32))


def _run_trunk(x, mods, caches, weights, nseq, t, tm, sample):
    (ln1_g, ln2_g, w_in, w_gate, b_gate, a_ln_g, a_ln_b, a_ws, a_bs, bias_tabs,
     pool_w, pool_b, pool_scale, conv_w, w_branch, w_out, w1, w3, w2, final_g) = weights
    depth = w_in.shape[0]
    m = x.shape[0]
    chunk = min(CHUNK, t)
    tr = min(512, t)
    kv_new = [[] for _ in PATTERNS]
    pool_new, conv_new, chunk_new = [], [], []
    for l in range(depth):
        sh1, sc1, g1, sh2, sc2, g2 = mods[l]
        hn = norm_modulate(x, ln1_g[l][None], sc1, sh1, tm, BF16)
        p = matmul(hn, w_in, l, tm, 1536, F32)
        ws = a_ws[l][None]
        bst = a_bs[l].T[None]
        ya, va = gmlp(p, a_ln_g[l][None], a_ln_b[l][None], ws, bst, min(512, m), chunk)
        if sample:
            c1, c2, c3, cpool, cconv = caches
            yb = attention_sample(p, c1, c2, c3, bias_tabs, l, nseq)
            hist_c = jnp.pad(cpool[l], ((0, 0), (1, 0), (0, 0)))
            hist_d = jnp.pad(cconv[l], ((0, 0), (8 - (CONV_W - 1), 0), (0, 0)))
            start = PAST_LEN
        else:
            yb = attention_prompt(p, bias_tabs, nseq, t)
            hist_c = jnp.zeros((nseq, 16, BR_W), F32)
            hist_d = jnp.zeros((nseq, 8, BR_W), F32)
            start = 0
        yc, yd, ztail = pool_conv(p, hist_c, hist_d, pool_w, pool_b[l][None], pool_scale[l][None],
                                  conv_w[l].T, l, nseq, t, tr, start)
        merged = gate_merge(hn, (ya, yb, yc, yd), w_gate, b_gate, w_branch, l, tm, 256)
        x = matmul_residual(merged, w_out, l, x, g1, tm, 1024)
        hn2 = norm_modulate(x, ln2_g[l][None], sc2, sh2, tm, BF16)
        hmid = swiglu_up(hn2, w1, w3, l, tm, 512)
        x = matmul_residual(hmid, w2, l, x, g2, tm, 512)

        p3 = p.reshape(nseq, t, N_IN)
        for gi, (win, _) in enumerate(PATTERNS):
            c0 = COL_B + gi * 3 * BR_W + BR_W
            if sample:
                kv = p3[:, :4, c0:c0 + 2 * BR_W].reshape(nseq, 4, 2, N_HEADS, GW)
            else:
                keep = min(win, t)
                kv = p3[:, t - keep:, c0:c0 + 2 * BR_W].reshape(nseq, keep, 2, N_HEADS, GW)
            kv_new[gi].append(kv)
        if sample:
            pool_new.append(p3[:, :4, COL_C:COL_C + BR_W])
            conv_new.append(ztail[:, :4])
            chunk_new.append(va.reshape(nseq, t, BR_W)[:, :4])
        else:
            pool_new.append(p3[:, t - POOL_HIST:, COL_C:COL_C + BR_W])
            conv_new.append(ztail[:, 8 - (CONV_W - 1):])
    zeros = jnp.zeros((1, 1, D_MODEL), F32)
    y = norm_modulate(x, final_g[None], zeros, zeros, tm, F32)
    states = [jnp.stack(s, axis=0) for s in kv_new] + [jnp.stack(pool_new, axis=0), jnp.stack(conv_new, axis=0)]
    if sample:
        states.append(jnp.stack(chunk_new, axis=0))
    return y, states


def kernel(x_prompt, x_sample, c_prompt, c_sample, cache_b1, cache_b2, cache_b3, cache_pool, cache_conv,
           ln1_g, ln2_g, w_ada, b_ada, w_in, w_gate, b_gate, a_ln_g, a_ln_b, a_ws, a_bs,
           rel_bias, pool_w, pool_b, pool_scale, conv_w, w_branch, w_out, w1, w3, w2, final_g):
    nb, seq, d = x_prompt.shape
    ns, dec = x_sample.shape[:2]
    depth = w_in.shape[0]

    c_all = jnp.concatenate([c_prompt, c_sample], axis=0)
    c_rows = -(-c_all.shape[0] // 8) * 8
    c_all = jnp.pad(c_all, ((0, c_rows - c_all.shape[0]), (0, 0)))
    mod = ada_modulation(c_all, w_ada, b_ada).reshape(depth, c_rows, 6, d)
    mods_p = [[mod[l, :nb, i][:, None, :] for i in range(6)] for l in range(depth)]
    mods_s = [[jnp.repeat(mod[l, nb:nb + ns, i], SAMPLE_ROWS, axis=0)[None] for i in range(6)]
              for l in range(depth)]

    shared = (ln1_g, ln2_g, w_in, w_gate, b_gate, a_ln_g, a_ln_b, a_ws, a_bs)
    tail = (pool_w, pool_b, pool_scale, conv_w, w_branch, w_out, w1, w3, w2, final_g)

    xp = x_prompt.reshape(nb * seq, d)
    y_p, sp = _run_trunk(xp, mods_p, None, shared + (_prompt_bias(rel_bias),) + tail, nb, seq, 512, False)

    xs = jnp.pad(x_sample, ((0, 0), (0, SAMPLE_ROWS - dec), (0, 0))).reshape(ns * SAMPLE_ROWS, d)
    caches = (cache_b1.reshape(depth, ns, NK, 2 * BR_W),
              cache_b2.reshape(depth, ns, NK, 8 * BR_W),
              cache_b3.reshape(depth, ns, NK, 32 * BR_W),
              cache_pool, cache_conv)
    y_s, ss = _run_trunk(xs, mods_s, caches, shared + (_sample_bias(rel_bias),) + tail, ns, SAMPLE_ROWS,
                         ns * SAMPLE_ROWS, True)

    y_prompt = y_p.reshape(nb, seq, d)
    y_sample = y_s.reshape(ns, SAMPLE_ROWS, d)[:, :dec]
    return (y_prompt, y_sample, sp[0], sp[1], sp[2], sp[3], sp[4],
            ss[0], ss[1], ss[2], ss[3], ss[4], ss[5])
```

```python
import functools
import math

import jax
import jax.numpy as jnp
import numpy as np
from jax import lax
from jax.experimental import pallas as pl
from jax.experimental.pallas import tpu as pltpu

F32 = jnp.float32
BF16 = jnp.bfloat16

D_MODEL = 2048
BR_W = 512
GW = 128
N_HEADS = 4
CHUNK = 128
PATTERNS = ((128, 1), (512, 4), (2048, 16))
NK = 128
POOL_WINDOWS = (2, 4, 8, 16)
POOL_HIST = 15
CONV_W = 3
N_BUCKETS = 32
MAX_DIST = 2048
D_FF = 5632
N_IN = 7680
COL_A = 0
COL_B = 2 * BR_W
COL_C = COL_B + 9 * BR_W
COL_D = COL_C + BR_W
EPS = 1e-6
NEG = -1e30
SAMPLE_ROWS = 8
PAST_LEN = 16384
ATTN_BLOCKS_PER_STEP = 4

VMEM_LIMIT = 56 * 1024 * 1024


def _t5_bucket(dist):
    max_exact = N_BUCKETS // 2
    n = np.maximum(dist, 0)
    nf = np.maximum(n, 1).astype(np.float32)
    large = max_exact + (np.log(nf / np.float32(max_exact)) / np.float32(math.log(MAX_DIST / max_exact))
                         * np.float32(N_BUCKETS - max_exact)).astype(np.int32)
    large = np.minimum(large, N_BUCKETS - 1)
    return np.where(n < max_exact, n, large)


def _params(n_axes):
    return pltpu.CompilerParams(dimension_semantics=("arbitrary",) * n_axes,
                                vmem_limit_bytes=VMEM_LIMIT)


def _ada_kernel(c_ref, w_ref, b_ref, o_ref):
    a = jax.nn.silu(c_ref[...]).astype(BF16)
    o_ref[0] = jnp.dot(a, w_ref[0].astype(BF16), preferred_element_type=F32) + b_ref[0]


def ada_modulation(c_all, w_ada, b_ada):
    depth, d, n = w_ada.shape
    r = c_all.shape[0]
    tn = 1024
    return pl.pallas_call(
        _ada_kernel,
        grid=(depth, n // tn),
        in_specs=[pl.BlockSpec((r, d), lambda l, j: (0, 0)),
                  pl.BlockSpec((1, d, tn), lambda l, j: (l, 0, j)),
                  pl.BlockSpec((1, 1, tn), lambda l, j: (l, 0, j))],
        out_specs=pl.BlockSpec((1, r, tn), lambda l, j: (l, 0, j)),
        out_shape=jax.ShapeDtypeStruct((depth, r, n), F32),
        compiler_params=_params(2),
        name="ada_modulation",
    )(c_all, w_ada, b_ada.reshape(depth, 1, n))


def _norm_kernel(x_ref, g_ref, sc_ref, sh_ref, o_ref):
    x = x_ref[...]
    y = x * lax.rsqrt(jnp.mean(x * x, axis=-1, keepdims=True) + EPS) * g_ref[...]
    o_ref[...] = (y * (1.0 + sc_ref[0]) + sh_ref[0]).astype(o_ref.dtype)


def norm_modulate(x, g, sc, sh, tm, out_dtype):
    m, d = x.shape
    nb, r, _ = sc.shape
    per = (m // tm) // nb
    mod_spec = pl.BlockSpec((1, r, d), lambda i: (i // per, 0, 0))
    return pl.pallas_call(
        _norm_kernel,
        grid=(m // tm,),
        in_specs=[pl.BlockSpec((tm, d), lambda i: (i, 0)),
                  pl.BlockSpec((1, d), lambda i: (0, 0)),
                  mod_spec, mod_spec],
        out_specs=pl.BlockSpec((tm, d), lambda i: (i, 0)),
        out_shape=jax.ShapeDtypeStruct((m, d), out_dtype),
        compiler_params=_params(1),
        name="norm_modulate",
    )(x, g, sc, sh)


def _mm_kernel(a_ref, w_ref, o_ref, wbf_ref):
    @pl.when(pl.program_id(1) == 0)
    def _():
        wbf_ref[...] = w_ref[0].astype(BF16)

    o_ref[...] = jnp.dot(a_ref[...], wbf_ref[...], preferred_element_type=F32).astype(o_ref.dtype)


def matmul(a, w, layer, tm, tn, out_dtype):
    m, k = a.shape
    n = w.shape[2]
    return pl.pallas_call(
        _mm_kernel,
        grid=(n // tn, m // tm),
        in_specs=[pl.BlockSpec((tm, k), lambda j, i: (i, 0)),
                  pl.BlockSpec((1, k, tn), lambda j, i: (layer, 0, j))],
        out_specs=pl.BlockSpec((tm, tn), lambda j, i: (i, j)),
        out_shape=jax.ShapeDtypeStruct((m, n), out_dtype),
        scratch_shapes=[pltpu.VMEM((k, tn), BF16)],
        compiler_params=_params(2),
        name="matmul",
    )(a, w)


def _mm_residual_kernel(a_ref, w_ref, x_ref, g_ref, o_ref, wbf_ref):
    @pl.when(pl.program_id(1) == 0)
    def _():
        wbf_ref[...] = w_ref[0].astype(BF16)

    y = jnp.dot(a_ref[...], wbf_ref[...], preferred_element_type=F32)
    o_ref[...] = x_ref[...] + g_ref[0] * y


def matmul_residual(a, w, layer, x, gate, tm, tn):
    m, k = a.shape
    n = w.shape[2]
    nb, r, _ = gate.shape
    per = (m // tm) // nb
    return pl.pallas_call(
        _mm_residual_kernel,
        grid=(n // tn, m // tm),
        in_specs=[pl.BlockSpec((tm, k), lambda j, i: (i, 0)),
                  pl.BlockSpec((1, k, tn), lambda j, i: (layer, 0, j)),
                  pl.BlockSpec((tm, tn), lambda j, i: (i, j)),
                  pl.BlockSpec((1, r, tn), lambda j, i: (i // per, 0, j))],
        out_specs=pl.BlockSpec((tm, tn), lambda j, i: (i, j)),
        out_shape=jax.ShapeDtypeStruct((m, n), F32),
        scratch_shapes=[pltpu.VMEM((k, tn), BF16)],
        compiler_params=_params(2),
        name="matmul_residual",
    )(a, w, x, gate)


def _swiglu_kernel(a_ref, w1_ref, w3_ref, o_ref, w1bf_ref, w3bf_ref):
    @pl.when(pl.program_id(1) == 0)
    def _():
        w1bf_ref[...] = w1_ref[0].astype(BF16)
        w3bf_ref[...] = w3_ref[0].astype(BF16)

    a = a_ref[...]
    h1 = jnp.dot(a, w1bf_ref[...], preferred_element_type=F32)
    h3 = jnp.dot(a, w3bf_ref[...], preferred_element_type=F32)
    o_ref[...] = (jax.nn.silu(h1) * h3).astype(o_ref.dtype)


def swiglu_up(a, w1, w3, layer, tm, tn):
    m, k = a.shape
    n = w1.shape[2]
    w_spec = pl.BlockSpec((1, k, tn), lambda j, i: (layer, 0, j))
    return pl.pallas_call(
        _swiglu_kernel,
        grid=(n // tn, m // tm),
        in_specs=[pl.BlockSpec((tm, k), lambda j, i: (i, 0)), w_spec, w_spec],
        out_specs=pl.BlockSpec((tm, tn), lambda j, i: (i, j)),
        out_shape=jax.ShapeDtypeStruct((m, n), BF16),
        scratch_shapes=[pltpu.VMEM((k, tn), BF16), pltpu.VMEM((k, tn), BF16)],
        compiler_params=_params(2),
        name="swiglu_up",
    )(a, w1, w3)


def _gate_merge_kernel(hn_ref, ya_ref, yb_ref, yc_ref, yd_ref, wg0_ref, wg1_ref, wg2_ref, wg3_ref,
                       wb_ref, bg_ref, o_ref, wgbf_ref, wbbf_ref):
    @pl.when(pl.program_id(1) == 0)
    def _():
        for g, wg_ref in enumerate((wg0_ref, wg1_ref, wg2_ref, wg3_ref)):
            wgbf_ref[g] = wg_ref[0].astype(BF16)
            wbbf_ref[g] = wb_ref[0, g].astype(BF16)

    hn = hn_ref[...]
    acc = None
    for g, y_ref in enumerate((ya_ref, yb_ref, yc_ref, yd_ref)):
        gate = jax.nn.sigmoid(jnp.dot(hn, wgbf_ref[g], preferred_element_type=F32) + bg_ref[0, g:g + 1, :])
        br = jnp.dot(y_ref[...], wbbf_ref[g], preferred_element_type=F32)
        acc = gate * br if acc is None else acc + gate * br
    o_ref[...] = acc.astype(o_ref.dtype)


def gate_merge(hn, ys, w_gate, b_gate, w_branch, layer, tm, tn):
    m, k = hn.shape
    depth = w_gate.shape[0]
    d = w_branch.shape[3]
    nj = d // tn
    wg_specs = [pl.BlockSpec((1, k, tn), functools.partial(lambda j, i, g: (layer, 0, g * nj + j), g=g))
                for g in range(4)]
    y_spec = pl.BlockSpec((tm, BR_W), lambda j, i: (i, 0))
    return pl.pallas_call(
        _gate_merge_kernel,
        grid=(nj, m // tm),
        in_specs=[pl.BlockSpec((tm, k), lambda j, i: (i, 0)), y_spec, y_spec, y_spec, y_spec,
                  *wg_specs,
                  pl.BlockSpec((1, 4, BR_W, tn), lambda j, i: (layer, 0, 0, j)),
                  pl.BlockSpec((1, 4, tn), lambda j, i: (layer, 0, j))],
        out_specs=pl.BlockSpec((tm, tn), lambda j, i: (i, j)),
        out_shape=jax.ShapeDtypeStruct((m, d), BF16),
        scratch_shapes=[pltpu.VMEM((4, k, tn), BF16), pltpu.VMEM((4, BR_W, tn), BF16)],
        compiler_params=_params(2),
        name="gate_merge",
    )(hn, *ys, w_gate, w_gate, w_gate, w_gate, w_branch, b_gate.reshape(depth, 4, d))


def _gmlp_kernel(pu_ref, pv_ref, lg_ref, lb_ref, ws_ref, bst_ref, ya_ref, va_ref, *, chunk):
    rows = pu_ref.shape[0]
    u = jax.nn.gelu(pu_ref[...])
    v = jax.nn.gelu(pv_ref[...])
    mu = jnp.mean(v, axis=-1, keepdims=True)
    var = jnp.mean(jnp.square(v - mu), axis=-1, keepdims=True)
    vn = (v - mu) * lax.rsqrt(var + 1e-5) * lg_ref[...] + lb_ref[...]
    va_ref[...] = vn
    causal = (lax.broadcasted_iota(jnp.int32, (CHUNK, CHUNK), 0)
              >= lax.broadcasted_iota(jnp.int32, (CHUNK, CHUNK), 1))
    for g in range(4):
        wm = jnp.where(causal, ws_ref[0, g], 0.0).astype(BF16)
        bias = bst_ref[0, :, g:g + 1]
        for c in range(rows // chunk):
            r0 = c * chunk
            vc = vn[r0:r0 + chunk, g * GW:(g + 1) * GW]
            if chunk < CHUNK:
                vc = jnp.concatenate([vc, jnp.zeros((CHUNK - chunk, GW), F32)], axis=0)
            mix = (jnp.dot(wm, vc.astype(BF16), preferred_element_type=F32) + bias)[:chunk]
            ya_ref[r0:r0 + chunk, g * GW:(g + 1) * GW] = (
                u[r0:r0 + chunk, g * GW:(g + 1) * GW] * mix).astype(ya_ref.dtype)


def gmlp(p, ln_g, ln_b, ws, bst, tr, chunk):
    m = p.shape[0]
    row_spec = lambda col: pl.BlockSpec((tr, BR_W), lambda i: (i, col))
    return pl.pallas_call(
        functools.partial(_gmlp_kernel, chunk=chunk),
        grid=(m // tr,),
        in_specs=[row_spec(0), row_spec(1),
                  pl.BlockSpec((1, BR_W), lambda i: (0, 0)),
                  pl.BlockSpec((1, BR_W), lambda i: (0, 0)),
                  pl.BlockSpec((1, 4, CHUNK, CHUNK), lambda i: (0, 0, 0, 0)),
                  pl.BlockSpec((1, CHUNK, 4), lambda i: (0, 0, 0))],
        out_specs=[row_spec(0), row_spec(0)],
        out_shape=[jax.ShapeDtypeStruct((m, BR_W), BF16), jax.ShapeDtypeStruct((m, BR_W), F32)],
        compiler_params=_params(1),
        name="gmlp",
    )(p, p, ln_g, ln_b, ws, bst)


def _bias_from_buckets(idx, rel_ref, col, shape):
    bias = jnp.full(shape, NEG, F32)
    for b in range(N_BUCKETS):
        bias = jnp.where(idx == b, rel_ref[b, col], bias)
    return bias


def _attn_prompt_kernel(rel_ref, q_ref, k_ref, v_ref, idx_ref, o_ref,
                        o0_ref, o1_ref, o2_ref, e0_ref, e1_ref, e2_ref, bm_ref):
    t = q_ref.shape[0]
    head = pl.program_id(1)
    group = pl.program_id(2)
    scale = GW ** -0.5
    outs = (o0_ref, o1_ref, o2_ref)
    lses = (e0_ref, e1_ref, e2_ref)

    bm_ref[...] = _bias_from_buckets(idx_ref[0], rel_ref, group * N_HEADS + head, (NK, 2 * NK))

    def rows(start, n, dil):
        return pl.ds(start, n) if dil == 1 else pl.ds(start, n, stride=dil)

    def blocks(gi, specs):
        dil = PATTERNS[gi][1]
        staged = []
        for first_block, q0, k0 in specs:
            nkeys = NK if first_block else 2 * NK
            qi = rows(q0, NK, dil)
            ki = rows(k0, nkeys, dil)
            q = q_ref[qi, :].astype(BF16)
            kk = k_ref[ki, :].astype(BF16)
            bm = bm_ref[:, NK:] if first_block else bm_ref[...]
            s = lax.dot_general(q, kk, (((1,), (1,)), ((), ())), preferred_element_type=F32) * scale + bm
            staged.append((qi, ki, s))
        probs = []
        for qi, ki, s in staged:
            m = jnp.max(s, axis=-1, keepdims=True)
            p = jnp.exp(s - m)
            l = jnp.sum(p, axis=-1, keepdims=True)
            probs.append((qi, ki, p.astype(BF16), m, l))
        for qi, ki, p, m, l in probs:
            acc = jnp.dot(p, v_ref[ki, :].astype(BF16), preferred_element_type=F32)
            outs[gi][qi, :] = acc * (1.0 / l)
            lses[gi][qi, :] = jnp.broadcast_to(m + jnp.log(l), (NK, GW))

    def run_group(gi):
        dil = PATTERNS[gi][1]
        span = NK * dil
        nb = t // span
        u = ATTN_BLOCKS_PER_STEP
        if dil == 1:
            blocks(gi, [(True, 0, 0)] + [(False, n * span, (n - 1) * span) for n in range(1, u)])

            def step(i, c):
                q0 = pl.multiple_of(u * i * span, span)
                blocks(gi, [(False, q0 + n * span, q0 + (n - 1) * span) for n in range(u)])
                return c

            lax.fori_loop(1, nb // u, step, 0)
        else:
            def bunch(i, c):
                res = [u * i + r for r in range(u)]
                blocks(gi, [(True, r, r) for r in res])

                def later(n, cc):
                    blocks(gi, [(False, n * span + r, (n - 1) * span + r) for r in res])
                    return cc

                return lax.fori_loop(1, nb, later, c)

            if dil == u:
                bunch(0, 0)
            else:
                lax.fori_loop(0, dil // u, bunch, 0)

    for gi in range(len(PATTERNS)):
        pl.when(group == gi)(functools.partial(run_group, gi))

    @pl.when(group == len(PATTERNS) - 1)
    def _():
        step = 256

        def merge(c, carry):
            sl = pl.ds(pl.multiple_of(c * step, step), step)
            e = [r[sl, :] for r in lses]
            top = jnp.maximum(jnp.maximum(e[0], e[1]), e[2])
            w = [jnp.exp(x - top) for x in e]
            num = w[0] * o0_ref[sl, :] + w[1] * o1_ref[sl, :] + w[2] * o2_ref[sl, :]
            o_ref[sl, :] = (num / (w[0] + w[1] + w[2])).astype(o_ref.dtype)
            return carry

        lax.fori_loop(0, t // step, merge, 0)


def _prompt_buckets():
    i = np.arange(NK)[:, None]
    j = np.arange(2 * NK)[None, :]
    diff = NK + i - j
    valid = (diff >= 0) & (diff <= NK)
    return np.stack([np.where(valid, _t5_bucket(diff * dil), -1) for _, dil in PATTERNS]).astype(np.int32)


def attention_prompt(p, rel_bias, nseq, t):
    def qkv_spec(which):
        return pl.BlockSpec((t, GW), lambda b, h, g: (b, COL_B // GW + g * 12 + which * 4 + h))

    return pl.pallas_call(
        _attn_prompt_kernel,
        grid=(nseq, N_HEADS, len(PATTERNS)),
        in_specs=[pl.BlockSpec(memory_space=pltpu.SMEM),
                  qkv_spec(0), qkv_spec(1), qkv_spec(2),
                  pl.BlockSpec((1, NK, 2 * NK), lambda b, h, g: (g, 0, 0))],
        out_specs=pl.BlockSpec((t, GW), lambda b, h, g: (b, h)),
        out_shape=jax.ShapeDtypeStruct((nseq * t, BR_W), BF16),
        scratch_shapes=[pltpu.VMEM((t, GW), F32)] * 6 + [pltpu.VMEM((NK, 2 * NK), F32)],
        compiler_params=_params(3),
        name="attention_prompt",
    )(rel_bias, p, p, p, jnp.asarray(_prompt_buckets()))


def _attn_sample_kernel(rel_ref, p_ref, c1_ref, c2_ref, c3_ref, ic1_ref, in1_ref, ic_ref, o_ref):
    scale = GW ** -0.5
    rowid = lax.broadcasted_iota(jnp.int32, (SAMPLE_ROWS, NK), 0)
    nt = (((1,), (1,)), ((), ()))
    n_real = 4
    for h in range(N_HEADS):
        outs, lses = [], []
        for gi, (win, dil) in enumerate(PATTERNS):
            base = COL_B + gi * 3 * BR_W + h * GW
            rel_col = gi * N_HEADS + h
            q = p_ref[:, base:base + GW].astype(BF16)
            kn = p_ref[:, base + BR_W:base + BR_W + GW].astype(BF16)
            vn = p_ref[:, base + 2 * BR_W:base + 2 * BR_W + GW].astype(BF16)
            if dil == 1:
                kc = c1_ref[0, 0, :, h * GW:(h + 1) * GW].astype(BF16)
                vc = c1_ref[0, 0, :, BR_W + h * GW:BR_W + (h + 1) * GW].astype(BF16)
                fill = jnp.zeros((NK - SAMPLE_ROWS, GW), F32)
                kn = jnp.concatenate([p_ref[:, base + BR_W:base + BR_W + GW], fill], axis=0).astype(BF16)
                vn = jnp.concatenate([p_ref[:, base + 2 * BR_W:base + 2 * BR_W + GW], fill], axis=0).astype(BF16)
                bias_c = _bias_from_buckets(ic1_ref[...], rel_ref, rel_col, (SAMPLE_ROWS, NK))
                bias_n = _bias_from_buckets(in1_ref[...], rel_ref, rel_col, (SAMPLE_ROWS, NK))
                s_c = lax.dot_general(q, kc, nt, preferred_element_type=F32) * scale + bias_c
                s_n = lax.dot_general(q, kn, nt, preferred_element_type=F32) * scale + bias_n
                m = jnp.maximum(jnp.max(s_c, axis=-1, keepdims=True), jnp.max(s_n, axis=-1, keepdims=True))
                p_c = jnp.exp(s_c - m)
                p_n = jnp.exp(s_n - m)
                l = jnp.sum(p_c, axis=-1, keepdims=True) + jnp.sum(p_n, axis=-1, keepdims=True)
                acc = (jnp.dot(p_c.astype(BF16), vc, preferred_element_type=F32)
                       + jnp.dot(p_n.astype(BF16), vn, preferred_element_type=F32))
            else:
                c_ref = c2_ref if gi == 1 else c3_ref
                s_c = jnp.zeros((SAMPLE_ROWS, NK), F32)
                for tq in range(n_real):
                    col = tq * 2 * BR_W + h * GW
                    kc = c_ref[0, 0, :, col:col + GW].astype(BF16)
                    s_t = lax.dot_general(q, kc, nt, preferred_element_type=F32)
                    s_c = jnp.where(rowid == tq, s_t, s_c)
                s_c = s_c * scale + _bias_from_buckets(ic_ref[gi - 1], rel_ref, rel_col, (SAMPLE_ROWS, NK))
                qk = jnp.sum(q.astype(F32) * kn.astype(F32), axis=-1, keepdims=True)
                s_n = qk * scale + rel_ref[0, rel_col]
                m = jnp.maximum(jnp.max(s_c, axis=-1, keepdims=True), s_n)
                p_c = jnp.exp(s_c - m)
                p_n = jnp.exp(s_n - m)
                l = jnp.sum(p_c, axis=-1, keepdims=True) + p_n
                pcb = p_c.astype(BF16)
                acc = p_n.astype(BF16).astype(F32) * vn.astype(F32)
                for tq in range(n_real):
                    col = tq * 2 * BR_W + BR_W + h * GW
                    vc = c_ref[0, 0, :, col:col + GW].astype(BF16)
                    o_t = jnp.dot(pcb, vc, preferred_element_type=F32)
                    acc = acc + jnp.where(rowid == tq, o_t, 0.0)
            outs.append(acc / l)
            lses.append(m + jnp.log(l))
        top = jnp.maximum(jnp.maximum(lses[0], lses[1]), lses[2])
        ws = [jnp.exp(x - top) for x in lses]
        den = ws[0] + ws[1] + ws[2]
        y = (ws[0] * outs[0] + ws[1] * outs[1] + ws[2] * outs[2]) / den
        o_ref[:, h * GW:(h + 1) * GW] = y.astype(o_ref.dtype)


def _sample_buckets():
    tq = np.arange(SAMPLE_ROWS)[:, None]
    steps = NK + tq - np.arange(NK)[None, :]
    ic1 = np.where(steps <= NK, _t5_bucket(steps), -1)
    jn = np.arange(NK)[None, :]
    in1 = np.where((tq - jn >= 0) & (jn < 4), _t5_bucket(tq - jn), -1)
    ic = np.stack([np.broadcast_to(_t5_bucket((NK - np.arange(NK)) * dil)[None, :], (SAMPLE_ROWS, NK))
                   for _, dil in PATTERNS[1:]])
    return ic1.astype(np.int32), in1.astype(np.int32), ic.astype(np.int32)


def attention_sample(p, c1, c2, c3, rel_bias, layer, nseq):
    ic1, in1, ic = (jnp.asarray(a) for a in _sample_buckets())
    full = lambda a: pl.BlockSpec(a.shape, lambda b: (0,) * a.ndim)
    cache_spec = lambda width: pl.BlockSpec((1, 1, NK, width), lambda b: (layer, b, 0, 0))
    return pl.pallas_call(
        _attn_sample_kernel,
        grid=(nseq,),
        in_specs=[pl.BlockSpec(memory_space=pltpu.SMEM),
                  pl.BlockSpec((SAMPLE_ROWS, N_IN), lambda b: (b, 0)),
                  cache_spec(2 * BR_W), cache_spec(8 * BR_W), cache_spec(8 * BR_W),
                  full(ic1), full(in1), full(ic)],
        out_specs=pl.BlockSpec((SAMPLE_ROWS, BR_W), lambda b: (b, 0)),
        out_shape=jax.ShapeDtypeStruct((nseq * SAMPLE_ROWS, BR_W), BF16),
        compiler_params=_params(1),
        name="attention_sample",
    )(rel_bias, p, c1, c2, c3, ic1, in1, ic)


def _pool_conv_kernel(pc_ref, pbg_ref, pcg_ref, phs_ref, hc_ref, hd_ref, pw_ref, pb_ref, ps_ref, cw_ref,
                      yc_ref, yd_ref, zt_ref, cbuf_ref, zbuf_ref, *, start):
    tr = pc_ref.shape[0]
    j = pl.program_id(1)
    hc = 16
    hz = 8

    @pl.when(j == 0)
    def _():
        cbuf_ref[0:hc, :] = hc_ref[0]
        zbuf_ref[0:hz, :] = hd_ref[0]

    x = pc_ref[...]
    cbuf_ref[hc:hc + tr, :] = x
    pos = start + j * tr + lax.broadcasted_iota(jnp.int32, (tr, 1), 0)
    for gi, win in enumerate(POOL_WINDOWS):
        cols = slice(gi * GW, (gi + 1) * GW)
        total = x[:, cols]
        for back in range(1, win):
            total = total + cbuf_ref[hc - back:hc - back + tr, cols]
        cnt = jnp.minimum(pos + 1, win).astype(F32)
        pooled = total / cnt - x[:, cols]
        y = jnp.dot(pooled.astype(BF16), pw_ref[0, gi].astype(BF16), preferred_element_type=F32)
        yc_ref[:, cols] = ((y + pb_ref[:, cols]) * ps_ref[:, cols]).astype(yc_ref.dtype)
    cbuf_ref[0:hc, :] = cbuf_ref[tr:tr + hc, :]

    z = pcg_ref[...] * phs_ref[...]
    zbuf_ref[hz:hz + tr, :] = z
    y = (zbuf_ref[hz - 2:hz - 2 + tr, :] * cw_ref[0:1, :] + zbuf_ref[hz - 1:hz - 1 + tr, :] * cw_ref[1:2, :]
         + z * cw_ref[2:3, :])
    yd_ref[...] = (pbg_ref[...] * y).astype(yd_ref.dtype)
    zt_ref[0] = z[tr - hz:, :]
    zbuf_ref[0:hz, :] = zbuf_ref[tr:tr + hz, :]


def pool_conv(p, hist_c, hist_d, pool_w, pool_b, pool_scale, conv_wt, layer, nseq, t, tr, start):
    per = t // tr
    col = lambda c: pl.BlockSpec((tr, BR_W), lambda b, j: (b * per + j, c))
    vec = pl.BlockSpec((1, BR_W), lambda b, j: (0, 0))
    return pl.pallas_call(
        functools.partial(_pool_conv_kernel, start=start),
        grid=(nseq, per),
        in_specs=[col(COL_C // BR_W), col(COL_D // BR_W), col(COL_D // BR_W + 1), col(COL_D // BR_W + 2),
                  pl.BlockSpec((1, 16, BR_W), lambda b, j: (b, 0, 0)),
                  pl.BlockSpec((1, 8, BR_W), lambda b, j: (b, 0, 0)),
                  pl.BlockSpec((1, 4, GW, GW), lambda b, j: (layer, 0, 0, 0)),
                  vec, vec,
                  pl.BlockSpec((CONV_W, BR_W), lambda b, j: (0, 0))],
        out_specs=[col(0), col(0), pl.BlockSpec((1, 8, BR_W), lambda b, j: (b, 0, 0))],
        out_shape=[jax.ShapeDtypeStruct((nseq * t, BR_W), BF16),
                   jax.ShapeDtypeStruct((nseq * t, BR_W), BF16),
                   jax.ShapeDtypeStruct((nseq, 8, BR_W), F32)],
        scratch_shapes=[pltpu.VMEM((16 + tr, BR_W), F32), pltpu.VMEM((8 + tr, BR_W), F32)],
        compiler_params=_params(2),
        name="pool_conv",
    )(p, p, p, p, hist_c, hist_d, pool_w, pool_b, pool_scale, conv_wt)


def _run_trunk(x, mods, caches, weights, nseq, t, tm, sample):
    (ln1_g, ln2_g, w_in, w_gate, b_gate, a_ln_g, a_ln_b, a_ws, a_bs, rel_bias,
     pool_w, pool_b, pool_scale, conv_w, w_branch, w_out, w1, w3, w2, final_g) = weights
    depth = w_in.shape[0]
    m = x.shape[0]
    chunk = min(CHUNK, t)
    tr = min(512, t)
    kv_new = [[] for _ in PATTERNS]
    pool_new, conv_new, chunk_new = [], [], []
    for l in range(depth):
        sh1, sc1, g1, sh2, sc2, g2 = mods[l]
        hn = norm_modulate(x, ln1_g[l][None], sc1, sh1, tm, BF16)
        p = matmul(hn, w_in, l, tm, 1536, F32)
        ws = a_ws[l][None]
        bst = a_bs[l].T[None]
        ya, va = gmlp(p, a_ln_g[l][None], a_ln_b[l][None], ws, bst, min(512, m), chunk)
        if sample:
            c1, c2, c3, cpool, cconv = caches
            yb = attention_sample(p, c1, c2, c3, rel_bias, l, nseq)
            hist_c = jnp.pad(cpool[l], ((0, 0), (1, 0), (0, 0)))
            hist_d = jnp.pad(cconv[l], ((0, 0), (8 - (CONV_W - 1), 0), (0, 0)))
            start = PAST_LEN
        else:
            yb = attention_prompt(p, rel_bias, nseq, t)
            hist_c = jnp.zeros((nseq, 16, BR_W), F32)
            hist_d = jnp.zeros((nseq, 8, BR_W), F32)
            start = 0
        yc, yd, ztail = pool_conv(p, hist_c, hist_d, pool_w, pool_b[l][None], pool_scale[l][None],
                                  conv_w[l].T, l, nseq, t, tr, start)
        merged = gate_merge(hn, (ya, yb, yc, yd), w_gate, b_gate, w_branch, l, tm, 256)
        x = matmul_residual(merged, w_out, l, x, g1, tm, 1024)
        hn2 = norm_modulate(x, ln2_g[l][None], sc2, sh2, tm, BF16)
        hmid = swiglu_up(hn2, w1, w3, l, tm, 512)
        x = matmul_residual(hmid, w2, l, x, g2, tm, 512)

        p3 = p.reshape(nseq, t, N_IN)
        for gi, (win, _) in enumerate(PATTERNS):
            c0 = COL_B + gi * 3 * BR_W + BR_W
            if sample:
                kv = p3[:, :4, c0:c0 + 2 * BR_W].reshape(nseq, 4, 2, N_HEADS, GW)
            else:
                keep = min(win, t)
                kv = p3[:, t - keep:, c0:c0 + 2 * BR_W].reshape(nseq, keep, 2, N_HEADS, GW)
            kv_new[gi].append(kv)
        if sample:
            pool_new.append(p3[:, :4, COL_C:COL_C + BR_W])
            conv_new.append(ztail[:, :4])
            chunk_new.append(va.reshape(nseq, t, BR_W)[:, :4])
        else:
            pool_new.append(p3[:, t - POOL_HIST:, COL_C:COL_C + BR_W])
            conv_new.append(ztail[:, 8 - (CONV_W - 1):])
    zeros = jnp.zeros((1, 1, D_MODEL), F32)
    y = norm_modulate(x, final_g[None], zeros, zeros, tm, F32)
    states = [jnp.stack(s, axis=0) for s in kv_new] + [jnp.stack(pool_new, axis=0), jnp.stack(conv_new, axis=0)]
    if sample:
        states.append(jnp.stack(chunk_new, axis=0))
    return y, states


def kernel(x_prompt, x_sample, c_prompt, c_sample, cache_b1, cache_b2, cache_b3, cache_pool, cache_conv,
           ln1_g, ln2_g, w_ada, b_ada, w_in, w_gate, b_gate, a_ln_g, a_ln_b, a_ws, a_bs,
           rel_bias, pool_w, pool_b, pool_scale, conv_w, w_branch, w_out, w1, w3, w2, final_g):
    nb, seq, d = x_prompt.shape
    ns, dec = x_sample.shape[:2]
    depth = w_in.shape[0]

    c_all = jnp.concatenate([c_prompt, c_sample], axis=0)
    c_rows = -(-c_all.shape[0] // 8) * 8
    c_all = jnp.pad(c_all, ((0, c_rows - c_all.shape[0]), (0, 0)))
    mod = ada_modulation(c_all, w_ada, b_ada).reshape(depth, c_rows, 6, d)
    mods_p = [[mod[l, :nb, i][:, None, :] for i in range(6)] for l in range(depth)]
    mods_s = [[jnp.repeat(mod[l, nb:nb + ns, i], SAMPLE_ROWS, axis=0)[None] for i in range(6)]
              for l in range(depth)]

    weights = (ln1_g, ln2_g, w_in, w_gate, b_gate, a_ln_g, a_ln_b, a_ws, a_bs, rel_bias,
               pool_w, pool_b, pool_scale, conv_w, w_branch, w_out, w1, w3, w2, final_g)

    xp = x_prompt.reshape(nb * seq, d)
    y_p, sp = _run_trunk(xp, mods_p, None, weights, nb, seq, 512, False)

    xs = jnp.pad(x_sample, ((0, 0), (0, SAMPLE_ROWS - dec), (0, 0))).reshape(ns * SAMPLE_ROWS, d)
    caches = (cache_b1.reshape(depth, ns, NK, 2 * BR_W),
              cache_b2.reshape(depth, ns, NK, 8 * BR_W),
              cache_b3.reshape(depth, ns, NK, 16, 2 * BR_W)[:, :, :, :4].reshape(depth, ns, NK, 8 * BR_W),
              cache_pool, cache_conv)
    y_s, ss = _run_trunk(xs, mods_s, caches, weights, ns, SAMPLE_ROWS, ns * SAMPLE_ROWS, True)

    y_prompt = y_p.reshape(nb, seq, d)
    y_sample = y_s.reshape(ns, SAMPLE_ROWS, d)[:, :dec]
    return (y_prompt, y_sample, sp[0], sp[1], sp[2], sp[3], sp[4],
            ss[0], ss[1], ss[2], ss[3], ss[4], ss[5])
```

```python
import functools
import math

import jax
import jax.numpy as jnp
import numpy as np
from jax import lax
from jax.experimental import pallas as pl
from jax.experimental.pallas import tpu as pltpu

F32 = jnp.float32
BF16 = jnp.bfloat16

D_MODEL = 2048
BR_W = 512
GW = 128
N_HEADS = 4
CHUNK = 128
PATTERNS = ((128, 1), (512, 4), (2048, 16))
NK = 128
POOL_WINDOWS = (2, 4, 8, 16)
POOL_HIST = 15
CONV_W = 3
N_BUCKETS = 32
MAX_DIST = 2048
D_FF = 5632
N_IN = 7680
COL_A = 0
COL_B = 2 * BR_W
COL_C = COL_B + 9 * BR_W
COL_D = COL_C + BR_W
EPS = 1e-6
NEG = -1e30
SAMPLE_ROWS = 8
PAST_LEN = 16384
ATTN_BLOCKS_PER_STEP = 4

VMEM_LIMIT = 56 * 1024 * 1024


def _t5_bucket(dist):
    max_exact = N_BUCKETS // 2
    n = np.maximum(dist, 0)
    nf = np.maximum(n, 1).astype(np.float32)
    large = max_exact + (np.log(nf / np.float32(max_exact)) / np.float32(math.log(MAX_DIST / max_exact))
                         * np.float32(N_BUCKETS - max_exact)).astype(np.int32)
    large = np.minimum(large, N_BUCKETS - 1)
    return np.where(n < max_exact, n, large)


def _params(n_axes):
    return pltpu.CompilerParams(dimension_semantics=("arbitrary",) * n_axes,
                                vmem_limit_bytes=VMEM_LIMIT)


def _ada_kernel(c_ref, w_ref, b_ref, o_ref):
    a = jax.nn.silu(c_ref[...]).astype(BF16)
    o_ref[0] = jnp.dot(a, w_ref[0].astype(BF16), preferred_element_type=F32) + b_ref[0]


def ada_modulation(c_all, w_ada, b_ada):
    depth, d, n = w_ada.shape
    r = c_all.shape[0]
    tn = 1024
    return pl.pallas_call(
        _ada_kernel,
        grid=(depth, n // tn),
        in_specs=[pl.BlockSpec((r, d), lambda l, j: (0, 0)),
                  pl.BlockSpec((1, d, tn), lambda l, j: (l, 0, j)),
                  pl.BlockSpec((1, 1, tn), lambda l, j: (l, 0, j))],
        out_specs=pl.BlockSpec((1, r, tn), lambda l, j: (l, 0, j)),
        out_shape=jax.ShapeDtypeStruct((depth, r, n), F32),
        compiler_params=_params(2),
        name="ada_modulation",
    )(c_all, w_ada, b_ada.reshape(depth, 1, n))


def _split_tiles(axis, body):
    i = pl.program_id(axis)
    last = pl.num_programs(axis) - 1
    pl.when(i < last)(functools.partial(body, False))
    pl.when(i == last)(functools.partial(body, True))


def _norm_rows(x, g, sc, sh):
    y = x * lax.rsqrt(jnp.mean(x * x, axis=-1, keepdims=True) + EPS) * g
    return y * (1.0 + sc) + sh


def _mod_specs(pair, width, per, grid_rank):
    prompt, sample = pair
    nb, ms = prompt.shape[0], sample.shape[1]
    if grid_rank == 1:
        return [pl.BlockSpec((1, 1, width), lambda i: (jnp.minimum(i // per, nb - 1), 0, 0)),
                pl.BlockSpec((1, ms, width), lambda i: (0, 0, 0))]
    return [pl.BlockSpec((1, 1, width), lambda j, i: (jnp.minimum(i // per, nb - 1), 0, j)),
            pl.BlockSpec((1, ms, width), lambda j, i: (0, 0, j))]


def _norm_in_kernel(xp_ref, xs_ref, g_ref, scp_ref, scs_ref, shp_ref, shs_ref, o_ref):
    ms = xs_ref.shape[0]

    def body(tail):
        if tail:
            o_ref[0:ms, :] = _norm_rows(xs_ref[...], g_ref[...], scs_ref[0], shs_ref[0]).astype(o_ref.dtype)
        else:
            o_ref[...] = _norm_rows(xp_ref[...], g_ref[...], scp_ref[0], shp_ref[0]).astype(o_ref.dtype)

    _split_tiles(0, body)


def norm_in(xp, xs, g, sc, sh, tm):
    mp, d = xp.shape
    ms = xs.shape[0]
    nf = mp // tm
    per = nf // sc[0].shape[0]
    return pl.pallas_call(
        _norm_in_kernel,
        grid=(nf + 1,),
        in_specs=[pl.BlockSpec((tm, d), lambda i: (jnp.minimum(i, nf - 1), 0)),
                  pl.BlockSpec((ms, d), lambda i: (0, 0)),
                  pl.BlockSpec((1, d), lambda i: (0, 0)),
                  *_mod_specs(sc, d, per, 1), *_mod_specs(sh, d, per, 1)],
        out_specs=pl.BlockSpec((tm, d), lambda i: (i, 0)),
        out_shape=jax.ShapeDtypeStruct((mp + ms, d), BF16),
        compiler_params=_params(1),
        name="norm_in",
    )(xp, xs, g, *sc, *sh)


def _final_norm_kernel(xp_ref, xs_ref, g_ref, yp_ref, ys_ref):
    def body(tail):
        src, dst = (xs_ref, ys_ref) if tail else (xp_ref, yp_ref)
        x = src[...]
        dst[...] = x * lax.rsqrt(jnp.mean(x * x, axis=-1, keepdims=True) + EPS) * g_ref[...]

    _split_tiles(0, body)


def final_norm(xp, xs, g, tm):
    mp, d = xp.shape
    ms = xs.shape[0]
    nf = mp // tm
    p_spec = pl.BlockSpec((tm, d), lambda i: (jnp.minimum(i, nf - 1), 0))
    s_spec = pl.BlockSpec((ms, d), lambda i: (0, 0))
    return pl.pallas_call(
        _final_norm_kernel,
        grid=(nf + 1,),
        in_specs=[p_spec, s_spec, pl.BlockSpec((1, d), lambda i: (0, 0))],
        out_specs=[p_spec, s_spec],
        out_shape=[jax.ShapeDtypeStruct((mp, d), F32), jax.ShapeDtypeStruct((ms, d), F32)],
        compiler_params=_params(1),
        name="final_norm",
    )(xp, xs, g)


def _mm_kernel(a_ref, w_ref, o_ref, wbf_ref, *, ms):
    @pl.when(pl.program_id(1) == 0)
    def _():
        wbf_ref[...] = w_ref[0].astype(BF16)

    def body(tail):
        rows = slice(0, ms) if tail else slice(None)
        o_ref[rows, :] = jnp.dot(a_ref[rows, :], wbf_ref[...], preferred_element_type=F32).astype(o_ref.dtype)

    _split_tiles(1, body)


def matmul(a, w, layer, ms, tm, tn, out_dtype):
    m, k = a.shape
    n = w.shape[2]
    nf = (m - ms) // tm
    return pl.pallas_call(
        functools.partial(_mm_kernel, ms=ms),
        grid=(n // tn, nf + 1),
        in_specs=[pl.BlockSpec((tm, k), lambda j, i: (i, 0)),
                  pl.BlockSpec((1, k, tn), lambda j, i: (layer, 0, j))],
        out_specs=pl.BlockSpec((tm, tn), lambda j, i: (i, j)),
        out_shape=jax.ShapeDtypeStruct((m, n), out_dtype),
        scratch_shapes=[pltpu.VMEM((k, tn), BF16)],
        compiler_params=_params(2),
        name="matmul",
    )(a, w)


def _mm_residual_kernel(a_ref, w_ref, xp_ref, xs_ref, gp_ref, gs_ref, op_ref, os_ref, wbf_ref):
    ms = xs_ref.shape[0]

    @pl.when(pl.program_id(1) == 0)
    def _():
        wbf_ref[...] = w_ref[0].astype(BF16)

    def body(tail):
        if tail:
            y = jnp.dot(a_ref[0:ms, :], wbf_ref[...], preferred_element_type=F32)
            os_ref[...] = xs_ref[...] + gs_ref[0] * y
        else:
            y = jnp.dot(a_ref[...], wbf_ref[...], preferred_element_type=F32)
            op_ref[...] = xp_ref[...] + gp_ref[0] * y

    _split_tiles(1, body)


def matmul_residual(a, w, layer, xp, xs, gate, tm, tn):
    m, k = a.shape
    n = w.shape[2]
    mp, ms = xp.shape[0], xs.shape[0]
    nf = mp // tm
    per = nf // gate[0].shape[0]
    p_spec = pl.BlockSpec((tm, tn), lambda j, i: (jnp.minimum(i, nf - 1), j))
    s_spec = pl.BlockSpec((ms, tn), lambda j, i: (0, j))
    return pl.pallas_call(
        _mm_residual_kernel,
        grid=(n // tn, nf + 1),
        in_specs=[pl.BlockSpec((tm, k), lambda j, i: (i, 0)),
                  pl.BlockSpec((1, k, tn), lambda j, i: (layer, 0, j)),
                  p_spec, s_spec, *_mod_specs(gate, tn, per, 2)],
        out_specs=[p_spec, s_spec],
        out_shape=[jax.ShapeDtypeStruct((mp, n), F32), jax.ShapeDtypeStruct((ms, n), F32)],
        scratch_shapes=[pltpu.VMEM((k, tn), BF16)],
        compiler_params=_params(2),
        name="matmul_residual",
    )(a, w, xp, xs, *gate)


def _out_proj_norm_kernel(a_ref, w_ref, xp_ref, xs_ref, gp_ref, gs_ref, ln_ref, scp_ref, scs_ref,
                          shp_ref, shs_ref, op_ref, os_ref, hn_ref, wbf_ref):
    ms = xs_ref.shape[0]

    @pl.when(pl.program_id(0) == 0)
    def _():
        wbf_ref[...] = w_ref[0].astype(BF16)

    def body(tail):
        if tail:
            x = xs_ref[...] + gs_ref[0] * jnp.dot(a_ref[0:ms, :], wbf_ref[...], preferred_element_type=F32)
            os_ref[...] = x
            hn_ref[0:ms, :] = _norm_rows(x, ln_ref[...], scs_ref[0], shs_ref[0]).astype(hn_ref.dtype)
        else:
            x = xp_ref[...] + gp_ref[0] * jnp.dot(a_ref[...], wbf_ref[...], preferred_element_type=F32)
            op_ref[...] = x
            hn_ref[...] = _norm_rows(x, ln_ref[...], scp_ref[0], shp_ref[0]).astype(hn_ref.dtype)

    _split_tiles(0, body)


def out_proj_norm(a, w, layer, xp, xs, gate, ln_g, sc, sh, tm):
    m, k = a.shape
    n = w.shape[2]
    mp, ms = xp.shape[0], xs.shape[0]
    nf = mp // tm
    per = nf // gate[0].shape[0]
    p_spec = pl.BlockSpec((tm, n), lambda i: (jnp.minimum(i, nf - 1), 0))
    s_spec = pl.BlockSpec((ms, n), lambda i: (0, 0))
    return pl.pallas_call(
        _out_proj_norm_kernel,
        grid=(nf + 1,),
        in_specs=[pl.BlockSpec((tm, k), lambda i: (i, 0)),
                  pl.BlockSpec((1, k, n), lambda i: (layer, 0, 0), pipeline_mode=pl.Buffered(1)),
                  p_spec, s_spec, *_mod_specs(gate, n, per, 1),
                  pl.BlockSpec((1, n), lambda i: (0, 0)),
                  *_mod_specs(sc, n, per, 1), *_mod_specs(sh, n, per, 1)],
        out_specs=[p_spec, s_spec, pl.BlockSpec((tm, n), lambda i: (i, 0))],
        out_shape=[jax.ShapeDtypeStruct((mp, n), F32), jax.ShapeDtypeStruct((ms, n), F32),
                   jax.ShapeDtypeStruct((m, n), BF16)],
        scratch_shapes=[pltpu.VMEM((k, n), BF16)],
        compiler_params=_params(1),
        name="out_proj_norm",
    )(a, w, xp, xs, *gate, ln_g, *sc, *sh)


def _swiglu_kernel(a_ref, w1_ref, w3_ref, o_ref, w1bf_ref, w3bf_ref, *, ms):
    @pl.when(pl.program_id(1) == 0)
    def _():
        w1bf_ref[...] = w1_ref[0].astype(BF16)
        w3bf_ref[...] = w3_ref[0].astype(BF16)

    def body(tail):
        rows = slice(0, ms) if tail else slice(None)
        a = a_ref[rows, :]
        h1 = jnp.dot(a, w1bf_ref[...], preferred_element_type=F32)
        h3 = jnp.dot(a, w3bf_ref[...], preferred_element_type=F32)
        o_ref[rows, :] = (jax.nn.silu(h1) * h3).astype(o_ref.dtype)

    _split_tiles(1, body)


def swiglu_up(a, w1, w3, layer, ms, tm, tn):
    m, k = a.shape
    n = w1.shape[2]
    nf = (m - ms) // tm
    w_spec = pl.BlockSpec((1, k, tn), lambda j, i: (layer, 0, j))
    return pl.pallas_call(
        functools.partial(_swiglu_kernel, ms=ms),
        grid=(n // tn, nf + 1),
        in_specs=[pl.BlockSpec((tm, k), lambda j, i: (i, 0)), w_spec, w_spec],
        out_specs=pl.BlockSpec((tm, tn), lambda j, i: (i, j)),
        out_shape=jax.ShapeDtypeStruct((m, n), BF16),
        scratch_shapes=[pltpu.VMEM((k, tn), BF16), pltpu.VMEM((k, tn), BF16)],
        compiler_params=_params(2),
        name="swiglu_up",
    )(a, w1, w3)


def _gate_merge_kernel(hn_ref, yap_ref, ybp_ref, ycp_ref, ydp_ref, yas_ref, ybs_ref, ycs_ref, yds_ref,
                       wg0_ref, wg1_ref, wg2_ref, wg3_ref, wb_ref, bg_ref, o_ref, wgbf_ref, wbbf_ref):
    ms = yas_ref.shape[0]

    @pl.when(pl.program_id(1) == 0)
    def _():
        for g, wg_ref in enumerate((wg0_ref, wg1_ref, wg2_ref, wg3_ref)):
            wgbf_ref[g] = wg_ref[0].astype(BF16)
            wbbf_ref[g] = wb_ref[0, g].astype(BF16)

    def body(tail):
        rows = slice(0, ms) if tail else slice(None)
        y_refs = (yas_ref, ybs_ref, ycs_ref, yds_ref) if tail else (yap_ref, ybp_ref, ycp_ref, ydp_ref)
        hn = hn_ref[rows, :]
        acc = None
        for g, y_ref in enumerate(y_refs):
            gate = jax.nn.sigmoid(jnp.dot(hn, wgbf_ref[g], preferred_element_type=F32) + bg_ref[0, g:g + 1, :])
            br = jnp.dot(y_ref[...], wbbf_ref[g], preferred_element_type=F32)
            acc = gate * br if acc is None else acc + gate * br
        o_ref[rows, :] = acc.astype(o_ref.dtype)

    _split_tiles(1, body)


def gate_merge(hn, ys_p, ys_s, w_gate, b_gate, w_branch, layer, tm, tn):
    m, k = hn.shape
    ms = ys_s[0].shape[0]
    nf = (m - ms) // tm
    depth = w_gate.shape[0]
    d = w_branch.shape[3]
    nj = d // tn
    wg_specs = [pl.BlockSpec((1, k, tn), functools.partial(lambda j, i, g: (layer, 0, g * nj + j), g=g))
                for g in range(4)]
    yp_spec = pl.BlockSpec((tm, BR_W), lambda j, i: (jnp.minimum(i, nf - 1), 0))
    ys_spec = pl.BlockSpec((ms, BR_W), lambda j, i: (0, 0))
    return pl.pallas_call(
        _gate_merge_kernel,
        grid=(nj, nf + 1),
        in_specs=[pl.BlockSpec((tm, k), lambda j, i: (i, 0)), *[yp_spec] * 4, *[ys_spec] * 4,
                  *wg_specs,
                  pl.BlockSpec((1, 4, BR_W, tn), lambda j, i: (layer, 0, 0, j)),
                  pl.BlockSpec((1, 4, tn), lambda j, i: (layer, 0, j))],
        out_specs=pl.BlockSpec((tm, tn), lambda j, i: (i, j)),
        out_shape=jax.ShapeDtypeStruct((m, d), BF16),
        scratch_shapes=[pltpu.VMEM((4, k, tn), BF16), pltpu.VMEM((4, BR_W, tn), BF16)],
        compiler_params=_params(2),
        name="gate_merge",
    )(hn, *ys_p, *ys_s, w_gate, w_gate, w_gate, w_gate, w_branch, b_gate.reshape(depth, 4, d))


def _gmlp_kernel(pu_ref, pv_ref, lg_ref, lb_ref, ws_ref, bst_ref, ya_ref, *maybe_va_ref, chunk):
    rows = pu_ref.shape[0]
    u = jax.nn.gelu(pu_ref[...])
    v = jax.nn.gelu(pv_ref[...])
    mu = jnp.mean(v, axis=-1, keepdims=True)
    var = jnp.mean(jnp.square(v - mu), axis=-1, keepdims=True)
    vn = (v - mu) * lax.rsqrt(var + 1e-5) * lg_ref[...] + lb_ref[...]
    for va_ref in maybe_va_ref:
        va_ref[...] = vn
    causal = (lax.broadcasted_iota(jnp.int32, (CHUNK, CHUNK), 0)
              >= lax.broadcasted_iota(jnp.int32, (CHUNK, CHUNK), 1))
    for g in range(4):
        wm = jnp.where(causal, ws_ref[0, g], 0.0).astype(BF16)
        bias = bst_ref[0, :, g:g + 1]
        for c in range(rows // chunk):
            r0 = c * chunk
            vc = vn[r0:r0 + chunk, g * GW:(g + 1) * GW]
            if chunk < CHUNK:
                vc = jnp.concatenate([vc, jnp.zeros((CHUNK - chunk, GW), F32)], axis=0)
            mix = (jnp.dot(wm, vc.astype(BF16), preferred_element_type=F32) + bias)[:chunk]
            ya_ref[r0:r0 + chunk, g * GW:(g + 1) * GW] = (
                u[r0:r0 + chunk, g * GW:(g + 1) * GW] * mix).astype(ya_ref.dtype)


def gmlp(p, ln_g, ln_b, ws, bst, row0, m, tr, chunk, emit_v):
    blk0 = row0 // tr
    in_spec = lambda col: pl.BlockSpec((tr, BR_W), lambda i: (blk0 + i, col))
    out_spec = pl.BlockSpec((tr, BR_W), lambda i: (i, 0))
    n_out = 2 if emit_v else 1
    return pl.pallas_call(
        functools.partial(_gmlp_kernel, chunk=chunk),
        grid=(m // tr,),
        in_specs=[in_spec(0), in_spec(1),
                  pl.BlockSpec((1, BR_W), lambda i: (0, 0)),
                  pl.BlockSpec((1, BR_W), lambda i: (0, 0)),
                  pl.BlockSpec((1, 4, CHUNK, CHUNK), lambda i: (0, 0, 0, 0)),
                  pl.BlockSpec((1, CHUNK, 4), lambda i: (0, 0, 0))],
        out_specs=[out_spec] * n_out,
        out_shape=[jax.ShapeDtypeStruct((m, BR_W), BF16), jax.ShapeDtypeStruct((m, BR_W), F32)][:n_out],
        compiler_params=_params(1),
        name="gmlp",
    )(p, p, ln_g, ln_b, ws, bst)


def _bias_from_buckets(idx, rel_ref, col, shape):
    bias = jnp.full(shape, NEG, F32)
    for b in range(N_BUCKETS):
        bias = jnp.where(idx == b, rel_ref[b, col], bias)
    return bias


def _attn_prompt_kernel(rel_ref, q_ref, k_ref, v_ref, idx_ref, o_ref,
                        o0_ref, o1_ref, o2_ref, e0_ref, e1_ref, e2_ref, bm_ref):
    t = q_ref.shape[0]
    head = pl.program_id(1)
    group = pl.program_id(2)
    scale = GW ** -0.5
    outs = (o0_ref, o1_ref, o2_ref)
    lses = (e0_ref, e1_ref, e2_ref)

    bm_ref[...] = _bias_from_buckets(idx_ref[0], rel_ref, group * N_HEADS + head, (NK, 2 * NK))

    def rows(start, n, dil):
        return pl.ds(start, n) if dil == 1 else pl.ds(start, n, stride=dil)

    def blocks(gi, specs):
        dil = PATTERNS[gi][1]
        staged = []
        for first_block, q0, k0 in specs:
            nkeys = NK if first_block else 2 * NK
            qi = rows(q0, NK, dil)
            ki = rows(k0, nkeys, dil)
            q = q_ref[qi, :].astype(BF16)
            kk = k_ref[ki, :].astype(BF16)
            bm = bm_ref[:, NK:] if first_block else bm_ref[...]
            s = lax.dot_general(q, kk, (((1,), (1,)), ((), ())), preferred_element_type=F32) * scale + bm
            staged.append((qi, ki, s))
        probs = []
        for qi, ki, s in staged:
            m = jnp.max(s, axis=-1, keepdims=True)
            p = jnp.exp(s - m)
            l = jnp.sum(p, axis=-1, keepdims=True)
            probs.append((qi, ki, p.astype(BF16), m, l))
        for qi, ki, p, m, l in probs:
            acc = jnp.dot(p, v_ref[ki, :].astype(BF16), preferred_element_type=F32)
            outs[gi][qi, :] = acc * (1.0 / l)
            lses[gi][qi, :] = jnp.broadcast_to(m + jnp.log(l), (NK, GW))

    def run_group(gi):
        dil = PATTERNS[gi][1]
        span = NK * dil
        nb = t // span
        u = ATTN_BLOCKS_PER_STEP
        if dil == 1:
            blocks(gi, [(True, 0, 0)] + [(False, n * span, (n - 1) * span) for n in range(1, u)])

            def step(i, c):
                q0 = pl.multiple_of(u * i * span, span)
                blocks(gi, [(False, q0 + n * span, q0 + (n - 1) * span) for n in range(u)])
                return c

            lax.fori_loop(1, nb // u, step, 0)
        else:
            def bunch(i, c):
                res = [u * i + r for r in range(u)]
                blocks(gi, [(True, r, r) for r in res])

                def later(n, cc):
                    blocks(gi, [(False, n * span + r, (n - 1) * span + r) for r in res])
                    return cc

                return lax.fori_loop(1, nb, later, c)

            if dil == u:
                bunch(0, 0)
            else:
                lax.fori_loop(0, dil // u, bunch, 0)

    for gi in range(len(PATTERNS)):
        pl.when(group == gi)(functools.partial(run_group, gi))

    @pl.when(group == len(PATTERNS) - 1)
    def _():
        step = 256

        def merge(c, carry):
            sl = pl.ds(pl.multiple_of(c * step, step), step)
            e = [r[sl, :] for r in lses]
            top = jnp.maximum(jnp.maximum(e[0], e[1]), e[2])
            w = [jnp.exp(x - top) for x in e]
            num = w[0] * o0_ref[sl, :] + w[1] * o1_ref[sl, :] + w[2] * o2_ref[sl, :]
            o_ref[sl, :] = (num / (w[0] + w[1] + w[2])).astype(o_ref.dtype)
            return carry

        lax.fori_loop(0, t // step, merge, 0)


def _prompt_buckets():
    i = np.arange(NK)[:, None]
    j = np.arange(2 * NK)[None, :]
    diff = NK + i - j
    valid = (diff >= 0) & (diff <= NK)
    return np.stack([np.where(valid, _t5_bucket(diff * dil), -1) for _, dil in PATTERNS]).astype(np.int32)


def attention_prompt(p, rel_bias, nseq, t):
    def qkv_spec(which):
        return pl.BlockSpec((t, GW), lambda b, h, g: (b, COL_B // GW + g * 12 + which * 4 + h))

    return pl.pallas_call(
        _attn_prompt_kernel,
        grid=(nseq, N_HEADS, len(PATTERNS)),
        in_specs=[pl.BlockSpec(memory_space=pltpu.SMEM),
                  qkv_spec(0), qkv_spec(1), qkv_spec(2),
                  pl.BlockSpec((1, NK, 2 * NK), lambda b, h, g: (g, 0, 0))],
        out_specs=pl.BlockSpec((t, GW), lambda b, h, g: (b, h)),
        out_shape=jax.ShapeDtypeStruct((nseq * t, BR_W), BF16),
        scratch_shapes=[pltpu.VMEM((t, GW), F32)] * 6 + [pltpu.VMEM((NK, 2 * NK), F32)],
        compiler_params=_params(3),
        name="attention_prompt",
    )(rel_bias, p, p, p, jnp.asarray(_prompt_buckets()))


def _attn_sample_kernel(rel_ref, p_ref, c1_ref, c2_ref, c3_ref, ic1_ref, in1_ref, ic_ref, o_ref):
    scale = GW ** -0.5
    rowid = lax.broadcasted_iota(jnp.int32, (SAMPLE_ROWS, NK), 0)
    nt = (((1,), (1,)), ((), ()))
    n_real = 4
    for h in range(N_HEADS):
        outs, lses = [], []
        for gi, (win, dil) in enumerate(PATTERNS):
            base = COL_B + gi * 3 * BR_W + h * GW
            rel_col = gi * N_HEADS + h
            q = p_ref[:, base:base + GW].astype(BF16)
            kn = p_ref[:, base + BR_W:base + BR_W + GW].astype(BF16)
            vn = p_ref[:, base + 2 * BR_W:base + 2 * BR_W + GW].astype(BF16)
            if dil == 1:
                kc = c1_ref[0, 0, :, h * GW:(h + 1) * GW].astype(BF16)
                vc = c1_ref[0, 0, :, BR_W + h * GW:BR_W + (h + 1) * GW].astype(BF16)
                fill = jnp.zeros((NK - SAMPLE_ROWS, GW), F32)
                kn = jnp.concatenate([p_ref[:, base + BR_W:base + BR_W + GW], fill], axis=0).astype(BF16)
                vn = jnp.concatenate([p_ref[:, base + 2 * BR_W:base + 2 * BR_W + GW], fill], axis=0).astype(BF16)
                bias_c = _bias_from_buckets(ic1_ref[...], rel_ref, rel_col, (SAMPLE_ROWS, NK))
                bias_n = _bias_from_buckets(in1_ref[...], rel_ref, rel_col, (SAMPLE_ROWS, NK))
                s_c = lax.dot_general(q, kc, nt, preferred_element_type=F32) * scale + bias_c
                s_n = lax.dot_general(q, kn, nt, preferred_element_type=F32) * scale + bias_n
                m = jnp.maximum(jnp.max(s_c, axis=-1, keepdims=True), jnp.max(s_n, axis=-1, keepdims=True))
                p_c = jnp.exp(s_c - m)
                p_n = jnp.exp(s_n - m)
                l = jnp.sum(p_c, axis=-1, keepdims=True) + jnp.sum(p_n, axis=-1, keepdims=True)
                acc = (jnp.dot(p_c.astype(BF16), vc, preferred_element_type=F32)
                       + jnp.dot(p_n.astype(BF16), vn, preferred_element_type=F32))
            else:
                c_ref = c2_ref if gi == 1 else c3_ref
                s_c = jnp.zeros((SAMPLE_ROWS, NK), F32)
                for tq in range(n_real):
                    col = tq * 2 * BR_W + h * GW
                    kc = c_ref[0, 0, :, col:col + GW].astype(BF16)
                    s_t = lax.dot_general(q, kc, nt, preferred_element_type=F32)
                    s_c = jnp.where(rowid == tq, s_t, s_c)
                s_c = s_c * scale + _bias_from_buckets(ic_ref[gi - 1], rel_ref, rel_col, (SAMPLE_ROWS, NK))
                qk = jnp.sum(q.astype(F32) * kn.astype(F32), axis=-1, keepdims=True)
                s_n = qk * scale + rel_ref[0, rel_col]
                m = jnp.maximum(jnp.max(s_c, axis=-1, keepdims=True), s_n)
                p_c = jnp.exp(s_c - m)
                p_n = jnp.exp(s_n - m)
                l = jnp.sum(p_c, axis=-1, keepdims=True) + p_n
                pcb = p_c.astype(BF16)
                acc = p_n.astype(BF16).astype(F32) * vn.astype(F32)
                for tq in range(n_real):
                    col = tq * 2 * BR_W + BR_W + h * GW
                    vc = c_ref[0, 0, :, col:col + GW].astype(BF16)
                    o_t = jnp.dot(pcb, vc, preferred_element_type=F32)
                    acc = acc + jnp.where(rowid == tq, o_t, 0.0)
            outs.append(acc / l)
            lses.append(m + jnp.log(l))
        top = jnp.maximum(jnp.maximum(lses[0], lses[1]), lses[2])
        ws = [jnp.exp(x - top) for x in lses]
        den = ws[0] + ws[1] + ws[2]
        y = (ws[0] * outs[0] + ws[1] * outs[1] + ws[2] * outs[2]) / den
        o_ref[:, h * GW:(h + 1) * GW] = y.astype(o_ref.dtype)


def _sample_buckets():
    tq = np.arange(SAMPLE_ROWS)[:, None]
    steps = NK + tq - np.arange(NK)[None, :]
    ic1 = np.where(steps <= NK, _t5_bucket(steps), -1)
    jn = np.arange(NK)[None, :]
    in1 = np.where((tq - jn >= 0) & (jn < 4), _t5_bucket(tq - jn), -1)
    ic = np.stack([np.broadcast_to(_t5_bucket((NK - np.arange(NK)) * dil)[None, :], (SAMPLE_ROWS, NK))
                   for _, dil in PATTERNS[1:]])
    return ic1.astype(np.int32), in1.astype(np.int32), ic.astype(np.int32)


def attention_sample(p, c1, c2, c3, rel_bias, layer, row0, nseq):
    ic1, in1, ic = (jnp.asarray(a) for a in _sample_buckets())
    full = lambda a: pl.BlockSpec(a.shape, lambda b: (0,) * a.ndim)
    cache_spec = lambda width: pl.BlockSpec((1, 1, NK, width), lambda b: (layer, b, 0, 0))
    blk0 = row0 // SAMPLE_ROWS
    return pl.pallas_call(
        _attn_sample_kernel,
        grid=(nseq,),
        in_specs=[pl.BlockSpec(memory_space=pltpu.SMEM),
                  pl.BlockSpec((SAMPLE_ROWS, N_IN), lambda b: (blk0 + b, 0)),
                  cache_spec(2 * BR_W), cache_spec(8 * BR_W), cache_spec(8 * BR_W),
                  full(ic1), full(in1), full(ic)],
        out_specs=pl.BlockSpec((SAMPLE_ROWS, BR_W), lambda b: (b, 0)),
        out_shape=jax.ShapeDtypeStruct((nseq * SAMPLE_ROWS, BR_W), BF16),
        compiler_params=_params(1),
        name="attention_sample",
    )(rel_bias, p, c1, c2, c3, ic1, in1, ic)


def _pool_conv_kernel(pc_ref, pbg_ref, pcg_ref, phs_ref, hc_ref, hd_ref, pw_ref, pb_ref, ps_ref, cw_ref,
                      yc_ref, yd_ref, zt_ref, cbuf_ref, zbuf_ref, *, start):
    tr = pc_ref.shape[0]
    j = pl.program_id(1)
    hc = 16
    hz = 8

    @pl.when(j == 0)
    def _():
        cbuf_ref[0:hc, :] = hc_ref[0]
        zbuf_ref[0:hz, :] = hd_ref[0]

    x = pc_ref[...]
    cbuf_ref[hc:hc + tr, :] = x
    pos = start + j * tr + lax.broadcasted_iota(jnp.int32, (tr, 1), 0)
    for gi, win in enumerate(POOL_WINDOWS):
        cols = slice(gi * GW, (gi + 1) * GW)
        total = x[:, cols]
        for back in range(1, win):
            total = total + cbuf_ref[hc - back:hc - back + tr, cols]
        cnt = jnp.minimum(pos + 1, win).astype(F32)
        pooled = total / cnt - x[:, cols]
        y = jnp.dot(pooled.astype(BF16), pw_ref[0, gi].astype(BF16), preferred_element_type=F32)
        yc_ref[:, cols] = ((y + pb_ref[:, cols]) * ps_ref[:, cols]).astype(yc_ref.dtype)
    cbuf_ref[0:hc, :] = cbuf_ref[tr:tr + hc, :]

    z = pcg_ref[...] * phs_ref[...]
    zbuf_ref[hz:hz + tr, :] = z
    y = (zbuf_ref[hz - 2:hz - 2 + tr, :] * cw_ref[0:1, :] + zbuf_ref[hz - 1:hz - 1 + tr, :] * cw_ref[1:2, :]
         + z * cw_ref[2:3, :])
    yd_ref[...] = (pbg_ref[...] * y).astype(yd_ref.dtype)
    zt_ref[0] = z[tr - hz:, :]
    zbuf_ref[0:hz, :] = zbuf_ref[tr:tr + hz, :]


def pool_conv(p, hist_c, hist_d, pool_w, pool_b, pool_scale, conv_wt, layer, row0, nseq, t, tr, start):
    per = t // tr
    blk0 = row0 // tr
    col = lambda c: pl.BlockSpec((tr, BR_W), lambda b, j: (b * per + j, c))
    pcol = lambda c: pl.BlockSpec((tr, BR_W), lambda b, j: (blk0 + b * per + j, c))
    vec = pl.BlockSpec((1, BR_W), lambda b, j: (0, 0))
    return pl.pallas_call(
        functools.partial(_pool_conv_kernel, start=start),
        grid=(nseq, per),
        in_specs=[pcol(COL_C // BR_W), pcol(COL_D // BR_W), pcol(COL_D // BR_W + 1), pcol(COL_D // BR_W + 2),
                  pl.BlockSpec((1, 16, BR_W), lambda b, j: (b, 0, 0)),
                  pl.BlockSpec((1, 8, BR_W), lambda b, j: (b, 0, 0)),
                  pl.BlockSpec((1, 4, GW, GW), lambda b, j: (layer, 0, 0, 0)),
                  vec, vec,
                  pl.BlockSpec((CONV_W, BR_W), lambda b, j: (0, 0))],
        out_specs=[col(0), col(0), pl.BlockSpec((1, 8, BR_W), lambda b, j: (b, 0, 0))],
        out_shape=[jax.ShapeDtypeStruct((nseq * t, BR_W), BF16),
                   jax.ShapeDtypeStruct((nseq * t, BR_W), BF16),
                   jax.ShapeDtypeStruct((nseq, 8, BR_W), F32)],
        scratch_shapes=[pltpu.VMEM((16 + tr, BR_W), F32), pltpu.VMEM((8 + tr, BR_W), F32)],
        compiler_params=_params(2),
        name="pool_conv",
    )(p, p, p, p, hist_c, hist_d, pool_w, pool_b, pool_scale, conv_wt)


TM_WIDE = 1024
TM_ROWS = 512


def kernel(x_prompt, x_sample, c_prompt, c_sample, cache_b1, cache_b2, cache_b3, cache_pool, cache_conv,
           ln1_g, ln2_g, w_ada, b_ada, w_in, w_gate, b_gate, a_ln_g, a_ln_b, a_ws, a_bs,
           rel_bias, pool_w, pool_b, pool_scale, conv_w, w_branch, w_out, w1, w3, w2, final_g):
    nb, seq, d = x_prompt.shape
    ns, dec = x_sample.shape[:2]
    depth = w_in.shape[0]
    mp = nb * seq
    ms = ns * SAMPLE_ROWS

    c_all = jnp.concatenate([c_prompt, c_sample], axis=0)
    c_rows = -(-c_all.shape[0] // 8) * 8
    c_all = jnp.pad(c_all, ((0, c_rows - c_all.shape[0]), (0, 0)))
    mod = ada_modulation(c_all, w_ada, b_ada).reshape(depth, c_rows, 6, d)
    mods = [[(mod[l, :nb, i][:, None, :], jnp.repeat(mod[l, nb:nb + ns, i], SAMPLE_ROWS, axis=0)[None])
             for i in range(6)] for l in range(depth)]

    xp = x_prompt.reshape(mp, d)
    xs = jnp.pad(x_sample, ((0, 0), (0, SAMPLE_ROWS - dec), (0, 0))).reshape(ms, d)
    c1 = cache_b1.reshape(depth, ns, NK, 2 * BR_W)
    c2 = cache_b2.reshape(depth, ns, NK, 8 * BR_W)
    c3 = cache_b3.reshape(depth, ns, NK, 16, 2, N_HEADS, GW)[:, :, :, :4].reshape(depth, ns, NK, 8 * BR_W)
    zero_hist_c = jnp.zeros((nb, 16, BR_W), F32)
    zero_hist_d = jnp.zeros((nb, 8, BR_W), F32)

    kv_p = [[] for _ in PATTERNS]
    kv_s = [[] for _ in PATTERNS]
    pool_p, conv_p, pool_s, conv_s, chunk_s = [], [], [], [], []
    for l in range(depth):
        sh1, sc1, g1, sh2, sc2, g2 = mods[l]
        hn = norm_in(xp, xs, ln1_g[l][None], sc1, sh1, TM_ROWS)
        p = matmul(hn, w_in, l, ms, TM_WIDE, 1280, F32)

        gm = (a_ln_g[l][None], a_ln_b[l][None], a_ws[l][None], a_bs[l].T[None])
        ya_p, = gmlp(p, *gm, 0, mp, 512, CHUNK, False)
        ya_s, va_s = gmlp(p, *gm, mp, ms, ms, SAMPLE_ROWS, True)
        yb_p = attention_prompt(p, rel_bias, nb, seq)
        yb_s = attention_sample(p, c1, c2, c3, rel_bias, l, mp, ns)
        pc = (pool_w, pool_b[l][None], pool_scale[l][None], conv_w[l].T, l)
        yc_p, yd_p, zt_p = pool_conv(p, zero_hist_c, zero_hist_d, *pc, 0, nb, seq, 512, 0)
        hist_c = jnp.pad(cache_pool[l], ((0, 0), (1, 0), (0, 0)))
        hist_d = jnp.pad(cache_conv[l], ((0, 0), (8 - (CONV_W - 1), 0), (0, 0)))
        yc_s, yd_s, zt_s = pool_conv(p, hist_c, hist_d, *pc, mp, ns, SAMPLE_ROWS, SAMPLE_ROWS, PAST_LEN)

        merged = gate_merge(hn, (ya_p, yb_p, yc_p, yd_p), (ya_s, yb_s, yc_s, yd_s),
                            w_gate, b_gate, w_branch, l, TM_WIDE, 256)
        xp, xs, hn2 = out_proj_norm(merged, w_out, l, xp, xs, g1, ln2_g[l][None], sc2, sh2, TM_ROWS)
        hmid = swiglu_up(hn2, w1, w3, l, ms, TM_WIDE, 512)
        xp, xs = matmul_residual(hmid, w2, l, xp, xs, g2, TM_ROWS, 512)

        ps3 = p[mp:].reshape(ns, SAMPLE_ROWS, N_IN)
        for gi, (win, _) in enumerate(PATTERNS):
            c0 = COL_B + gi * 3 * BR_W + BR_W
            keep = min(win, seq)
            kv_p[gi].append(jnp.stack([p[(b + 1) * seq - keep:(b + 1) * seq, c0:c0 + 2 * BR_W]
                                       for b in range(nb)]).reshape(nb, keep, 2, N_HEADS, GW))
            kv_s[gi].append(ps3[:, :dec, c0:c0 + 2 * BR_W].reshape(ns, dec, 2, N_HEADS, GW))
        pool_p.append(jnp.stack([p[(b + 1) * seq - POOL_HIST:(b + 1) * seq, COL_C:COL_C + BR_W]
                                 for b in range(nb)]))
        conv_p.append(zt_p[:, 8 - (CONV_W - 1):])
        pool_s.append(ps3[:, :dec, COL_C:COL_C + BR_W])
        conv_s.append(zt_s[:, :dec])
        chunk_s.append(va_s.reshape(ns, SAMPLE_ROWS, BR_W)[:, :dec])

    yp, ys = final_norm(xp, xs, final_g[None], TM_ROWS)
    stack = lambda parts: jnp.stack(parts, axis=0)
    return (yp.reshape(nb, seq, d), ys.reshape(ns, SAMPLE_ROWS, d)[:, :dec],
            stack(kv_p[0]), stack(kv_p[1]), stack(kv_p[2]), stack(pool_p), stack(conv_p),
            stack(kv_s[0]), stack(kv_s[1]), stack(kv_s[2]), stack(pool_s), stack(conv_s), stack(chunk_s))
```

```python
import functools
import math

import jax
import jax.numpy as jnp
import numpy as np
from jax import lax
from jax.experimental import pallas as pl
from jax.experimental.pallas import tpu as pltpu

F32 = jnp.float32
BF16 = jnp.bfloat16

D_MODEL = 2048
BR_W = 512
GW = 128
N_HEADS = 4
CHUNK = 128
PATTERNS = ((128, 1), (512, 4), (2048, 16))
NK = 128
POOL_WINDOWS = (2, 4, 8, 16)
POOL_HIST = 15
CONV_W = 3
N_BUCKETS = 32
MAX_DIST = 2048
D_FF = 5632
N_IN = 7680
COL_A = 0
COL_B = 2 * BR_W
COL_C = COL_B + 9 * BR_W
COL_D = COL_C + BR_W
EPS = 1e-6
NEG = -1e30
SAMPLE_ROWS = 8
PAST_LEN = 16384
ATTN_BLOCKS_PER_STEP = 4

VMEM_LIMIT = 56 * 1024 * 1024


def _t5_bucket(dist):
    max_exact = N_BUCKETS // 2
    n = np.maximum(dist, 0)
    nf = np.maximum(n, 1).astype(np.float32)
    large = max_exact + (np.log(nf / np.float32(max_exact)) / np.float32(math.log(MAX_DIST / max_exact))
                         * np.float32(N_BUCKETS - max_exact)).astype(np.int32)
    large = np.minimum(large, N_BUCKETS - 1)
    return np.where(n < max_exact, n, large)


def _params(n_axes):
    return pltpu.CompilerParams(dimension_semantics=("arbitrary",) * n_axes,
                                vmem_limit_bytes=VMEM_LIMIT)


def _ada_kernel(c_ref, w_ref, b_ref, o_ref):
    a = jax.nn.silu(c_ref[...]).astype(BF16)
    o_ref[0] = jnp.dot(a, w_ref[0].astype(BF16), preferred_element_type=F32) + b_ref[0]


def ada_modulation(c_all, w_ada, b_ada):
    depth, d, n = w_ada.shape
    r = c_all.shape[0]
    tn = 1024
    return pl.pallas_call(
        _ada_kernel,
        grid=(depth, n // tn),
        in_specs=[pl.BlockSpec((r, d), lambda l, j: (0, 0)),
                  pl.BlockSpec((1, d, tn), lambda l, j: (l, 0, j)),
                  pl.BlockSpec((1, 1, tn), lambda l, j: (l, 0, j))],
        out_specs=pl.BlockSpec((1, r, tn), lambda l, j: (l, 0, j)),
        out_shape=jax.ShapeDtypeStruct((depth, r, n), F32),
        compiler_params=_params(2),
        name="ada_modulation",
    )(c_all, w_ada, b_ada.reshape(depth, 1, n))


def _split_tiles(axis, body):
    i = pl.program_id(axis)
    pl.when(i == 0)(functools.partial(body, True))
    pl.when(i > 0)(functools.partial(body, False))


def _joint_tile(i, nf):
    return (i + nf) % (nf + 1)


def _prompt_tile(i):
    return jnp.maximum(i - 1, 0)


def _norm_rows(x, g, sc, sh):
    y = x * lax.rsqrt(jnp.mean(x * x, axis=-1, keepdims=True) + EPS) * g
    return y * (1.0 + sc) + sh


def _mod_specs(pair, width, per, grid_rank):
    prompt, sample = pair
    nb, ms = prompt.shape[0], sample.shape[1]
    if grid_rank == 1:
        return [pl.BlockSpec((1, 1, width), lambda i: (_prompt_tile(i) // per, 0, 0)),
                pl.BlockSpec((1, ms, width), lambda i: (0, 0, 0))]
    return [pl.BlockSpec((1, 1, width), lambda j, i: (_prompt_tile(i) // per, 0, j)),
            pl.BlockSpec((1, ms, width), lambda j, i: (0, 0, j))]


def _norm_in_kernel(xp_ref, xs_ref, g_ref, scp_ref, scs_ref, shp_ref, shs_ref, o_ref):
    ms = xs_ref.shape[0]

    def body(tail):
        if tail:
            o_ref[0:ms, :] = _norm_rows(xs_ref[...], g_ref[...], scs_ref[0], shs_ref[0]).astype(o_ref.dtype)
        else:
            o_ref[...] = _norm_rows(xp_ref[...], g_ref[...], scp_ref[0], shp_ref[0]).astype(o_ref.dtype)

    _split_tiles(0, body)


def norm_in(xp, xs, g, sc, sh, tm):
    mp, d = xp.shape
    ms = xs.shape[0]
    nf = mp // tm
    per = nf // sc[0].shape[0]
    return pl.pallas_call(
        _norm_in_kernel,
        grid=(nf + 1,),
        in_specs=[pl.BlockSpec((tm, d), lambda i: (_prompt_tile(i), 0)),
                  pl.BlockSpec((ms, d), lambda i: (0, 0)),
                  pl.BlockSpec((1, d), lambda i: (0, 0)),
                  *_mod_specs(sc, d, per, 1), *_mod_specs(sh, d, per, 1)],
        out_specs=pl.BlockSpec((tm, d), lambda i: (_joint_tile(i, nf), 0)),
        out_shape=jax.ShapeDtypeStruct((mp + ms, d), BF16),
        compiler_params=_params(1),
        name="norm_in",
    )(xp, xs, g, *sc, *sh)


def _final_norm_kernel(xp_ref, xs_ref, g_ref, yp_ref, ys_ref):
    def body(tail):
        src, dst = (xs_ref, ys_ref) if tail else (xp_ref, yp_ref)
        x = src[...]
        dst[...] = x * lax.rsqrt(jnp.mean(x * x, axis=-1, keepdims=True) + EPS) * g_ref[...]

    _split_tiles(0, body)


def final_norm(xp, xs, g, tm):
    mp, d = xp.shape
    ms = xs.shape[0]
    nf = mp // tm
    p_spec = pl.BlockSpec((tm, d), lambda i: (_prompt_tile(i), 0))
    s_spec = pl.BlockSpec((ms, d), lambda i: (0, 0))
    return pl.pallas_call(
        _final_norm_kernel,
        grid=(nf + 1,),
        in_specs=[p_spec, s_spec, pl.BlockSpec((1, d), lambda i: (0, 0))],
        out_specs=[p_spec, s_spec],
        out_shape=[jax.ShapeDtypeStruct((mp, d), F32), jax.ShapeDtypeStruct((ms, d), F32)],
        compiler_params=_params(1),
        name="final_norm",
    )(xp, xs, g)


def _mm_kernel(a_ref, w_ref, o_ref, wbf_ref, *, ms):
    @pl.when(pl.program_id(1) == 0)
    def _():
        wbf_ref[...] = w_ref[0].astype(BF16)

    def body(tail):
        rows = slice(0, ms) if tail else slice(None)
        o_ref[rows, :] = jnp.dot(a_ref[rows, :], wbf_ref[...], preferred_element_type=F32).astype(o_ref.dtype)

    _split_tiles(1, body)


def matmul(a, w, layer, ms, tm, tn, out_dtype):
    m, k = a.shape
    n = w.shape[2]
    nf = (m - ms) // tm
    return pl.pallas_call(
        functools.partial(_mm_kernel, ms=ms),
        grid=(n // tn, nf + 1),
        in_specs=[pl.BlockSpec((tm, k), lambda j, i: (_joint_tile(i, nf), 0)),
                  pl.BlockSpec((1, k, tn), lambda j, i: (layer, 0, j))],
        out_specs=pl.BlockSpec((tm, tn), lambda j, i: (_joint_tile(i, nf), j)),
        out_shape=jax.ShapeDtypeStruct((m, n), out_dtype),
        scratch_shapes=[pltpu.VMEM((k, tn), BF16)],
        compiler_params=_params(2),
        name="matmul",
    )(a, w)


def _mm_residual_kernel(a_ref, w_ref, xp_ref, xs_ref, gp_ref, gs_ref, op_ref, os_ref, wbf_ref):
    ms = xs_ref.shape[0]

    @pl.when(pl.program_id(1) == 0)
    def _():
        wbf_ref[...] = w_ref[0].astype(BF16)

    def body(tail):
        if tail:
            y = jnp.dot(a_ref[0:ms, :], wbf_ref[...], preferred_element_type=F32)
            os_ref[...] = xs_ref[...] + gs_ref[0] * y
        else:
            y = jnp.dot(a_ref[...], wbf_ref[...], preferred_element_type=F32)
            op_ref[...] = xp_ref[...] + gp_ref[0] * y

    _split_tiles(1, body)


def matmul_residual(a, w, layer, xp, xs, gate, tm, tn):
    m, k = a.shape
    n = w.shape[2]
    mp, ms = xp.shape[0], xs.shape[0]
    nf = mp // tm
    per = nf // gate[0].shape[0]
    p_spec = pl.BlockSpec((tm, tn), lambda j, i: (_prompt_tile(i), j))
    s_spec = pl.BlockSpec((ms, tn), lambda j, i: (0, j))
    return pl.pallas_call(
        _mm_residual_kernel,
        grid=(n // tn, nf + 1),
        in_specs=[pl.BlockSpec((tm, k), lambda j, i: (_joint_tile(i, nf), 0)),
                  pl.BlockSpec((1, k, tn), lambda j, i: (layer, 0, j)),
                  p_spec, s_spec, *_mod_specs(gate, tn, per, 2)],
        out_specs=[p_spec, s_spec],
        out_shape=[jax.ShapeDtypeStruct((mp, n), F32), jax.ShapeDtypeStruct((ms, n), F32)],
        scratch_shapes=[pltpu.VMEM((k, tn), BF16)],
        compiler_params=_params(2),
        name="matmul_residual",
    )(a, w, xp, xs, *gate)


def _out_proj_norm_kernel(a_ref, w_ref, xp_ref, xs_ref, gp_ref, gs_ref, ln_ref, scp_ref, scs_ref,
                          shp_ref, shs_ref, op_ref, os_ref, hn_ref, wbf_ref):
    ms = xs_ref.shape[0]

    @pl.when(pl.program_id(0) == 0)
    def _():
        wbf_ref[...] = w_ref[0].astype(BF16)

    def body(tail):
        if tail:
            x = xs_ref[...] + gs_ref[0] * jnp.dot(a_ref[0:ms, :], wbf_ref[...], preferred_element_type=F32)
            os_ref[...] = x
            hn_ref[0:ms, :] = _norm_rows(x, ln_ref[...], scs_ref[0], shs_ref[0]).astype(hn_ref.dtype)
        else:
            x = xp_ref[...] + gp_ref[0] * jnp.dot(a_ref[...], wbf_ref[...], preferred_element_type=F32)
            op_ref[...] = x
            hn_ref[...] = _norm_rows(x, ln_ref[...], scp_ref[0], shp_ref[0]).astype(hn_ref.dtype)

    _split_tiles(0, body)


def out_proj_norm(a, w, layer, xp, xs, gate, ln_g, sc, sh, tm):
    m, k = a.shape
    n = w.shape[2]
    mp, ms = xp.shape[0], xs.shape[0]
    nf = mp // tm
    per = nf // gate[0].shape[0]
    p_spec = pl.BlockSpec((tm, n), lambda i: (_prompt_tile(i), 0))
    s_spec = pl.BlockSpec((ms, n), lambda i: (0, 0))
    return pl.pallas_call(
        _out_proj_norm_kernel,
        grid=(nf + 1,),
        in_specs=[pl.BlockSpec((tm, k), lambda i: (_joint_tile(i, nf), 0)),
                  pl.BlockSpec((1, k, n), lambda i: (layer, 0, 0), pipeline_mode=pl.Buffered(1)),
                  p_spec, s_spec, *_mod_specs(gate, n, per, 1),
                  pl.BlockSpec((1, n), lambda i: (0, 0)),
                  *_mod_specs(sc, n, per, 1), *_mod_specs(sh, n, per, 1)],
        out_specs=[p_spec, s_spec, pl.BlockSpec((tm, n), lambda i: (_joint_tile(i, nf), 0))],
        out_shape=[jax.ShapeDtypeStruct((mp, n), F32), jax.ShapeDtypeStruct((ms, n), F32),
                   jax.ShapeDtypeStruct((m, n), BF16)],
        scratch_shapes=[pltpu.VMEM((k, n), BF16)],
        compiler_params=_params(1),
        name="out_proj_norm",
    )(a, w, xp, xs, *gate, ln_g, *sc, *sh)


def _swiglu_kernel(a_ref, w1_ref, w3_ref, o_ref, w1bf_ref, w3bf_ref, *, ms):
    @pl.when(pl.program_id(1) == 0)
    def _():
        w1bf_ref[...] = w1_ref[0].astype(BF16)
        w3bf_ref[...] = w3_ref[0].astype(BF16)

    def body(tail):
        rows = slice(0, ms) if tail else slice(None)
        a = a_ref[rows, :]
        h1 = jnp.dot(a, w1bf_ref[...], preferred_element_type=F32)
        h3 = jnp.dot(a, w3bf_ref[...], preferred_element_type=F32)
        o_ref[rows, :] = (jax.nn.silu(h1) * h3).astype(o_ref.dtype)

    _split_tiles(1, body)


def swiglu_up(a, w1, w3, layer, ms, tm, tn):
    m, k = a.shape
    n = w1.shape[2]
    nf = (m - ms) // tm
    w_spec = pl.BlockSpec((1, k, tn), lambda j, i: (layer, 0, j))
    return pl.pallas_call(
        functools.partial(_swiglu_kernel, ms=ms),
        grid=(n // tn, nf + 1),
        in_specs=[pl.BlockSpec((tm, k), lambda j, i: (_joint_tile(i, nf), 0)), w_spec, w_spec],
        out_specs=pl.BlockSpec((tm, tn), lambda j, i: (_joint_tile(i, nf), j)),
        out_shape=jax.ShapeDtypeStruct((m, n), BF16),
        scratch_shapes=[pltpu.VMEM((k, tn), BF16), pltpu.VMEM((k, tn), BF16)],
        compiler_params=_params(2),
        name="swiglu_up",
    )(a, w1, w3)


def _gate_merge_kernel(hn_ref, yap_ref, ybp_ref, ycp_ref, ydp_ref, yas_ref, ybs_ref, ycs_ref, yds_ref,
                       wg0_ref, wg1_ref, wg2_ref, wg3_ref, wb_ref, bg_ref, o_ref, wgbf_ref, wbbf_ref):
    ms = yas_ref.shape[0]

    @pl.when(pl.program_id(1) == 0)
    def _():
        for g, wg_ref in enumerate((wg0_ref, wg1_ref, wg2_ref, wg3_ref)):
            wgbf_ref[g] = wg_ref[0].astype(BF16)
            wbbf_ref[g] = wb_ref[0, g].astype(BF16)

    def body(tail):
        rows = slice(0, ms) if tail else slice(None)
        y_refs = (yas_ref, ybs_ref, ycs_ref, yds_ref) if tail else (yap_ref, ybp_ref, ycp_ref, ydp_ref)
        hn = hn_ref[rows, :]
        acc = None
        for g, y_ref in enumerate(y_refs):
            gate = jax.nn.sigmoid(jnp.dot(hn, wgbf_ref[g], preferred_element_type=F32) + bg_ref[0, g:g + 1, :])
            br = jnp.dot(y_ref[...], wbbf_ref[g], preferred_element_type=F32)
            acc = gate * br if acc is None else acc + gate * br
        o_ref[rows, :] = acc.astype(o_ref.dtype)

    _split_tiles(1, body)


def gate_merge(hn, ys_p, ys_s, w_gate, b_gate, w_branch, layer, tm, tn):
    m, k = hn.shape
    ms = ys_s[0].shape[0]
    nf = (m - ms) // tm
    depth = w_gate.shape[0]
    d = w_branch.shape[3]
    nj = d // tn
    wg_specs = [pl.BlockSpec((1, k, tn), functools.partial(lambda j, i, g: (layer, 0, g * nj + j), g=g))
                for g in range(4)]
    yp_spec = pl.BlockSpec((tm, BR_W), lambda j, i: (_prompt_tile(i), 0))
    ys_spec = pl.BlockSpec((ms, BR_W), lambda j, i: (0, 0))
    return pl.pallas_call(
        _gate_merge_kernel,
        grid=(nj, nf + 1),
        in_specs=[pl.BlockSpec((tm, k), lambda j, i: (_joint_tile(i, nf), 0)), *[yp_spec] * 4, *[ys_spec] * 4,
                  *wg_specs,
                  pl.BlockSpec((1, 4, BR_W, tn), lambda j, i: (layer, 0, 0, j)),
                  pl.BlockSpec((1, 4, tn), lambda j, i: (layer, 0, j))],
        out_specs=pl.BlockSpec((tm, tn), lambda j, i: (_joint_tile(i, nf), j)),
        out_shape=jax.ShapeDtypeStruct((m, d), BF16),
        scratch_shapes=[pltpu.VMEM((4, k, tn), BF16), pltpu.VMEM((4, BR_W, tn), BF16)],
        compiler_params=_params(2),
        name="gate_merge",
    )(hn, *ys_p, *ys_s, w_gate, w_gate, w_gate, w_gate, w_branch, b_gate.reshape(depth, 4, d))


def _gmlp_kernel(pu_ref, pv_ref, lg_ref, lb_ref, ws_ref, bst_ref, ya_ref, *maybe_va_ref, chunk):
    rows = pu_ref.shape[0]
    u = jax.nn.gelu(pu_ref[...])
    v = jax.nn.gelu(pv_ref[...])
    mu = jnp.mean(v, axis=-1, keepdims=True)
    var = jnp.mean(jnp.square(v - mu), axis=-1, keepdims=True)
    vn = (v - mu) * lax.rsqrt(var + 1e-5) * lg_ref[...] + lb_ref[...]
    for va_ref in maybe_va_ref:
        va_ref[...] = vn
    causal = (lax.broadcasted_iota(jnp.int32, (CHUNK, CHUNK), 0)
              >= lax.broadcasted_iota(jnp.int32, (CHUNK, CHUNK), 1))
    for g in range(4):
        wm = jnp.where(causal, ws_ref[0, g], 0.0).astype(BF16)
        bias = bst_ref[0, :, g:g + 1]
        for c in range(rows // chunk):
            r0 = c * chunk
            vc = vn[r0:r0 + chunk, g * GW:(g + 1) * GW]
            if chunk < CHUNK:
                vc = jnp.concatenate([vc, jnp.zeros((CHUNK - chunk, GW), F32)], axis=0)
            mix = (jnp.dot(wm, vc.astype(BF16), preferred_element_type=F32) + bias)[:chunk]
            ya_ref[r0:r0 + chunk, g * GW:(g + 1) * GW] = (
                u[r0:r0 + chunk, g * GW:(g + 1) * GW] * mix).astype(ya_ref.dtype)


def gmlp(p, ln_g, ln_b, ws, bst, row0, m, tr, chunk, emit_v):
    blk0 = row0 // tr
    in_spec = lambda col: pl.BlockSpec((tr, BR_W), lambda i: (blk0 + i, col))
    out_spec = pl.BlockSpec((tr, BR_W), lambda i: (i, 0))
    n_out = 2 if emit_v else 1
    return pl.pallas_call(
        functools.partial(_gmlp_kernel, chunk=chunk),
        grid=(m // tr,),
        in_specs=[in_spec(0), in_spec(1),
                  pl.BlockSpec((1, BR_W), lambda i: (0, 0)),
                  pl.BlockSpec((1, BR_W), lambda i: (0, 0)),
                  pl.BlockSpec((1, 4, CHUNK, CHUNK), lambda i: (0, 0, 0, 0)),
                  pl.BlockSpec((1, CHUNK, 4), lambda i: (0, 0, 0))],
        out_specs=[out_spec] * n_out,
        out_shape=[jax.ShapeDtypeStruct((m, BR_W), BF16), jax.ShapeDtypeStruct((m, BR_W), F32)][:n_out],
        compiler_params=_params(1),
        name="gmlp",
    )(p, p, ln_g, ln_b, ws, bst)


def _bias_from_buckets(idx, rel_ref, col, shape):
    bias = jnp.full(shape, NEG, F32)
    for b in range(N_BUCKETS):
        bias = jnp.where(idx == b, rel_ref[b, col], bias)
    return bias


def _attn_prompt_kernel(rel_ref, q_ref, k_ref, v_ref, idx_ref, o_ref,
                        o0_ref, o1_ref, o2_ref, e0_ref, e1_ref, e2_ref, bm_ref):
    t = q_ref.shape[0]
    head = pl.program_id(1)
    group = pl.program_id(2)
    scale = GW ** -0.5
    outs = (o0_ref, o1_ref, o2_ref)
    lses = (e0_ref, e1_ref, e2_ref)

    bm_ref[...] = _bias_from_buckets(idx_ref[0], rel_ref, group * N_HEADS + head, (NK, 2 * NK))

    def rows(start, n, dil):
        return pl.ds(start, n) if dil == 1 else pl.ds(start, n, stride=dil)

    def blocks(gi, specs):
        dil = PATTERNS[gi][1]
        staged = []
        for first_block, q0, k0 in specs:
            nkeys = NK if first_block else 2 * NK
            qi = rows(q0, NK, dil)
            ki = rows(k0, nkeys, dil)
            q = q_ref[qi, :].astype(BF16)
            kk = k_ref[ki, :].astype(BF16)
            bm = bm_ref[:, NK:] if first_block else bm_ref[...]
            s = lax.dot_general(q, kk, (((1,), (1,)), ((), ())), preferred_element_type=F32) * scale + bm
            staged.append((qi, ki, s))
        probs = []
        for qi, ki, s in staged:
            m = jnp.max(s, axis=-1, keepdims=True)
            p = jnp.exp(s - m)
            l = jnp.sum(p, axis=-1, keepdims=True)
            probs.append((qi, ki, p.astype(BF16), m, l))
        for qi, ki, p, m, l in probs:
            acc = jnp.dot(p, v_ref[ki, :].astype(BF16), preferred_element_type=F32)
            outs[gi][qi, :] = acc * (1.0 / l)
            lses[gi][qi, :] = jnp.broadcast_to(m + jnp.log(l), (NK, GW))

    def run_group(gi):
        dil = PATTERNS[gi][1]
        span = NK * dil
        nb = t // span
        u = ATTN_BLOCKS_PER_STEP
        if dil == 1:
            blocks(gi, [(True, 0, 0)] + [(False, n * span, (n - 1) * span) for n in range(1, u)])

            def step(i, c):
                q0 = pl.multiple_of(u * i * span, span)
                blocks(gi, [(False, q0 + n * span, q0 + (n - 1) * span) for n in range(u)])
                return c

            lax.fori_loop(1, nb // u, step, 0)
        else:
            def bunch(i, c):
                res = [u * i + r for r in range(u)]
                blocks(gi, [(True, r, r) for r in res])

                def later(n, cc):
                    blocks(gi, [(False, n * span + r, (n - 1) * span + r) for r in res])
                    return cc

                return lax.fori_loop(1, nb, later, c)

            if dil == u:
                bunch(0, 0)
            else:
                lax.fori_loop(0, dil // u, bunch, 0)

    for gi in range(len(PATTERNS)):
        pl.when(group == gi)(functools.partial(run_group, gi))

    @pl.when(group == len(PATTERNS) - 1)
    def _():
        step = 256

        def merge(c, carry):
            sl = pl.ds(pl.multiple_of(c * step, step), step)
            e = [r[sl, :] for r in lses]
            top = jnp.maximum(jnp.maximum(e[0], e[1]), e[2])
            w = [jnp.exp(x - top) for x in e]
            num = w[0] * o0_ref[sl, :] + w[1] * o1_ref[sl, :] + w[2] * o2_ref[sl, :]
            o_ref[sl, :] = (num / (w[0] + w[1] + w[2])).astype(o_ref.dtype)
            return carry

        lax.fori_loop(0, t // step, merge, 0)


def _prompt_buckets():
    i = np.arange(NK)[:, None]
    j = np.arange(2 * NK)[None, :]
    diff = NK + i - j
    valid = (diff >= 0) & (diff <= NK)
    return np.stack([np.where(valid, _t5_bucket(diff * dil), -1) for _, dil in PATTERNS]).astype(np.int32)


def attention_prompt(p, rel_bias, nseq, t):
    def qkv_spec(which):
        return pl.BlockSpec((t, GW), lambda b, h, g: (b, COL_B // GW + g * 12 + which * 4 + h))

    return pl.pallas_call(
        _attn_prompt_kernel,
        grid=(nseq, N_HEADS, len(PATTERNS)),
        in_specs=[pl.BlockSpec(memory_space=pltpu.SMEM),
                  qkv_spec(0), qkv_spec(1), qkv_spec(2),
                  pl.BlockSpec((1, NK, 2 * NK), lambda b, h, g: (g, 0, 0))],
        out_specs=pl.BlockSpec((t, GW), lambda b, h, g: (b, h)),
        out_shape=jax.ShapeDtypeStruct((nseq * t, BR_W), BF16),
        scratch_shapes=[pltpu.VMEM((t, GW), F32)] * 6 + [pltpu.VMEM((NK, 2 * NK), F32)],
        compiler_params=_params(3),
        name="attention_prompt",
    )(rel_bias, p, p, p, jnp.asarray(_prompt_buckets()))


N_NEW = 4
TILE_ROWS = 2 * N_HEADS


def _key_to_value_rows(x):
    n, r, w = x.shape
    return pltpu.roll(x.reshape(n * r, w), N_HEADS, axis=0).reshape(n, r, w)


def _attend(x, xn, q, bias, bias_n, scale):
    s = jnp.sum(x * q[None], axis=-1, keepdims=True) * scale + bias
    sn = jnp.sum(xn * q[None], axis=-1, keepdims=True) * scale + bias_n
    m = jnp.maximum(jnp.max(s, axis=0, keepdims=True), jnp.max(sn, axis=0, keepdims=True))
    p = jnp.exp(s - m)
    pn = jnp.exp(sn - m)
    l = jnp.sum(p, axis=0, keepdims=True) + jnp.sum(pn, axis=0, keepdims=True)
    acc = jnp.sum(_key_to_value_rows(p) * x, axis=0) + jnp.sum(_key_to_value_rows(pn) * xn, axis=0)
    return acc, m, l


def _attn_sample_kernel(rel_ref, q_ref, xn_ref, c1_ref, c2_ref, c3_ref, o_ref,
                        b1_ref, b1n_ref, b23_ref, b23n_ref):
    scale = GW ** -0.5
    rows = N_NEW * TILE_ROWS

    @pl.when(pl.program_id(0) == 0)
    def _():
        head = lax.broadcasted_iota(jnp.int32, (rows, GW), 0) % N_HEADS
        neg = jnp.full((rows, GW), NEG, F32)

        def tiles_for(gi):
            out = []
            for b in range(N_BUCKETS):
                v = [rel_ref[b, gi * N_HEADS + h] for h in range(N_HEADS)]
                out.append(jnp.where(head == 0, v[0], jnp.where(head == 1, v[1], jnp.where(head == 2, v[2], v[3]))))
            return out

        t1 = tiles_for(0)
        for t in range(N_NEW):
            for pos in range(NK):
                step = NK + t - pos
                tile = t1[int(_t5_bucket(np.int64(step)))] if step <= NK else neg
                b1_ref[t, pos] = tile[:TILE_ROWS]
            for j in range(N_NEW):
                tile = t1[int(_t5_bucket(np.int64(t - j)))] if j <= t else neg
                b1n_ref[t, j] = tile[:TILE_ROWS]
        for gi in (1, 2):
            dil = PATTERNS[gi][1]
            tg = tiles_for(gi)
            for jj in range(NK):
                b23_ref[gi - 1, jj] = tg[int(_t5_bucket(np.int64((NK - jj) * dil)))]
            b23n_ref[gi - 1] = tg[0]

    outs, lses = [], []
    x1 = c1_ref[0, 0]
    xn1 = xn_ref[0, 0].reshape(N_NEW, TILE_ROWS, GW)
    o1, e1 = [], []
    for t in range(N_NEW):
        q = q_ref[0, 0, t * TILE_ROWS:(t + 1) * TILE_ROWS, :]
        acc, m, l = _attend(x1, xn1, q, b1_ref[t], b1n_ref[t], scale)
        o1.append(acc)
        e1.append((m, l))
    outs.append(jnp.concatenate(o1, axis=0))
    lses.append((jnp.concatenate([m[0] for m, _ in e1], axis=0), jnp.concatenate([l[0] for _, l in e1], axis=0)))
    for gi, c_ref in ((1, c2_ref), (2, c3_ref)):
        acc, m, l = _attend(c_ref[0, 0], xn_ref[gi, 0][None], q_ref[gi, 0], b23_ref[gi - 1],
                            b23n_ref[gi - 1][None], scale)
        outs.append(acc)
        lses.append((m[0], l[0]))

    lse = [_key_to_value_rows((m + jnp.log(l))[None])[0] for m, l in lses]
    den = [_key_to_value_rows(l[None])[0] for _, l in lses]
    top = jnp.maximum(jnp.maximum(lse[0], lse[1]), lse[2])
    w = [jnp.exp(e - top) for e in lse]
    num = w[0] * outs[0] / den[0] + w[1] * outs[1] / den[1] + w[2] * outs[2] / den[2]
    o_ref[0] = num / (w[0] + w[1] + w[2])


def attention_sample(p, cache_b1, cache_b2, cache_b3, rel_bias, layer, row0, nseq):
    depth = cache_b1.shape[0]
    rows = N_NEW * TILE_ROWS
    ps = p[row0:].reshape(nseq, SAMPLE_ROWS, N_IN)[:, :N_NEW]
    qs, xns = [], []
    for gi in range(len(PATTERNS)):
        c0 = COL_B + gi * 3 * BR_W
        q = ps[:, :, c0:c0 + BR_W].reshape(nseq, N_NEW, N_HEADS, GW)
        qs.append(jnp.concatenate([q, jnp.zeros_like(q)], axis=2).reshape(nseq, rows, GW))
        xns.append(ps[:, :, c0 + BR_W:c0 + 3 * BR_W].reshape(nseq, rows, GW))
    q_all = jnp.stack(qs)
    xn_all = jnp.stack(xns)
    c1 = cache_b1.reshape(depth, nseq, NK, TILE_ROWS, GW)
    c2 = cache_b2.reshape(depth, nseq, NK, 4 * TILE_ROWS, GW)
    c3 = cache_b3.reshape(depth, nseq, NK, 16 * TILE_ROWS, GW)
    cache_spec = lambda r: pl.BlockSpec((1, 1, NK, r, GW), lambda b: (layer, b, 0, 0, 0))
    tok_spec = pl.BlockSpec((len(PATTERNS), 1, rows, GW), lambda b: (0, b, 0, 0))
    y = pl.pallas_call(
        _attn_sample_kernel,
        grid=(nseq,),
        in_specs=[pl.BlockSpec(memory_space=pltpu.SMEM), tok_spec, tok_spec,
                  cache_spec(TILE_ROWS), cache_spec(rows), cache_spec(rows)],
        out_specs=pl.BlockSpec((1, rows, GW), lambda b: (b, 0, 0)),
        out_shape=jax.ShapeDtypeStruct((nseq, rows, GW), F32),
        scratch_shapes=[pltpu.VMEM((N_NEW, NK, TILE_ROWS, GW), F32), pltpu.VMEM((N_NEW, N_NEW, TILE_ROWS, GW), F32),
                        pltpu.VMEM((2, NK, rows, GW), F32), pltpu.VMEM((2, rows, GW), F32)],
        compiler_params=_params(1),
        name="attention_sample",
    )(rel_bias, q_all, xn_all, c1, c2, c3)
    y = y.reshape(nseq, N_NEW, 2, BR_W)[:, :, 1]
    y = jnp.pad(y, ((0, 0), (0, SAMPLE_ROWS - N_NEW), (0, 0)))
    return y.reshape(nseq * SAMPLE_ROWS, BR_W).astype(BF16)


def _pool_conv_kernel(pc_ref, pbg_ref, pcg_ref, phs_ref, hc_ref, hd_ref, pw_ref, pb_ref, ps_ref, cw_ref,
                      yc_ref, yd_ref, zt_ref, cbuf_ref, zbuf_ref, *, start):
    tr = pc_ref.shape[0]
    j = pl.program_id(1)
    hc = 16
    hz = 8

    @pl.when(j == 0)
    def _():
        cbuf_ref[0:hc, :] = hc_ref[0]
        zbuf_ref[0:hz, :] = hd_ref[0]

    x = pc_ref[...]
    cbuf_ref[hc:hc + tr, :] = x
    pos = start + j * tr + lax.broadcasted_iota(jnp.int32, (tr, 1), 0)
    for gi, win in enumerate(POOL_WINDOWS):
        cols = slice(gi * GW, (gi + 1) * GW)
        total = x[:, cols]
        for back in range(1, win):
            total = total + cbuf_ref[hc - back:hc - back + tr, cols]
        cnt = jnp.minimum(pos + 1, win).astype(F32)
        pooled = total / cnt - x[:, cols]
        y = jnp.dot(pooled.astype(BF16), pw_ref[0, gi].astype(BF16), preferred_element_type=F32)
        yc_ref[:, cols] = ((y + pb_ref[:, cols]) * ps_ref[:, cols]).astype(yc_ref.dtype)
    cbuf_ref[0:hc, :] = cbuf_ref[tr:tr + hc, :]

    z = pcg_ref[...] * phs_ref[...]
    zbuf_ref[hz:hz + tr, :] = z
    y = (zbuf_ref[hz - 2:hz - 2 + tr, :] * cw_ref[0:1, :] + zbuf_ref[hz - 1:hz - 1 + tr, :] * cw_ref[1:2, :]
         + z * cw_ref[2:3, :])
    yd_ref[...] = (pbg_ref[...] * y).astype(yd_ref.dtype)
    zt_ref[0] = z[tr - hz:, :]
    zbuf_ref[0:hz, :] = zbuf_ref[tr:tr + hz, :]


def pool_conv(p, hist_c, hist_d, pool_w, pool_b, pool_scale, conv_wt, layer, row0, nseq, t, tr, start):
    per = t // tr
    blk0 = row0 // tr
    col = lambda c: pl.BlockSpec((tr, BR_W), lambda b, j: (b * per + j, c))
    pcol = lambda c: pl.BlockSpec((tr, BR_W), lambda b, j: (blk0 + b * per + j, c))
    vec = pl.BlockSpec((1, BR_W), lambda b, j: (0, 0))
    return pl.pallas_call(
        functools.partial(_pool_conv_kernel, start=start),
        grid=(nseq, per),
        in_specs=[pcol(COL_C // BR_W), pcol(COL_D // BR_W), pcol(COL_D // BR_W + 1), pcol(COL_D // BR_W + 2),
                  pl.BlockSpec((1, 16, BR_W), lambda b, j: (b, 0, 0)),
                  pl.BlockSpec((1, 8, BR_W), lambda b, j: (b, 0, 0)),
                  pl.BlockSpec((1, 4, GW, GW), lambda b, j: (layer, 0, 0, 0)),
                  vec, vec,
                  pl.BlockSpec((CONV_W, BR_W), lambda b, j: (0, 0))],
        out_specs=[col(0), col(0), pl.BlockSpec((1, 8, BR_W), lambda b, j: (b, 0, 0))],
        out_shape=[jax.ShapeDtypeStruct((nseq * t, BR_W), BF16),
                   jax.ShapeDtypeStruct((nseq * t, BR_W), BF16),
                   jax.ShapeDtypeStruct((nseq, 8, BR_W), F32)],
        scratch_shapes=[pltpu.VMEM((16 + tr, BR_W), F32), pltpu.VMEM((8 + tr, BR_W), F32)],
        compiler_params=_params(2),
        name="pool_conv",
    )(p, p, p, p, hist_c, hist_d, pool_w, pool_b, pool_scale, conv_wt)


TM_WIDE = 1024
TM_ROWS = 512


def kernel(x_prompt, x_sample, c_prompt, c_sample, cache_b1, cache_b2, cache_b3, cache_pool, cache_conv,
           ln1_g, ln2_g, w_ada, b_ada, w_in, w_gate, b_gate, a_ln_g, a_ln_b, a_ws, a_bs,
           rel_bias, pool_w, pool_b, pool_scale, conv_w, w_branch, w_out, w1, w3, w2, final_g):
    nb, seq, d = x_prompt.shape
    ns, dec = x_sample.shape[:2]
    depth = w_in.shape[0]
    mp = nb * seq
    ms = ns * SAMPLE_ROWS

    c_all = jnp.concatenate([c_prompt, c_sample], axis=0)
    c_rows = -(-c_all.shape[0] // 8) * 8
    c_all = jnp.pad(c_all, ((0, c_rows - c_all.shape[0]), (0, 0)))
    mod = ada_modulation(c_all, w_ada, b_ada).reshape(depth, c_rows, 6, d)
    mods = [[(mod[l, :nb, i][:, None, :], jnp.repeat(mod[l, nb:nb + ns, i], SAMPLE_ROWS, axis=0)[None])
             for i in range(6)] for l in range(depth)]

    xp = x_prompt.reshape(mp, d)
    xs = jnp.pad(x_sample, ((0, 0), (0, SAMPLE_ROWS - dec), (0, 0))).reshape(ms, d)
    zero_hist_c = jnp.zeros((nb, 16, BR_W), F32)
    zero_hist_d = jnp.zeros((nb, 8, BR_W), F32)

    kv_p = [[] for _ in PATTERNS]
    kv_s = [[] for _ in PATTERNS]
    pool_p, conv_p, pool_s, conv_s, chunk_s = [], [], [], [], []
    for l in range(depth):
        sh1, sc1, g1, sh2, sc2, g2 = mods[l]
        hn = norm_in(xp, xs, ln1_g[l][None], sc1, sh1, TM_ROWS)
        p = matmul(hn, w_in, l, ms, TM_WIDE, 1280, F32)

        gm = (a_ln_g[l][None], a_ln_b[l][None], a_ws[l][None], a_bs[l].T[None])
        ya_p, = gmlp(p, *gm, 0, mp, 512, CHUNK, False)
        ya_s, va_s = gmlp(p, *gm, mp, ms, ms, SAMPLE_ROWS, True)
        yb_p = attention_prompt(p, rel_bias, nb, seq)
        yb_s = attention_sample(p, cache_b1, cache_b2, cache_b3, rel_bias, l, mp, ns)
        pc = (pool_w, pool_b[l][None], pool_scale[l][None], conv_w[l].T, l)
        yc_p, yd_p, zt_p = pool_conv(p, zero_hist_c, zero_hist_d, *pc, 0, nb, seq, 512, 0)
        hist_c = jnp.pad(cache_pool[l], ((0, 0), (1, 0), (0, 0)))
        hist_d = jnp.pad(cache_conv[l], ((0, 0), (8 - (CONV_W - 1), 0), (0, 0)))
        yc_s, yd_s, zt_s = pool_conv(p, hist_c, hist_d, *pc, mp, ns, SAMPLE_ROWS, SAMPLE_ROWS, PAST_LEN)

        merged = gate_merge(hn, (ya_p, yb_p, yc_p, yd_p), (ya_s, yb_s, yc_s, yd_s),
                            w_gate, b_gate, w_branch, l, TM_WIDE, 256)
        xp, xs, hn2 = out_proj_norm(merged, w_out, l, xp, xs, g1, ln2_g[l][None], sc2, sh2, TM_ROWS)
        hmid = swiglu_up(hn2, w1, w3, l, ms, TM_WIDE, 512)
        xp, xs = matmul_residual(hmid, w2, l, xp, xs, g2, TM_ROWS, 512)

        ps3 = p[mp:].reshape(ns, SAMPLE_ROWS, N_IN)
        for gi, (win, _) in enumerate(PATTERNS):
            c0 = COL_B + gi * 3 * BR_W + BR_W
            keep = min(win, seq)
            kv_p[gi].append(jnp.stack([p[(b + 1) * seq - keep:(b + 1) * seq, c0:c0 + 2 * BR_W]
                                       for b in range(nb)]).reshape(nb, keep, 2, N_HEADS, GW))
            kv_s[gi].append(ps3[:, :dec, c0:c0 + 2 * BR_W].reshape(ns, dec, 2, N_HEADS, GW))
        pool_p.append(jnp.stack([p[(b + 1) * seq - POOL_HIST:(b + 1) * seq, COL_C:COL_C + BR_W]
                                 for b in range(nb)]))
        conv_p.append(zt_p[:, 8 - (CONV_W - 1):])
        pool_s.append(ps3[:, :dec, COL_C:COL_C + BR_W])
        conv_s.append(zt_s[:, :dec])
        chunk_s.append(va_s.reshape(ns, SAMPLE_ROWS, BR_W)[:, :dec])

    yp, ys = final_norm(xp, xs, final_g[None], TM_ROWS)
    stack = lambda parts: jnp.stack(parts, axis=0)
    return (yp.reshape(nb, seq, d), ys.reshape(ns, SAMPLE_ROWS, d)[:, :dec],
            stack(kv_p[0]), stack(kv_p[1]), stack(kv_p[2]), stack(pool_p), stack(conv_p),
            stack(kv_s[0]), stack(kv_s[1]), stack(kv_s[2]), stack(pool_s), stack(conv_s), stack(chunk_s))
```

```python
import functools
import math

import jax
import jax.numpy as jnp
import numpy as np
from jax import lax
from jax.experimental import pallas as pl
from jax.experimental.pallas import tpu as pltpu

F32 = jnp.float32
BF16 = jnp.bfloat16

D_MODEL = 2048
BR_W = 512
GW = 128
N_HEADS = 4
CHUNK = 128
PATTERNS = ((128, 1), (512, 4), (2048, 16))
NK = 128
POOL_WINDOWS = (2, 4, 8, 16)
POOL_HIST = 15
CONV_W = 3
N_BUCKETS = 32
MAX_DIST = 2048
D_FF = 5632
N_IN = 7680
COL_A = 0
COL_B = 2 * BR_W
COL_C = COL_B + 9 * BR_W
COL_D = COL_C + BR_W
EPS = 1e-6
NEG = -1e30
SAMPLE_ROWS = 8
PAST_LEN = 16384
ATTN_BLOCKS_PER_STEP = 4

VMEM_LIMIT = 56 * 1024 * 1024


def _t5_bucket(dist):
    max_exact = N_BUCKETS // 2
    n = np.maximum(dist, 0)
    nf = np.maximum(n, 1).astype(np.float32)
    large = max_exact + (np.log(nf / np.float32(max_exact)) / np.float32(math.log(MAX_DIST / max_exact))
                         * np.float32(N_BUCKETS - max_exact)).astype(np.int32)
    large = np.minimum(large, N_BUCKETS - 1)
    return np.where(n < max_exact, n, large)


def _params(n_axes):
    return pltpu.CompilerParams(dimension_semantics=("arbitrary",) * n_axes,
                                vmem_limit_bytes=VMEM_LIMIT)


def _ada_kernel(c_ref, w_ref, b_ref, o_ref):
    a = jax.nn.silu(c_ref[...]).astype(BF16)
    o_ref[0] = jnp.dot(a, w_ref[0].astype(BF16), preferred_element_type=F32) + b_ref[0]


def ada_modulation(c_all, w_ada, b_ada):
    depth, d, n = w_ada.shape
    r = c_all.shape[0]
    tn = 1024
    return pl.pallas_call(
        _ada_kernel,
        grid=(depth, n // tn),
        in_specs=[pl.BlockSpec((r, d), lambda l, j: (0, 0)),
                  pl.BlockSpec((1, d, tn), lambda l, j: (l, 0, j)),
                  pl.BlockSpec((1, 1, tn), lambda l, j: (l, 0, j))],
        out_specs=pl.BlockSpec((1, r, tn), lambda l, j: (l, 0, j)),
        out_shape=jax.ShapeDtypeStruct((depth, r, n), F32),
        compiler_params=_params(2),
        name="ada_modulation",
    )(c_all, w_ada, b_ada.reshape(depth, 1, n))


def _split_tiles(axis, body):
    i = pl.program_id(axis)
    pl.when(i == 0)(functools.partial(body, True))
    pl.when(i > 0)(functools.partial(body, False))


def _joint_tile(i, nf):
    return (i + nf) % (nf + 1)


def _prompt_tile(i):
    return jnp.maximum(i - 1, 0)


def _norm_rows(x, g, sc, sh):
    y = x * lax.rsqrt(jnp.mean(x * x, axis=-1, keepdims=True) + EPS) * g
    return y * (1.0 + sc) + sh


def _mod_specs(pair, width, per, grid_rank):
    prompt, sample = pair
    nb, ms = prompt.shape[0], sample.shape[1]
    if grid_rank == 1:
        return [pl.BlockSpec((1, 1, width), lambda i: (_prompt_tile(i) // per, 0, 0)),
                pl.BlockSpec((1, ms, width), lambda i: (0, 0, 0))]
    return [pl.BlockSpec((1, 1, width), lambda j, i: (_prompt_tile(i) // per, 0, j)),
            pl.BlockSpec((1, ms, width), lambda j, i: (0, 0, j))]


NORM_CHUNK = 16


def _norm_chunks(x_ref, o_ref, g_ref, sc, sh):
    def step(c, carry):
        sl = pl.ds(pl.multiple_of(c * NORM_CHUNK, NORM_CHUNK), NORM_CHUNK)
        o_ref[sl, :] = _norm_rows(x_ref[sl, :], g_ref[...], sc, sh).astype(o_ref.dtype)
        return carry

    lax.fori_loop(0, x_ref.shape[0] // NORM_CHUNK, step, 0, unroll=4)


def _norm_in_kernel(xp_ref, xs_ref, g_ref, scp_ref, scs_ref, shp_ref, shs_ref, o_ref):
    ms = xs_ref.shape[0]

    def body(tail):
        if tail:
            o_ref[0:ms, :] = _norm_rows(xs_ref[...], g_ref[...], scs_ref[0], shs_ref[0]).astype(o_ref.dtype)
        else:
            _norm_chunks(xp_ref, o_ref, g_ref, scp_ref[0], shp_ref[0])

    _split_tiles(0, body)


def norm_in(xp, xs, g, sc, sh, tm):
    mp, d = xp.shape
    ms = xs.shape[0]
    nf = mp // tm
    per = nf // sc[0].shape[0]
    return pl.pallas_call(
        _norm_in_kernel,
        grid=(nf + 1,),
        in_specs=[pl.BlockSpec((tm, d), lambda i: (_prompt_tile(i), 0)),
                  pl.BlockSpec((ms, d), lambda i: (0, 0)),
                  pl.BlockSpec((1, d), lambda i: (0, 0)),
                  *_mod_specs(sc, d, per, 1), *_mod_specs(sh, d, per, 1)],
        out_specs=pl.BlockSpec((tm, d), lambda i: (_joint_tile(i, nf), 0)),
        out_shape=jax.ShapeDtypeStruct((mp + ms, d), BF16),
        compiler_params=_params(1),
        name="norm_in",
    )(xp, xs, g, *sc, *sh)


def _final_norm_kernel(xp_ref, xs_ref, g_ref, yp_ref, ys_ref):
    def body(tail):
        src, dst = (xs_ref, ys_ref) if tail else (xp_ref, yp_ref)
        x = src[...]
        dst[...] = x * lax.rsqrt(jnp.mean(x * x, axis=-1, keepdims=True) + EPS) * g_ref[...]

    _split_tiles(0, body)


def final_norm(xp, xs, g, tm):
    mp, d = xp.shape
    ms = xs.shape[0]
    nf = mp // tm
    p_spec = pl.BlockSpec((tm, d), lambda i: (_prompt_tile(i), 0))
    s_spec = pl.BlockSpec((ms, d), lambda i: (0, 0))
    return pl.pallas_call(
        _final_norm_kernel,
        grid=(nf + 1,),
        in_specs=[p_spec, s_spec, pl.BlockSpec((1, d), lambda i: (0, 0))],
        out_specs=[p_spec, s_spec],
        out_shape=[jax.ShapeDtypeStruct((mp, d), F32), jax.ShapeDtypeStruct((ms, d), F32)],
        compiler_params=_params(1),
        name="final_norm",
    )(xp, xs, g)


def _mm_kernel(a_ref, w_ref, o_ref, wbf_ref, *, ms):
    @pl.when(pl.program_id(1) == 0)
    def _():
        wbf_ref[...] = w_ref[0].astype(BF16)

    def body(tail):
        rows = slice(0, ms) if tail else slice(None)
        o_ref[rows, :] = jnp.dot(a_ref[rows, :], wbf_ref[...], preferred_element_type=F32).astype(o_ref.dtype)

    _split_tiles(1, body)


def matmul(a, w, layer, ms, tm, tn, out_dtype):
    m, k = a.shape
    n = w.shape[2]
    nf = (m - ms) // tm
    return pl.pallas_call(
        functools.partial(_mm_kernel, ms=ms),
        grid=(n // tn, nf + 1),
        in_specs=[pl.BlockSpec((tm, k), lambda j, i: (_joint_tile(i, nf), 0)),
                  pl.BlockSpec((1, k, tn), lambda j, i: (layer, 0, j))],
        out_specs=pl.BlockSpec((tm, tn), lambda j, i: (_joint_tile(i, nf), j)),
        out_shape=jax.ShapeDtypeStruct((m, n), out_dtype),
        scratch_shapes=[pltpu.VMEM((k, tn), BF16)],
        compiler_params=_params(2),
        name="matmul",
    )(a, w)


def _mm_residual_kernel(a_ref, w_ref, xp_ref, xs_ref, gp_ref, gs_ref, op_ref, os_ref, wbf_ref):
    ms = xs_ref.shape[0]

    @pl.when(pl.program_id(1) == 0)
    def _():
        wbf_ref[...] = w_ref[0].astype(BF16)

    def body(tail):
        if tail:
            y = jnp.dot(a_ref[0:ms, :], wbf_ref[...], preferred_element_type=F32)
            os_ref[...] = xs_ref[...] + gs_ref[0] * y
        else:
            y = jnp.dot(a_ref[...], wbf_ref[...], preferred_element_type=F32)
            op_ref[...] = xp_ref[...] + gp_ref[0] * y

    _split_tiles(1, body)


def matmul_residual(a, w, layer, xp, xs, gate, tm, tn):
    m, k = a.shape
    n = w.shape[2]
    mp, ms = xp.shape[0], xs.shape[0]
    nf = mp // tm
    per = nf // gate[0].shape[0]
    p_spec = pl.BlockSpec((tm, tn), lambda j, i: (_prompt_tile(i), j))
    s_spec = pl.BlockSpec((ms, tn), lambda j, i: (0, j))
    return pl.pallas_call(
        _mm_residual_kernel,
        grid=(n // tn, nf + 1),
        in_specs=[pl.BlockSpec((tm, k), lambda j, i: (_joint_tile(i, nf), 0)),
                  pl.BlockSpec((1, k, tn), lambda j, i: (layer, 0, j)),
                  p_spec, s_spec, *_mod_specs(gate, tn, per, 2)],
        out_specs=[p_spec, s_spec],
        out_shape=[jax.ShapeDtypeStruct((mp, n), F32), jax.ShapeDtypeStruct((ms, n), F32)],
        scratch_shapes=[pltpu.VMEM((k, tn), BF16)],
        compiler_params=_params(2),
        name="matmul_residual",
    )(a, w, xp, xs, *gate)


def _out_proj_norm_kernel(a_ref, w_ref, xp_ref, xs_ref, gp_ref, gs_ref, ln_ref, scp_ref, scs_ref,
                          shp_ref, shs_ref, op_ref, os_ref, hn_ref, wbf_ref):
    ms = xs_ref.shape[0]

    @pl.when(pl.program_id(0) == 0)
    def _():
        wbf_ref[...] = w_ref[0].astype(BF16)

    def body(tail):
        if tail:
            x = xs_ref[...] + gs_ref[0] * jnp.dot(a_ref[0:ms, :], wbf_ref[...], preferred_element_type=F32)
            os_ref[...] = x
            hn_ref[0:ms, :] = _norm_rows(x, ln_ref[...], scs_ref[0], shs_ref[0]).astype(hn_ref.dtype)
        else:
            x = xp_ref[...] + gp_ref[0] * jnp.dot(a_ref[...], wbf_ref[...], preferred_element_type=F32)
            op_ref[...] = x
            _norm_chunks(op_ref, hn_ref, ln_ref, scp_ref[0], shp_ref[0])

    _split_tiles(0, body)


def out_proj_norm(a, w, layer, xp, xs, gate, ln_g, sc, sh, tm):
    m, k = a.shape
    n = w.shape[2]
    mp, ms = xp.shape[0], xs.shape[0]
    nf = mp // tm
    per = nf // gate[0].shape[0]
    p_spec = pl.BlockSpec((tm, n), lambda i: (_prompt_tile(i), 0))
    s_spec = pl.BlockSpec((ms, n), lambda i: (0, 0))
    return pl.pallas_call(
        _out_proj_norm_kernel,
        grid=(nf + 1,),
        in_specs=[pl.BlockSpec((tm, k), lambda i: (_joint_tile(i, nf), 0)),
                  pl.BlockSpec((1, k, n), lambda i: (layer, 0, 0), pipeline_mode=pl.Buffered(1)),
                  p_spec, s_spec, *_mod_specs(gate, n, per, 1),
                  pl.BlockSpec((1, n), lambda i: (0, 0)),
                  *_mod_specs(sc, n, per, 1), *_mod_specs(sh, n, per, 1)],
        out_specs=[p_spec, s_spec, pl.BlockSpec((tm, n), lambda i: (_joint_tile(i, nf), 0))],
        out_shape=[jax.ShapeDtypeStruct((mp, n), F32), jax.ShapeDtypeStruct((ms, n), F32),
                   jax.ShapeDtypeStruct((m, n), BF16)],
        scratch_shapes=[pltpu.VMEM((k, n), BF16)],
        compiler_params=_params(1),
        name="out_proj_norm",
    )(a, w, xp, xs, *gate, ln_g, *sc, *sh)


def _swiglu_kernel(a_ref, w1_ref, w3_ref, o_ref, w1bf_ref, w3bf_ref, *, ms):
    @pl.when(pl.program_id(1) == 0)
    def _():
        w1bf_ref[...] = w1_ref[0].astype(BF16)
        w3bf_ref[...] = w3_ref[0].astype(BF16)

    def body(tail):
        rows = slice(0, ms) if tail else slice(None)
        a = a_ref[rows, :]
        h1 = jnp.dot(a, w1bf_ref[...], preferred_element_type=F32)
        h3 = jnp.dot(a, w3bf_ref[...], preferred_element_type=F32)
        o_ref[rows, :] = (jax.nn.silu(h1) * h3).astype(o_ref.dtype)

    _split_tiles(1, body)


def swiglu_up(a, w1, w3, layer, ms, tm, tn):
    m, k = a.shape
    n = w1.shape[2]
    nf = (m - ms) // tm
    w_spec = pl.BlockSpec((1, k, tn), lambda j, i: (layer, 0, j))
    return pl.pallas_call(
        functools.partial(_swiglu_kernel, ms=ms),
        grid=(n // tn, nf + 1),
        in_specs=[pl.BlockSpec((tm, k), lambda j, i: (_joint_tile(i, nf), 0)), w_spec, w_spec],
        out_specs=pl.BlockSpec((tm, tn), lambda j, i: (_joint_tile(i, nf), j)),
        out_shape=jax.ShapeDtypeStruct((m, n), BF16),
        scratch_shapes=[pltpu.VMEM((k, tn), BF16), pltpu.VMEM((k, tn), BF16)],
        compiler_params=_params(2),
        name="swiglu_up",
    )(a, w1, w3)


def _gate_merge_kernel(hn_ref, yap_ref, ybp_ref, ycp_ref, ydp_ref, yas_ref, ybs_ref, ycs_ref, yds_ref,
                       wg0_ref, wg1_ref, wg2_ref, wg3_ref, wb_ref, bg_ref, o_ref, wgbf_ref, wbbf_ref):
    ms = yas_ref.shape[0]

    @pl.when(pl.program_id(1) == 0)
    def _():
        for g, wg_ref in enumerate((wg0_ref, wg1_ref, wg2_ref, wg3_ref)):
            wgbf_ref[g] = wg_ref[0].astype(BF16)
            wbbf_ref[g] = wb_ref[0, g].astype(BF16)

    def body(tail):
        rows = slice(0, ms) if tail else slice(None)
        y_refs = (yas_ref, ybs_ref, ycs_ref, yds_ref) if tail else (yap_ref, ybp_ref, ycp_ref, ydp_ref)
        hn = hn_ref[rows, :]
        acc = None
        for g, y_ref in enumerate(y_refs):
            gate = jax.nn.sigmoid(jnp.dot(hn, wgbf_ref[g], preferred_element_type=F32) + bg_ref[0, g:g + 1, :])
            br = jnp.dot(y_ref[...], wbbf_ref[g], preferred_element_type=F32)
            acc = gate * br if acc is None else acc + gate * br
        o_ref[rows, :] = acc.astype(o_ref.dtype)

    _split_tiles(1, body)


def gate_merge(hn, ys_p, ys_s, w_gate, b_gate, w_branch, layer, tm, tn):
    m, k = hn.shape
    ms = ys_s[0].shape[0]
    nf = (m - ms) // tm
    depth = w_gate.shape[0]
    d = w_branch.shape[3]
    nj = d // tn
    wg_specs = [pl.BlockSpec((1, k, tn), functools.partial(lambda j, i, g: (layer, 0, g * nj + j), g=g))
                for g in range(4)]
    yp_spec = pl.BlockSpec((tm, BR_W), lambda j, i: (_prompt_tile(i), 0))
    ys_spec = pl.BlockSpec((ms, BR_W), lambda j, i: (0, 0))
    return pl.pallas_call(
        _gate_merge_kernel,
        grid=(nj, nf + 1),
        in_specs=[pl.BlockSpec((tm, k), lambda j, i: (_joint_tile(i, nf), 0)), *[yp_spec] * 4, *[ys_spec] * 4,
                  *wg_specs,
                  pl.BlockSpec((1, 4, BR_W, tn), lambda j, i: (layer, 0, 0, j)),
                  pl.BlockSpec((1, 4, tn), lambda j, i: (layer, 0, j))],
        out_specs=pl.BlockSpec((tm, tn), lambda j, i: (_joint_tile(i, nf), j)),
        out_shape=jax.ShapeDtypeStruct((m, d), BF16),
        scratch_shapes=[pltpu.VMEM((4, k, tn), BF16), pltpu.VMEM((4, BR_W, tn), BF16)],
        compiler_params=_params(2),
        name="gate_merge",
    )(hn, *ys_p, *ys_s, w_gate, w_gate, w_gate, w_gate, w_branch, b_gate.reshape(depth, 4, d))


def _gmlp_kernel(pu_ref, pv_ref, lg_ref, lb_ref, ws_ref, bst_ref, ya_ref, *maybe_va_ref, chunk):
    rows = pu_ref.shape[0]
    u = jax.nn.gelu(pu_ref[...])
    v = jax.nn.gelu(pv_ref[...])
    mu = jnp.mean(v, axis=-1, keepdims=True)
    var = jnp.mean(jnp.square(v - mu), axis=-1, keepdims=True)
    vn = (v - mu) * lax.rsqrt(var + 1e-5) * lg_ref[...] + lb_ref[...]
    for va_ref in maybe_va_ref:
        va_ref[...] = vn
    causal = (lax.broadcasted_iota(jnp.int32, (CHUNK, CHUNK), 0)
              >= lax.broadcasted_iota(jnp.int32, (CHUNK, CHUNK), 1))
    for g in range(4):
        wm = jnp.where(causal, ws_ref[0, g], 0.0).astype(BF16)
        bias = bst_ref[0, :, g:g + 1]
        for c in range(rows // chunk):
            r0 = c * chunk
            vc = vn[r0:r0 + chunk, g * GW:(g + 1) * GW]
            if chunk < CHUNK:
                vc = jnp.concatenate([vc, jnp.zeros((CHUNK - chunk, GW), F32)], axis=0)
            mix = (jnp.dot(wm, vc.astype(BF16), preferred_element_type=F32) + bias)[:chunk]
            ya_ref[r0:r0 + chunk, g * GW:(g + 1) * GW] = (
                u[r0:r0 + chunk, g * GW:(g + 1) * GW] * mix).astype(ya_ref.dtype)


def gmlp(p, ln_g, ln_b, ws, bst, row0, m, tr, chunk, emit_v):
    blk0 = row0 // tr
    in_spec = lambda col: pl.BlockSpec((tr, BR_W), lambda i: (blk0 + i, col))
    out_spec = pl.BlockSpec((tr, BR_W), lambda i: (i, 0))
    n_out = 2 if emit_v else 1
    return pl.pallas_call(
        functools.partial(_gmlp_kernel, chunk=chunk),
        grid=(m // tr,),
        in_specs=[in_spec(0), in_spec(1),
                  pl.BlockSpec((1, BR_W), lambda i: (0, 0)),
                  pl.BlockSpec((1, BR_W), lambda i: (0, 0)),
                  pl.BlockSpec((1, 4, CHUNK, CHUNK), lambda i: (0, 0, 0, 0)),
                  pl.BlockSpec((1, CHUNK, 4), lambda i: (0, 0, 0))],
        out_specs=[out_spec] * n_out,
        out_shape=[jax.ShapeDtypeStruct((m, BR_W), BF16), jax.ShapeDtypeStruct((m, BR_W), F32)][:n_out],
        compiler_params=_params(1),
        name="gmlp",
    )(p, p, ln_g, ln_b, ws, bst)


def _bias_from_buckets(idx, rel_ref, col, shape):
    bias = jnp.full(shape, NEG, F32)
    for b in range(N_BUCKETS):
        bias = jnp.where(idx == b, rel_ref[b, col], bias)
    return bias


def _bias_mask_kernel(rel_ref, idx_ref, o_ref):
    o_ref[0, 0] = _bias_from_buckets(idx_ref[0], rel_ref, pl.program_id(0) * N_HEADS + pl.program_id(1),
                                     (NK, 2 * NK))


def prompt_bias_mask(rel_bias):
    return pl.pallas_call(
        _bias_mask_kernel,
        grid=(len(PATTERNS), N_HEADS),
        in_specs=[pl.BlockSpec(memory_space=pltpu.SMEM),
                  pl.BlockSpec((1, NK, 2 * NK), lambda g, h: (g, 0, 0))],
        out_specs=pl.BlockSpec((1, 1, NK, 2 * NK), lambda g, h: (g, h, 0, 0)),
        out_shape=jax.ShapeDtypeStruct((len(PATTERNS), N_HEADS, NK, 2 * NK), F32),
        compiler_params=_params(2),
        name="prompt_bias_mask",
    )(rel_bias, jnp.asarray(_prompt_buckets()))


def _attn_prompt_kernel(q_ref, k_ref, v_ref, bm_ref, o_ref,
                        o0_ref, o1_ref, o2_ref, e0_ref, e1_ref, e2_ref):
    t = q_ref.shape[0]
    group = pl.program_id(2)
    scale = GW ** -0.5
    outs = (o0_ref, o1_ref, o2_ref)
    lses = (e0_ref, e1_ref, e2_ref)

    def rows(start, n, dil):
        return pl.ds(start, n) if dil == 1 else pl.ds(start, n, stride=dil)

    def blocks(gi, specs):
        dil = PATTERNS[gi][1]
        staged = []
        for first_block, q0, k0 in specs:
            nkeys = NK if first_block else 2 * NK
            qi = rows(q0, NK, dil)
            ki = rows(k0, nkeys, dil)
            q = q_ref[qi, :].astype(BF16)
            kk = k_ref[ki, :].astype(BF16)
            bm = bm_ref[0, 0, :, NK:] if first_block else bm_ref[0, 0]
            s = lax.dot_general(q, kk, (((1,), (1,)), ((), ())), preferred_element_type=F32) * scale + bm
            staged.append((qi, ki, s))
        probs = []
        for qi, ki, s in staged:
            m = jnp.max(s, axis=-1, keepdims=True)
            p = jnp.exp(s - m)
            l = jnp.sum(p, axis=-1, keepdims=True)
            probs.append((qi, ki, p.astype(BF16), m, l))
        for qi, ki, p, m, l in probs:
            acc = jnp.dot(p, v_ref[ki, :].astype(BF16), preferred_element_type=F32)
            outs[gi][qi, :] = acc * (1.0 / l)
            lses[gi][qi, :] = jnp.broadcast_to(m + jnp.log(l), (NK, GW))

    def run_group(gi):
        dil = PATTERNS[gi][1]
        span = NK * dil
        nb = t // span
        u = ATTN_BLOCKS_PER_STEP
        if dil == 1:
            blocks(gi, [(True, 0, 0)] + [(False, n * span, (n - 1) * span) for n in range(1, u)])

            def step(i, c):
                q0 = pl.multiple_of(u * i * span, span)
                blocks(gi, [(False, q0 + n * span, q0 + (n - 1) * span) for n in range(u)])
                return c

            lax.fori_loop(1, nb // u, step, 0)
        else:
            def bunch(i, c):
                res = [u * i + r for r in range(u)]
                blocks(gi, [(True, r, r) for r in res])

                def later(n, cc):
                    blocks(gi, [(False, n * span + r, (n - 1) * span + r) for r in res])
                    return cc

                return lax.fori_loop(1, nb, later, c)

            if dil == u:
                bunch(0, 0)
            else:
                lax.fori_loop(0, dil // u, bunch, 0)

    for gi in range(len(PATTERNS)):
        pl.when(group == gi)(functools.partial(run_group, gi))

    @pl.when(group == len(PATTERNS) - 1)
    def _():
        step = 256

        def merge(c, carry):
            sl = pl.ds(pl.multiple_of(c * step, step), step)
            e = [r[sl, :] for r in lses]
            top = jnp.maximum(jnp.maximum(e[0], e[1]), e[2])
            w = [jnp.exp(x - top) for x in e]
            num = w[0] * o0_ref[sl, :] + w[1] * o1_ref[sl, :] + w[2] * o2_ref[sl, :]
            o_ref[sl, :] = (num / (w[0] + w[1] + w[2])).astype(o_ref.dtype)
            return carry

        lax.fori_loop(0, t // step, merge, 0)


def _prompt_buckets():
    i = np.arange(NK)[:, None]
    j = np.arange(2 * NK)[None, :]
    diff = NK + i - j
    valid = (diff >= 0) & (diff <= NK)
    return np.stack([np.where(valid, _t5_bucket(diff * dil), -1) for _, dil in PATTERNS]).astype(np.int32)


def attention_prompt(p, bias_mask, nseq, t):
    def qkv_spec(which):
        return pl.BlockSpec((t, GW), lambda b, h, g: (b, COL_B // GW + g * 12 + which * 4 + h))

    return pl.pallas_call(
        _attn_prompt_kernel,
        grid=(nseq, N_HEADS, len(PATTERNS)),
        in_specs=[qkv_spec(0), qkv_spec(1), qkv_spec(2),
                  pl.BlockSpec((1, 1, NK, 2 * NK), lambda b, h, g: (g, h, 0, 0))],
        out_specs=pl.BlockSpec((t, GW), lambda b, h, g: (b, h)),
        out_shape=jax.ShapeDtypeStruct((nseq * t, BR_W), BF16),
        scratch_shapes=[pltpu.VMEM((t, GW), F32)] * 6,
        compiler_params=_params(3),
        name="attention_prompt",
    )(p, p, p, bias_mask)


N_NEW = 4
TILE_ROWS = 2 * N_HEADS


def _key_to_value_rows(x):
    n, r, w = x.shape
    return pltpu.roll(x.reshape(n * r, w), N_HEADS, axis=0).reshape(n, r, w)


def _attend(x, xn, q, bias, bias_n, scale):
    s = jnp.sum(x * q[None], axis=-1, keepdims=True) * scale + bias
    sn = jnp.sum(xn * q[None], axis=-1, keepdims=True) * scale + bias_n
    m = jnp.maximum(jnp.max(s, axis=0, keepdims=True), jnp.max(sn, axis=0, keepdims=True))
    p = jnp.exp(s - m)
    pn = jnp.exp(sn - m)
    l = jnp.sum(p, axis=0, keepdims=True) + jnp.sum(pn, axis=0, keepdims=True)
    acc = jnp.sum(_key_to_value_rows(p) * x, axis=0) + jnp.sum(_key_to_value_rows(pn) * xn, axis=0)
    return acc, m, l


def _attn_sample_kernel(rel_ref, q_ref, xn_ref, c1_ref, c2_ref, c3_ref, o_ref,
                        b1_ref, b1n_ref, b23_ref, b23n_ref):
    scale = GW ** -0.5
    rows = N_NEW * TILE_ROWS

    @pl.when(pl.program_id(0) == 0)
    def _():
        head = lax.broadcasted_iota(jnp.int32, (rows, GW), 0) % N_HEADS
        neg = jnp.full((rows, GW), NEG, F32)

        def tiles_for(gi):
            out = []
            for b in range(N_BUCKETS):
                v = [rel_ref[b, gi * N_HEADS + h] for h in range(N_HEADS)]
                out.append(jnp.where(head == 0, v[0], jnp.where(head == 1, v[1], jnp.where(head == 2, v[2], v[3]))))
            return out

        t1 = tiles_for(0)
        for t in range(N_NEW):
            for pos in range(NK):
                step = NK + t - pos
                tile = t1[int(_t5_bucket(np.int64(step)))] if step <= NK else neg
                b1_ref[t, pos] = tile[:TILE_ROWS]
            for j in range(N_NEW):
                tile = t1[int(_t5_bucket(np.int64(t - j)))] if j <= t else neg
                b1n_ref[t, j] = tile[:TILE_ROWS]
        for gi in (1, 2):
            dil = PATTERNS[gi][1]
            tg = tiles_for(gi)
            for jj in range(NK):
                b23_ref[gi - 1, jj] = tg[int(_t5_bucket(np.int64((NK - jj) * dil)))]
            b23n_ref[gi - 1] = tg[0]

    outs, lses = [], []
    x1 = c1_ref[0, 0]
    xn1 = xn_ref[0, 0].reshape(N_NEW, TILE_ROWS, GW)
    o1, e1 = [], []
    for t in range(N_NEW):
        q = q_ref[0, 0, t * TILE_ROWS:(t + 1) * TILE_ROWS, :]
        acc, m, l = _attend(x1, xn1, q, b1_ref[t], b1n_ref[t], scale)
        o1.append(acc)
        e1.append((m, l))
    outs.append(jnp.concatenate(o1, axis=0))
    lses.append((jnp.concatenate([m[0] for m, _ in e1], axis=0), jnp.concatenate([l[0] for _, l in e1], axis=0)))
    for gi, c_ref in ((1, c2_ref), (2, c3_ref)):
        acc, m, l = _attend(c_ref[0, 0], xn_ref[gi, 0][None], q_ref[gi, 0], b23_ref[gi - 1],
                            b23n_ref[gi - 1][None], scale)
        outs.append(acc)
        lses.append((m[0], l[0]))

    lse = [_key_to_value_rows((m + jnp.log(l))[None])[0] for m, l in lses]
    den = [_key_to_value_rows(l[None])[0] for _, l in lses]
    top = jnp.maximum(jnp.maximum(lse[0], lse[1]), lse[2])
    w = [jnp.exp(e - top) for e in lse]
    num = w[0] * outs[0] / den[0] + w[1] * outs[1] / den[1] + w[2] * outs[2] / den[2]
    o_ref[0] = num / (w[0] + w[1] + w[2])


def attention_sample(p, cache_b1, cache_b2, cache_b3, rel_bias, layer, row0, nseq):
    depth = cache_b1.shape[0]
    rows = N_NEW * TILE_ROWS
    ps = p[row0:].reshape(nseq, SAMPLE_ROWS, N_IN)[:, :N_NEW]
    qs, xns = [], []
    for gi in range(len(PATTERNS)):
        c0 = COL_B + gi * 3 * BR_W
        q = ps[:, :, c0:c0 + BR_W].reshape(nseq, N_NEW, N_HEADS, GW)
        qs.append(jnp.concatenate([q, jnp.zeros_like(q)], axis=2).reshape(nseq, rows, GW))
        xns.append(ps[:, :, c0 + BR_W:c0 + 3 * BR_W].reshape(nseq, rows, GW))
    q_all = jnp.stack(qs)
    xn_all = jnp.stack(xns)
    c1 = cache_b1.reshape(depth, nseq, NK, TILE_ROWS, GW)
    c2 = cache_b2.reshape(depth, nseq, NK, 4 * TILE_ROWS, GW)
    c3 = cache_b3.reshape(depth, nseq, NK, 16 * TILE_ROWS, GW)
    cache_spec = lambda r: pl.BlockSpec((1, 1, NK, r, GW), lambda b: (layer, b, 0, 0, 0))
    tok_spec = pl.BlockSpec((len(PATTERNS), 1, rows, GW), lambda b: (0, b, 0, 0))
    y = pl.pallas_call(
        _attn_sample_kernel,
        grid=(nseq,),
        in_specs=[pl.BlockSpec(memory_space=pltpu.SMEM), tok_spec, tok_spec,
                  cache_spec(TILE_ROWS), cache_spec(rows), cache_spec(rows)],
        out_specs=pl.BlockSpec((1, rows, GW), lambda b: (b, 0, 0)),
        out_shape=jax.ShapeDtypeStruct((nseq, rows, GW), F32),
        scratch_shapes=[pltpu.VMEM((N_NEW, NK, TILE_ROWS, GW), F32), pltpu.VMEM((N_NEW, N_NEW, TILE_ROWS, GW), F32),
                        pltpu.VMEM((2, NK, rows, GW), F32), pltpu.VMEM((2, rows, GW), F32)],
        compiler_params=_params(1),
        name="attention_sample",
    )(rel_bias, q_all, xn_all, c1, c2, c3)
    y = y.reshape(nseq, N_NEW, 2, BR_W)[:, :, 1]
    y = jnp.pad(y, ((0, 0), (0, SAMPLE_ROWS - N_NEW), (0, 0)))
    return y.reshape(nseq * SAMPLE_ROWS, BR_W).astype(BF16)


def _pool_conv_kernel(pc_ref, pbg_ref, pcg_ref, phs_ref, hc_ref, hd_ref, pw_ref, pb_ref, ps_ref, cw_ref,
                      yc_ref, yd_ref, zt_ref, cbuf_ref, zbuf_ref, *, start):
    tr = pc_ref.shape[0]
    j = pl.program_id(1)
    hc = 16
    hz = 8

    @pl.when(j == 0)
    def _():
        cbuf_ref[0:hc, :] = hc_ref[0]
        zbuf_ref[0:hz, :] = hd_ref[0]

    x = pc_ref[...]
    cbuf_ref[hc:hc + tr, :] = x
    pos = start + j * tr + lax.broadcasted_iota(jnp.int32, (tr, 1), 0)
    for gi, win in enumerate(POOL_WINDOWS):
        cols = slice(gi * GW, (gi + 1) * GW)
        total = x[:, cols]
        for back in range(1, win):
            total = total + cbuf_ref[hc - back:hc - back + tr, cols]
        cnt = jnp.minimum(pos + 1, win).astype(F32)
        pooled = total / cnt - x[:, cols]
        y = jnp.dot(pooled.astype(BF16), pw_ref[0, gi].astype(BF16), preferred_element_type=F32)
        yc_ref[:, cols] = ((y + pb_ref[:, cols]) * ps_ref[:, cols]).astype(yc_ref.dtype)
    cbuf_ref[0:hc, :] = cbuf_ref[tr:tr + hc, :]

    z = pcg_ref[...] * phs_ref[...]
    zbuf_ref[hz:hz + tr, :] = z
    y = (zbuf_ref[hz - 2:hz - 2 + tr, :] * cw_ref[0:1, :] + zbuf_ref[hz - 1:hz - 1 + tr, :] * cw_ref[1:2, :]
         + z * cw_ref[2:3, :])
    yd_ref[...] = (pbg_ref[...] * y).astype(yd_ref.dtype)
    zt_ref[0] = z[tr - hz:, :]
    zbuf_ref[0:hz, :] = zbuf_ref[tr:tr + hz, :]


def pool_conv(p, hist_c, hist_d, pool_w, pool_b, pool_scale, conv_wt, layer, row0, nseq, t, tr, start):
    per = t // tr
    blk0 = row0 // tr
    col = lambda c: pl.BlockSpec((tr, BR_W), lambda b, j: (b * per + j, c))
    pcol = lambda c: pl.BlockSpec((tr, BR_W), lambda b, j: (blk0 + b * per + j, c))
    vec = pl.BlockSpec((1, BR_W), lambda b, j: (0, 0))
    return pl.pallas_call(
        functools.partial(_pool_conv_kernel, start=start),
        grid=(nseq, per),
        in_specs=[pcol(COL_C // BR_W), pcol(COL_D // BR_W), pcol(COL_D // BR_W + 1), pcol(COL_D // BR_W + 2),
                  pl.BlockSpec((1, 16, BR_W), lambda b, j: (b, 0, 0)),
                  pl.BlockSpec((1, 8, BR_W), lambda b, j: (b, 0, 0)),
                  pl.BlockSpec((1, 4, GW, GW), lambda b, j: (layer, 0, 0, 0)),
                  vec, vec,
                  pl.BlockSpec((CONV_W, BR_W), lambda b, j: (0, 0))],
        out_specs=[col(0), col(0), pl.BlockSpec((1, 8, BR_W), lambda b, j: (b, 0, 0))],
        out_shape=[jax.ShapeDtypeStruct((nseq * t, BR_W), BF16),
                   jax.ShapeDtypeStruct((nseq * t, BR_W), BF16),
                   jax.ShapeDtypeStruct((nseq, 8, BR_W), F32)],
        scratch_shapes=[pltpu.VMEM((16 + tr, BR_W), F32), pltpu.VMEM((8 + tr, BR_W), F32)],
        compiler_params=_params(2),
        name="pool_conv",
    )(p, p, p, p, hist_c, hist_d, pool_w, pool_b, pool_scale, conv_wt)


CACHE_ROWS = 512


def _cache_writer_kernel(k1_ref, v1_ref, k2_ref, v2_ref, k3_ref, v3_ref, b1_any, b2_any, b3_any,
                         o1_ref, o2_ref, o3_ref):
    def scatter(o_ref, k_ref, v_ref, row0, n):
        for kv, src in enumerate((k_ref, v_ref)):
            for h in range(N_HEADS):
                o_ref[0, 0, pl.ds(kv * N_HEADS + h, n, stride=TILE_ROWS), :] = src[row0:row0 + n, h * GW:(h + 1) * GW]

    scatter(o3_ref, k3_ref, v3_ref, 0, CACHE_ROWS)

    @pl.when(pl.program_id(1) == pl.num_programs(1) - 1)
    def _():
        scatter(o2_ref, k2_ref, v2_ref, CACHE_ROWS - PATTERNS[1][0], PATTERNS[1][0])
        scatter(o1_ref, k1_ref, v1_ref, CACHE_ROWS - PATTERNS[0][0], PATTERNS[0][0])


def cache_writer(p, bufs, layer, nseq, t):
    b1, b2, b3 = bufs
    per = t // CACHE_ROWS
    steps = PATTERNS[2][0] // CACHE_ROWS

    def slab(gi, which, whole_window):
        col = (COL_B + gi * 3 * BR_W + (1 + which) * BR_W) // BR_W
        if whole_window:
            return pl.BlockSpec((CACHE_ROWS, BR_W), lambda b, j: ((b + 1) * per - steps + j, col))
        return pl.BlockSpec((CACHE_ROWS, BR_W), lambda b, j: ((b + 1) * per - 1, col))

    any_spec = pl.BlockSpec(memory_space=pl.ANY)
    tile = lambda rows: (1, 1, rows * TILE_ROWS, GW)
    return pl.pallas_call(
        _cache_writer_kernel,
        grid=(nseq, steps),
        in_specs=[slab(0, 0, False), slab(0, 1, False), slab(1, 0, False), slab(1, 1, False),
                  slab(2, 0, True), slab(2, 1, True),
                  any_spec, any_spec, any_spec],
        out_specs=[pl.BlockSpec(tile(PATTERNS[0][0]), lambda b, j: (layer, b, 0, 0)),
                   pl.BlockSpec(tile(PATTERNS[1][0]), lambda b, j: (layer, b, 0, 0)),
                   pl.BlockSpec(tile(CACHE_ROWS), lambda b, j: (layer, b, j, 0))],
        out_shape=[jax.ShapeDtypeStruct(x.shape, x.dtype) for x in bufs],
        input_output_aliases={6: 0, 7: 1, 8: 2},
        compiler_params=_params(2),
        name="cache_writer",
    )(p, p, p, p, p, p, b1, b2, b3)


TM_WIDE = 1024
TM_ROWS = 512


def kernel(x_prompt, x_sample, c_prompt, c_sample, cache_b1, cache_b2, cache_b3, cache_pool, cache_conv,
           ln1_g, ln2_g, w_ada, b_ada, w_in, w_gate, b_gate, a_ln_g, a_ln_b, a_ws, a_bs,
           rel_bias, pool_w, pool_b, pool_scale, conv_w, w_branch, w_out, w1, w3, w2, final_g):
    nb, seq, d = x_prompt.shape
    ns, dec = x_sample.shape[:2]
    depth = w_in.shape[0]
    mp = nb * seq
    ms = ns * SAMPLE_ROWS

    c_all = jnp.concatenate([c_prompt, c_sample], axis=0)
    c_rows = -(-c_all.shape[0] // 8) * 8
    c_all = jnp.pad(c_all, ((0, c_rows - c_all.shape[0]), (0, 0)))
    mod = ada_modulation(c_all, w_ada, b_ada).reshape(depth, c_rows, 6, d)
    mods = [[(mod[l, :nb, i][:, None, :], jnp.repeat(mod[l, nb:nb + ns, i], SAMPLE_ROWS, axis=0)[None])
             for i in range(6)] for l in range(depth)]

    xp = x_prompt.reshape(mp, d)
    xs = jnp.pad(x_sample, ((0, 0), (0, SAMPLE_ROWS - dec), (0, 0))).reshape(ms, d)
    zero_hist_c = jnp.zeros((nb, 16, BR_W), F32)
    zero_hist_d = jnp.zeros((nb, 8, BR_W), F32)

    bias_mask = prompt_bias_mask(rel_bias)
    kv_bufs = tuple(jnp.zeros((depth, nb, win * TILE_ROWS, GW), F32) for win, _ in PATTERNS)
    kv_s = [[] for _ in PATTERNS]
    pool_p, conv_p, pool_s, conv_s, chunk_s = [], [], [], [], []
    for l in range(depth):
        sh1, sc1, g1, sh2, sc2, g2 = mods[l]
        hn = norm_in(xp, xs, ln1_g[l][None], sc1, sh1, TM_ROWS)
        p = matmul(hn, w_in, l, ms, TM_WIDE, 1280, F32)

        gm = (a_ln_g[l][None], a_ln_b[l][None], a_ws[l][None], a_bs[l].T[None])
        ya_p, = gmlp(p, *gm, 0, mp, 512, CHUNK, False)
        ya_s, va_s = gmlp(p, *gm, mp, ms, ms, SAMPLE_ROWS, True)
        yb_p = attention_prompt(p, bias_mask, nb, seq)
        yb_s = attention_sample(p, cache_b1, cache_b2, cache_b3, rel_bias, l, mp, ns)
        pc = (pool_w, pool_b[l][None], pool_scale[l][None], conv_w[l].T, l)
        yc_p, yd_p, zt_p = pool_conv(p, zero_hist_c, zero_hist_d, *pc, 0, nb, seq, 512, 0)
        hist_c = jnp.pad(cache_pool[l], ((0, 0), (1, 0), (0, 0)))
        hist_d = jnp.pad(cache_conv[l], ((0, 0), (8 - (CONV_W - 1), 0), (0, 0)))
        yc_s, yd_s, zt_s = pool_conv(p, hist_c, hist_d, *pc, mp, ns, SAMPLE_ROWS, SAMPLE_ROWS, PAST_LEN)

        merged = gate_merge(hn, (ya_p, yb_p, yc_p, yd_p), (ya_s, yb_s, yc_s, yd_s),
                            w_gate, b_gate, w_branch, l, TM_WIDE, 256)
        xp, xs, hn2 = out_proj_norm(merged, w_out, l, xp, xs, g1, ln2_g[l][None], sc2, sh2, TM_ROWS)
        hmid = swiglu_up(hn2, w1, w3, l, ms, TM_WIDE, 512)
        xp, xs = matmul_residual(hmid, w2, l, xp, xs, g2, TM_ROWS, 512)

        kv_bufs = cache_writer(p, kv_bufs, l, nb, seq)
        ps3 = p[mp:].reshape(ns, SAMPLE_ROWS, N_IN)
        for gi in range(len(PATTERNS)):
            c0 = COL_B + gi * 3 * BR_W + BR_W
            kv_s[gi].append(ps3[:, :dec, c0:c0 + 2 * BR_W].reshape(ns, dec, 2, N_HEADS, GW))
        pool_p.append(jnp.stack([p[(b + 1) * seq - POOL_HIST:(b + 1) * seq, COL_C:COL_C + BR_W]
                                 for b in range(nb)]))
        conv_p.append(zt_p[:, 8 - (CONV_W - 1):])
        pool_s.append(ps3[:, :dec, COL_C:COL_C + BR_W])
        conv_s.append(zt_s[:, :dec])
        chunk_s.append(va_s.reshape(ns, SAMPLE_ROWS, BR_W)[:, :dec])

    yp, ys = final_norm(xp, xs, final_g[None], TM_ROWS)
    stack = lambda parts: jnp.stack(parts, axis=0)
    kv_p = [buf.reshape(depth, nb, win, 2, N_HEADS, GW) for buf, (win, _) in zip(kv_bufs, PATTERNS)]
    return (yp.reshape(nb, seq, d), ys.reshape(ns, SAMPLE_ROWS, d)[:, :dec],
            kv_p[0], kv_p[1], kv_p[2], stack(pool_p), stack(conv_p),
            stack(kv_s[0]), stack(kv_s[1]), stack(kv_s[2]), stack(pool_s), stack(conv_s), stack(chunk_s))
```

```python
import functools
import math

import jax
import jax.numpy as jnp
import numpy as np
from jax import lax
from jax.experimental import pallas as pl
from jax.experimental.pallas import tpu as pltpu

F32 = jnp.float32
BF16 = jnp.bfloat16

D_MODEL = 2048
BR_W = 512
GW = 128
N_HEADS = 4
CHUNK = 128
PATTERNS = ((128, 1), (512, 4), (2048, 16))
NK = 128
POOL_WINDOWS = (2, 4, 8, 16)
POOL_HIST = 15
CONV_W = 3
N_BUCKETS = 32
MAX_DIST = 2048
D_FF = 5632
N_IN = 7680
COL_A = 0
COL_B = 2 * BR_W
COL_C = COL_B + 9 * BR_W
COL_D = COL_C + BR_W
EPS = 1e-6
NEG = -1e30
SAMPLE_ROWS = 8
PAST_LEN = 16384
ATTN_BLOCKS_PER_STEP = 4

VMEM_LIMIT = 56 * 1024 * 1024


def _t5_bucket(dist):
    max_exact = N_BUCKETS // 2
    n = np.maximum(dist, 0)
    nf = np.maximum(n, 1).astype(np.float32)
    large = max_exact + (np.log(nf / np.float32(max_exact)) / np.float32(math.log(MAX_DIST / max_exact))
                         * np.float32(N_BUCKETS - max_exact)).astype(np.int32)
    large = np.minimum(large, N_BUCKETS - 1)
    return np.where(n < max_exact, n, large)


def _params(n_axes):
    return pltpu.CompilerParams(dimension_semantics=("arbitrary",) * n_axes,
                                vmem_limit_bytes=VMEM_LIMIT)


def _ada_kernel(c_ref, w_ref, b_ref, o_ref):
    a = jax.nn.silu(c_ref[...]).astype(BF16)
    o_ref[0] = jnp.dot(a, w_ref[0].astype(BF16), preferred_element_type=F32) + b_ref[0]


def ada_modulation(c_all, w_ada, b_ada):
    depth, d, n = w_ada.shape
    r = c_all.shape[0]
    tn = 1024
    return pl.pallas_call(
        _ada_kernel,
        grid=(depth, n // tn),
        in_specs=[pl.BlockSpec((r, d), lambda l, j: (0, 0)),
                  pl.BlockSpec((1, d, tn), lambda l, j: (l, 0, j)),
                  pl.BlockSpec((1, 1, tn), lambda l, j: (l, 0, j))],
        out_specs=pl.BlockSpec((1, r, tn), lambda l, j: (l, 0, j)),
        out_shape=jax.ShapeDtypeStruct((depth, r, n), F32),
        compiler_params=_params(2),
        name="ada_modulation",
    )(c_all, w_ada, b_ada.reshape(depth, 1, n))


def _split_tiles(axis, body):
    i = pl.program_id(axis)
    pl.when(i == 0)(functools.partial(body, True))
    pl.when(i > 0)(functools.partial(body, False))


def _joint_tile(i, nf):
    return (i + nf) % (nf + 1)


def _prompt_tile(i):
    return jnp.maximum(i - 1, 0)


def _norm_rows(x, g, sc, sh):
    y = x * lax.rsqrt(jnp.mean(x * x, axis=-1, keepdims=True) + EPS) * g
    return y * (1.0 + sc) + sh


def _mod_specs(pair, width, per, grid_rank):
    prompt, sample = pair
    nb, ms = prompt.shape[0], sample.shape[1]
    if grid_rank == 1:
        return [pl.BlockSpec((1, 1, width), lambda i: (_prompt_tile(i) // per, 0, 0)),
                pl.BlockSpec((1, ms, width), lambda i: (0, 0, 0))]
    return [pl.BlockSpec((1, 1, width), lambda j, i: (_prompt_tile(i) // per, 0, j)),
            pl.BlockSpec((1, ms, width), lambda j, i: (0, 0, j))]


def _norm_in_kernel(xp_ref, xs_ref, g_ref, scp_ref, scs_ref, shp_ref, shs_ref, o_ref):
    ms = xs_ref.shape[0]

    def body(tail):
        if tail:
            o_ref[0:ms, :] = _norm_rows(xs_ref[...], g_ref[...], scs_ref[0], shs_ref[0]).astype(o_ref.dtype)
        else:
            o_ref[...] = _norm_rows(xp_ref[...], g_ref[...], scp_ref[0], shp_ref[0]).astype(o_ref.dtype)

    _split_tiles(0, body)


def norm_in(xp, xs, g, sc, sh, tm):
    mp, d = xp.shape
    ms = xs.shape[0]
    nf = mp // tm
    per = nf // sc[0].shape[0]
    return pl.pallas_call(
        _norm_in_kernel,
        grid=(nf + 1,),
        in_specs=[pl.BlockSpec((tm, d), lambda i: (_prompt_tile(i), 0)),
                  pl.BlockSpec((ms, d), lambda i: (0, 0)),
                  pl.BlockSpec((1, d), lambda i: (0, 0)),
                  *_mod_specs(sc, d, per, 1), *_mod_specs(sh, d, per, 1)],
        out_specs=pl.BlockSpec((tm, d), lambda i: (_joint_tile(i, nf), 0)),
        out_shape=jax.ShapeDtypeStruct((mp + ms, d), BF16),
        compiler_params=_params(1),
        name="norm_in",
    )(xp, xs, g, *sc, *sh)


def _final_norm_kernel(xp_ref, xs_ref, g_ref, yp_ref, ys_ref):
    def body(tail):
        src, dst = (xs_ref, ys_ref) if tail else (xp_ref, yp_ref)
        x = src[...]
        dst[...] = x * lax.rsqrt(jnp.mean(x * x, axis=-1, keepdims=True) + EPS) * g_ref[...]

    _split_tiles(0, body)


def final_norm(xp, xs, g, tm):
    mp, d = xp.shape
    ms = xs.shape[0]
    nf = mp // tm
    p_spec = pl.BlockSpec((tm, d), lambda i: (_prompt_tile(i), 0))
    s_spec = pl.BlockSpec((ms, d), lambda i: (0, 0))
    return pl.pallas_call(
        _final_norm_kernel,
        grid=(nf + 1,),
        in_specs=[p_spec, s_spec, pl.BlockSpec((1, d), lambda i: (0, 0))],
        out_specs=[p_spec, s_spec],
        out_shape=[jax.ShapeDtypeStruct((mp, d), F32), jax.ShapeDtypeStruct((ms, d), F32)],
        compiler_params=_params(1),
        name="final_norm",
    )(xp, xs, g)


def _mm_kernel(a_ref, w_ref, o_ref, wbf_ref, *, ms):
    @pl.when(pl.program_id(1) == 0)
    def _():
        wbf_ref[...] = w_ref[0].astype(BF16)

    def body(tail):
        rows = slice(0, ms) if tail else slice(None)
        o_ref[rows, :] = jnp.dot(a_ref[rows, :], wbf_ref[...], preferred_element_type=F32).astype(o_ref.dtype)

    _split_tiles(1, body)


def matmul(a, w, layer, ms, tm, tn, out_dtype):
    m, k = a.shape
    n = w.shape[2]
    nf = (m - ms) // tm
    return pl.pallas_call(
        functools.partial(_mm_kernel, ms=ms),
        grid=(n // tn, nf + 1),
        in_specs=[pl.BlockSpec((tm, k), lambda j, i: (_joint_tile(i, nf), 0)),
                  pl.BlockSpec((1, k, tn), lambda j, i: (layer, 0, j))],
        out_specs=pl.BlockSpec((tm, tn), lambda j, i: (_joint_tile(i, nf), j)),
        out_shape=jax.ShapeDtypeStruct((m, n), out_dtype),
        scratch_shapes=[pltpu.VMEM((k, tn), BF16)],
        compiler_params=_params(2),
        name="matmul",
    )(a, w)


def _mm_residual_kernel(a_ref, w_ref, xp_ref, xs_ref, gp_ref, gs_ref, op_ref, os_ref, wbf_ref):
    ms = xs_ref.shape[0]

    @pl.when(pl.program_id(1) == 0)
    def _():
        wbf_ref[...] = w_ref[0].astype(BF16)

    def body(tail):
        if tail:
            y = jnp.dot(a_ref[0:ms, :], wbf_ref[...], preferred_element_type=F32)
            os_ref[...] = xs_ref[...] + gs_ref[0] * y
        else:
            y = jnp.dot(a_ref[...], wbf_ref[...], preferred_element_type=F32)
            op_ref[...] = xp_ref[...] + gp_ref[0] * y

    _split_tiles(1, body)


def matmul_residual(a, w, layer, xp, xs, gate, tm, tn):
    m, k = a.shape
    n = w.shape[2]
    mp, ms = xp.shape[0], xs.shape[0]
    nf = mp // tm
    per = nf // gate[0].shape[0]
    p_spec = pl.BlockSpec((tm, tn), lambda j, i: (_prompt_tile(i), j))
    s_spec = pl.BlockSpec((ms, tn), lambda j, i: (0, j))
    return pl.pallas_call(
        _mm_residual_kernel,
        grid=(n // tn, nf + 1),
        in_specs=[pl.BlockSpec((tm, k), lambda j, i: (_joint_tile(i, nf), 0)),
                  pl.BlockSpec((1, k, tn), lambda j, i: (layer, 0, j)),
                  p_spec, s_spec, *_mod_specs(gate, tn, per, 2)],
        out_specs=[p_spec, s_spec],
        out_shape=[jax.ShapeDtypeStruct((mp, n), F32), jax.ShapeDtypeStruct((ms, n), F32)],
        scratch_shapes=[pltpu.VMEM((k, tn), BF16)],
        compiler_params=_params(2),
        name="matmul_residual",
    )(a, w, xp, xs, *gate)


def _out_proj_norm_kernel(a_ref, w_ref, xp_ref, xs_ref, gp_ref, gs_ref, ln_ref, scp_ref, scs_ref,
                          shp_ref, shs_ref, op_ref, os_ref, hn_ref, wbf_ref):
    ms = xs_ref.shape[0]

    @pl.when(pl.program_id(0) == 0)
    def _():
        wbf_ref[...] = w_ref[0].astype(BF16)

    def body(tail):
        if tail:
            x = xs_ref[...] + gs_ref[0] * jnp.dot(a_ref[0:ms, :], wbf_ref[...], preferred_element_type=F32)
            os_ref[...] = x
            hn_ref[0:ms, :] = _norm_rows(x, ln_ref[...], scs_ref[0], shs_ref[0]).astype(hn_ref.dtype)
        else:
            x = xp_ref[...] + gp_ref[0] * jnp.dot(a_ref[...], wbf_ref[...], preferred_element_type=F32)
            op_ref[...] = x
            hn_ref[...] = _norm_rows(x, ln_ref[...], scp_ref[0], shp_ref[0]).astype(hn_ref.dtype)

    _split_tiles(0, body)


def out_proj_norm(a, w, layer, xp, xs, gate, ln_g, sc, sh, tm):
    m, k = a.shape
    n = w.shape[2]
    mp, ms = xp.shape[0], xs.shape[0]
    nf = mp // tm
    per = nf // gate[0].shape[0]
    p_spec = pl.BlockSpec((tm, n), lambda i: (_prompt_tile(i), 0))
    s_spec = pl.BlockSpec((ms, n), lambda i: (0, 0))
    return pl.pallas_call(
        _out_proj_norm_kernel,
        grid=(nf + 1,),
        in_specs=[pl.BlockSpec((tm, k), lambda i: (_joint_tile(i, nf), 0)),
                  pl.BlockSpec((1, k, n), lambda i: (layer, 0, 0), pipeline_mode=pl.Buffered(1)),
                  p_spec, s_spec, *_mod_specs(gate, n, per, 1),
                  pl.BlockSpec((1, n), lambda i: (0, 0)),
                  *_mod_specs(sc, n, per, 1), *_mod_specs(sh, n, per, 1)],
        out_specs=[p_spec, s_spec, pl.BlockSpec((tm, n), lambda i: (_joint_tile(i, nf), 0))],
        out_shape=[jax.ShapeDtypeStruct((mp, n), F32), jax.ShapeDtypeStruct((ms, n), F32),
                   jax.ShapeDtypeStruct((m, n), BF16)],
        scratch_shapes=[pltpu.VMEM((k, n), BF16)],
        compiler_params=_params(1),
        name="out_proj_norm",
    )(a, w, xp, xs, *gate, ln_g, *sc, *sh)


def _swiglu_kernel(a_ref, w1_ref, w3_ref, o_ref, w1bf_ref, w3bf_ref, *, ms):
    @pl.when(pl.program_id(1) == 0)
    def _():
        w1bf_ref[...] = w1_ref[0].astype(BF16)
        w3bf_ref[...] = w3_ref[0].astype(BF16)

    def body(tail):
        rows = slice(0, ms) if tail else slice(None)
        a = a_ref[rows, :]
        h1 = jnp.dot(a, w1bf_ref[...], preferred_element_type=F32)
        h3 = jnp.dot(a, w3bf_ref[...], preferred_element_type=F32)
        o_ref[rows, :] = (jax.nn.silu(h1) * h3).astype(o_ref.dtype)

    _split_tiles(1, body)


def swiglu_up(a, w1, w3, layer, ms, tm, tn):
    m, k = a.shape
    n = w1.shape[2]
    nf = (m - ms) // tm
    w_spec = pl.BlockSpec((1, k, tn), lambda j, i: (layer, 0, j))
    return pl.pallas_call(
        functools.partial(_swiglu_kernel, ms=ms),
        grid=(n // tn, nf + 1),
        in_specs=[pl.BlockSpec((tm, k), lambda j, i: (_joint_tile(i, nf), 0)), w_spec, w_spec],
        out_specs=pl.BlockSpec((tm, tn), lambda j, i: (_joint_tile(i, nf), j)),
        out_shape=jax.ShapeDtypeStruct((m, n), BF16),
        scratch_shapes=[pltpu.VMEM((k, tn), BF16), pltpu.VMEM((k, tn), BF16)],
        compiler_params=_params(2),
        name="swiglu_up",
    )(a, w1, w3)


def _gate_merge_kernel(hn_ref, yap_ref, ybp_ref, ycp_ref, ydp_ref, yas_ref, ybs_ref, ycs_ref, yds_ref,
                       wg0_ref, wg1_ref, wg2_ref, wg3_ref, wb_ref, bg_ref, o_ref, wgbf_ref, wbbf_ref):
    ms = yas_ref.shape[0]

    @pl.when(pl.program_id(1) == 0)
    def _():
        for g, wg_ref in enumerate((wg0_ref, wg1_ref, wg2_ref, wg3_ref)):
            wgbf_ref[g] = wg_ref[0].astype(BF16)
            wbbf_ref[g] = wb_ref[0, g].astype(BF16)

    def body(tail):
        rows = slice(0, ms) if tail else slice(None)
        y_refs = (yas_ref, ybs_ref, ycs_ref, yds_ref) if tail else (yap_ref, ybp_ref, ycp_ref, ydp_ref)
        hn = hn_ref[rows, :]
        acc = None
        for g, y_ref in enumerate(y_refs):
            gate = jax.nn.sigmoid(jnp.dot(hn, wgbf_ref[g], preferred_element_type=F32) + bg_ref[0, g:g + 1, :])
            br = jnp.dot(y_ref[...], wbbf_ref[g], preferred_element_type=F32)
            acc = gate * br if acc is None else acc + gate * br
        o_ref[rows, :] = acc.astype(o_ref.dtype)

    _split_tiles(1, body)


def gate_merge(hn, ys_p, ys_s, w_gate, b_gate, w_branch, layer, tm, tn):
    m, k = hn.shape
    ms = ys_s[0].shape[0]
    nf = (m - ms) // tm
    depth = w_gate.shape[0]
    d = w_branch.shape[3]
    nj = d // tn
    wg_specs = [pl.BlockSpec((1, k, tn), functools.partial(lambda j, i, g: (layer, 0, g * nj + j), g=g))
                for g in range(4)]
    yp_spec = pl.BlockSpec((tm, BR_W), lambda j, i: (_prompt_tile(i), 0))
    ys_spec = pl.BlockSpec((ms, BR_W), lambda j, i: (0, 0))
    return pl.pallas_call(
        _gate_merge_kernel,
        grid=(nj, nf + 1),
        in_specs=[pl.BlockSpec((tm, k), lambda j, i: (_joint_tile(i, nf), 0)), *[yp_spec] * 4, *[ys_spec] * 4,
                  *wg_specs,
                  pl.BlockSpec((1, 4, BR_W, tn), lambda j, i: (layer, 0, 0, j)),
                  pl.BlockSpec((1, 4, tn), lambda j, i: (layer, 0, j))],
        out_specs=pl.BlockSpec((tm, tn), lambda j, i: (_joint_tile(i, nf), j)),
        out_shape=jax.ShapeDtypeStruct((m, d), BF16),
        scratch_shapes=[pltpu.VMEM((4, k, tn), BF16), pltpu.VMEM((4, BR_W, tn), BF16)],
        compiler_params=_params(2),
        name="gate_merge",
    )(hn, *ys_p, *ys_s, w_gate, w_gate, w_gate, w_gate, w_branch, b_gate.reshape(depth, 4, d))


def _gmlp_kernel(pu_ref, pv_ref, lg_ref, lb_ref, ws_ref, bst_ref, ya_ref, *maybe_va_ref, chunk):
    rows = pu_ref.shape[0]
    u = jax.nn.gelu(pu_ref[...])
    v = jax.nn.gelu(pv_ref[...])
    mu = jnp.mean(v, axis=-1, keepdims=True)
    var = jnp.mean(jnp.square(v - mu), axis=-1, keepdims=True)
    vn = (v - mu) * lax.rsqrt(var + 1e-5) * lg_ref[...] + lb_ref[...]
    for va_ref in maybe_va_ref:
        va_ref[...] = vn
    causal = (lax.broadcasted_iota(jnp.int32, (CHUNK, CHUNK), 0)
              >= lax.broadcasted_iota(jnp.int32, (CHUNK, CHUNK), 1))
    for g in range(4):
        wm = jnp.where(causal, ws_ref[0, g], 0.0).astype(BF16)
        bias = bst_ref[0, :, g:g + 1]
        for c in range(rows // chunk):
            r0 = c * chunk
            vc = vn[r0:r0 + chunk, g * GW:(g + 1) * GW]
            if chunk < CHUNK:
                vc = jnp.concatenate([vc, jnp.zeros((CHUNK - chunk, GW), F32)], axis=0)
            mix = (jnp.dot(wm, vc.astype(BF16), preferred_element_type=F32) + bias)[:chunk]
            ya_ref[r0:r0 + chunk, g * GW:(g + 1) * GW] = (
                u[r0:r0 + chunk, g * GW:(g + 1) * GW] * mix).astype(ya_ref.dtype)


def gmlp(p, ln_g, ln_b, ws, bst, row0, m, tr, chunk, emit_v):
    blk0 = row0 // tr
    in_spec = lambda col: pl.BlockSpec((tr, BR_W), lambda i: (blk0 + i, col))
    out_spec = pl.BlockSpec((tr, BR_W), lambda i: (i, 0))
    n_out = 2 if emit_v else 1
    return pl.pallas_call(
        functools.partial(_gmlp_kernel, chunk=chunk),
        grid=(m // tr,),
        in_specs=[in_spec(0), in_spec(1),
                  pl.BlockSpec((1, BR_W), lambda i: (0, 0)),
                  pl.BlockSpec((1, BR_W), lambda i: (0, 0)),
                  pl.BlockSpec((1, 4, CHUNK, CHUNK), lambda i: (0, 0, 0, 0)),
                  pl.BlockSpec((1, CHUNK, 4), lambda i: (0, 0, 0))],
        out_specs=[out_spec] * n_out,
        out_shape=[jax.ShapeDtypeStruct((m, BR_W), BF16), jax.ShapeDtypeStruct((m, BR_W), F32)][:n_out],
        compiler_params=_params(1),
        name="gmlp",
    )(p, p, ln_g, ln_b, ws, bst)


def _bias_from_buckets(idx, rel_ref, col, shape):
    bias = jnp.full(shape, NEG, F32)
    for b in range(N_BUCKETS):
        bias = jnp.where(idx == b, rel_ref[b, col], bias)
    return bias


def _bias_mask_kernel(rel_ref, idx_ref, o_ref):
    o_ref[0, 0] = _bias_from_buckets(idx_ref[0], rel_ref, pl.program_id(0) * N_HEADS + pl.program_id(1),
                                     (NK, 2 * NK))


def prompt_bias_mask(rel_bias):
    return pl.pallas_call(
        _bias_mask_kernel,
        grid=(len(PATTERNS), N_HEADS),
        in_specs=[pl.BlockSpec(memory_space=pltpu.SMEM),
                  pl.BlockSpec((1, NK, 2 * NK), lambda g, h: (g, 0, 0))],
        out_specs=pl.BlockSpec((1, 1, NK, 2 * NK), lambda g, h: (g, h, 0, 0)),
        out_shape=jax.ShapeDtypeStruct((len(PATTERNS), N_HEADS, NK, 2 * NK), F32),
        compiler_params=_params(2),
        name="prompt_bias_mask",
    )(rel_bias, jnp.asarray(_prompt_buckets()))


def _attn_prompt_kernel(q_ref, k_ref, v_ref, bm_ref, o_ref,
                        o0_ref, o1_ref, o2_ref, e0_ref, e1_ref, e2_ref):
    t = q_ref.shape[0]
    group = pl.program_id(2)
    scale = GW ** -0.5
    outs = (o0_ref, o1_ref, o2_ref)
    lses = (e0_ref, e1_ref, e2_ref)

    def rows(start, n, dil):
        return pl.ds(start, n) if dil == 1 else pl.ds(start, n, stride=dil)

    def blocks(gi, specs):
        dil = PATTERNS[gi][1]
        staged = []
        for first_block, q0, k0 in specs:
            nkeys = NK if first_block else 2 * NK
            qi = rows(q0, NK, dil)
            ki = rows(k0, nkeys, dil)
            q = q_ref[qi, :].astype(BF16)
            kk = k_ref[ki, :].astype(BF16)
            bm = bm_ref[0, 0, :, NK:] if first_block else bm_ref[0, 0]
            s = lax.dot_general(q, kk, (((1,), (1,)), ((), ())), preferred_element_type=F32) * scale + bm
            staged.append((qi, ki, s))
        probs = []
        for qi, ki, s in staged:
            m = jnp.max(s, axis=-1, keepdims=True)
            p = jnp.exp(s - m)
            l = jnp.sum(p, axis=-1, keepdims=True)
            probs.append((qi, ki, p.astype(BF16), m, l))
        for qi, ki, p, m, l in probs:
            acc = jnp.dot(p, v_ref[ki, :].astype(BF16), preferred_element_type=F32)
            outs[gi][qi, :] = acc * (1.0 / l)
            lses[gi][qi, :] = jnp.broadcast_to(m + jnp.log(l), (NK, GW))

    def run_group(gi):
        dil = PATTERNS[gi][1]
        span = NK * dil
        nb = t // span
        u = ATTN_BLOCKS_PER_STEP
        if dil == 1:
            blocks(gi, [(True, 0, 0)] + [(False, n * span, (n - 1) * span) for n in range(1, u)])

            def step(i, c):
                q0 = pl.multiple_of(u * i * span, span)
                blocks(gi, [(False, q0 + n * span, q0 + (n - 1) * span) for n in range(u)])
                return c

            lax.fori_loop(1, nb // u, step, 0)
        else:
            def bunch(i, c):
                res = [u * i + r for r in range(u)]
                blocks(gi, [(True, r, r) for r in res])

                def later(n, cc):
                    blocks(gi, [(False, n * span + r, (n - 1) * span + r) for r in res])
                    return cc

                return lax.fori_loop(1, nb, later, c)

            if dil == u:
                bunch(0, 0)
            else:
                lax.fori_loop(0, dil // u, bunch, 0)

    for gi in range(len(PATTERNS)):
        pl.when(group == gi)(functools.partial(run_group, gi))

    @pl.when(group == len(PATTERNS) - 1)
    def _():
        step = 256

        def merge(c, carry):
            sl = pl.ds(pl.multiple_of(c * step, step), step)
            e = [r[sl, :] for r in lses]
            top = jnp.maximum(jnp.maximum(e[0], e[1]), e[2])
            w = [jnp.exp(x - top) for x in e]
            num = w[0] * o0_ref[sl, :] + w[1] * o1_ref[sl, :] + w[2] * o2_ref[sl, :]
            o_ref[sl, :] = (num / (w[0] + w[1] + w[2])).astype(o_ref.dtype)
            return carry

        lax.fori_loop(0, t // step, merge, 0)


def _prompt_buckets():
    i = np.arange(NK)[:, None]
    j = np.arange(2 * NK)[None, :]
    diff = NK + i - j
    valid = (diff >= 0) & (diff <= NK)
    return np.stack([np.where(valid, _t5_bucket(diff * dil), -1) for _, dil in PATTERNS]).astype(np.int32)


def attention_prompt(p, bias_mask, nseq, t):
    def qkv_spec(which):
        return pl.BlockSpec((t, GW), lambda b, h, g: (b, COL_B // GW + g * 12 + which * 4 + h))

    return pl.pallas_call(
        _attn_prompt_kernel,
        grid=(nseq, N_HEADS, len(PATTERNS)),
        in_specs=[qkv_spec(0), qkv_spec(1), qkv_spec(2),
                  pl.BlockSpec((1, 1, NK, 2 * NK), lambda b, h, g: (g, h, 0, 0))],
        out_specs=pl.BlockSpec((t, GW), lambda b, h, g: (b, h)),
        out_shape=jax.ShapeDtypeStruct((nseq * t, BR_W), BF16),
        scratch_shapes=[pltpu.VMEM((t, GW), F32)] * 6,
        compiler_params=_params(3),
        name="attention_prompt",
    )(p, p, p, bias_mask)


N_NEW = 4
TILE_ROWS = 2 * N_HEADS


def _key_to_value_rows(x):
    n, r, w = x.shape
    return pltpu.roll(x.reshape(n * r, w), N_HEADS, axis=0).reshape(n, r, w)


def _attend(x, xn, q, bias, bias_n, scale):
    s = jnp.sum(x * q[None], axis=-1, keepdims=True) * scale + bias
    sn = jnp.sum(xn * q[None], axis=-1, keepdims=True) * scale + bias_n
    m = jnp.maximum(jnp.max(s, axis=0, keepdims=True), jnp.max(sn, axis=0, keepdims=True))
    p = jnp.exp(s - m)
    pn = jnp.exp(sn - m)
    l = jnp.sum(p, axis=0, keepdims=True) + jnp.sum(pn, axis=0, keepdims=True)
    acc = jnp.sum(_key_to_value_rows(p) * x, axis=0) + jnp.sum(_key_to_value_rows(pn) * xn, axis=0)
    return acc, m, l


def _attn_sample_kernel(rel_ref, q_ref, xn_ref, c1_ref, c2_ref, c3_ref, o_ref,
                        b1_ref, b1n_ref, b23_ref, b23n_ref):
    scale = GW ** -0.5
    rows = N_NEW * TILE_ROWS

    @pl.when(pl.program_id(0) == 0)
    def _():
        head = lax.broadcasted_iota(jnp.int32, (rows, GW), 0) % N_HEADS
        neg = jnp.full((rows, GW), NEG, F32)

        def tiles_for(gi):
            out = []
            for b in range(N_BUCKETS):
                v = [rel_ref[b, gi * N_HEADS + h] for h in range(N_HEADS)]
                out.append(jnp.where(head == 0, v[0], jnp.where(head == 1, v[1], jnp.where(head == 2, v[2], v[3]))))
            return out

        t1 = tiles_for(0)
        for t in range(N_NEW):
            for pos in range(NK):
                step = NK + t - pos
                tile = t1[int(_t5_bucket(np.int64(step)))] if step <= NK else neg
                b1_ref[t, pos] = tile[:TILE_ROWS]
            for j in range(N_NEW):
                tile = t1[int(_t5_bucket(np.int64(t - j)))] if j <= t else neg
                b1n_ref[t, j] = tile[:TILE_ROWS]
        for gi in (1, 2):
            dil = PATTERNS[gi][1]
            tg = tiles_for(gi)
            for jj in range(NK):
                b23_ref[gi - 1, jj] = tg[int(_t5_bucket(np.int64((NK - jj) * dil)))]
            b23n_ref[gi - 1] = tg[0]

    outs, lses = [], []
    x1 = c1_ref[0, 0]
    xn1 = xn_ref[0, 0].reshape(N_NEW, TILE_ROWS, GW)
    o1, e1 = [], []
    for t in range(N_NEW):
        q = q_ref[0, 0, t * TILE_ROWS:(t + 1) * TILE_ROWS, :]
        acc, m, l = _attend(x1, xn1, q, b1_ref[t], b1n_ref[t], scale)
        o1.append(acc)
        e1.append((m, l))
    outs.append(jnp.concatenate(o1, axis=0))
    lses.append((jnp.concatenate([m[0] for m, _ in e1], axis=0), jnp.concatenate([l[0] for _, l in e1], axis=0)))
    for gi, c_ref in ((1, c2_ref), (2, c3_ref)):
        acc, m, l = _attend(c_ref[0, 0], xn_ref[gi, 0][None], q_ref[gi, 0], b23_ref[gi - 1],
                            b23n_ref[gi - 1][None], scale)
        outs.append(acc)
        lses.append((m[0], l[0]))

    lse = [_key_to_value_rows((m + jnp.log(l))[None])[0] for m, l in lses]
    den = [_key_to_value_rows(l[None])[0] for _, l in lses]
    top = jnp.maximum(jnp.maximum(lse[0], lse[1]), lse[2])
    w = [jnp.exp(e - top) for e in lse]
    num = w[0] * outs[0] / den[0] + w[1] * outs[1] / den[1] + w[2] * outs[2] / den[2]
    o_ref[0] = num / (w[0] + w[1] + w[2])


def attention_sample(p, cache_b1, cache_b2, cache_b3, rel_bias, layer, row0, nseq):
    depth = cache_b1.shape[0]
    rows = N_NEW * TILE_ROWS
    ps = p[row0:].reshape(nseq, SAMPLE_ROWS, N_IN)[:, :N_NEW]
    qs, xns = [], []
    for gi in range(len(PATTERNS)):
        c0 = COL_B + gi * 3 * BR_W
        q = ps[:, :, c0:c0 + BR_W].reshape(nseq, N_NEW, N_HEADS, GW)
        qs.append(jnp.concatenate([q, jnp.zeros_like(q)], axis=2).reshape(nseq, rows, GW))
        xns.append(ps[:, :, c0 + BR_W:c0 + 3 * BR_W].reshape(nseq, rows, GW))
    q_all = jnp.stack(qs)
    xn_all = jnp.stack(xns)
    c1 = cache_b1.reshape(depth, nseq, NK, TILE_ROWS, GW)
    c2 = cache_b2.reshape(depth, nseq, NK, 4 * TILE_ROWS, GW)
    c3 = cache_b3.reshape(depth, nseq, NK, 16 * TILE_ROWS, GW)
    cache_spec = lambda r: pl.BlockSpec((1, 1, NK, r, GW), lambda b: (layer, b, 0, 0, 0))
    tok_spec = pl.BlockSpec((len(PATTERNS), 1, rows, GW), lambda b: (0, b, 0, 0))
    y = pl.pallas_call(
        _attn_sample_kernel,
        grid=(nseq,),
        in_specs=[pl.BlockSpec(memory_space=pltpu.SMEM), tok_spec, tok_spec,
                  cache_spec(TILE_ROWS), cache_spec(rows), cache_spec(rows)],
        out_specs=pl.BlockSpec((1, rows, GW), lambda b: (b, 0, 0)),
        out_shape=jax.ShapeDtypeStruct((nseq, rows, GW), F32),
        scratch_shapes=[pltpu.VMEM((N_NEW, NK, TILE_ROWS, GW), F32), pltpu.VMEM((N_NEW, N_NEW, TILE_ROWS, GW), F32),
                        pltpu.VMEM((2, NK, rows, GW), F32), pltpu.VMEM((2, rows, GW), F32)],
        compiler_params=_params(1),
        name="attention_sample",
    )(rel_bias, q_all, xn_all, c1, c2, c3)
    y = y.reshape(nseq, N_NEW, 2, BR_W)[:, :, 1]
    y = jnp.pad(y, ((0, 0), (0, SAMPLE_ROWS - N_NEW), (0, 0)))
    return y.reshape(nseq * SAMPLE_ROWS, BR_W).astype(BF16)


def _pool_conv_kernel(pc_ref, pbg_ref, pcg_ref, phs_ref, hc_ref, hd_ref, pw_ref, pb_ref, ps_ref, cw_ref,
                      yc_ref, yd_ref, zt_ref, cbuf_ref, zbuf_ref, *, start):
    tr = pc_ref.shape[0]
    j = pl.program_id(1)
    hc = 16
    hz = 8

    @pl.when(j == 0)
    def _():
        cbuf_ref[0:hc, :] = hc_ref[0]
        zbuf_ref[0:hz, :] = hd_ref[0]

    x = pc_ref[...]
    cbuf_ref[hc:hc + tr, :] = x
    pos = start + j * tr + lax.broadcasted_iota(jnp.int32, (tr, 1), 0)
    for gi, win in enumerate(POOL_WINDOWS):
        cols = slice(gi * GW, (gi + 1) * GW)
        total = x[:, cols]
        for back in range(1, win):
            total = total + cbuf_ref[hc - back:hc - back + tr, cols]
        cnt = jnp.minimum(pos + 1, win).astype(F32)
        pooled = total / cnt - x[:, cols]
        y = jnp.dot(pooled.astype(BF16), pw_ref[0, gi].astype(BF16), preferred_element_type=F32)
        yc_ref[:, cols] = ((y + pb_ref[:, cols]) * ps_ref[:, cols]).astype(yc_ref.dtype)
    cbuf_ref[0:hc, :] = cbuf_ref[tr:tr + hc, :]

    z = pcg_ref[...] * phs_ref[...]
    zbuf_ref[hz:hz + tr, :] = z
    y = (zbuf_ref[hz - 2:hz - 2 + tr, :] * cw_ref[0:1, :] + zbuf_ref[hz - 1:hz - 1 + tr, :] * cw_ref[1:2, :]
         + z * cw_ref[2:3, :])
    yd_ref[...] = (pbg_ref[...] * y).astype(yd_ref.dtype)
    zt_ref[0] = z[tr - hz:, :]
    zbuf_ref[0:hz, :] = zbuf_ref[tr:tr + hz, :]


def pool_conv(p, hist_c, hist_d, pool_w, pool_b, pool_scale, conv_wt, layer, row0, nseq, t, tr, start):
    per = t // tr
    blk0 = row0 // tr
    col = lambda c: pl.BlockSpec((tr, BR_W), lambda b, j: (b * per + j, c))
    pcol = lambda c: pl.BlockSpec((tr, BR_W), lambda b, j: (blk0 + b * per + j, c))
    vec = pl.BlockSpec((1, BR_W), lambda b, j: (0, 0))
    return pl.pallas_call(
        functools.partial(_pool_conv_kernel, start=start),
        grid=(nseq, per),
        in_specs=[pcol(COL_C // BR_W), pcol(COL_D // BR_W), pcol(COL_D // BR_W + 1), pcol(COL_D // BR_W + 2),
                  pl.BlockSpec((1, 16, BR_W), lambda b, j: (b, 0, 0)),
                  pl.BlockSpec((1, 8, BR_W), lambda b, j: (b, 0, 0)),
                  pl.BlockSpec((1, 4, GW, GW), lambda b, j: (layer, 0, 0, 0)),
                  vec, vec,
                  pl.BlockSpec((CONV_W, BR_W), lambda b, j: (0, 0))],
        out_specs=[col(0), col(0), pl.BlockSpec((1, 8, BR_W), lambda b, j: (b, 0, 0))],
        out_shape=[jax.ShapeDtypeStruct((nseq * t, BR_W), BF16),
                   jax.ShapeDtypeStruct((nseq * t, BR_W), BF16),
                   jax.ShapeDtypeStruct((nseq, 8, BR_W), F32)],
        scratch_shapes=[pltpu.VMEM((16 + tr, BR_W), F32), pltpu.VMEM((8 + tr, BR_W), F32)],
        compiler_params=_params(2),
        name="pool_conv",
    )(p, p, p, p, hist_c, hist_d, pool_w, pool_b, pool_scale, conv_wt)


CACHE_ROWS = 512


def _cache_writer_kernel(k1_ref, v1_ref, k2_ref, v2_ref, k3_ref, v3_ref, b1_any, b2_any, b3_any,
                         o1_ref, o2_ref, o3_ref):
    def scatter(o_ref, k_ref, v_ref, row0, n):
        for kv, src in enumerate((k_ref, v_ref)):
            for h in range(N_HEADS):
                o_ref[0, 0, pl.ds(kv * N_HEADS + h, n, stride=TILE_ROWS), :] = src[row0:row0 + n, h * GW:(h + 1) * GW]

    scatter(o3_ref, k3_ref, v3_ref, 0, CACHE_ROWS)

    @pl.when(pl.program_id(1) == pl.num_programs(1) - 1)
    def _():
        scatter(o2_ref, k2_ref, v2_ref, CACHE_ROWS - PATTERNS[1][0], PATTERNS[1][0])
        scatter(o1_ref, k1_ref, v1_ref, CACHE_ROWS - PATTERNS[0][0], PATTERNS[0][0])


def cache_writer(p, bufs, layer, nseq, t):
    b1, b2, b3 = bufs
    per = t // CACHE_ROWS
    steps = PATTERNS[2][0] // CACHE_ROWS

    def slab(gi, which, whole_window):
        col = (COL_B + gi * 3 * BR_W + (1 + which) * BR_W) // BR_W
        if whole_window:
            return pl.BlockSpec((CACHE_ROWS, BR_W), lambda b, j: ((b + 1) * per - steps + j, col))
        return pl.BlockSpec((CACHE_ROWS, BR_W), lambda b, j: ((b + 1) * per - 1, col))

    any_spec = pl.BlockSpec(memory_space=pl.ANY)
    tile = lambda rows: (1, 1, rows * TILE_ROWS, GW)
    return pl.pallas_call(
        _cache_writer_kernel,
        grid=(nseq, steps),
        in_specs=[slab(0, 0, False), slab(0, 1, False), slab(1, 0, False), slab(1, 1, False),
                  slab(2, 0, True), slab(2, 1, True),
                  any_spec, any_spec, any_spec],
        out_specs=[pl.BlockSpec(tile(PATTERNS[0][0]), lambda b, j: (layer, b, 0, 0)),
                   pl.BlockSpec(tile(PATTERNS[1][0]), lambda b, j: (layer, b, 0, 0)),
                   pl.BlockSpec(tile(CACHE_ROWS), lambda b, j: (layer, b, j, 0))],
        out_shape=[jax.ShapeDtypeStruct(x.shape, x.dtype) for x in bufs],
        input_output_aliases={6: 0, 7: 1, 8: 2},
        compiler_params=_params(2),
        name="cache_writer",
    )(p, p, p, p, p, p, b1, b2, b3)


TM_WIDE = 1024
TM_ROWS = 512


def kernel(x_prompt, x_sample, c_prompt, c_sample, cache_b1, cache_b2, cache_b3, cache_pool, cache_conv,
           ln1_g, ln2_g, w_ada, b_ada, w_in, w_gate, b_gate, a_ln_g, a_ln_b, a_ws, a_bs,
           rel_bias, pool_w, pool_b, pool_scale, conv_w, w_branch, w_out, w1, w3, w2, final_g):
    nb, seq, d = x_prompt.shape
    ns, dec = x_sample.shape[:2]
    depth = w_in.shape[0]
    mp = nb * seq
    ms = ns * SAMPLE_ROWS

    c_all = jnp.concatenate([c_prompt, c_sample], axis=0)
    c_rows = -(-c_all.shape[0] // 8) * 8
    c_all = jnp.pad(c_all, ((0, c_rows - c_all.shape[0]), (0, 0)))
    mod = ada_modulation(c_all, w_ada, b_ada).reshape(depth, c_rows, 6, d)
    mods = [[(mod[l, :nb, i][:, None, :], jnp.repeat(mod[l, nb:nb + ns, i], SAMPLE_ROWS, axis=0)[None])
             for i in range(6)] for l in range(depth)]

    xp = x_prompt.reshape(mp, d)
    xs = jnp.pad(x_sample, ((0, 0), (0, SAMPLE_ROWS - dec), (0, 0))).reshape(ms, d)
    zero_hist_c = jnp.zeros((nb, 16, BR_W), F32)
    zero_hist_d = jnp.zeros((nb, 8, BR_W), F32)

    bias_mask = prompt_bias_mask(rel_bias)
    kv_bufs = tuple(jnp.zeros((depth, nb, win * TILE_ROWS, GW), F32) for win, _ in PATTERNS)
    kv_s = [[] for _ in PATTERNS]
    pool_p, conv_p, pool_s, conv_s, chunk_s = [], [], [], [], []
    for l in range(depth):
        sh1, sc1, g1, sh2, sc2, g2 = mods[l]
        hn = norm_in(xp, xs, ln1_g[l][None], sc1, sh1, TM_WIDE)
        p = matmul(hn, w_in, l, ms, TM_WIDE, 1280, F32)

        gm = (a_ln_g[l][None], a_ln_b[l][None], a_ws[l][None], a_bs[l].T[None])
        ya_p, = gmlp(p, *gm, 0, mp, 512, CHUNK, False)
        ya_s, va_s = gmlp(p, *gm, mp, ms, ms, SAMPLE_ROWS, True)
        yb_p = attention_prompt(p, bias_mask, nb, seq)
        yb_s = attention_sample(p, cache_b1, cache_b2, cache_b3, rel_bias, l, mp, ns)
        pc = (pool_w, pool_b[l][None], pool_scale[l][None], conv_w[l].T, l)
        yc_p, yd_p, zt_p = pool_conv(p, zero_hist_c, zero_hist_d, *pc, 0, nb, seq, 512, 0)
        hist_c = jnp.pad(cache_pool[l], ((0, 0), (1, 0), (0, 0)))
        hist_d = jnp.pad(cache_conv[l], ((0, 0), (8 - (CONV_W - 1), 0), (0, 0)))
        yc_s, yd_s, zt_s = pool_conv(p, hist_c, hist_d, *pc, mp, ns, SAMPLE_ROWS, SAMPLE_ROWS, PAST_LEN)

        merged = gate_merge(hn, (ya_p, yb_p, yc_p, yd_p), (ya_s, yb_s, yc_s, yd_s),
                            w_gate, b_gate, w_branch, l, TM_WIDE, 256)
        xp, xs, hn2 = out_proj_norm(merged, w_out, l, xp, xs, g1, ln2_g[l][None], sc2, sh2, TM_ROWS)
        hmid = swiglu_up(hn2, w1, w3, l, ms, TM_WIDE, 512)
        xp, xs = matmul_residual(hmid, w2, l, xp, xs, g2, TM_ROWS, 512)

        kv_bufs = cache_writer(p, kv_bufs, l, nb, seq)
        ps3 = p[mp:].reshape(ns, SAMPLE_ROWS, N_IN)
        for gi in range(len(PATTERNS)):
            c0 = COL_B + gi * 3 * BR_W + BR_W
            kv_s[gi].append(ps3[:, :dec, c0:c0 + 2 * BR_W].reshape(ns, dec, 2, N_HEADS, GW))
        pool_p.append(jnp.stack([p[(b + 1) * seq - POOL_HIST:(b + 1) * seq, COL_C:COL_C + BR_W]
                                 for b in range(nb)]))
        conv_p.append(zt_p[:, 8 - (CONV_W - 1):])
        pool_s.append(ps3[:, :dec, COL_C:COL_C + BR_W])
        conv_s.append(zt_s[:, :dec])
        chunk_s.append(va_s.reshape(ns, SAMPLE_ROWS, BR_W)[:, :dec])

    yp, ys = final_norm(xp, xs, final_g[None], TM_WIDE)
    stack = lambda parts: jnp.stack(parts, axis=0)
    kv_p = [buf.reshape(depth, nb, win, 2, N_HEADS, GW) for buf, (win, _) in zip(kv_bufs, PATTERNS)]
    return (yp.reshape(nb, seq, d), ys.reshape(ns, SAMPLE_ROWS, d)[:, :dec],
            kv_p[0], kv_p[1], kv_p[2], stack(pool_p), stack(conv_p),
            stack(kv_s[0]), stack(kv_s[1]), stack(kv_s[2]), stack(pool_s), stack(conv_s), stack(chunk_s))
```

```python
import functools
import math
from typing import NamedTuple

import jax
import jax.numpy as jnp
import numpy as np
from jax import lax
from jax.experimental import pallas as pl
from jax.experimental.pallas import tpu as pltpu

F32 = jnp.float32
BF16 = jnp.bfloat16

D_MODEL = 2048
BR_W = 512
GW = 128
N_HEADS = 4
CHUNK = 128
PATTERNS = ((128, 1), (512, 4), (2048, 16))
NK = 128
POOL_WINDOWS = (2, 4, 8, 16)
POOL_HIST = 15
CONV_W = 3
N_BUCKETS = 32
MAX_DIST = 2048
D_FF = 5632
N_IN = 7680
COL_A = 0
COL_B = 2 * BR_W
COL_C = COL_B + 9 * BR_W
COL_D = COL_C + BR_W
EPS = 1e-6
NEG = -1e30
SAMPLE_ROWS = 8
PAST_LEN = 16384
ATTN_BLOCKS_PER_STEP = 4

VMEM_LIMIT = 56 * 1024 * 1024


def _t5_bucket(dist):
    max_exact = N_BUCKETS // 2
    n = np.maximum(dist, 0)
    nf = np.maximum(n, 1).astype(np.float32)
    large = max_exact + (np.log(nf / np.float32(max_exact)) / np.float32(math.log(MAX_DIST / max_exact))
                         * np.float32(N_BUCKETS - max_exact)).astype(np.int32)
    large = np.minimum(large, N_BUCKETS - 1)
    return np.where(n < max_exact, n, large)


def _params(n_axes):
    return pltpu.CompilerParams(dimension_semantics=("arbitrary",) * n_axes,
                                vmem_limit_bytes=VMEM_LIMIT)


def _ada_kernel(c_ref, w_ref, b_ref, o_ref):
    a = jax.nn.silu(c_ref[...]).astype(BF16)
    o_ref[0] = jnp.dot(a, w_ref[0].astype(BF16), preferred_element_type=F32) + b_ref[0]


def ada_modulation(c_all, w_ada, b_ada):
    depth, d, n = w_ada.shape
    r = c_all.shape[0]
    tn = 1024
    return pl.pallas_call(
        _ada_kernel,
        grid=(depth, n // tn),
        in_specs=[pl.BlockSpec((r, d), lambda l, j: (0, 0)),
                  pl.BlockSpec((1, d, tn), lambda l, j: (l, 0, j)),
                  pl.BlockSpec((1, 1, tn), lambda l, j: (l, 0, j))],
        out_specs=pl.BlockSpec((1, r, tn), lambda l, j: (l, 0, j)),
        out_shape=jax.ShapeDtypeStruct((depth, r, n), F32),
        compiler_params=_params(2),
        name="ada_modulation",
    )(c_all, w_ada, b_ada.reshape(depth, 1, n))


def _split_tiles(axis, body):
    i = pl.program_id(axis)
    pl.when(i == 0)(functools.partial(body, True))
    pl.when(i > 0)(functools.partial(body, False))


def _joint_tile(i, nf):
    return (i + nf) % (nf + 1)


def _prompt_tile(i):
    return jnp.maximum(i - 1, 0)


def _norm_rows(x, g, sc, sh):
    y = x * lax.rsqrt(jnp.mean(x * x, axis=-1, keepdims=True) + EPS) * g
    return y * (1.0 + sc) + sh


class Mod(NamedTuple):
    prompt: jax.Array
    sample: jax.Array
    index: int
    nb: int


def _mod_specs(mod, width, per, grid_rank):
    ms = mod.sample.shape[1]
    base = mod.index * mod.nb
    if grid_rank == 1:
        return [pl.BlockSpec((1, 1, width), lambda i: (base + _prompt_tile(i) // per, 0, 0)),
                pl.BlockSpec((1, ms, width), lambda i: (mod.index, 0, 0))]
    return [pl.BlockSpec((1, 1, width), lambda j, i: (base + _prompt_tile(i) // per, 0, j)),
            pl.BlockSpec((1, ms, width), lambda j, i: (mod.index, 0, j))]


def _norm_in_kernel(xp_ref, xs_ref, g_ref, scp_ref, scs_ref, shp_ref, shs_ref, o_ref):
    ms = xs_ref.shape[0]

    def body(tail):
        if tail:
            o_ref[0:ms, :] = _norm_rows(xs_ref[...], g_ref[...], scs_ref[0], shs_ref[0]).astype(o_ref.dtype)
        else:
            o_ref[...] = _norm_rows(xp_ref[...], g_ref[...], scp_ref[0], shp_ref[0]).astype(o_ref.dtype)

    _split_tiles(0, body)


def norm_in(xp, xs, g, sc, sh, tm):
    mp, d = xp.shape
    ms = xs.shape[0]
    nf = mp // tm
    per = nf // sc.nb
    return pl.pallas_call(
        _norm_in_kernel,
        grid=(nf + 1,),
        in_specs=[pl.BlockSpec((tm, d), lambda i: (_prompt_tile(i), 0)),
                  pl.BlockSpec((ms, d), lambda i: (0, 0)),
                  pl.BlockSpec((1, d), lambda i: (0, 0)),
                  *_mod_specs(sc, d, per, 1), *_mod_specs(sh, d, per, 1)],
        out_specs=pl.BlockSpec((tm, d), lambda i: (_joint_tile(i, nf), 0)),
        out_shape=jax.ShapeDtypeStruct((mp + ms, d), BF16),
        compiler_params=_params(1),
        name="norm_in",
    )(xp, xs, g, sc.prompt, sc.sample, sh.prompt, sh.sample)


def _final_norm_kernel(xp_ref, xs_ref, g_ref, yp_ref, ys_ref):
    def body(tail):
        src, dst = (xs_ref, ys_ref) if tail else (xp_ref, yp_ref)
        x = src[...]
        dst[...] = x * lax.rsqrt(jnp.mean(x * x, axis=-1, keepdims=True) + EPS) * g_ref[...]

    _split_tiles(0, body)


def final_norm(xp, xs, g, tm):
    mp, d = xp.shape
    ms = xs.shape[0]
    nf = mp // tm
    p_spec = pl.BlockSpec((tm, d), lambda i: (_prompt_tile(i), 0))
    s_spec = pl.BlockSpec((ms, d), lambda i: (0, 0))
    return pl.pallas_call(
        _final_norm_kernel,
        grid=(nf + 1,),
        in_specs=[p_spec, s_spec, pl.BlockSpec((1, d), lambda i: (0, 0))],
        out_specs=[p_spec, s_spec],
        out_shape=[jax.ShapeDtypeStruct((mp, d), F32), jax.ShapeDtypeStruct((ms, d), F32)],
        compiler_params=_params(1),
        name="final_norm",
    )(xp, xs, g)


def _mm_kernel(a_ref, w_ref, o_ref, wbf_ref, *, ms):
    @pl.when(pl.program_id(1) == 0)
    def _():
        wbf_ref[...] = w_ref[0].astype(BF16)

    def body(tail):
        rows = slice(0, ms) if tail else slice(None)
        o_ref[rows, :] = jnp.dot(a_ref[rows, :], wbf_ref[...], preferred_element_type=F32).astype(o_ref.dtype)

    _split_tiles(1, body)


def matmul(a, w, layer, ms, tm, tn, out_dtype):
    m, k = a.shape
    n = w.shape[2]
    nf = (m - ms) // tm
    return pl.pallas_call(
        functools.partial(_mm_kernel, ms=ms),
        grid=(n // tn, nf + 1),
        in_specs=[pl.BlockSpec((tm, k), lambda j, i: (_joint_tile(i, nf), 0)),
                  pl.BlockSpec((1, k, tn), lambda j, i: (layer, 0, j))],
        out_specs=pl.BlockSpec((tm, tn), lambda j, i: (_joint_tile(i, nf), j)),
        out_shape=jax.ShapeDtypeStruct((m, n), out_dtype),
        scratch_shapes=[pltpu.VMEM((k, tn), BF16)],
        compiler_params=_params(2),
        name="matmul",
    )(a, w)


def _mm_residual_kernel(a_ref, w_ref, xp_ref, xs_ref, gp_ref, gs_ref, op_ref, os_ref, wbf_ref):
    ms = xs_ref.shape[0]

    @pl.when(pl.program_id(1) == 0)
    def _():
        wbf_ref[...] = w_ref[0].astype(BF16)

    def body(tail):
        if tail:
            y = jnp.dot(a_ref[0:ms, :], wbf_ref[...], preferred_element_type=F32)
            os_ref[...] = xs_ref[...] + gs_ref[0] * y
        else:
            y = jnp.dot(a_ref[...], wbf_ref[...], preferred_element_type=F32)
            op_ref[...] = xp_ref[...] + gp_ref[0] * y

    _split_tiles(1, body)


def matmul_residual(a, w, layer, xp, xs, gate, tm, tn):
    m, k = a.shape
    n = w.shape[2]
    mp, ms = xp.shape[0], xs.shape[0]
    nf = mp // tm
    per = nf // gate.nb
    p_spec = pl.BlockSpec((tm, tn), lambda j, i: (_prompt_tile(i), j))
    s_spec = pl.BlockSpec((ms, tn), lambda j, i: (0, j))
    return pl.pallas_call(
        _mm_residual_kernel,
        grid=(n // tn, nf + 1),
        in_specs=[pl.BlockSpec((tm, k), lambda j, i: (_joint_tile(i, nf), 0)),
                  pl.BlockSpec((1, k, tn), lambda j, i: (layer, 0, j)),
                  p_spec, s_spec, *_mod_specs(gate, tn, per, 2)],
        out_specs=[p_spec, s_spec],
        out_shape=[jax.ShapeDtypeStruct((mp, n), F32), jax.ShapeDtypeStruct((ms, n), F32)],
        scratch_shapes=[pltpu.VMEM((k, tn), BF16)],
        compiler_params=_params(2),
        name="matmul_residual",
    )(a, w, xp, xs, gate.prompt, gate.sample)


def _out_proj_norm_kernel(a_ref, w_ref, xp_ref, xs_ref, gp_ref, gs_ref, ln_ref, scp_ref, scs_ref,
                          shp_ref, shs_ref, op_ref, os_ref, hn_ref, wbf_ref):
    ms = xs_ref.shape[0]

    @pl.when(pl.program_id(0) == 0)
    def _():
        wbf_ref[...] = w_ref[0].astype(BF16)

    def body(tail):
        if tail:
            x = xs_ref[...] + gs_ref[0] * jnp.dot(a_ref[0:ms, :], wbf_ref[...], preferred_element_type=F32)
            os_ref[...] = x
            hn_ref[0:ms, :] = _norm_rows(x, ln_ref[...], scs_ref[0], shs_ref[0]).astype(hn_ref.dtype)
        else:
            x = xp_ref[...] + gp_ref[0] * jnp.dot(a_ref[...], wbf_ref[...], preferred_element_type=F32)
            op_ref[...] = x
            hn_ref[...] = _norm_rows(x, ln_ref[...], scp_ref[0], shp_ref[0]).astype(hn_ref.dtype)

    _split_tiles(0, body)


def out_proj_norm(a, w, layer, xp, xs, gate, ln_g, sc, sh, tm):
    m, k = a.shape
    n = w.shape[2]
    mp, ms = xp.shape[0], xs.shape[0]
    nf = mp // tm
    per = nf // gate.nb
    p_spec = pl.BlockSpec((tm, n), lambda i: (_prompt_tile(i), 0))
    s_spec = pl.BlockSpec((ms, n), lambda i: (0, 0))
    return pl.pallas_call(
        _out_proj_norm_kernel,
        grid=(nf + 1,),
        in_specs=[pl.BlockSpec((tm, k), lambda i: (_joint_tile(i, nf), 0)),
                  pl.BlockSpec((1, k, n), lambda i: (layer, 0, 0), pipeline_mode=pl.Buffered(1)),
                  p_spec, s_spec, *_mod_specs(gate, n, per, 1),
                  pl.BlockSpec((1, n), lambda i: (0, 0)),
                  *_mod_specs(sc, n, per, 1), *_mod_specs(sh, n, per, 1)],
        out_specs=[p_spec, s_spec, pl.BlockSpec((tm, n), lambda i: (_joint_tile(i, nf), 0))],
        out_shape=[jax.ShapeDtypeStruct((mp, n), F32), jax.ShapeDtypeStruct((ms, n), F32),
                   jax.ShapeDtypeStruct((m, n), BF16)],
        scratch_shapes=[pltpu.VMEM((k, n), BF16)],
        compiler_params=_params(1),
        name="out_proj_norm",
    )(a, w, xp, xs, gate.prompt, gate.sample, ln_g, sc.prompt, sc.sample, sh.prompt, sh.sample)


def _swiglu_kernel(a_ref, w1_ref, w3_ref, o_ref, w1bf_ref, w3bf_ref, *, ms):
    @pl.when(pl.program_id(1) == 0)
    def _():
        w1bf_ref[...] = w1_ref[0].astype(BF16)
        w3bf_ref[...] = w3_ref[0].astype(BF16)

    def body(tail):
        rows = slice(0, ms) if tail else slice(None)
        a = a_ref[rows, :]
        h1 = jnp.dot(a, w1bf_ref[...], preferred_element_type=F32)
        h3 = jnp.dot(a, w3bf_ref[...], preferred_element_type=F32)
        o_ref[rows, :] = (jax.nn.silu(h1) * h3).astype(o_ref.dtype)

    _split_tiles(1, body)


def swiglu_up(a, w1, w3, layer, ms, tm, tn):
    m, k = a.shape
    n = w1.shape[2]
    nf = (m - ms) // tm
    w_spec = pl.BlockSpec((1, k, tn), lambda j, i: (layer, 0, j))
    return pl.pallas_call(
        functools.partial(_swiglu_kernel, ms=ms),
        grid=(n // tn, nf + 1),
        in_specs=[pl.BlockSpec((tm, k), lambda j, i: (_joint_tile(i, nf), 0)), w_spec, w_spec],
        out_specs=pl.BlockSpec((tm, tn), lambda j, i: (_joint_tile(i, nf), j)),
        out_shape=jax.ShapeDtypeStruct((m, n), BF16),
        scratch_shapes=[pltpu.VMEM((k, tn), BF16), pltpu.VMEM((k, tn), BF16)],
        compiler_params=_params(2),
        name="swiglu_up",
    )(a, w1, w3)


def _gate_merge_kernel(hn_ref, yap_ref, ybp_ref, ycp_ref, ydp_ref, yas_ref, ybs_ref, ycs_ref, yds_ref,
                       wg0_ref, wg1_ref, wg2_ref, wg3_ref, wb_ref, bg_ref, o_ref, wgbf_ref, wbbf_ref):
    ms = yas_ref.shape[0]

    @pl.when(pl.program_id(1) == 0)
    def _():
        for g, wg_ref in enumerate((wg0_ref, wg1_ref, wg2_ref, wg3_ref)):
            wgbf_ref[g] = wg_ref[0].astype(BF16)
            wbbf_ref[g] = wb_ref[0, g].astype(BF16)

    def body(tail):
        rows = slice(0, ms) if tail else slice(None)
        y_refs = (yas_ref, ybs_ref, ycs_ref, yds_ref) if tail else (yap_ref, ybp_ref, ycp_ref, ydp_ref)
        hn = hn_ref[rows, :]
        acc = None
        for g, y_ref in enumerate(y_refs):
            gate = jax.nn.sigmoid(jnp.dot(hn, wgbf_ref[g], preferred_element_type=F32) + bg_ref[0, g:g + 1, :])
            br = jnp.dot(y_ref[...], wbbf_ref[g], preferred_element_type=F32)
            acc = gate * br if acc is None else acc + gate * br
        o_ref[rows, :] = acc.astype(o_ref.dtype)

    _split_tiles(1, body)


def gate_merge(hn, ys_p, ys_s, w_gate, b_gate, w_branch, layer, tm, tn):
    m, k = hn.shape
    ms = ys_s[0].shape[0]
    nf = (m - ms) // tm
    depth = w_gate.shape[0]
    d = w_branch.shape[3]
    nj = d // tn
    wg_specs = [pl.BlockSpec((1, k, tn), functools.partial(lambda j, i, g: (layer, 0, g * nj + j), g=g))
                for g in range(4)]
    yp_spec = pl.BlockSpec((tm, BR_W), lambda j, i: (_prompt_tile(i), 0))
    ys_spec = pl.BlockSpec((ms, BR_W), lambda j, i: (0, 0))
    return pl.pallas_call(
        _gate_merge_kernel,
        grid=(nj, nf + 1),
        in_specs=[pl.BlockSpec((tm, k), lambda j, i: (_joint_tile(i, nf), 0)), *[yp_spec] * 4, *[ys_spec] * 4,
                  *wg_specs,
                  pl.BlockSpec((1, 4, BR_W, tn), lambda j, i: (layer, 0, 0, j)),
                  pl.BlockSpec((1, 4, tn), lambda j, i: (layer, 0, j))],
        out_specs=pl.BlockSpec((tm, tn), lambda j, i: (_joint_tile(i, nf), j)),
        out_shape=jax.ShapeDtypeStruct((m, d), BF16),
        scratch_shapes=[pltpu.VMEM((4, k, tn), BF16), pltpu.VMEM((4, BR_W, tn), BF16)],
        compiler_params=_params(2),
        name="gate_merge",
    )(hn, *ys_p, *ys_s, w_gate, w_gate, w_gate, w_gate, w_branch, b_gate.reshape(depth, 4, d))


def _gmlp_kernel(pu_ref, pv_ref, lg_ref, lb_ref, ws_ref, bst_ref, ya_ref, *maybe_va_ref, chunk):
    rows = pu_ref.shape[0]
    u = jax.nn.gelu(pu_ref[...])
    v = jax.nn.gelu(pv_ref[...])
    mu = jnp.mean(v, axis=-1, keepdims=True)
    var = jnp.mean(jnp.square(v - mu), axis=-1, keepdims=True)
    vn = (v - mu) * lax.rsqrt(var + 1e-5) * lg_ref[...] + lb_ref[...]
    for va_ref in maybe_va_ref:
        va_ref[...] = vn
    causal = (lax.broadcasted_iota(jnp.int32, (CHUNK, CHUNK), 0)
              >= lax.broadcasted_iota(jnp.int32, (CHUNK, CHUNK), 1))
    for g in range(4):
        wm = jnp.where(causal, ws_ref[0, g], 0.0).astype(BF16)
        bias = bst_ref[0, :, g:g + 1]
        for c in range(rows // chunk):
            r0 = c * chunk
            vc = vn[r0:r0 + chunk, g * GW:(g + 1) * GW]
            if chunk < CHUNK:
                vc = jnp.concatenate([vc, jnp.zeros((CHUNK - chunk, GW), F32)], axis=0)
            mix = (jnp.dot(wm, vc.astype(BF16), preferred_element_type=F32) + bias)[:chunk]
            ya_ref[r0:r0 + chunk, g * GW:(g + 1) * GW] = (
                u[r0:r0 + chunk, g * GW:(g + 1) * GW] * mix).astype(ya_ref.dtype)


def gmlp(p, ln_g, ln_b, ws, bst, row0, m, tr, chunk, emit_v):
    blk0 = row0 // tr
    in_spec = lambda col: pl.BlockSpec((tr, BR_W), lambda i: (blk0 + i, col))
    out_spec = pl.BlockSpec((tr, BR_W), lambda i: (i, 0))
    n_out = 2 if emit_v else 1
    return pl.pallas_call(
        functools.partial(_gmlp_kernel, chunk=chunk),
        grid=(m // tr,),
        in_specs=[in_spec(0), in_spec(1),
                  pl.BlockSpec((1, BR_W), lambda i: (0, 0)),
                  pl.BlockSpec((1, BR_W), lambda i: (0, 0)),
                  pl.BlockSpec((1, 4, CHUNK, CHUNK), lambda i: (0, 0, 0, 0)),
                  pl.BlockSpec((1, CHUNK, 4), lambda i: (0, 0, 0))],
        out_specs=[out_spec] * n_out,
        out_shape=[jax.ShapeDtypeStruct((m, BR_W), BF16), jax.ShapeDtypeStruct((m, BR_W), F32)][:n_out],
        compiler_params=_params(1),
        name="gmlp",
    )(p, p, ln_g, ln_b, ws, bst)


def _bias_from_buckets(idx, rel_ref, col, shape):
    bias = jnp.full(shape, NEG, F32)
    for b in range(N_BUCKETS):
        bias = jnp.where(idx == b, rel_ref[b, col], bias)
    return bias


def _bias_mask_kernel(rel_ref, idx_ref, o_ref):
    o_ref[0, 0] = _bias_from_buckets(idx_ref[0], rel_ref, pl.program_id(0) * N_HEADS + pl.program_id(1),
                                     (NK, 2 * NK))


def prompt_bias_mask(rel_bias):
    return pl.pallas_call(
        _bias_mask_kernel,
        grid=(len(PATTERNS), N_HEADS),
        in_specs=[pl.BlockSpec(memory_space=pltpu.SMEM),
                  pl.BlockSpec((1, NK, 2 * NK), lambda g, h: (g, 0, 0))],
        out_specs=pl.BlockSpec((1, 1, NK, 2 * NK), lambda g, h: (g, h, 0, 0)),
        out_shape=jax.ShapeDtypeStruct((len(PATTERNS), N_HEADS, NK, 2 * NK), F32),
        compiler_params=_params(2),
        name="prompt_bias_mask",
    )(rel_bias, jnp.asarray(_prompt_buckets()))


def _attn_prompt_kernel(q_ref, k_ref, v_ref, bm_ref, o_ref,
                        o0_ref, o1_ref, o2_ref, e0_ref, e1_ref, e2_ref):
    t = q_ref.shape[0]
    group = pl.program_id(2)
    scale = GW ** -0.5
    outs = (o0_ref, o1_ref, o2_ref)
    lses = (e0_ref, e1_ref, e2_ref)

    def rows(start, n, dil):
        return pl.ds(start, n) if dil == 1 else pl.ds(start, n, stride=dil)

    def blocks(gi, specs):
        dil = PATTERNS[gi][1]
        staged = []
        for first_block, q0, k0 in specs:
            nkeys = NK if first_block else 2 * NK
            qi = rows(q0, NK, dil)
            ki = rows(k0, nkeys, dil)
            q = q_ref[qi, :].astype(BF16)
            kk = k_ref[ki, :].astype(BF16)
            bm = bm_ref[0, 0, :, NK:] if first_block else bm_ref[0, 0]
            s = lax.dot_general(q, kk, (((1,), (1,)), ((), ())), preferred_element_type=F32) * scale + bm
            staged.append((qi, ki, s))
        probs = []
        for qi, ki, s in staged:
            m = jnp.max(s, axis=-1, keepdims=True)
            p = jnp.exp(s - m)
            l = jnp.sum(p, axis=-1, keepdims=True)
            probs.append((qi, ki, p.astype(BF16), m, l))
        for qi, ki, p, m, l in probs:
            acc = jnp.dot(p, v_ref[ki, :].astype(BF16), preferred_element_type=F32)
            outs[gi][qi, :] = acc * (1.0 / l)
            lses[gi][qi, :] = jnp.broadcast_to(m + jnp.log(l), (NK, GW))

    def run_group(gi):
        dil = PATTERNS[gi][1]
        span = NK * dil
        nb = t // span
        u = ATTN_BLOCKS_PER_STEP
        if dil == 1:
            blocks(gi, [(True, 0, 0)] + [(False, n * span, (n - 1) * span) for n in range(1, u)])

            def step(i, c):
                q0 = pl.multiple_of(u * i * span, span)
                blocks(gi, [(False, q0 + n * span, q0 + (n - 1) * span) for n in range(u)])
                return c

            lax.fori_loop(1, nb // u, step, 0)
        else:
            def bunch(i, c):
                res = [u * i + r for r in range(u)]
                blocks(gi, [(True, r, r) for r in res])

                def later(n, cc):
                    blocks(gi, [(False, n * span + r, (n - 1) * span + r) for r in res])
                    return cc

                return lax.fori_loop(1, nb, later, c)

            if dil == u:
                bunch(0, 0)
            else:
                lax.fori_loop(0, dil // u, bunch, 0)

    for gi in range(len(PATTERNS)):
        pl.when(group == gi)(functools.partial(run_group, gi))

    @pl.when(group == len(PATTERNS) - 1)
    def _():
        step = 256

        def merge(c, carry):
            sl = pl.ds(pl.multiple_of(c * step, step), step)
            e = [r[sl, :] for r in lses]
            top = jnp.maximum(jnp.maximum(e[0], e[1]), e[2])
            w = [jnp.exp(x - top) for x in e]
            num = w[0] * o0_ref[sl, :] + w[1] * o1_ref[sl, :] + w[2] * o2_ref[sl, :]
            o_ref[sl, :] = (num / (w[0] + w[1] + w[2])).astype(o_ref.dtype)
            return carry

        lax.fori_loop(0, t // step, merge, 0)


def _prompt_buckets():
    i = np.arange(NK)[:, None]
    j = np.arange(2 * NK)[None, :]
    diff = NK + i - j
    valid = (diff >= 0) & (diff <= NK)
    return np.stack([np.where(valid, _t5_bucket(diff * dil), -1) for _, dil in PATTERNS]).astype(np.int32)


def attention_prompt(p, bias_mask, nseq, t):
    def qkv_spec(which):
        return pl.BlockSpec((t, GW), lambda b, h, g: (b, COL_B // GW + g * 12 + which * 4 + h))

    return pl.pallas_call(
        _attn_prompt_kernel,
        grid=(nseq, N_HEADS, len(PATTERNS)),
        in_specs=[qkv_spec(0), qkv_spec(1), qkv_spec(2),
                  pl.BlockSpec((1, 1, NK, 2 * NK), lambda b, h, g: (g, h, 0, 0))],
        out_specs=pl.BlockSpec((t, GW), lambda b, h, g: (b, h)),
        out_shape=jax.ShapeDtypeStruct((nseq * t, BR_W), BF16),
        scratch_shapes=[pltpu.VMEM((t, GW), F32)] * 6,
        compiler_params=_params(3),
        name="attention_prompt",
    )(p, p, p, bias_mask)


N_NEW = 4
TILE_ROWS = 2 * N_HEADS


def _key_to_value_rows(x):
    n, r, w = x.shape
    return pltpu.roll(x.reshape(n * r, w), N_HEADS, axis=0).reshape(n, r, w)


def _attend(x, xn, q, bias, bias_n, scale):
    s = jnp.sum(x * q[None], axis=-1, keepdims=True) * scale + bias
    sn = jnp.sum(xn * q[None], axis=-1, keepdims=True) * scale + bias_n
    m = jnp.maximum(jnp.max(s, axis=0, keepdims=True), jnp.max(sn, axis=0, keepdims=True))
    p = jnp.exp(s - m)
    pn = jnp.exp(sn - m)
    l = jnp.sum(p, axis=0, keepdims=True) + jnp.sum(pn, axis=0, keepdims=True)
    acc = jnp.sum(_key_to_value_rows(p) * x, axis=0) + jnp.sum(_key_to_value_rows(pn) * xn, axis=0)
    return acc, m, l


def _attn_sample_kernel(rel_ref, q_ref, xn_ref, c1_ref, c2_ref, c3_ref, o_ref,
                        b1_ref, b1n_ref, b23_ref, b23n_ref):
    scale = GW ** -0.5
    rows = N_NEW * TILE_ROWS

    @pl.when(pl.program_id(0) == 0)
    def _():
        head = lax.broadcasted_iota(jnp.int32, (rows, GW), 0) % N_HEADS
        neg = jnp.full((rows, GW), NEG, F32)

        def tiles_for(gi):
            out = []
            for b in range(N_BUCKETS):
                v = [rel_ref[b, gi * N_HEADS + h] for h in range(N_HEADS)]
                out.append(jnp.where(head == 0, v[0], jnp.where(head == 1, v[1], jnp.where(head == 2, v[2], v[3]))))
            return out

        t1 = tiles_for(0)
        for t in range(N_NEW):
            for pos in range(NK):
                step = NK + t - pos
                tile = t1[int(_t5_bucket(np.int64(step)))] if step <= NK else neg
                b1_ref[t, pos] = tile[:TILE_ROWS]
            for j in range(N_NEW):
                tile = t1[int(_t5_bucket(np.int64(t - j)))] if j <= t else neg
                b1n_ref[t, j] = tile[:TILE_ROWS]
        for gi in (1, 2):
            dil = PATTERNS[gi][1]
            tg = tiles_for(gi)
            for jj in range(NK):
                b23_ref[gi - 1, jj] = tg[int(_t5_bucket(np.int64((NK - jj) * dil)))]
            b23n_ref[gi - 1] = tg[0]

    outs, lses = [], []
    x1 = c1_ref[0, 0]
    xn1 = xn_ref[0, 0].reshape(N_NEW, TILE_ROWS, GW)
    o1, e1 = [], []
    for t in range(N_NEW):
        q = q_ref[0, 0, t * TILE_ROWS:(t + 1) * TILE_ROWS, :]
        acc, m, l = _attend(x1, xn1, q, b1_ref[t], b1n_ref[t], scale)
        o1.append(acc)
        e1.append((m, l))
    outs.append(jnp.concatenate(o1, axis=0))
    lses.append((jnp.concatenate([m[0] for m, _ in e1], axis=0), jnp.concatenate([l[0] for _, l in e1], axis=0)))
    for gi, c_ref in ((1, c2_ref), (2, c3_ref)):
        acc, m, l = _attend(c_ref[0, 0], xn_ref[gi, 0][None], q_ref[gi, 0], b23_ref[gi - 1],
                            b23n_ref[gi - 1][None], scale)
        outs.append(acc)
        lses.append((m[0], l[0]))

    lse = [_key_to_value_rows((m + jnp.log(l))[None])[0] for m, l in lses]
    den = [_key_to_value_rows(l[None])[0] for _, l in lses]
    top = jnp.maximum(jnp.maximum(lse[0], lse[1]), lse[2])
    w = [jnp.exp(e - top) for e in lse]
    num = w[0] * outs[0] / den[0] + w[1] * outs[1] / den[1] + w[2] * outs[2] / den[2]
    o_ref[0] = num / (w[0] + w[1] + w[2])


def attention_sample(p, cache_b1, cache_b2, cache_b3, rel_bias, layer, row0, nseq):
    depth = cache_b1.shape[0]
    rows = N_NEW * TILE_ROWS
    n_groups = len(PATTERNS)
    qkv = p[row0:, COL_B:COL_B + n_groups * 3 * BR_W].reshape(nseq, SAMPLE_ROWS, n_groups, 3, N_HEADS, GW)
    qkv = qkv[:, :N_NEW].transpose(2, 0, 1, 3, 4, 5)
    q = qkv[:, :, :, 0:1]
    q_all = jnp.concatenate([q, jnp.zeros_like(q)], axis=3).reshape(n_groups, nseq, rows, GW)
    xn_all = qkv[:, :, :, 1:3].reshape(n_groups, nseq, rows, GW)
    c1 = cache_b1.reshape(depth, nseq, NK, TILE_ROWS, GW)
    c2 = cache_b2.reshape(depth, nseq, NK, 4 * TILE_ROWS, GW)
    c3 = cache_b3.reshape(depth, nseq, NK, 16 * TILE_ROWS, GW)
    cache_spec = lambda r: pl.BlockSpec((1, 1, NK, r, GW), lambda b: (layer, b, 0, 0, 0))
    tok_spec = pl.BlockSpec((len(PATTERNS), 1, rows, GW), lambda b: (0, b, 0, 0))
    y = pl.pallas_call(
        _attn_sample_kernel,
        grid=(nseq,),
        in_specs=[pl.BlockSpec(memory_space=pltpu.SMEM), tok_spec, tok_spec,
                  cache_spec(TILE_ROWS), cache_spec(rows), cache_spec(rows)],
        out_specs=pl.BlockSpec((1, rows, GW), lambda b: (b, 0, 0)),
        out_shape=jax.ShapeDtypeStruct((nseq, rows, GW), F32),
        scratch_shapes=[pltpu.VMEM((N_NEW, NK, TILE_ROWS, GW), F32), pltpu.VMEM((N_NEW, N_NEW, TILE_ROWS, GW), F32),
                        pltpu.VMEM((2, NK, rows, GW), F32), pltpu.VMEM((2, rows, GW), F32)],
        compiler_params=_params(1),
        name="attention_sample",
    )(rel_bias, q_all, xn_all, c1, c2, c3)
    y = y.reshape(nseq, N_NEW, 2, BR_W)[:, :, 1]
    y = jnp.pad(y, ((0, 0), (0, SAMPLE_ROWS - N_NEW), (0, 0)))
    return y.reshape(nseq * SAMPLE_ROWS, BR_W).astype(BF16)


def _pool_conv_kernel(pc_ref, pbg_ref, pcg_ref, phs_ref, hc_ref, hd_ref, pw_ref, pb_ref, ps_ref, cw_ref,
                      yc_ref, yd_ref, zt_ref, cbuf_ref, zbuf_ref, *, start):
    tr = pc_ref.shape[0]
    j = pl.program_id(1)
    hc = 16
    hz = 8

    @pl.when(j == 0)
    def _():
        cbuf_ref[0:hc, :] = hc_ref[0]
        zbuf_ref[0:hz, :] = hd_ref[0]

    x = pc_ref[...]
    cbuf_ref[hc:hc + tr, :] = x
    pos = start + j * tr + lax.broadcasted_iota(jnp.int32, (tr, 1), 0)
    for gi, win in enumerate(POOL_WINDOWS):
        cols = slice(gi * GW, (gi + 1) * GW)
        total = x[:, cols]
        for back in range(1, win):
            total = total + cbuf_ref[hc - back:hc - back + tr, cols]
        cnt = jnp.minimum(pos + 1, win).astype(F32)
        pooled = total / cnt - x[:, cols]
        y = jnp.dot(pooled.astype(BF16), pw_ref[0, gi].astype(BF16), preferred_element_type=F32)
        yc_ref[:, cols] = ((y + pb_ref[:, cols]) * ps_ref[:, cols]).astype(yc_ref.dtype)
    cbuf_ref[0:hc, :] = cbuf_ref[tr:tr + hc, :]

    z = pcg_ref[...] * phs_ref[...]
    zbuf_ref[hz:hz + tr, :] = z
    y = (zbuf_ref[hz - 2:hz - 2 + tr, :] * cw_ref[0:1, :] + zbuf_ref[hz - 1:hz - 1 + tr, :] * cw_ref[1:2, :]
         + z * cw_ref[2:3, :])
    yd_ref[...] = (pbg_ref[...] * y).astype(yd_ref.dtype)
    zt_ref[0] = z[tr - hz:, :]
    zbuf_ref[0:hz, :] = zbuf_ref[tr:tr + hz, :]


def pool_conv(p, hist_c, hist_d, pool_w, pool_b, pool_scale, conv_wt, layer, row0, nseq, t, tr, start):
    per = t // tr
    blk0 = row0 // tr
    col = lambda c: pl.BlockSpec((tr, BR_W), lambda b, j: (b * per + j, c))
    pcol = lambda c: pl.BlockSpec((tr, BR_W), lambda b, j: (blk0 + b * per + j, c))
    vec = pl.BlockSpec((1, BR_W), lambda b, j: (0, 0))
    return pl.pallas_call(
        functools.partial(_pool_conv_kernel, start=start),
        grid=(nseq, per),
        in_specs=[pcol(COL_C // BR_W), pcol(COL_D // BR_W), pcol(COL_D // BR_W + 1), pcol(COL_D // BR_W + 2),
                  pl.BlockSpec((1, 16, BR_W), lambda b, j: (b, 0, 0)),
                  pl.BlockSpec((1, 8, BR_W), lambda b, j: (b, 0, 0)),
                  pl.BlockSpec((1, 4, GW, GW), lambda b, j: (layer, 0, 0, 0)),
                  vec, vec,
                  pl.BlockSpec((CONV_W, BR_W), lambda b, j: (0, 0))],
        out_specs=[col(0), col(0), pl.BlockSpec((1, 8, BR_W), lambda b, j: (b, 0, 0))],
        out_shape=[jax.ShapeDtypeStruct((nseq * t, BR_W), BF16),
                   jax.ShapeDtypeStruct((nseq * t, BR_W), BF16),
                   jax.ShapeDtypeStruct((nseq, 8, BR_W), F32)],
        scratch_shapes=[pltpu.VMEM((16 + tr, BR_W), F32), pltpu.VMEM((8 + tr, BR_W), F32)],
        compiler_params=_params(2),
        name="pool_conv",
    )(p, p, p, p, hist_c, hist_d, pool_w, pool_b, pool_scale, conv_wt)


CACHE_ROWS = 512


def _cache_writer_kernel(k1_ref, v1_ref, k2_ref, v2_ref, k3_ref, v3_ref, b1_any, b2_any, b3_any,
                         o1_ref, o2_ref, o3_ref):
    def scatter(o_ref, k_ref, v_ref, row0, n):
        for kv, src in enumerate((k_ref, v_ref)):
            for h in range(N_HEADS):
                o_ref[0, 0, pl.ds(kv * N_HEADS + h, n, stride=TILE_ROWS), :] = src[row0:row0 + n, h * GW:(h + 1) * GW]

    scatter(o3_ref, k3_ref, v3_ref, 0, CACHE_ROWS)

    @pl.when(pl.program_id(1) == pl.num_programs(1) - 1)
    def _():
        scatter(o2_ref, k2_ref, v2_ref, CACHE_ROWS - PATTERNS[1][0], PATTERNS[1][0])
        scatter(o1_ref, k1_ref, v1_ref, CACHE_ROWS - PATTERNS[0][0], PATTERNS[0][0])


def cache_writer(p, bufs, layer, nseq, t):
    b1, b2, b3 = bufs
    per = t // CACHE_ROWS
    steps = PATTERNS[2][0] // CACHE_ROWS

    def slab(gi, which, whole_window):
        col = (COL_B + gi * 3 * BR_W + (1 + which) * BR_W) // BR_W
        if whole_window:
            return pl.BlockSpec((CACHE_ROWS, BR_W), lambda b, j: ((b + 1) * per - steps + j, col))
        return pl.BlockSpec((CACHE_ROWS, BR_W), lambda b, j: ((b + 1) * per - 1, col))

    any_spec = pl.BlockSpec(memory_space=pl.ANY)
    tile = lambda rows: (1, 1, rows * TILE_ROWS, GW)
    return pl.pallas_call(
        _cache_writer_kernel,
        grid=(nseq, steps),
        in_specs=[slab(0, 0, False), slab(0, 1, False), slab(1, 0, False), slab(1, 1, False),
                  slab(2, 0, True), slab(2, 1, True),
                  any_spec, any_spec, any_spec],
        out_specs=[pl.BlockSpec(tile(PATTERNS[0][0]), lambda b, j: (layer, b, 0, 0)),
                   pl.BlockSpec(tile(PATTERNS[1][0]), lambda b, j: (layer, b, 0, 0)),
                   pl.BlockSpec(tile(CACHE_ROWS), lambda b, j: (layer, b, j, 0))],
        out_shape=[jax.ShapeDtypeStruct(x.shape, x.dtype) for x in bufs],
        input_output_aliases={6: 0, 7: 1, 8: 2},
        compiler_params=_params(2),
        name="cache_writer",
    )(p, p, p, p, p, p, b1, b2, b3)


TM_WIDE = 1024
TM_ROWS = 512


def kernel(x_prompt, x_sample, c_prompt, c_sample, cache_b1, cache_b2, cache_b3, cache_pool, cache_conv,
           ln1_g, ln2_g, w_ada, b_ada, w_in, w_gate, b_gate, a_ln_g, a_ln_b, a_ws, a_bs,
           rel_bias, pool_w, pool_b, pool_scale, conv_w, w_branch, w_out, w1, w3, w2, final_g):
    nb, seq, d = x_prompt.shape
    ns, dec = x_sample.shape[:2]
    depth = w_in.shape[0]
    mp = nb * seq
    ms = ns * SAMPLE_ROWS

    c_all = jnp.concatenate([c_prompt, c_sample], axis=0)
    c_rows = -(-c_all.shape[0] // 8) * 8
    c_all = jnp.pad(c_all, ((0, c_rows - c_all.shape[0]), (0, 0)))
    mod = ada_modulation(c_all, w_ada, b_ada).reshape(depth, c_rows, 6, d)
    mod_p = mod[:, :nb].transpose(0, 2, 1, 3).reshape(depth * 6 * nb, 1, d)
    mod_s = jnp.repeat(mod[:, nb:nb + ns], SAMPLE_ROWS, axis=1).transpose(0, 2, 1, 3).reshape(depth * 6, ms, d)
    mods = [[Mod(mod_p, mod_s, l * 6 + i, nb) for i in range(6)] for l in range(depth)]

    xp = x_prompt.reshape(mp, d)
    xs = jnp.pad(x_sample, ((0, 0), (0, SAMPLE_ROWS - dec), (0, 0))).reshape(ms, d)
    zero_hist_c = jnp.zeros((nb, 16, BR_W), F32)
    zero_hist_d = jnp.zeros((nb, 8, BR_W), F32)

    bias_mask = prompt_bias_mask(rel_bias)
    kv_bufs = tuple(jnp.zeros((depth, nb, win * TILE_ROWS, GW), F32) for win, _ in PATTERNS)
    pool_p, conv_p, proj_s, conv_s, chunk_s = [], [], [], [], []
    a_bst = a_bs.transpose(0, 2, 1)
    conv_wt = conv_w.transpose(0, 2, 1)
    hist_c_all = jnp.pad(cache_pool, ((0, 0), (0, 0), (1, 0), (0, 0)))
    hist_d_all = jnp.pad(cache_conv, ((0, 0), (0, 0), (8 - (CONV_W - 1), 0), (0, 0)))
    for l in range(depth):
        sh1, sc1, g1, sh2, sc2, g2 = mods[l]
        hn = norm_in(xp, xs, ln1_g[l][None], sc1, sh1, TM_WIDE)
        p = matmul(hn, w_in, l, ms, TM_WIDE, 1280, F32)

        gm = (a_ln_g[l][None], a_ln_b[l][None], a_ws[l][None], a_bst[l][None])
        ya_p, = gmlp(p, *gm, 0, mp, 512, CHUNK, False)
        ya_s, va_s = gmlp(p, *gm, mp, ms, ms, SAMPLE_ROWS, True)
        yb_p = attention_prompt(p, bias_mask, nb, seq)
        yb_s = attention_sample(p, cache_b1, cache_b2, cache_b3, rel_bias, l, mp, ns)
        pc = (pool_w, pool_b[l][None], pool_scale[l][None], conv_wt[l], l)
        yc_p, yd_p, zt_p = pool_conv(p, zero_hist_c, zero_hist_d, *pc, 0, nb, seq, 512, 0)
        yc_s, yd_s, zt_s = pool_conv(p, hist_c_all[l], hist_d_all[l], *pc, mp, ns, SAMPLE_ROWS, SAMPLE_ROWS,
                                     PAST_LEN)

        merged = gate_merge(hn, (ya_p, yb_p, yc_p, yd_p), (ya_s, yb_s, yc_s, yd_s),
                            w_gate, b_gate, w_branch, l, TM_WIDE, 256)
        xp, xs, hn2 = out_proj_norm(merged, w_out, l, xp, xs, g1, ln2_g[l][None], sc2, sh2, TM_ROWS)
        hmid = swiglu_up(hn2, w1, w3, l, ms, TM_WIDE, 512)
        xp, xs = matmul_residual(hmid, w2, l, xp, xs, g2, TM_ROWS, 512)

        kv_bufs = cache_writer(p, kv_bufs, l, nb, seq)
        pool_p.append(jnp.stack([p[(b + 1) * seq - POOL_HIST:(b + 1) * seq, COL_C:COL_C + BR_W]
                                 for b in range(nb)]))
        conv_p.append(zt_p)
        proj_s.append(p[mp:])
        conv_s.append(zt_s)
        chunk_s.append(va_s)

    yp, ys = final_norm(xp, xs, final_g[None], TM_WIDE)
    stack = lambda parts: jnp.stack(parts, axis=0)
    kv_p = [buf.reshape(depth, nb, win, 2, N_HEADS, GW) for buf, (win, _) in zip(kv_bufs, PATTERNS)]
    ps = stack(proj_s).reshape(depth, ns, SAMPLE_ROWS, N_IN)[:, :, :dec]
    kv_s = [ps[..., COL_B + (3 * gi + 1) * BR_W:COL_B + (3 * gi + 3) * BR_W].reshape(depth, ns, dec, 2, N_HEADS, GW)
            for gi in range(len(PATTERNS))]
    return (yp.reshape(nb, seq, d), ys.reshape(ns, SAMPLE_ROWS, d)[:, :dec],
            kv_p[0], kv_p[1], kv_p[2], stack(pool_p), stack(conv_p)[:, :, 8 - (CONV_W - 1):],
            kv_s[0], kv_s[1], kv_s[2], ps[..., COL_C:COL_C + BR_W], stack(conv_s)[:, :, :dec],
            stack(chunk_s).reshape(depth, ns, SAMPLE_ROWS, BR_W)[:, :, :dec])
```

```python
import functools
import math
from typing import NamedTuple

import jax
import jax.numpy as jnp
import numpy as np
from jax import lax
from jax.experimental import pallas as pl
from jax.experimental.pallas import tpu as pltpu

F32 = jnp.float32
BF16 = jnp.bfloat16

D_MODEL = 2048
BR_W = 512
GW = 128
N_HEADS = 4
CHUNK = 128
PATTERNS = ((128, 1), (512, 4), (2048, 16))
NK = 128
POOL_WINDOWS = (2, 4, 8, 16)
POOL_HIST = 15
CONV_W = 3
N_BUCKETS = 32
MAX_DIST = 2048
D_FF = 5632
N_IN = 7680
COL_A = 0
COL_B = 2 * BR_W
COL_C = COL_B + 9 * BR_W
COL_D = COL_C + BR_W
EPS = 1e-6
NEG = -1e30
SAMPLE_ROWS = 8
PAST_LEN = 16384
ATTN_BLOCKS_PER_STEP = 4

VMEM_LIMIT = 56 * 1024 * 1024


def _t5_bucket(dist):
    max_exact = N_BUCKETS // 2
    n = np.maximum(dist, 0)
    nf = np.maximum(n, 1).astype(np.float32)
    large = max_exact + (np.log(nf / np.float32(max_exact)) / np.float32(math.log(MAX_DIST / max_exact))
                         * np.float32(N_BUCKETS - max_exact)).astype(np.int32)
    large = np.minimum(large, N_BUCKETS - 1)
    return np.where(n < max_exact, n, large)


def _params(n_axes):
    return pltpu.CompilerParams(dimension_semantics=("arbitrary",) * n_axes,
                                vmem_limit_bytes=VMEM_LIMIT)


def _ada_kernel(c_ref, w_ref, b_ref, o_ref):
    a = jax.nn.silu(c_ref[...]).astype(BF16)
    o_ref[0] = jnp.dot(a, w_ref[0].astype(BF16), preferred_element_type=F32) + b_ref[0]


def ada_modulation(c_all, w_ada, b_ada):
    depth, d, n = w_ada.shape
    r = c_all.shape[0]
    tn = 1024
    return pl.pallas_call(
        _ada_kernel,
        grid=(depth, n // tn),
        in_specs=[pl.BlockSpec((r, d), lambda l, j: (0, 0)),
                  pl.BlockSpec((1, d, tn), lambda l, j: (l, 0, j)),
                  pl.BlockSpec((1, 1, tn), lambda l, j: (l, 0, j))],
        out_specs=pl.BlockSpec((1, r, tn), lambda l, j: (l, 0, j)),
        out_shape=jax.ShapeDtypeStruct((depth, r, n), F32),
        compiler_params=_params(2),
        name="ada_modulation",
    )(c_all, w_ada, b_ada.reshape(depth, 1, n))


def _split_tiles(axis, body):
    i = pl.program_id(axis)
    pl.when(i == 0)(functools.partial(body, True))
    pl.when(i > 0)(functools.partial(body, False))


def _joint_tile(i, nf):
    return (i + nf) % (nf + 1)


def _prompt_tile(i):
    return jnp.maximum(i - 1, 0)


def _norm_rows(x, g, sc, sh):
    y = x * lax.rsqrt(jnp.mean(x * x, axis=-1, keepdims=True) + EPS) * g
    return y * (1.0 + sc) + sh


class Mod(NamedTuple):
    prompt: jax.Array
    sample: jax.Array
    index: int
    nb: int


def _mod_specs(mod, width, per, grid_rank):
    ms = mod.sample.shape[1]
    base = mod.index * mod.nb
    if grid_rank == 1:
        return [pl.BlockSpec((1, 1, width), lambda i: (base + _prompt_tile(i) // per, 0, 0)),
                pl.BlockSpec((1, ms, width), lambda i: (mod.index, 0, 0))]
    return [pl.BlockSpec((1, 1, width), lambda j, i: (base + _prompt_tile(i) // per, 0, j)),
            pl.BlockSpec((1, ms, width), lambda j, i: (mod.index, 0, j))]


def _norm_in_kernel(xp_ref, xs_ref, g_ref, scp_ref, scs_ref, shp_ref, shs_ref, o_ref):
    ms = xs_ref.shape[0]

    def body(tail):
        if tail:
            o_ref[0:ms, :] = _norm_rows(xs_ref[...], g_ref[...], scs_ref[0], shs_ref[0]).astype(o_ref.dtype)
        else:
            o_ref[...] = _norm_rows(xp_ref[...], g_ref[...], scp_ref[0], shp_ref[0]).astype(o_ref.dtype)

    _split_tiles(0, body)


def norm_in(xp, xs, g, sc, sh, tm):
    mp, d = xp.shape
    ms = xs.shape[0]
    nf = mp // tm
    per = nf // sc.nb
    return pl.pallas_call(
        _norm_in_kernel,
        grid=(nf + 1,),
        in_specs=[pl.BlockSpec((tm, d), lambda i: (_prompt_tile(i), 0)),
                  pl.BlockSpec((ms, d), lambda i: (0, 0)),
                  pl.BlockSpec((1, d), lambda i: (0, 0)),
                  *_mod_specs(sc, d, per, 1), *_mod_specs(sh, d, per, 1)],
        out_specs=pl.BlockSpec((tm, d), lambda i: (_joint_tile(i, nf), 0)),
        out_shape=jax.ShapeDtypeStruct((mp + ms, d), BF16),
        compiler_params=_params(1),
        name="norm_in",
    )(xp, xs, g, sc.prompt, sc.sample, sh.prompt, sh.sample)


def _final_norm_kernel(xp_ref, xs_ref, g_ref, yp_ref, ys_ref):
    def body(tail):
        src, dst = (xs_ref, ys_ref) if tail else (xp_ref, yp_ref)
        x = src[...]
        dst[...] = x * lax.rsqrt(jnp.mean(x * x, axis=-1, keepdims=True) + EPS) * g_ref[...]

    _split_tiles(0, body)


def final_norm(xp, xs, g, tm):
    mp, d = xp.shape
    ms = xs.shape[0]
    nf = mp // tm
    p_spec = pl.BlockSpec((tm, d), lambda i: (_prompt_tile(i), 0))
    s_spec = pl.BlockSpec((ms, d), lambda i: (0, 0))
    return pl.pallas_call(
        _final_norm_kernel,
        grid=(nf + 1,),
        in_specs=[p_spec, s_spec, pl.BlockSpec((1, d), lambda i: (0, 0))],
        out_specs=[p_spec, s_spec],
        out_shape=[jax.ShapeDtypeStruct((mp, d), F32), jax.ShapeDtypeStruct((ms, d), F32)],
        compiler_params=_params(1),
        name="final_norm",
    )(xp, xs, g)


def _mm_kernel(a_ref, w_ref, o_ref, wbf_ref, *, ms):
    @pl.when(pl.program_id(1) == 0)
    def _():
        wbf_ref[...] = w_ref[0].astype(BF16)

    def body(tail):
        rows = slice(0, ms) if tail else slice(None)
        o_ref[rows, :] = jnp.dot(a_ref[rows, :], wbf_ref[...], preferred_element_type=F32).astype(o_ref.dtype)

    _split_tiles(1, body)


def matmul(a, w, layer, ms, tm, tn, out_dtype):
    m, k = a.shape
    n = w.shape[2]
    nf = (m - ms) // tm
    return pl.pallas_call(
        functools.partial(_mm_kernel, ms=ms),
        grid=(n // tn, nf + 1),
        in_specs=[pl.BlockSpec((tm, k), lambda j, i: (_joint_tile(i, nf), 0)),
                  pl.BlockSpec((1, k, tn), lambda j, i: (layer, 0, j))],
        out_specs=pl.BlockSpec((tm, tn), lambda j, i: (_joint_tile(i, nf), j)),
        out_shape=jax.ShapeDtypeStruct((m, n), out_dtype),
        scratch_shapes=[pltpu.VMEM((k, tn), BF16)],
        compiler_params=_params(2),
        name="matmul",
    )(a, w)


IN_TN = 1536


def _in_proj_kernel(a_ref, w_ref, lg_ref, lb_ref, ws_ref, bst_ref, pw_ref, pb_ref, ps_ref, cw_ref,
                    o_ref, ya_ref, yc_ref, yd_ref, zt_ref, wbf_ref, cbuf_ref, zbuf_ref, *, ms, per_seq):
    j = pl.program_id(0)
    i = pl.program_id(1)
    tm = a_ref.shape[0]
    tile = i - 1
    first = (tile % per_seq) == 0

    def idle(skip=None):
        for ref in (ya_ref, yc_ref, yd_ref):
            if ref is not skip:
                ref[...] = jnp.zeros(ref.shape, ref.dtype)
        if skip is not yd_ref:
            zt_ref[...] = jnp.zeros(zt_ref.shape, zt_ref.dtype)

    def project():
        acc = jnp.dot(a_ref[...], wbf_ref[...], preferred_element_type=F32)
        o_ref[...] = acc
        return acc

    @pl.when(i == 0)
    def _():
        wbf_ref[...] = w_ref[0].astype(BF16)
        o_ref[0:ms, :] = jnp.dot(a_ref[0:ms, :], wbf_ref[...], preferred_element_type=F32)
        idle()

    @pl.when((i > 0) & first & (j == COL_C // IN_TN))
    def _():
        cbuf_ref[0:POOL_HALO, :] = jnp.zeros((POOL_HALO, BR_W), F32)

    @pl.when((i > 0) & first & (j == COL_D // IN_TN))
    def _():
        zbuf_ref[0:CONV_HALO, :] = jnp.zeros((CONV_HALO, BR_W), F32)

    @pl.when((i > 0) & (j == COL_A // IN_TN))
    def _():
        acc = project()
        _gmlp_tile(acc[:, 0:BR_W], acc[:, BR_W:2 * BR_W], lg_ref, lb_ref, ws_ref, bst_ref, ya_ref, CHUNK)
        idle(ya_ref)

    @pl.when((i > 0) & (j == COL_C // IN_TN))
    def _():
        acc = project()
        c0 = COL_C % IN_TN
        _pool_tile(acc[:, c0:c0 + BR_W], (tile % per_seq) * tm, cbuf_ref, pw_ref, pb_ref, ps_ref, yc_ref)
        idle(yc_ref)

    @pl.when((i > 0) & (j == COL_D // IN_TN))
    def _():
        acc = project()
        _conv_tile(acc[:, 0:BR_W], acc[:, BR_W:2 * BR_W], acc[:, 2 * BR_W:3 * BR_W], zbuf_ref, cw_ref, yd_ref, zt_ref)
        idle(yd_ref)

    @pl.when((i > 0) & (j != COL_A // IN_TN) & (j != COL_C // IN_TN) & (j != COL_D // IN_TN))
    def _():
        project()
        idle()


def in_proj_mixers(a, w, layer, ms, tm, nseq, gm, pc):
    m, k = a.shape
    n = w.shape[2]
    nf = (m - ms) // tm
    per_seq = nf // nseq
    ln_g, ln_b, ws, bst = gm
    pool_w, pool_b, pool_scale, conv_wt, _ = pc
    ga, gc, gd = COL_A // IN_TN, COL_C // IN_TN, COL_D // IN_TN

    def parked(j, i, tile_j, live, n_live):
        before = (j < tile_j) | ((j == tile_j) & (i == 0))
        return jnp.where((j == tile_j) & (i > 0), live, jnp.where(before, n_live, n_live + 1))

    def mixer_spec(tile_j):
        return pl.BlockSpec((tm, BR_W), lambda j, i: (parked(j, i, tile_j, i - 1, nf), 0))

    vec = pl.BlockSpec((1, BR_W), lambda j, i: (0, 0))
    return pl.pallas_call(
        functools.partial(_in_proj_kernel, ms=ms, per_seq=per_seq),
        grid=(n // IN_TN, nf + 1),
        in_specs=[pl.BlockSpec((tm, k), lambda j, i: (_joint_tile(i, nf), 0)),
                  pl.BlockSpec((1, k, IN_TN), lambda j, i: (layer, 0, j)),
                  vec, vec,
                  pl.BlockSpec((1, 4, CHUNK, CHUNK), lambda j, i: (0, 0, 0, 0)),
                  pl.BlockSpec((1, CHUNK, 4), lambda j, i: (0, 0, 0)),
                  pl.BlockSpec((1, 4, GW, GW), lambda j, i: (layer, 0, 0, 0)),
                  vec, vec,
                  pl.BlockSpec((CONV_W, BR_W), lambda j, i: (0, 0))],
        out_specs=[pl.BlockSpec((tm, IN_TN), lambda j, i: (_joint_tile(i, nf), j)),
                   mixer_spec(ga), mixer_spec(gc), mixer_spec(gd),
                   pl.BlockSpec((1, CONV_HALO, BR_W),
                                lambda j, i: (parked(j, i, gd, (i - 1) // per_seq, nseq), 0, 0))],
        out_shape=[jax.ShapeDtypeStruct((m, n), F32)]
        + [jax.ShapeDtypeStruct(((nf + 2) * tm, BR_W), BF16)] * 3
        + [jax.ShapeDtypeStruct((nseq + 2, CONV_HALO, BR_W), F32)],
        scratch_shapes=[pltpu.VMEM((k, IN_TN), BF16),
                        pltpu.VMEM((POOL_HALO + tm, BR_W), F32), pltpu.VMEM((CONV_HALO + tm, BR_W), F32)],
        compiler_params=_params(2),
        name="in_proj_mixers",
    )(a, w, ln_g, ln_b, ws, bst, pool_w, pool_b, pool_scale, conv_wt)


def _mm_residual_kernel(a_ref, w_ref, xp_ref, xs_ref, gp_ref, gs_ref, op_ref, os_ref, wbf_ref):
    ms = xs_ref.shape[0]

    @pl.when(pl.program_id(1) == 0)
    def _():
        wbf_ref[...] = w_ref[0].astype(BF16)

    def body(tail):
        if tail:
            y = jnp.dot(a_ref[0:ms, :], wbf_ref[...], preferred_element_type=F32)
            os_ref[...] = xs_ref[...] + gs_ref[0] * y
        else:
            y = jnp.dot(a_ref[...], wbf_ref[...], preferred_element_type=F32)
            op_ref[...] = xp_ref[...] + gp_ref[0] * y

    _split_tiles(1, body)


def matmul_residual(a, w, layer, xp, xs, gate, tm, tn):
    m, k = a.shape
    n = w.shape[2]
    mp, ms = xp.shape[0], xs.shape[0]
    nf = mp // tm
    per = nf // gate.nb
    p_spec = pl.BlockSpec((tm, tn), lambda j, i: (_prompt_tile(i), j))
    s_spec = pl.BlockSpec((ms, tn), lambda j, i: (0, j))
    return pl.pallas_call(
        _mm_residual_kernel,
        grid=(n // tn, nf + 1),
        in_specs=[pl.BlockSpec((tm, k), lambda j, i: (_joint_tile(i, nf), 0)),
                  pl.BlockSpec((1, k, tn), lambda j, i: (layer, 0, j)),
                  p_spec, s_spec, *_mod_specs(gate, tn, per, 2)],
        out_specs=[p_spec, s_spec],
        out_shape=[jax.ShapeDtypeStruct((mp, n), F32), jax.ShapeDtypeStruct((ms, n), F32)],
        scratch_shapes=[pltpu.VMEM((k, tn), BF16)],
        compiler_params=_params(2),
        name="matmul_residual",
    )(a, w, xp, xs, gate.prompt, gate.sample)


def _out_proj_norm_kernel(a_ref, w_ref, xp_ref, xs_ref, gp_ref, gs_ref, ln_ref, scp_ref, scs_ref,
                          shp_ref, shs_ref, op_ref, os_ref, hn_ref, wbf_ref):
    ms = xs_ref.shape[0]

    @pl.when(pl.program_id(0) == 0)
    def _():
        wbf_ref[...] = w_ref[0].astype(BF16)

    def body(tail):
        if tail:
            x = xs_ref[...] + gs_ref[0] * jnp.dot(a_ref[0:ms, :], wbf_ref[...], preferred_element_type=F32)
            os_ref[...] = x
            hn_ref[0:ms, :] = _norm_rows(x, ln_ref[...], scs_ref[0], shs_ref[0]).astype(hn_ref.dtype)
        else:
            x = xp_ref[...] + gp_ref[0] * jnp.dot(a_ref[...], wbf_ref[...], preferred_element_type=F32)
            op_ref[...] = x
            hn_ref[...] = _norm_rows(x, ln_ref[...], scp_ref[0], shp_ref[0]).astype(hn_ref.dtype)

    _split_tiles(0, body)


def out_proj_norm(a, w, layer, xp, xs, gate, ln_g, sc, sh, tm):
    m, k = a.shape
    n = w.shape[2]
    mp, ms = xp.shape[0], xs.shape[0]
    nf = mp // tm
    per = nf // gate.nb
    p_spec = pl.BlockSpec((tm, n), lambda i: (_prompt_tile(i), 0))
    s_spec = pl.BlockSpec((ms, n), lambda i: (0, 0))
    return pl.pallas_call(
        _out_proj_norm_kernel,
        grid=(nf + 1,),
        in_specs=[pl.BlockSpec((tm, k), lambda i: (_joint_tile(i, nf), 0)),
                  pl.BlockSpec((1, k, n), lambda i: (layer, 0, 0), pipeline_mode=pl.Buffered(1)),
                  p_spec, s_spec, *_mod_specs(gate, n, per, 1),
                  pl.BlockSpec((1, n), lambda i: (0, 0)),
                  *_mod_specs(sc, n, per, 1), *_mod_specs(sh, n, per, 1)],
        out_specs=[p_spec, s_spec, pl.BlockSpec((tm, n), lambda i: (_joint_tile(i, nf), 0))],
        out_shape=[jax.ShapeDtypeStruct((mp, n), F32), jax.ShapeDtypeStruct((ms, n), F32),
                   jax.ShapeDtypeStruct((m, n), BF16)],
        scratch_shapes=[pltpu.VMEM((k, n), BF16)],
        compiler_params=_params(1),
        name="out_proj_norm",
    )(a, w, xp, xs, gate.prompt, gate.sample, ln_g, sc.prompt, sc.sample, sh.prompt, sh.sample)


def _swiglu_kernel(a_ref, w1_ref, w3_ref, o_ref, w1bf_ref, w3bf_ref, *, ms):
    @pl.when(pl.program_id(1) == 0)
    def _():
        w1bf_ref[...] = w1_ref[0].astype(BF16)
        w3bf_ref[...] = w3_ref[0].astype(BF16)

    def body(tail):
        rows = slice(0, ms) if tail else slice(None)
        a = a_ref[rows, :]
        h1 = jnp.dot(a, w1bf_ref[...], preferred_element_type=F32)
        h3 = jnp.dot(a, w3bf_ref[...], preferred_element_type=F32)
        o_ref[rows, :] = (jax.nn.silu(h1) * h3).astype(o_ref.dtype)

    _split_tiles(1, body)


def swiglu_up(a, w1, w3, layer, ms, tm, tn):
    m, k = a.shape
    n = w1.shape[2]
    nf = (m - ms) // tm
    w_spec = pl.BlockSpec((1, k, tn), lambda j, i: (layer, 0, j))
    return pl.pallas_call(
        functools.partial(_swiglu_kernel, ms=ms),
        grid=(n // tn, nf + 1),
        in_specs=[pl.BlockSpec((tm, k), lambda j, i: (_joint_tile(i, nf), 0)), w_spec, w_spec],
        out_specs=pl.BlockSpec((tm, tn), lambda j, i: (_joint_tile(i, nf), j)),
        out_shape=jax.ShapeDtypeStruct((m, n), BF16),
        scratch_shapes=[pltpu.VMEM((k, tn), BF16), pltpu.VMEM((k, tn), BF16)],
        compiler_params=_params(2),
        name="swiglu_up",
    )(a, w1, w3)


def _gate_merge_kernel(hn_ref, yap_ref, ybp_ref, ycp_ref, ydp_ref, yas_ref, ybs_ref, ycs_ref, yds_ref,
                       wg0_ref, wg1_ref, wg2_ref, wg3_ref, wb_ref, bg_ref, o_ref, wgbf_ref, wbbf_ref):
    ms = yas_ref.shape[0]

    @pl.when(pl.program_id(1) == 0)
    def _():
        for g, wg_ref in enumerate((wg0_ref, wg1_ref, wg2_ref, wg3_ref)):
            wgbf_ref[g] = wg_ref[0].astype(BF16)
            wbbf_ref[g] = wb_ref[0, g].astype(BF16)

    def body(tail):
        rows = slice(0, ms) if tail else slice(None)
        y_refs = (yas_ref, ybs_ref, ycs_ref, yds_ref) if tail else (yap_ref, ybp_ref, ycp_ref, ydp_ref)
        hn = hn_ref[rows, :]
        acc = None
        for g, y_ref in enumerate(y_refs):
            gate = jax.nn.sigmoid(jnp.dot(hn, wgbf_ref[g], preferred_element_type=F32) + bg_ref[0, g:g + 1, :])
            br = jnp.dot(y_ref[...], wbbf_ref[g], preferred_element_type=F32)
            acc = gate * br if acc is None else acc + gate * br
        o_ref[rows, :] = acc.astype(o_ref.dtype)

    _split_tiles(1, body)


def gate_merge(hn, ys_p, ys_s, w_gate, b_gate, w_branch, layer, tm, tn):
    m, k = hn.shape
    ms = ys_s[0].shape[0]
    nf = (m - ms) // tm
    depth = w_gate.shape[0]
    d = w_branch.shape[3]
    nj = d // tn
    wg_specs = [pl.BlockSpec((1, k, tn), functools.partial(lambda j, i, g: (layer, 0, g * nj + j), g=g))
                for g in range(4)]
    yp_spec = pl.BlockSpec((tm, BR_W), lambda j, i: (_prompt_tile(i), 0))
    ys_spec = pl.BlockSpec((ms, BR_W), lambda j, i: (0, 0))
    return pl.pallas_call(
        _gate_merge_kernel,
        grid=(nj, nf + 1),
        in_specs=[pl.BlockSpec((tm, k), lambda j, i: (_joint_tile(i, nf), 0)), *[yp_spec] * 4, *[ys_spec] * 4,
                  *wg_specs,
                  pl.BlockSpec((1, 4, BR_W, tn), lambda j, i: (layer, 0, 0, j)),
                  pl.BlockSpec((1, 4, tn), lambda j, i: (layer, 0, j))],
        out_specs=pl.BlockSpec((tm, tn), lambda j, i: (_joint_tile(i, nf), j)),
        out_shape=jax.ShapeDtypeStruct((m, d), BF16),
        scratch_shapes=[pltpu.VMEM((4, k, tn), BF16), pltpu.VMEM((4, BR_W, tn), BF16)],
        compiler_params=_params(2),
        name="gate_merge",
    )(hn, *ys_p, *ys_s, w_gate, w_gate, w_gate, w_gate, w_branch, b_gate.reshape(depth, 4, d))


def _gmlp_kernel(pu_ref, pv_ref, lg_ref, lb_ref, ws_ref, bst_ref, ya_ref, *maybe_va_ref, chunk):
    vn = _gmlp_tile(pu_ref[...], pv_ref[...], lg_ref, lb_ref, ws_ref, bst_ref, ya_ref, chunk)
    for va_ref in maybe_va_ref:
        va_ref[...] = vn


def _gmlp_tile(pu, pv, lg_ref, lb_ref, ws_ref, bst_ref, ya_ref, chunk):
    rows = pu.shape[0]
    u = jax.nn.gelu(pu)
    v = jax.nn.gelu(pv)
    mu = jnp.mean(v, axis=-1, keepdims=True)
    var = jnp.mean(jnp.square(v - mu), axis=-1, keepdims=True)
    vn = (v - mu) * lax.rsqrt(var + 1e-5) * lg_ref[...] + lb_ref[...]
    causal = (lax.broadcasted_iota(jnp.int32, (CHUNK, CHUNK), 0)
              >= lax.broadcasted_iota(jnp.int32, (CHUNK, CHUNK), 1))
    for g in range(4):
        wm = jnp.where(causal, ws_ref[0, g], 0.0).astype(BF16)
        bias = bst_ref[0, :, g:g + 1]
        for c in range(rows // chunk):
            r0 = c * chunk
            vc = vn[r0:r0 + chunk, g * GW:(g + 1) * GW]
            if chunk < CHUNK:
                vc = jnp.concatenate([vc, jnp.zeros((CHUNK - chunk, GW), F32)], axis=0)
            mix = (jnp.dot(wm, vc.astype(BF16), preferred_element_type=F32) + bias)[:chunk]
            ya_ref[r0:r0 + chunk, g * GW:(g + 1) * GW] = (
                u[r0:r0 + chunk, g * GW:(g + 1) * GW] * mix).astype(ya_ref.dtype)
    return vn


def gmlp(p, ln_g, ln_b, ws, bst, row0, m, tr, chunk, emit_v):
    blk0 = row0 // tr
    in_spec = lambda col: pl.BlockSpec((tr, BR_W), lambda i: (blk0 + i, col))
    out_spec = pl.BlockSpec((tr, BR_W), lambda i: (i, 0))
    n_out = 2 if emit_v else 1
    return pl.pallas_call(
        functools.partial(_gmlp_kernel, chunk=chunk),
        grid=(m // tr,),
        in_specs=[in_spec(0), in_spec(1),
                  pl.BlockSpec((1, BR_W), lambda i: (0, 0)),
                  pl.BlockSpec((1, BR_W), lambda i: (0, 0)),
                  pl.BlockSpec((1, 4, CHUNK, CHUNK), lambda i: (0, 0, 0, 0)),
                  pl.BlockSpec((1, CHUNK, 4), lambda i: (0, 0, 0))],
        out_specs=[out_spec] * n_out,
        out_shape=[jax.ShapeDtypeStruct((m, BR_W), BF16), jax.ShapeDtypeStruct((m, BR_W), F32)][:n_out],
        compiler_params=_params(1),
        name="gmlp",
    )(p, p, ln_g, ln_b, ws, bst)


def _bias_from_buckets(idx, rel_ref, col, shape):
    bias = jnp.full(shape, NEG, F32)
    for b in range(N_BUCKETS):
        bias = jnp.where(idx == b, rel_ref[b, col], bias)
    return bias


def _bias_mask_kernel(rel_ref, idx_ref, o_ref):
    o_ref[0, 0] = _bias_from_buckets(idx_ref[0], rel_ref, pl.program_id(0) * N_HEADS + pl.program_id(1),
                                     (NK, 2 * NK))


def prompt_bias_mask(rel_bias):
    return pl.pallas_call(
        _bias_mask_kernel,
        grid=(len(PATTERNS), N_HEADS),
        in_specs=[pl.BlockSpec(memory_space=pltpu.SMEM),
                  pl.BlockSpec((1, NK, 2 * NK), lambda g, h: (g, 0, 0))],
        out_specs=pl.BlockSpec((1, 1, NK, 2 * NK), lambda g, h: (g, h, 0, 0)),
        out_shape=jax.ShapeDtypeStruct((len(PATTERNS), N_HEADS, NK, 2 * NK), F32),
        compiler_params=_params(2),
        name="prompt_bias_mask",
    )(rel_bias, jnp.asarray(_prompt_buckets()))


def _attn_prompt_kernel(q_ref, k_ref, v_ref, bm_ref, o_ref,
                        o0_ref, o1_ref, o2_ref, e0_ref, e1_ref, e2_ref):
    t = q_ref.shape[0]
    group = pl.program_id(2)
    scale = GW ** -0.5
    outs = (o0_ref, o1_ref, o2_ref)
    lses = (e0_ref, e1_ref, e2_ref)

    def rows(start, n, dil):
        return pl.ds(start, n) if dil == 1 else pl.ds(start, n, stride=dil)

    def blocks(gi, specs):
        dil = PATTERNS[gi][1]
        staged = []
        for first_block, q0, k0 in specs:
            nkeys = NK if first_block else 2 * NK
            qi = rows(q0, NK, dil)
            ki = rows(k0, nkeys, dil)
            q = q_ref[qi, :].astype(BF16)
            kk = k_ref[ki, :].astype(BF16)
            bm = bm_ref[0, 0, :, NK:] if first_block else bm_ref[0, 0]
            s = lax.dot_general(q, kk, (((1,), (1,)), ((), ())), preferred_element_type=F32) * scale + bm
            staged.append((qi, ki, s))
        probs = []
        for qi, ki, s in staged:
            m = jnp.max(s, axis=-1, keepdims=True)
            p = jnp.exp(s - m)
            l = jnp.sum(p, axis=-1, keepdims=True)
            probs.append((qi, ki, p.astype(BF16), m, l))
        for qi, ki, p, m, l in probs:
            acc = jnp.dot(p, v_ref[ki, :].astype(BF16), preferred_element_type=F32)
            outs[gi][qi, :] = acc * (1.0 / l)
            lses[gi][qi, :] = jnp.broadcast_to(m + jnp.log(l), (NK, GW))

    def run_group(gi):
        dil = PATTERNS[gi][1]
        span = NK * dil
        nb = t // span
        u = ATTN_BLOCKS_PER_STEP
        if dil == 1:
            blocks(gi, [(True, 0, 0)] + [(False, n * span, (n - 1) * span) for n in range(1, u)])

            def step(i, c):
                q0 = pl.multiple_of(u * i * span, span)
                blocks(gi, [(False, q0 + n * span, q0 + (n - 1) * span) for n in range(u)])
                return c

            lax.fori_loop(1, nb // u, step, 0)
        else:
            def bunch(i, c):
                res = [u * i + r for r in range(u)]
                blocks(gi, [(True, r, r) for r in res])

                def later(n, cc):
                    blocks(gi, [(False, n * span + r, (n - 1) * span + r) for r in res])
                    return cc

                return lax.fori_loop(1, nb, later, c)

            if dil == u:
                bunch(0, 0)
            else:
                lax.fori_loop(0, dil // u, bunch, 0)

    for gi in range(len(PATTERNS)):
        pl.when(group == gi)(functools.partial(run_group, gi))

    @pl.when(group == len(PATTERNS) - 1)
    def _():
        step = 256

        def merge(c, carry):
            sl = pl.ds(pl.multiple_of(c * step, step), step)
            e = [r[sl, :] for r in lses]
            top = jnp.maximum(jnp.maximum(e[0], e[1]), e[2])
            w = [jnp.exp(x - top) for x in e]
            num = w[0] * o0_ref[sl, :] + w[1] * o1_ref[sl, :] + w[2] * o2_ref[sl, :]
            o_ref[sl, :] = (num / (w[0] + w[1] + w[2])).astype(o_ref.dtype)
            return carry

        lax.fori_loop(0, t // step, merge, 0)


def _prompt_buckets():
    i = np.arange(NK)[:, None]
    j = np.arange(2 * NK)[None, :]
    diff = NK + i - j
    valid = (diff >= 0) & (diff <= NK)
    return np.stack([np.where(valid, _t5_bucket(diff * dil), -1) for _, dil in PATTERNS]).astype(np.int32)


def attention_prompt(p, bias_mask, nseq, t):
    def qkv_spec(which):
        return pl.BlockSpec((t, GW), lambda b, h, g: (b, COL_B // GW + g * 12 + which * 4 + h))

    return pl.pallas_call(
        _attn_prompt_kernel,
        grid=(nseq, N_HEADS, len(PATTERNS)),
        in_specs=[qkv_spec(0), qkv_spec(1), qkv_spec(2),
                  pl.BlockSpec((1, 1, NK, 2 * NK), lambda b, h, g: (g, h, 0, 0))],
        out_specs=pl.BlockSpec((t, GW), lambda b, h, g: (b, h)),
        out_shape=jax.ShapeDtypeStruct((nseq * t, BR_W), BF16),
        scratch_shapes=[pltpu.VMEM((t, GW), F32)] * 6,
        compiler_params=_params(3),
        name="attention_prompt",
    )(p, p, p, bias_mask)


N_NEW = 4
TILE_ROWS = 2 * N_HEADS


def _key_to_value_rows(x):
    n, r, w = x.shape
    return pltpu.roll(x.reshape(n * r, w), N_HEADS, axis=0).reshape(n, r, w)


def _attend(x, xn, q, bias, bias_n, scale):
    s = jnp.sum(x * q[None], axis=-1, keepdims=True) * scale + bias
    sn = jnp.sum(xn * q[None], axis=-1, keepdims=True) * scale + bias_n
    m = jnp.maximum(jnp.max(s, axis=0, keepdims=True), jnp.max(sn, axis=0, keepdims=True))
    p = jnp.exp(s - m)
    pn = jnp.exp(sn - m)
    l = jnp.sum(p, axis=0, keepdims=True) + jnp.sum(pn, axis=0, keepdims=True)
    acc = jnp.sum(_key_to_value_rows(p) * x, axis=0) + jnp.sum(_key_to_value_rows(pn) * xn, axis=0)
    return acc, m, l


def _attn_sample_kernel(rel_ref, q_ref, xn_ref, c1_ref, c2_ref, c3_ref, o_ref,
                        b1_ref, b1n_ref, b23_ref, b23n_ref):
    scale = GW ** -0.5
    rows = N_NEW * TILE_ROWS

    @pl.when(pl.program_id(0) == 0)
    def _():
        head = lax.broadcasted_iota(jnp.int32, (rows, GW), 0) % N_HEADS
        neg = jnp.full((rows, GW), NEG, F32)

        def tiles_for(gi):
            out = []
            for b in range(N_BUCKETS):
                v = [rel_ref[b, gi * N_HEADS + h] for h in range(N_HEADS)]
                out.append(jnp.where(head == 0, v[0], jnp.where(head == 1, v[1], jnp.where(head == 2, v[2], v[3]))))
            return out

        t1 = tiles_for(0)
        for t in range(N_NEW):
            for pos in range(NK):
                step = NK + t - pos
                tile = t1[int(_t5_bucket(np.int64(step)))] if step <= NK else neg
                b1_ref[t, pos] = tile[:TILE_ROWS]
            for j in range(N_NEW):
                tile = t1[int(_t5_bucket(np.int64(t - j)))] if j <= t else neg
                b1n_ref[t, j] = tile[:TILE_ROWS]
        for gi in (1, 2):
            dil = PATTERNS[gi][1]
            tg = tiles_for(gi)
            for jj in range(NK):
                b23_ref[gi - 1, jj] = tg[int(_t5_bucket(np.int64((NK - jj) * dil)))]
            b23n_ref[gi - 1] = tg[0]

    outs, lses = [], []
    x1 = c1_ref[0, 0]
    xn1 = xn_ref[0, 0].reshape(N_NEW, TILE_ROWS, GW)
    o1, e1 = [], []
    for t in range(N_NEW):
        q = q_ref[0, 0, t * TILE_ROWS:(t + 1) * TILE_ROWS, :]
        acc, m, l = _attend(x1, xn1, q, b1_ref[t], b1n_ref[t], scale)
        o1.append(acc)
        e1.append((m, l))
    outs.append(jnp.concatenate(o1, axis=0))
    lses.append((jnp.concatenate([m[0] for m, _ in e1], axis=0), jnp.concatenate([l[0] for _, l in e1], axis=0)))
    for gi, c_ref in ((1, c2_ref), (2, c3_ref)):
        acc, m, l = _attend(c_ref[0, 0], xn_ref[gi, 0][None], q_ref[gi, 0], b23_ref[gi - 1],
                            b23n_ref[gi - 1][None], scale)
        outs.append(acc)
        lses.append((m[0], l[0]))

    lse = [_key_to_value_rows((m + jnp.log(l))[None])[0] for m, l in lses]
    den = [_key_to_value_rows(l[None])[0] for _, l in lses]
    top = jnp.maximum(jnp.maximum(lse[0], lse[1]), lse[2])
    w = [jnp.exp(e - top) for e in lse]
    num = w[0] * outs[0] / den[0] + w[1] * outs[1] / den[1] + w[2] * outs[2] / den[2]
    o_ref[0] = num / (w[0] + w[1] + w[2])


def attention_sample(p, cache_b1, cache_b2, cache_b3, rel_bias, layer, row0, nseq):
    depth = cache_b1.shape[0]
    rows = N_NEW * TILE_ROWS
    n_groups = len(PATTERNS)
    qkv = p[row0:, COL_B:COL_B + n_groups * 3 * BR_W].reshape(nseq, SAMPLE_ROWS, n_groups, 3, N_HEADS, GW)
    qkv = qkv[:, :N_NEW].transpose(2, 0, 1, 3, 4, 5)
    q = qkv[:, :, :, 0:1]
    q_all = jnp.concatenate([q, jnp.zeros_like(q)], axis=3).reshape(n_groups, nseq, rows, GW)
    xn_all = qkv[:, :, :, 1:3].reshape(n_groups, nseq, rows, GW)
    c1 = cache_b1.reshape(depth, nseq, NK, TILE_ROWS, GW)
    c2 = cache_b2.reshape(depth, nseq, NK, 4 * TILE_ROWS, GW)
    c3 = cache_b3.reshape(depth, nseq, NK, 16 * TILE_ROWS, GW)
    cache_spec = lambda r: pl.BlockSpec((1, 1, NK, r, GW), lambda b: (layer, b, 0, 0, 0))
    tok_spec = pl.BlockSpec((len(PATTERNS), 1, rows, GW), lambda b: (0, b, 0, 0))
    y = pl.pallas_call(
        _attn_sample_kernel,
        grid=(nseq,),
        in_specs=[pl.BlockSpec(memory_space=pltpu.SMEM), tok_spec, tok_spec,
                  cache_spec(TILE_ROWS), cache_spec(rows), cache_spec(rows)],
        out_specs=pl.BlockSpec((1, rows, GW), lambda b: (b, 0, 0)),
        out_shape=jax.ShapeDtypeStruct((nseq, rows, GW), F32),
        scratch_shapes=[pltpu.VMEM((N_NEW, NK, TILE_ROWS, GW), F32), pltpu.VMEM((N_NEW, N_NEW, TILE_ROWS, GW), F32),
                        pltpu.VMEM((2, NK, rows, GW), F32), pltpu.VMEM((2, rows, GW), F32)],
        compiler_params=_params(1),
        name="attention_sample",
    )(rel_bias, q_all, xn_all, c1, c2, c3)
    y = y.reshape(nseq, N_NEW, 2, BR_W)[:, :, 1]
    y = jnp.pad(y, ((0, 0), (0, SAMPLE_ROWS - N_NEW), (0, 0)))
    return y.reshape(nseq * SAMPLE_ROWS, BR_W).astype(BF16)


def _pool_conv_kernel(pc_ref, pbg_ref, pcg_ref, phs_ref, hc_ref, hd_ref, pw_ref, pb_ref, ps_ref, cw_ref,
                      yc_ref, yd_ref, zt_ref, cbuf_ref, zbuf_ref, *, start):
    tr = pc_ref.shape[0]
    j = pl.program_id(1)

    @pl.when(j == 0)
    def _():
        cbuf_ref[0:POOL_HALO, :] = hc_ref[0]
        zbuf_ref[0:CONV_HALO, :] = hd_ref[0]

    _pool_tile(pc_ref[...], start + j * tr, cbuf_ref, pw_ref, pb_ref, ps_ref, yc_ref)
    _conv_tile(pbg_ref[...], pcg_ref[...], phs_ref[...], zbuf_ref, cw_ref, yd_ref, zt_ref)


POOL_HALO = 16
CONV_HALO = 8


def _pool_tile(x, pos0, cbuf_ref, pw_ref, pb_ref, ps_ref, yc_ref):
    tr = x.shape[0]
    hc = POOL_HALO
    cbuf_ref[hc:hc + tr, :] = x
    pos = pos0 + lax.broadcasted_iota(jnp.int32, (tr, 1), 0)
    for gi, win in enumerate(POOL_WINDOWS):
        cols = slice(gi * GW, (gi + 1) * GW)
        total = x[:, cols]
        for back in range(1, win):
            total = total + cbuf_ref[hc - back:hc - back + tr, cols]
        cnt = jnp.minimum(pos + 1, win).astype(F32)
        pooled = total / cnt - x[:, cols]
        y = jnp.dot(pooled.astype(BF16), pw_ref[0, gi].astype(BF16), preferred_element_type=F32)
        yc_ref[:, cols] = ((y + pb_ref[:, cols]) * ps_ref[:, cols]).astype(yc_ref.dtype)
    cbuf_ref[0:hc, :] = cbuf_ref[tr:tr + hc, :]


def _conv_tile(bg, cg, hs, zbuf_ref, cw_ref, yd_ref, zt_ref):
    tr = bg.shape[0]
    hz = CONV_HALO
    z = cg * hs
    zbuf_ref[hz:hz + tr, :] = z
    y = (zbuf_ref[hz - 2:hz - 2 + tr, :] * cw_ref[0:1, :] + zbuf_ref[hz - 1:hz - 1 + tr, :] * cw_ref[1:2, :]
         + z * cw_ref[2:3, :])
    yd_ref[...] = (bg * y).astype(yd_ref.dtype)
    zt_ref[0] = z[tr - hz:, :]
    zbuf_ref[0:hz, :] = zbuf_ref[tr:tr + hz, :]


def pool_conv(p, hist_c, hist_d, pool_w, pool_b, pool_scale, conv_wt, layer, row0, nseq, t, tr, start):
    per = t // tr
    blk0 = row0 // tr
    col = lambda c: pl.BlockSpec((tr, BR_W), lambda b, j: (b * per + j, c))
    pcol = lambda c: pl.BlockSpec((tr, BR_W), lambda b, j: (blk0 + b * per + j, c))
    vec = pl.BlockSpec((1, BR_W), lambda b, j: (0, 0))
    return pl.pallas_call(
        functools.partial(_pool_conv_kernel, start=start),
        grid=(nseq, per),
        in_specs=[pcol(COL_C // BR_W), pcol(COL_D // BR_W), pcol(COL_D // BR_W + 1), pcol(COL_D // BR_W + 2),
                  pl.BlockSpec((1, 16, BR_W), lambda b, j: (b, 0, 0)),
                  pl.BlockSpec((1, 8, BR_W), lambda b, j: (b, 0, 0)),
                  pl.BlockSpec((1, 4, GW, GW), lambda b, j: (layer, 0, 0, 0)),
                  vec, vec,
                  pl.BlockSpec((CONV_W, BR_W), lambda b, j: (0, 0))],
        out_specs=[col(0), col(0), pl.BlockSpec((1, 8, BR_W), lambda b, j: (b, 0, 0))],
        out_shape=[jax.ShapeDtypeStruct((nseq * t, BR_W), BF16),
                   jax.ShapeDtypeStruct((nseq * t, BR_W), BF16),
                   jax.ShapeDtypeStruct((nseq, 8, BR_W), F32)],
        scratch_shapes=[pltpu.VMEM((16 + tr, BR_W), F32), pltpu.VMEM((8 + tr, BR_W), F32)],
        compiler_params=_params(2),
        name="pool_conv",
    )(p, p, p, p, hist_c, hist_d, pool_w, pool_b, pool_scale, conv_wt)


CACHE_ROWS = 512


def _cache_writer_kernel(k1_ref, v1_ref, k2_ref, v2_ref, k3_ref, v3_ref, b1_any, b2_any, b3_any,
                         o1_ref, o2_ref, o3_ref):
    def scatter(o_ref, k_ref, v_ref, row0, n):
        for kv, src in enumerate((k_ref, v_ref)):
            for h in range(N_HEADS):
                o_ref[0, 0, pl.ds(kv * N_HEADS + h, n, stride=TILE_ROWS), :] = src[row0:row0 + n, h * GW:(h + 1) * GW]

    scatter(o3_ref, k3_ref, v3_ref, 0, CACHE_ROWS)

    @pl.when(pl.program_id(1) == pl.num_programs(1) - 1)
    def _():
        scatter(o2_ref, k2_ref, v2_ref, CACHE_ROWS - PATTERNS[1][0], PATTERNS[1][0])
        scatter(o1_ref, k1_ref, v1_ref, CACHE_ROWS - PATTERNS[0][0], PATTERNS[0][0])


def cache_writer(p, bufs, layer, nseq, t):
    b1, b2, b3 = bufs
    per = t // CACHE_ROWS
    steps = PATTERNS[2][0] // CACHE_ROWS

    def slab(gi, which, whole_window):
        col = (COL_B + gi * 3 * BR_W + (1 + which) * BR_W) // BR_W
        if whole_window:
            return pl.BlockSpec((CACHE_ROWS, BR_W), lambda b, j: ((b + 1) * per - steps + j, col))
        return pl.BlockSpec((CACHE_ROWS, BR_W), lambda b, j: ((b + 1) * per - 1, col))

    any_spec = pl.BlockSpec(memory_space=pl.ANY)
    tile = lambda rows: (1, 1, rows * TILE_ROWS, GW)
    return pl.pallas_call(
        _cache_writer_kernel,
        grid=(nseq, steps),
        in_specs=[slab(0, 0, False), slab(0, 1, False), slab(1, 0, False), slab(1, 1, False),
                  slab(2, 0, True), slab(2, 1, True),
                  any_spec, any_spec, any_spec],
        out_specs=[pl.BlockSpec(tile(PATTERNS[0][0]), lambda b, j: (layer, b, 0, 0)),
                   pl.BlockSpec(tile(PATTERNS[1][0]), lambda b, j: (layer, b, 0, 0)),
                   pl.BlockSpec(tile(CACHE_ROWS), lambda b, j: (layer, b, j, 0))],
        out_shape=[jax.ShapeDtypeStruct(x.shape, x.dtype) for x in bufs],
        input_output_aliases={6: 0, 7: 1, 8: 2},
        compiler_params=_params(2),
        name="cache_writer",
    )(p, p, p, p, p, p, b1, b2, b3)


TM_WIDE = 1024
TM_ROWS = 512


def kernel(x_prompt, x_sample, c_prompt, c_sample, cache_b1, cache_b2, cache_b3, cache_pool, cache_conv,
           ln1_g, ln2_g, w_ada, b_ada, w_in, w_gate, b_gate, a_ln_g, a_ln_b, a_ws, a_bs,
           rel_bias, pool_w, pool_b, pool_scale, conv_w, w_branch, w_out, w1, w3, w2, final_g):
    nb, seq, d = x_prompt.shape
    ns, dec = x_sample.shape[:2]
    depth = w_in.shape[0]
    mp = nb * seq
    ms = ns * SAMPLE_ROWS

    c_all = jnp.concatenate([c_prompt, c_sample], axis=0)
    c_rows = -(-c_all.shape[0] // 8) * 8
    c_all = jnp.pad(c_all, ((0, c_rows - c_all.shape[0]), (0, 0)))
    mod = ada_modulation(c_all, w_ada, b_ada).reshape(depth, c_rows, 6, d)
    mod_p = mod[:, :nb].transpose(0, 2, 1, 3).reshape(depth * 6 * nb, 1, d)
    mod_s = jnp.repeat(mod[:, nb:nb + ns], SAMPLE_ROWS, axis=1).transpose(0, 2, 1, 3).reshape(depth * 6, ms, d)
    mods = [[Mod(mod_p, mod_s, l * 6 + i, nb) for i in range(6)] for l in range(depth)]

    xp = x_prompt.reshape(mp, d)
    xs = jnp.pad(x_sample, ((0, 0), (0, SAMPLE_ROWS - dec), (0, 0))).reshape(ms, d)
    zero_hist_c = jnp.zeros((nb, 16, BR_W), F32)
    zero_hist_d = jnp.zeros((nb, 8, BR_W), F32)

    bias_mask = prompt_bias_mask(rel_bias)
    kv_bufs = tuple(jnp.zeros((depth, nb, win * TILE_ROWS, GW), F32) for win, _ in PATTERNS)
    pool_p, conv_p, proj_s, conv_s, chunk_s = [], [], [], [], []
    a_bst = a_bs.transpose(0, 2, 1)
    conv_wt = conv_w.transpose(0, 2, 1)
    hist_c_all = jnp.pad(cache_pool, ((0, 0), (0, 0), (1, 0), (0, 0)))
    hist_d_all = jnp.pad(cache_conv, ((0, 0), (0, 0), (8 - (CONV_W - 1), 0), (0, 0)))
    for l in range(depth):
        sh1, sc1, g1, sh2, sc2, g2 = mods[l]
        hn = norm_in(xp, xs, ln1_g[l][None], sc1, sh1, TM_WIDE)
        gm = (a_ln_g[l][None], a_ln_b[l][None], a_ws[l][None], a_bst[l][None])
        pc = (pool_w, pool_b[l][None], pool_scale[l][None], conv_wt[l], l)
        p, ya_p, yc_p, yd_p, zt_p = in_proj_mixers(hn, w_in, l, ms, TM_ROWS, nb, gm, pc)
        ya_s, va_s = gmlp(p, *gm, mp, ms, ms, SAMPLE_ROWS, True)
        yb_p = attention_prompt(p, bias_mask, nb, seq)
        yb_s = attention_sample(p, cache_b1, cache_b2, cache_b3, rel_bias, l, mp, ns)
        yc_s, yd_s, zt_s = pool_conv(p, hist_c_all[l], hist_d_all[l], *pc, mp, ns, SAMPLE_ROWS, SAMPLE_ROWS,
                                     PAST_LEN)

        merged = gate_merge(hn, (ya_p, yb_p, yc_p, yd_p), (ya_s, yb_s, yc_s, yd_s),
                            w_gate, b_gate, w_branch, l, TM_WIDE, 256)
        xp, xs, hn2 = out_proj_norm(merged, w_out, l, xp, xs, g1, ln2_g[l][None], sc2, sh2, TM_ROWS)
        hmid = swiglu_up(hn2, w1, w3, l, ms, TM_WIDE, 512)
        xp, xs = matmul_residual(hmid, w2, l, xp, xs, g2, TM_ROWS, 512)

        kv_bufs = cache_writer(p, kv_bufs, l, nb, seq)
        pool_p.append(jnp.stack([p[(b + 1) * seq - POOL_HIST:(b + 1) * seq, COL_C:COL_C + BR_W]
                                 for b in range(nb)]))
        conv_p.append(zt_p[:nb])
        proj_s.append(p[mp:])
        conv_s.append(zt_s)
        chunk_s.append(va_s)

    yp, ys = final_norm(xp, xs, final_g[None], TM_WIDE)
    stack = lambda parts: jnp.stack(parts, axis=0)
    kv_p = [buf.reshape(depth, nb, win, 2, N_HEADS, GW) for buf, (win, _) in zip(kv_bufs, PATTERNS)]
    ps = stack(proj_s).reshape(depth, ns, SAMPLE_ROWS, N_IN)[:, :, :dec]
    kv_s = [ps[..., COL_B + (3 * gi + 1) * BR_W:COL_B + (3 * gi + 3) * BR_W].reshape(depth, ns, dec, 2, N_HEADS, GW)
            for gi in range(len(PATTERNS))]
    return (yp.reshape(nb, seq, d), ys.reshape(ns, SAMPLE_ROWS, d)[:, :dec],
            kv_p[0], kv_p[1], kv_p[2], stack(pool_p), stack(conv_p)[:, :, 8 - (CONV_W - 1):],
            kv_s[0], kv_s[1], kv_s[2], ps[..., COL_C:COL_C + BR_W], stack(conv_s)[:, :, :dec],
            stack(chunk_s).reshape(depth, ns, SAMPLE_ROWS, BR_W)[:, :, :dec])
```

```python
import functools
import math
from typing import NamedTuple

import jax
import jax.numpy as jnp
import numpy as np
from jax import lax
from jax.experimental import pallas as pl
from jax.experimental.pallas import tpu as pltpu

F32 = jnp.float32
BF16 = jnp.bfloat16

D_MODEL = 2048
BR_W = 512
GW = 128
N_HEADS = 4
CHUNK = 128
PATTERNS = ((128, 1), (512, 4), (2048, 16))
NK = 128
POOL_WINDOWS = (2, 4, 8, 16)
POOL_HIST = 15
CONV_W = 3
N_BUCKETS = 32
MAX_DIST = 2048
D_FF = 5632
N_IN = 7680
COL_A = 0
COL_B = 2 * BR_W
COL_C = COL_B + 9 * BR_W
COL_D = COL_C + BR_W
EPS = 1e-6
NEG = -1e30
SAMPLE_ROWS = 8
PAST_LEN = 16384
ATTN_BLOCKS_PER_STEP = 4

VMEM_LIMIT = 56 * 1024 * 1024


def _t5_bucket(dist):
    max_exact = N_BUCKETS // 2
    n = np.maximum(dist, 0)
    nf = np.maximum(n, 1).astype(np.float32)
    large = max_exact + (np.log(nf / np.float32(max_exact)) / np.float32(math.log(MAX_DIST / max_exact))
                         * np.float32(N_BUCKETS - max_exact)).astype(np.int32)
    large = np.minimum(large, N_BUCKETS - 1)
    return np.where(n < max_exact, n, large)


def _params(n_axes):
    return pltpu.CompilerParams(dimension_semantics=("arbitrary",) * n_axes,
                                vmem_limit_bytes=VMEM_LIMIT)


def _ada_kernel(c_ref, w_ref, b_ref, o_ref):
    a = jax.nn.silu(c_ref[...]).astype(BF16)
    o_ref[0] = jnp.dot(a, w_ref[0].astype(BF16), preferred_element_type=F32) + b_ref[0]


def ada_modulation(c_all, w_ada, b_ada):
    depth, d, n = w_ada.shape
    r = c_all.shape[0]
    tn = 1024
    return pl.pallas_call(
        _ada_kernel,
        grid=(depth, n // tn),
        in_specs=[pl.BlockSpec((r, d), lambda l, j: (0, 0)),
                  pl.BlockSpec((1, d, tn), lambda l, j: (l, 0, j)),
                  pl.BlockSpec((1, 1, tn), lambda l, j: (l, 0, j))],
        out_specs=pl.BlockSpec((1, r, tn), lambda l, j: (l, 0, j)),
        out_shape=jax.ShapeDtypeStruct((depth, r, n), F32),
        compiler_params=_params(2),
        name="ada_modulation",
    )(c_all, w_ada, b_ada.reshape(depth, 1, n))


def _split_tiles(axis, body):
    i = pl.program_id(axis)
    pl.when(i == 0)(functools.partial(body, True))
    pl.when(i > 0)(functools.partial(body, False))


def _joint_tile(i, nf):
    return (i + nf) % (nf + 1)


def _prompt_tile(i):
    return jnp.maximum(i - 1, 0)


def _norm_rows(x, g, sc, sh):
    y = x * lax.rsqrt(jnp.mean(x * x, axis=-1, keepdims=True) + EPS) * g
    return y * (1.0 + sc) + sh


class Mod(NamedTuple):
    prompt: jax.Array
    sample: jax.Array
    index: int
    nb: int


def _mod_specs(mod, width, per, grid_rank):
    ms = mod.sample.shape[1]
    base = mod.index * mod.nb
    if grid_rank == 1:
        return [pl.BlockSpec((1, 1, width), lambda i: (base + _prompt_tile(i) // per, 0, 0)),
                pl.BlockSpec((1, ms, width), lambda i: (mod.index, 0, 0))]
    return [pl.BlockSpec((1, 1, width), lambda j, i: (base + _prompt_tile(i) // per, 0, j)),
            pl.BlockSpec((1, ms, width), lambda j, i: (mod.index, 0, j))]


def _norm_in_kernel(xp_ref, xs_ref, g_ref, scp_ref, scs_ref, shp_ref, shs_ref, o_ref):
    ms = xs_ref.shape[0]

    def body(tail):
        if tail:
            o_ref[0:ms, :] = _norm_rows(xs_ref[...], g_ref[...], scs_ref[0], shs_ref[0]).astype(o_ref.dtype)
        else:
            o_ref[...] = _norm_rows(xp_ref[...], g_ref[...], scp_ref[0], shp_ref[0]).astype(o_ref.dtype)

    _split_tiles(0, body)


def norm_in(xp, xs, g, sc, sh, tm):
    mp, d = xp.shape
    ms = xs.shape[0]
    nf = mp // tm
    per = nf // sc.nb
    return pl.pallas_call(
        _norm_in_kernel,
        grid=(nf + 1,),
        in_specs=[pl.BlockSpec((tm, d), lambda i: (_prompt_tile(i), 0)),
                  pl.BlockSpec((ms, d), lambda i: (0, 0)),
                  pl.BlockSpec((1, d), lambda i: (0, 0)),
                  *_mod_specs(sc, d, per, 1), *_mod_specs(sh, d, per, 1)],
        out_specs=pl.BlockSpec((tm, d), lambda i: (_joint_tile(i, nf), 0)),
        out_shape=jax.ShapeDtypeStruct((mp + ms, d), BF16),
        compiler_params=_params(1),
        name="norm_in",
    )(xp, xs, g, sc.prompt, sc.sample, sh.prompt, sh.sample)


def _final_norm_kernel(xp_ref, xs_ref, g_ref, yp_ref, ys_ref):
    def body(tail):
        src, dst = (xs_ref, ys_ref) if tail else (xp_ref, yp_ref)
        x = src[...]
        dst[...] = x * lax.rsqrt(jnp.mean(x * x, axis=-1, keepdims=True) + EPS) * g_ref[...]

    _split_tiles(0, body)


def final_norm(xp, xs, g, tm):
    mp, d = xp.shape
    ms = xs.shape[0]
    nf = mp // tm
    p_spec = pl.BlockSpec((tm, d), lambda i: (_prompt_tile(i), 0))
    s_spec = pl.BlockSpec((ms, d), lambda i: (0, 0))
    return pl.pallas_call(
        _final_norm_kernel,
        grid=(nf + 1,),
        in_specs=[p_spec, s_spec, pl.BlockSpec((1, d), lambda i: (0, 0))],
        out_specs=[p_spec, s_spec],
        out_shape=[jax.ShapeDtypeStruct((mp, d), F32), jax.ShapeDtypeStruct((ms, d), F32)],
        compiler_params=_params(1),
        name="final_norm",
    )(xp, xs, g)


IN_TN = 1536


def _in_proj_kernel(a_ref, w_ref, lg_ref, lb_ref, ws_ref, bst_ref, pw_ref, pb_ref, ps_ref, cw_ref,
                    o_ref, ya_ref, yc_ref, yd_ref, zt_ref, wbf_ref, cbuf_ref, zbuf_ref, *, ms, per_seq):
    j = pl.program_id(0)
    i = pl.program_id(1)
    tm = a_ref.shape[0]
    tile = i - 1
    first = (tile % per_seq) == 0

    def idle(skip=None):
        for ref in (ya_ref, yc_ref, yd_ref):
            if ref is not skip:
                ref[...] = jnp.zeros(ref.shape, ref.dtype)
        if skip is not yd_ref:
            zt_ref[...] = jnp.zeros(zt_ref.shape, zt_ref.dtype)

    def project():
        acc = jnp.dot(a_ref[...], wbf_ref[...], preferred_element_type=F32)
        o_ref[...] = acc
        return acc

    @pl.when(i == 0)
    def _():
        wbf_ref[...] = w_ref[0].astype(BF16)
        o_ref[0:ms, :] = jnp.dot(a_ref[0:ms, :], wbf_ref[...], preferred_element_type=F32)
        idle()

    @pl.when((i > 0) & first & (j == COL_C // IN_TN))
    def _():
        cbuf_ref[0:POOL_HALO, :] = jnp.zeros((POOL_HALO, BR_W), F32)

    @pl.when((i > 0) & first & (j == COL_D // IN_TN))
    def _():
        zbuf_ref[0:CONV_HALO, :] = jnp.zeros((CONV_HALO, BR_W), F32)

    @pl.when((i > 0) & (j == COL_A // IN_TN))
    def _():
        acc = project()
        _gmlp_tile(acc[:, 0:BR_W], acc[:, BR_W:2 * BR_W], lg_ref, lb_ref, ws_ref, bst_ref, ya_ref, CHUNK)
        idle(ya_ref)

    @pl.when((i > 0) & (j == COL_C // IN_TN))
    def _():
        acc = project()
        c0 = COL_C % IN_TN
        _pool_tile(acc[:, c0:c0 + BR_W], (tile % per_seq) * tm, cbuf_ref, pw_ref, pb_ref, ps_ref, yc_ref)
        idle(yc_ref)

    @pl.when((i > 0) & (j == COL_D // IN_TN))
    def _():
        acc = project()
        _conv_tile(acc[:, 0:BR_W], acc[:, BR_W:2 * BR_W], acc[:, 2 * BR_W:3 * BR_W], zbuf_ref, cw_ref, yd_ref, zt_ref)
        idle(yd_ref)

    @pl.when((i > 0) & (j != COL_A // IN_TN) & (j != COL_C // IN_TN) & (j != COL_D // IN_TN))
    def _():
        project()
        idle()


def in_proj_mixers(a, w, layer, ms, tm, nseq, gm, pc):
    m, k = a.shape
    n = w.shape[2]
    nf = (m - ms) // tm
    per_seq = nf // nseq
    ln_g, ln_b, ws, bst = gm
    pool_w, pool_b, pool_scale, conv_wt, _ = pc
    ga, gc, gd = COL_A // IN_TN, COL_C // IN_TN, COL_D // IN_TN

    def parked(j, i, tile_j, live, n_live):
        before = (j < tile_j) | ((j == tile_j) & (i == 0))
        return jnp.where((j == tile_j) & (i > 0), live, jnp.where(before, n_live, n_live + 1))

    def mixer_spec(tile_j):
        return pl.BlockSpec((tm, BR_W), lambda j, i: (parked(j, i, tile_j, i - 1, nf), 0))

    vec = pl.BlockSpec((1, BR_W), lambda j, i: (0, 0))
    return pl.pallas_call(
        functools.partial(_in_proj_kernel, ms=ms, per_seq=per_seq),
        grid=(n // IN_TN, nf + 1),
        in_specs=[pl.BlockSpec((tm, k), lambda j, i: (_joint_tile(i, nf), 0)),
                  pl.BlockSpec((1, k, IN_TN), lambda j, i: (layer, 0, j)),
                  vec, vec,
                  pl.BlockSpec((1, 4, CHUNK, CHUNK), lambda j, i: (0, 0, 0, 0)),
                  pl.BlockSpec((1, CHUNK, 4), lambda j, i: (0, 0, 0)),
                  pl.BlockSpec((1, 4, GW, GW), lambda j, i: (layer, 0, 0, 0)),
                  vec, vec,
                  pl.BlockSpec((CONV_W, BR_W), lambda j, i: (0, 0))],
        out_specs=[pl.BlockSpec((tm, IN_TN), lambda j, i: (_joint_tile(i, nf), j)),
                   mixer_spec(ga), mixer_spec(gc), mixer_spec(gd),
                   pl.BlockSpec((1, CONV_HALO, BR_W),
                                lambda j, i: (parked(j, i, gd, (i - 1) // per_seq, nseq), 0, 0))],
        out_shape=[jax.ShapeDtypeStruct((m, n), F32)]
        + [jax.ShapeDtypeStruct(((nf + 2) * tm, BR_W), BF16)] * 3
        + [jax.ShapeDtypeStruct((nseq + 2, CONV_HALO, BR_W), F32)],
        scratch_shapes=[pltpu.VMEM((k, IN_TN), BF16),
                        pltpu.VMEM((POOL_HALO + tm, BR_W), F32), pltpu.VMEM((CONV_HALO + tm, BR_W), F32)],
        compiler_params=_params(2),
        name="in_proj_mixers",
    )(a, w, ln_g, ln_b, ws, bst, pool_w, pool_b, pool_scale, conv_wt)


def _mm_residual_kernel(a_ref, w_ref, xp_ref, xs_ref, gp_ref, gs_ref, op_ref, os_ref, wbf_ref):
    ms = xs_ref.shape[0]

    @pl.when(pl.program_id(1) == 0)
    def _():
        wbf_ref[...] = w_ref[0].astype(BF16)

    def body(tail):
        if tail:
            y = jnp.dot(a_ref[0:ms, :], wbf_ref[...], preferred_element_type=F32)
            os_ref[...] = xs_ref[...] + gs_ref[0] * y
        else:
            y = jnp.dot(a_ref[...], wbf_ref[...], preferred_element_type=F32)
            op_ref[...] = xp_ref[...] + gp_ref[0] * y

    _split_tiles(1, body)


def matmul_residual(a, w, layer, xp, xs, gate, tm, tn):
    m, k = a.shape
    n = w.shape[2]
    mp, ms = xp.shape[0], xs.shape[0]
    nf = mp // tm
    per = nf // gate.nb
    p_spec = pl.BlockSpec((tm, tn), lambda j, i: (_prompt_tile(i), j))
    s_spec = pl.BlockSpec((ms, tn), lambda j, i: (0, j))
    return pl.pallas_call(
        _mm_residual_kernel,
        grid=(n // tn, nf + 1),
        in_specs=[pl.BlockSpec((tm, k), lambda j, i: (_joint_tile(i, nf), 0)),
                  pl.BlockSpec((1, k, tn), lambda j, i: (layer, 0, j)),
                  p_spec, s_spec, *_mod_specs(gate, tn, per, 2)],
        out_specs=[p_spec, s_spec],
        out_shape=[jax.ShapeDtypeStruct((mp, n), F32), jax.ShapeDtypeStruct((ms, n), F32)],
        scratch_shapes=[pltpu.VMEM((k, tn), BF16)],
        compiler_params=_params(2),
        name="matmul_residual",
    )(a, w, xp, xs, gate.prompt, gate.sample)


def _out_proj_norm_kernel(a_ref, w_ref, xp_ref, xs_ref, gp_ref, gs_ref, ln_ref, scp_ref, scs_ref,
                          shp_ref, shs_ref, op_ref, os_ref, hn_ref, wbf_ref):
    ms = xs_ref.shape[0]

    @pl.when(pl.program_id(0) == 0)
    def _():
        wbf_ref[...] = w_ref[0].astype(BF16)

    def body(tail):
        if tail:
            x = xs_ref[...] + gs_ref[0] * jnp.dot(a_ref[0:ms, :], wbf_ref[...], preferred_element_type=F32)
            os_ref[...] = x
            hn_ref[0:ms, :] = _norm_rows(x, ln_ref[...], scs_ref[0], shs_ref[0]).astype(hn_ref.dtype)
        else:
            x = xp_ref[...] + gp_ref[0] * jnp.dot(a_ref[...], wbf_ref[...], preferred_element_type=F32)
            op_ref[...] = x
            hn_ref[...] = _norm_rows(x, ln_ref[...], scp_ref[0], shp_ref[0]).astype(hn_ref.dtype)

    _split_tiles(0, body)


def out_proj_norm(a, w, layer, xp, xs, gate, ln_g, sc, sh, tm):
    m, k = a.shape
    n = w.shape[2]
    mp, ms = xp.shape[0], xs.shape[0]
    nf = mp // tm
    per = nf // gate.nb
    p_spec = pl.BlockSpec((tm, n), lambda i: (_prompt_tile(i), 0))
    s_spec = pl.BlockSpec((ms, n), lambda i: (0, 0))
    return pl.pallas_call(
        _out_proj_norm_kernel,
        grid=(nf + 1,),
        in_specs=[pl.BlockSpec((tm, k), lambda i: (_joint_tile(i, nf), 0)),
                  pl.BlockSpec((1, k, n), lambda i: (layer, 0, 0), pipeline_mode=pl.Buffered(1)),
                  p_spec, s_spec, *_mod_specs(gate, n, per, 1),
                  pl.BlockSpec((1, n), lambda i: (0, 0)),
                  *_mod_specs(sc, n, per, 1), *_mod_specs(sh, n, per, 1)],
        out_specs=[p_spec, s_spec, pl.BlockSpec((tm, n), lambda i: (_joint_tile(i, nf), 0))],
        out_shape=[jax.ShapeDtypeStruct((mp, n), F32), jax.ShapeDtypeStruct((ms, n), F32),
                   jax.ShapeDtypeStruct((m, n), BF16)],
        scratch_shapes=[pltpu.VMEM((k, n), BF16)],
        compiler_params=_params(1),
        name="out_proj_norm",
    )(a, w, xp, xs, gate.prompt, gate.sample, ln_g, sc.prompt, sc.sample, sh.prompt, sh.sample)


def _swiglu_kernel(a_ref, w1_ref, w3_ref, o_ref, w1bf_ref, w3bf_ref, *, ms):
    @pl.when(pl.program_id(1) == 0)
    def _():
        w1bf_ref[...] = w1_ref[0].astype(BF16)
        w3bf_ref[...] = w3_ref[0].astype(BF16)

    def body(tail):
        rows = slice(0, ms) if tail else slice(None)
        a = a_ref[rows, :]
        h1 = jnp.dot(a, w1bf_ref[...], preferred_element_type=F32)
        h3 = jnp.dot(a, w3bf_ref[...], preferred_element_type=F32)
        o_ref[rows, :] = (jax.nn.silu(h1) * h3).astype(o_ref.dtype)

    _split_tiles(1, body)


def swiglu_up(a, w1, w3, layer, ms, tm, tn):
    m, k = a.shape
    n = w1.shape[2]
    nf = (m - ms) // tm
    w_spec = pl.BlockSpec((1, k, tn), lambda j, i: (layer, 0, j))
    return pl.pallas_call(
        functools.partial(_swiglu_kernel, ms=ms),
        grid=(n // tn, nf + 1),
        in_specs=[pl.BlockSpec((tm, k), lambda j, i: (_joint_tile(i, nf), 0)), w_spec, w_spec],
        out_specs=pl.BlockSpec((tm, tn), lambda j, i: (_joint_tile(i, nf), j)),
        out_shape=jax.ShapeDtypeStruct((m, n), BF16),
        scratch_shapes=[pltpu.VMEM((k, tn), BF16), pltpu.VMEM((k, tn), BF16)],
        compiler_params=_params(2),
        name="swiglu_up",
    )(a, w1, w3)


def _gate_merge_kernel(hn_ref, yap_ref, ybp_ref, ycp_ref, ydp_ref, yas_ref, ybs_ref, ycs_ref, yds_ref,
                       wg0_ref, wg1_ref, wg2_ref, wg3_ref, wb_ref, bg_ref, o_ref, wgbf_ref, wbbf_ref):
    ms = yas_ref.shape[0]

    @pl.when(pl.program_id(1) == 0)
    def _():
        for g, wg_ref in enumerate((wg0_ref, wg1_ref, wg2_ref, wg3_ref)):
            wgbf_ref[g] = wg_ref[0].astype(BF16)
            wbbf_ref[g] = wb_ref[0, g].astype(BF16)

    def body(tail):
        rows = slice(0, ms) if tail else slice(None)
        y_refs = (yas_ref, ybs_ref, ycs_ref, yds_ref) if tail else (yap_ref, ybp_ref, ycp_ref, ydp_ref)
        hn = hn_ref[rows, :]
        acc = None
        for g, y_ref in enumerate(y_refs):
            gate = jax.nn.sigmoid(jnp.dot(hn, wgbf_ref[g], preferred_element_type=F32) + bg_ref[0, g:g + 1, :])
            br = jnp.dot(y_ref[...], wbbf_ref[g], preferred_element_type=F32)
            acc = gate * br if acc is None else acc + gate * br
        o_ref[rows, :] = acc.astype(o_ref.dtype)

    _split_tiles(1, body)


def gate_merge(hn, ys_p, ys_s, w_gate, b_gate, w_branch, layer, tm, tn):
    m, k = hn.shape
    ms = ys_s[0].shape[0]
    nf = (m - ms) // tm
    depth = w_gate.shape[0]
    d = w_branch.shape[3]
    nj = d // tn
    wg_specs = [pl.BlockSpec((1, k, tn), functools.partial(lambda j, i, g: (layer, 0, g * nj + j), g=g))
                for g in range(4)]
    yp_spec = pl.BlockSpec((tm, BR_W), lambda j, i: (_prompt_tile(i), 0))
    ys_spec = pl.BlockSpec((ms, BR_W), lambda j, i: (0, 0))
    return pl.pallas_call(
        _gate_merge_kernel,
        grid=(nj, nf + 1),
        in_specs=[pl.BlockSpec((tm, k), lambda j, i: (_joint_tile(i, nf), 0)), *[yp_spec] * 4, *[ys_spec] * 4,
                  *wg_specs,
                  pl.BlockSpec((1, 4, BR_W, tn), lambda j, i: (layer, 0, 0, j)),
                  pl.BlockSpec((1, 4, tn), lambda j, i: (layer, 0, j))],
        out_specs=pl.BlockSpec((tm, tn), lambda j, i: (_joint_tile(i, nf), j)),
        out_shape=jax.ShapeDtypeStruct((m, d), BF16),
        scratch_shapes=[pltpu.VMEM((4, k, tn), BF16), pltpu.VMEM((4, BR_W, tn), BF16)],
        compiler_params=_params(2),
        name="gate_merge",
    )(hn, *ys_p, *ys_s, w_gate, w_gate, w_gate, w_gate, w_branch, b_gate.reshape(depth, 4, d))


def _gmlp_kernel(pu_ref, pv_ref, lg_ref, lb_ref, ws_ref, bst_ref, ya_ref, va_ref, *, chunk):
    va_ref[...] = _gmlp_tile(pu_ref[...], pv_ref[...], lg_ref, lb_ref, ws_ref, bst_ref, ya_ref, chunk)


def _gmlp_tile(pu, pv, lg_ref, lb_ref, ws_ref, bst_ref, ya_ref, chunk):
    rows = pu.shape[0]
    u = jax.nn.gelu(pu)
    v = jax.nn.gelu(pv)
    mu = jnp.mean(v, axis=-1, keepdims=True)
    var = jnp.mean(jnp.square(v - mu), axis=-1, keepdims=True)
    vn = (v - mu) * lax.rsqrt(var + 1e-5) * lg_ref[...] + lb_ref[...]
    causal = (lax.broadcasted_iota(jnp.int32, (CHUNK, CHUNK), 0)
              >= lax.broadcasted_iota(jnp.int32, (CHUNK, CHUNK), 1))
    for g in range(4):
        wm = jnp.where(causal, ws_ref[0, g], 0.0).astype(BF16)
        bias = bst_ref[0, :, g:g + 1]
        for c in range(rows // chunk):
            r0 = c * chunk
            vc = vn[r0:r0 + chunk, g * GW:(g + 1) * GW]
            if chunk < CHUNK:
                vc = jnp.concatenate([vc, jnp.zeros((CHUNK - chunk, GW), F32)], axis=0)
            mix = (jnp.dot(wm, vc.astype(BF16), preferred_element_type=F32) + bias)[:chunk]
            ya_ref[r0:r0 + chunk, g * GW:(g + 1) * GW] = (
                u[r0:r0 + chunk, g * GW:(g + 1) * GW] * mix).astype(ya_ref.dtype)
    return vn


def gmlp(p, ln_g, ln_b, ws, bst, row0, m, tr, chunk):
    blk0 = row0 // tr
    in_spec = lambda col: pl.BlockSpec((tr, BR_W), lambda i: (blk0 + i, col))
    out_spec = pl.BlockSpec((tr, BR_W), lambda i: (i, 0))
    return pl.pallas_call(
        functools.partial(_gmlp_kernel, chunk=chunk),
        grid=(m // tr,),
        in_specs=[in_spec(0), in_spec(1),
                  pl.BlockSpec((1, BR_W), lambda i: (0, 0)),
                  pl.BlockSpec((1, BR_W), lambda i: (0, 0)),
                  pl.BlockSpec((1, 4, CHUNK, CHUNK), lambda i: (0, 0, 0, 0)),
                  pl.BlockSpec((1, CHUNK, 4), lambda i: (0, 0, 0))],
        out_specs=[out_spec, out_spec],
        out_shape=[jax.ShapeDtypeStruct((m, BR_W), BF16), jax.ShapeDtypeStruct((m, BR_W), F32)],
        compiler_params=_params(1),
        name="gmlp",
    )(p, p, ln_g, ln_b, ws, bst)


def _bias_from_buckets(idx, rel_ref, col, shape):
    bias = jnp.full(shape, NEG, F32)
    for b in range(N_BUCKETS):
        bias = jnp.where(idx == b, rel_ref[b, col], bias)
    return bias


def _bias_mask_kernel(rel_ref, idx_ref, o_ref):
    o_ref[0, 0] = _bias_from_buckets(idx_ref[0], rel_ref, pl.program_id(0) * N_HEADS + pl.program_id(1),
                                     (NK, 2 * NK))


def prompt_bias_mask(rel_bias):
    return pl.pallas_call(
        _bias_mask_kernel,
        grid=(len(PATTERNS), N_HEADS),
        in_specs=[pl.BlockSpec(memory_space=pltpu.SMEM),
                  pl.BlockSpec((1, NK, 2 * NK), lambda g, h: (g, 0, 0))],
        out_specs=pl.BlockSpec((1, 1, NK, 2 * NK), lambda g, h: (g, h, 0, 0)),
        out_shape=jax.ShapeDtypeStruct((len(PATTERNS), N_HEADS, NK, 2 * NK), F32),
        compiler_params=_params(2),
        name="prompt_bias_mask",
    )(rel_bias, jnp.asarray(_prompt_buckets()))


def _attn_prompt_kernel(q_ref, k_ref, v_ref, bm_ref, o_ref,
                        o0_ref, o1_ref, o2_ref, e0_ref, e1_ref, e2_ref):
    t = q_ref.shape[0]
    group = pl.program_id(2)
    scale = GW ** -0.5
    outs = (o0_ref, o1_ref, o2_ref)
    lses = (e0_ref, e1_ref, e2_ref)

    def rows(start, n, dil):
        return pl.ds(start, n) if dil == 1 else pl.ds(start, n, stride=dil)

    def blocks(gi, specs):
        dil = PATTERNS[gi][1]
        staged = []
        for first_block, q0, k0 in specs:
            nkeys = NK if first_block else 2 * NK
            qi = rows(q0, NK, dil)
            ki = rows(k0, nkeys, dil)
            q = q_ref[qi, :].astype(BF16)
            kk = k_ref[ki, :].astype(BF16)
            bm = bm_ref[0, 0, :, NK:] if first_block else bm_ref[0, 0]
            s = lax.dot_general(q, kk, (((1,), (1,)), ((), ())), preferred_element_type=F32) * scale + bm
            staged.append((qi, ki, s))
        probs = []
        for qi, ki, s in staged:
            m = jnp.max(s, axis=-1, keepdims=True)
            p = jnp.exp(s - m)
            l = jnp.sum(p, axis=-1, keepdims=True)
            probs.append((qi, ki, p.astype(BF16), m, l))
        for qi, ki, p, m, l in probs:
            acc = jnp.dot(p, v_ref[ki, :].astype(BF16), preferred_element_type=F32)
            outs[gi][qi, :] = acc * (1.0 / l)
            lses[gi][qi, :] = jnp.broadcast_to(m + jnp.log(l), (NK, GW))

    def run_group(gi):
        dil = PATTERNS[gi][1]
        span = NK * dil
        nb = t // span
        u = ATTN_BLOCKS_PER_STEP
        if dil == 1:
            blocks(gi, [(True, 0, 0)] + [(False, n * span, (n - 1) * span) for n in range(1, u)])

            def step(i, c):
                q0 = pl.multiple_of(u * i * span, span)
                blocks(gi, [(False, q0 + n * span, q0 + (n - 1) * span) for n in range(u)])
                return c

            lax.fori_loop(1, nb // u, step, 0)
        else:
            def bunch(i, c):
                res = [u * i + r for r in range(u)]
                blocks(gi, [(True, r, r) for r in res])

                def later(n, cc):
                    blocks(gi, [(False, n * span + r, (n - 1) * span + r) for r in res])
                    return cc

                return lax.fori_loop(1, nb, later, c)

            if dil == u:
                bunch(0, 0)
            else:
                lax.fori_loop(0, dil // u, bunch, 0)

    for gi in range(len(PATTERNS)):
        pl.when(group == gi)(functools.partial(run_group, gi))

    @pl.when(group == len(PATTERNS) - 1)
    def _():
        step = 256

        def merge(c, carry):
            sl = pl.ds(pl.multiple_of(c * step, step), step)
            e = [r[sl, :] for r in lses]
            top = jnp.maximum(jnp.maximum(e[0], e[1]), e[2])
            w = [jnp.exp(x - top) for x in e]
            num = w[0] * o0_ref[sl, :] + w[1] * o1_ref[sl, :] + w[2] * o2_ref[sl, :]
            o_ref[sl, :] = (num / (w[0] + w[1] + w[2])).astype(o_ref.dtype)
            return carry

        lax.fori_loop(0, t // step, merge, 0)


def _prompt_buckets():
    i = np.arange(NK)[:, None]
    j = np.arange(2 * NK)[None, :]
    diff = NK + i - j
    valid = (diff >= 0) & (diff <= NK)
    return np.stack([np.where(valid, _t5_bucket(diff * dil), -1) for _, dil in PATTERNS]).astype(np.int32)


def attention_prompt(p, bias_mask, nseq, t):
    def qkv_spec(which):
        return pl.BlockSpec((t, GW), lambda b, h, g: (b, COL_B // GW + g * 12 + which * 4 + h))

    return pl.pallas_call(
        _attn_prompt_kernel,
        grid=(nseq, N_HEADS, len(PATTERNS)),
        in_specs=[qkv_spec(0), qkv_spec(1), qkv_spec(2),
                  pl.BlockSpec((1, 1, NK, 2 * NK), lambda b, h, g: (g, h, 0, 0))],
        out_specs=pl.BlockSpec((t, GW), lambda b, h, g: (b, h)),
        out_shape=jax.ShapeDtypeStruct((nseq * t, BR_W), BF16),
        scratch_shapes=[pltpu.VMEM((t, GW), F32)] * 6,
        compiler_params=_params(3),
        name="attention_prompt",
    )(p, p, p, bias_mask)


N_NEW = 4
TILE_ROWS = 2 * N_HEADS


def _key_to_value_rows(x):
    n, r, w = x.shape
    return pltpu.roll(x.reshape(n * r, w), N_HEADS, axis=0).reshape(n, r, w)


def _attend(x, xn, q, bias, bias_n, scale):
    s = jnp.sum(x * q[None], axis=-1, keepdims=True) * scale + bias
    sn = jnp.sum(xn * q[None], axis=-1, keepdims=True) * scale + bias_n
    m = jnp.maximum(jnp.max(s, axis=0, keepdims=True), jnp.max(sn, axis=0, keepdims=True))
    p = jnp.exp(s - m)
    pn = jnp.exp(sn - m)
    l = jnp.sum(p, axis=0, keepdims=True) + jnp.sum(pn, axis=0, keepdims=True)
    acc = jnp.sum(_key_to_value_rows(p) * x, axis=0) + jnp.sum(_key_to_value_rows(pn) * xn, axis=0)
    return acc, m, l


def _attn_sample_kernel(rel_ref, q_ref, xn_ref, c1_ref, c2_ref, c3_ref, o_ref,
                        b1_ref, b1n_ref, b23_ref, b23n_ref):
    scale = GW ** -0.5
    rows = N_NEW * TILE_ROWS

    @pl.when(pl.program_id(0) == 0)
    def _():
        head = lax.broadcasted_iota(jnp.int32, (rows, GW), 0) % N_HEADS
        neg = jnp.full((rows, GW), NEG, F32)

        def tiles_for(gi):
            out = []
            for b in range(N_BUCKETS):
                v = [rel_ref[b, gi * N_HEADS + h] for h in range(N_HEADS)]
                out.append(jnp.where(head == 0, v[0], jnp.where(head == 1, v[1], jnp.where(head == 2, v[2], v[3]))))
            return out

        t1 = tiles_for(0)
        for t in range(N_NEW):
            for pos in range(NK):
                step = NK + t - pos
                tile = t1[int(_t5_bucket(np.int64(step)))] if step <= NK else neg
                b1_ref[t, pos] = tile[:TILE_ROWS]
            for j in range(N_NEW):
                tile = t1[int(_t5_bucket(np.int64(t - j)))] if j <= t else neg
                b1n_ref[t, j] = tile[:TILE_ROWS]
        for gi in (1, 2):
            dil = PATTERNS[gi][1]
            tg = tiles_for(gi)
            for jj in range(NK):
                b23_ref[gi - 1, jj] = tg[int(_t5_bucket(np.int64((NK - jj) * dil)))]
            b23n_ref[gi - 1] = tg[0]

    outs, lses = [], []
    x1 = c1_ref[0, 0]
    xn1 = xn_ref[0, 0].reshape(N_NEW, TILE_ROWS, GW)
    o1, e1 = [], []
    for t in range(N_NEW):
        q = q_ref[0, 0, t * TILE_ROWS:(t + 1) * TILE_ROWS, :]
        acc, m, l = _attend(x1, xn1, q, b1_ref[t], b1n_ref[t], scale)
        o1.append(acc)
        e1.append((m, l))
    outs.append(jnp.concatenate(o1, axis=0))
    lses.append((jnp.concatenate([m[0] for m, _ in e1], axis=0), jnp.concatenate([l[0] for _, l in e1], axis=0)))
    for gi, c_ref in ((1, c2_ref), (2, c3_ref)):
        acc, m, l = _attend(c_ref[0, 0], xn_ref[gi, 0][None], q_ref[gi, 0], b23_ref[gi - 1],
                            b23n_ref[gi - 1][None], scale)
        outs.append(acc)
        lses.append((m[0], l[0]))

    lse = [_key_to_value_rows((m + jnp.log(l))[None])[0] for m, l in lses]
    den = [_key_to_value_rows(l[None])[0] for _, l in lses]
    top = jnp.maximum(jnp.maximum(lse[0], lse[1]), lse[2])
    w = [jnp.exp(e - top) for e in lse]
    num = w[0] * outs[0] / den[0] + w[1] * outs[1] / den[1] + w[2] * outs[2] / den[2]
    o_ref[0] = num / (w[0] + w[1] + w[2])


def attention_sample(p, cache_b1, cache_b2, cache_b3, rel_bias, layer, row0, nseq):
    depth = cache_b1.shape[0]
    rows = N_NEW * TILE_ROWS
    n_groups = len(PATTERNS)
    qkv = p[row0:, COL_B:COL_B + n_groups * 3 * BR_W].reshape(nseq, SAMPLE_ROWS, n_groups, 3, N_HEADS, GW)
    qkv = qkv[:, :N_NEW].transpose(2, 0, 1, 3, 4, 5)
    q = qkv[:, :, :, 0:1]
    q_all = jnp.concatenate([q, jnp.zeros_like(q)], axis=3).reshape(n_groups, nseq, rows, GW)
    xn_all = qkv[:, :, :, 1:3].reshape(n_groups, nseq, rows, GW)
    c1 = cache_b1.reshape(depth, nseq, NK, TILE_ROWS, GW)
    c2 = cache_b2.reshape(depth, nseq, NK, 4 * TILE_ROWS, GW)
    c3 = cache_b3.reshape(depth, nseq, NK, 16 * TILE_ROWS, GW)
    cache_spec = lambda r: pl.BlockSpec((1, 1, NK, r, GW), lambda b: (layer, b, 0, 0, 0))
    tok_spec = pl.BlockSpec((len(PATTERNS), 1, rows, GW), lambda b: (0, b, 0, 0))
    y = pl.pallas_call(
        _attn_sample_kernel,
        grid=(nseq,),
        in_specs=[pl.BlockSpec(memory_space=pltpu.SMEM), tok_spec, tok_spec,
                  cache_spec(TILE_ROWS), cache_spec(rows), cache_spec(rows)],
        out_specs=pl.BlockSpec((1, rows, GW), lambda b: (b, 0, 0)),
        out_shape=jax.ShapeDtypeStruct((nseq, rows, GW), F32),
        scratch_shapes=[pltpu.VMEM((N_NEW, NK, TILE_ROWS, GW), F32), pltpu.VMEM((N_NEW, N_NEW, TILE_ROWS, GW), F32),
                        pltpu.VMEM((2, NK, rows, GW), F32), pltpu.VMEM((2, rows, GW), F32)],
        compiler_params=_params(1),
        name="attention_sample",
    )(rel_bias, q_all, xn_all, c1, c2, c3)
    y = y.reshape(nseq, N_NEW, 2, BR_W)[:, :, 1]
    y = jnp.pad(y, ((0, 0), (0, SAMPLE_ROWS - N_NEW), (0, 0)))
    return y.reshape(nseq * SAMPLE_ROWS, BR_W).astype(BF16)


def _pool_conv_kernel(pc_ref, pbg_ref, pcg_ref, phs_ref, hc_ref, hd_ref, pw_ref, pb_ref, ps_ref, cw_ref,
                      yc_ref, yd_ref, zt_ref, cbuf_ref, zbuf_ref, *, start):
    tr = pc_ref.shape[0]
    j = pl.program_id(1)

    @pl.when(j == 0)
    def _():
        cbuf_ref[0:POOL_HALO, :] = hc_ref[0]
        zbuf_ref[0:CONV_HALO, :] = hd_ref[0]

    _pool_tile(pc_ref[...], start + j * tr, cbuf_ref, pw_ref, pb_ref, ps_ref, yc_ref)
    _conv_tile(pbg_ref[...], pcg_ref[...], phs_ref[...], zbuf_ref, cw_ref, yd_ref, zt_ref)


POOL_HALO = 16
CONV_HALO = 8


def _pool_tile(x, pos0, cbuf_ref, pw_ref, pb_ref, ps_ref, yc_ref):
    tr = x.shape[0]
    hc = POOL_HALO
    cbuf_ref[hc:hc + tr, :] = x
    pos = pos0 + lax.broadcasted_iota(jnp.int32, (tr, 1), 0)
    for gi, win in enumerate(POOL_WINDOWS):
        cols = slice(gi * GW, (gi + 1) * GW)
        total = x[:, cols]
        for back in range(1, win):
            total = total + cbuf_ref[hc - back:hc - back + tr, cols]
        cnt = jnp.minimum(pos + 1, win).astype(F32)
        pooled = total / cnt - x[:, cols]
        y = jnp.dot(pooled.astype(BF16), pw_ref[0, gi].astype(BF16), preferred_element_type=F32)
        yc_ref[:, cols] = ((y + pb_ref[:, cols]) * ps_ref[:, cols]).astype(yc_ref.dtype)
    cbuf_ref[0:hc, :] = cbuf_ref[tr:tr + hc, :]


def _conv_tile(bg, cg, hs, zbuf_ref, cw_ref, yd_ref, zt_ref):
    tr = bg.shape[0]
    hz = CONV_HALO
    z = cg * hs
    zbuf_ref[hz:hz + tr, :] = z
    y = (zbuf_ref[hz - 2:hz - 2 + tr, :] * cw_ref[0:1, :] + zbuf_ref[hz - 1:hz - 1 + tr, :] * cw_ref[1:2, :]
         + z * cw_ref[2:3, :])
    yd_ref[...] = (bg * y).astype(yd_ref.dtype)
    zt_ref[0] = z[tr - hz:, :]
    zbuf_ref[0:hz, :] = zbuf_ref[tr:tr + hz, :]


def pool_conv(p, hist_c, hist_d, pool_w, pool_b, pool_scale, conv_wt, layer, row0, nseq, t, tr, start):
    per = t // tr
    blk0 = row0 // tr
    col = lambda c: pl.BlockSpec((tr, BR_W), lambda b, j: (b * per + j, c))
    pcol = lambda c: pl.BlockSpec((tr, BR_W), lambda b, j: (blk0 + b * per + j, c))
    vec = pl.BlockSpec((1, BR_W), lambda b, j: (0, 0))
    return pl.pallas_call(
        functools.partial(_pool_conv_kernel, start=start),
        grid=(nseq, per),
        in_specs=[pcol(COL_C // BR_W), pcol(COL_D // BR_W), pcol(COL_D // BR_W + 1), pcol(COL_D // BR_W + 2),
                  pl.BlockSpec((1, 16, BR_W), lambda b, j: (b, 0, 0)),
                  pl.BlockSpec((1, 8, BR_W), lambda b, j: (b, 0, 0)),
                  pl.BlockSpec((1, 4, GW, GW), lambda b, j: (layer, 0, 0, 0)),
                  vec, vec,
                  pl.BlockSpec((CONV_W, BR_W), lambda b, j: (0, 0))],
        out_specs=[col(0), col(0), pl.BlockSpec((1, 8, BR_W), lambda b, j: (b, 0, 0))],
        out_shape=[jax.ShapeDtypeStruct((nseq * t, BR_W), BF16),
                   jax.ShapeDtypeStruct((nseq * t, BR_W), BF16),
                   jax.ShapeDtypeStruct((nseq, 8, BR_W), F32)],
        scratch_shapes=[pltpu.VMEM((16 + tr, BR_W), F32), pltpu.VMEM((8 + tr, BR_W), F32)],
        compiler_params=_params(2),
        name="pool_conv",
    )(p, p, p, p, hist_c, hist_d, pool_w, pool_b, pool_scale, conv_wt)


CACHE_ROWS = 512


def _cache_writer_kernel(k1_ref, v1_ref, k2_ref, v2_ref, k3_ref, v3_ref, b1_any, b2_any, b3_any,
                         o1_ref, o2_ref, o3_ref):
    def scatter(o_ref, k_ref, v_ref, row0, n):
        for kv, src in enumerate((k_ref, v_ref)):
            for h in range(N_HEADS):
                o_ref[0, 0, pl.ds(kv * N_HEADS + h, n, stride=TILE_ROWS), :] = src[row0:row0 + n, h * GW:(h + 1) * GW]

    scatter(o3_ref, k3_ref, v3_ref, 0, CACHE_ROWS)

    @pl.when(pl.program_id(1) == pl.num_programs(1) - 1)
    def _():
        scatter(o2_ref, k2_ref, v2_ref, CACHE_ROWS - PATTERNS[1][0], PATTERNS[1][0])
        scatter(o1_ref, k1_ref, v1_ref, CACHE_ROWS - PATTERNS[0][0], PATTERNS[0][0])


def cache_writer(p, bufs, layer, nseq, t):
    b1, b2, b3 = bufs
    per = t // CACHE_ROWS
    steps = PATTERNS[2][0] // CACHE_ROWS

    def slab(gi, which, whole_window):
        col = (COL_B + gi * 3 * BR_W + (1 + which) * BR_W) // BR_W
        if whole_window:
            return pl.BlockSpec((CACHE_ROWS, BR_W), lambda b, j: ((b + 1) * per - steps + j, col))
        return pl.BlockSpec((CACHE_ROWS, BR_W), lambda b, j: ((b + 1) * per - 1, col))

    any_spec = pl.BlockSpec(memory_space=pl.ANY)
    tile = lambda rows: (1, 1, rows * TILE_ROWS, GW)
    return pl.pallas_call(
        _cache_writer_kernel,
        grid=(nseq, steps),
        in_specs=[slab(0, 0, False), slab(0, 1, False), slab(1, 0, False), slab(1, 1, False),
                  slab(2, 0, True), slab(2, 1, True),
                  any_spec, any_spec, any_spec],
        out_specs=[pl.BlockSpec(tile(PATTERNS[0][0]), lambda b, j: (layer, b, 0, 0)),
                   pl.BlockSpec(tile(PATTERNS[1][0]), lambda b, j: (layer, b, 0, 0)),
                   pl.BlockSpec(tile(CACHE_ROWS), lambda b, j: (layer, b, j, 0))],
        out_shape=[jax.ShapeDtypeStruct(x.shape, x.dtype) for x in bufs],
        input_output_aliases={6: 0, 7: 1, 8: 2},
        compiler_params=_params(2),
        name="cache_writer",
    )(p, p, p, p, p, p, b1, b2, b3)


TM_WIDE = 1024
TM_ROWS = 512


def kernel(x_prompt, x_sample, c_prompt, c_sample, cache_b1, cache_b2, cache_b3, cache_pool, cache_conv,
           ln1_g, ln2_g, w_ada, b_ada, w_in, w_gate, b_gate, a_ln_g, a_ln_b, a_ws, a_bs,
           rel_bias, pool_w, pool_b, pool_scale, conv_w, w_branch, w_out, w1, w3, w2, final_g):
    nb, seq, d = x_prompt.shape
    ns, dec = x_sample.shape[:2]
    depth = w_in.shape[0]
    mp = nb * seq
    ms = ns * SAMPLE_ROWS

    c_all = jnp.concatenate([c_prompt, c_sample], axis=0)
    c_rows = -(-c_all.shape[0] // 8) * 8
    c_all = jnp.pad(c_all, ((0, c_rows - c_all.shape[0]), (0, 0)))
    mod = ada_modulation(c_all, w_ada, b_ada).reshape(depth, c_rows, 6, d)
    mod_p = mod[:, :nb].transpose(0, 2, 1, 3).reshape(depth * 6 * nb, 1, d)
    mod_s = jnp.repeat(mod[:, nb:nb + ns], SAMPLE_ROWS, axis=1).transpose(0, 2, 1, 3).reshape(depth * 6, ms, d)
    mods = [[Mod(mod_p, mod_s, l * 6 + i, nb) for i in range(6)] for l in range(depth)]

    xp = x_prompt.reshape(mp, d)
    xs = jnp.pad(x_sample, ((0, 0), (0, SAMPLE_ROWS - dec), (0, 0))).reshape(ms, d)

    bias_mask = prompt_bias_mask(rel_bias)
    kv_bufs = tuple(jnp.zeros((depth, nb, win * TILE_ROWS, GW), F32) for win, _ in PATTERNS)
    pool_p, conv_p, proj_s, conv_s, chunk_s = [], [], [], [], []
    a_bst = a_bs.transpose(0, 2, 1)
    conv_wt = conv_w.transpose(0, 2, 1)
    hist_c_all = jnp.pad(cache_pool, ((0, 0), (0, 0), (1, 0), (0, 0)))
    hist_d_all = jnp.pad(cache_conv, ((0, 0), (0, 0), (8 - (CONV_W - 1), 0), (0, 0)))
    for l in range(depth):
        sh1, sc1, g1, sh2, sc2, g2 = mods[l]
        hn = norm_in(xp, xs, ln1_g[l][None], sc1, sh1, TM_WIDE)
        gm = (a_ln_g[l][None], a_ln_b[l][None], a_ws[l][None], a_bst[l][None])
        pc = (pool_w, pool_b[l][None], pool_scale[l][None], conv_wt[l], l)
        p, ya_p, yc_p, yd_p, zt_p = in_proj_mixers(hn, w_in, l, ms, TM_ROWS, nb, gm, pc)
        ya_s, va_s = gmlp(p, *gm, mp, ms, ms, SAMPLE_ROWS)
        yb_p = attention_prompt(p, bias_mask, nb, seq)
        yb_s = attention_sample(p, cache_b1, cache_b2, cache_b3, rel_bias, l, mp, ns)
        yc_s, yd_s, zt_s = pool_conv(p, hist_c_all[l], hist_d_all[l], *pc, mp, ns, SAMPLE_ROWS, SAMPLE_ROWS,
                                     PAST_LEN)

        merged = gate_merge(hn, (ya_p, yb_p, yc_p, yd_p), (ya_s, yb_s, yc_s, yd_s),
                            w_gate, b_gate, w_branch, l, TM_WIDE, 256)
        xp, xs, hn2 = out_proj_norm(merged, w_out, l, xp, xs, g1, ln2_g[l][None], sc2, sh2, TM_ROWS)
        hmid = swiglu_up(hn2, w1, w3, l, ms, TM_WIDE, 512)
        xp, xs = matmul_residual(hmid, w2, l, xp, xs, g2, TM_ROWS, 512)

        kv_bufs = cache_writer(p, kv_bufs, l, nb, seq)
        pool_p.append(jnp.stack([p[(b + 1) * seq - POOL_HIST:(b + 1) * seq, COL_C:COL_C + BR_W]
                                 for b in range(nb)]))
        conv_p.append(zt_p[:nb])
        proj_s.append(p[mp:])
        conv_s.append(zt_s)
        chunk_s.append(va_s)

    yp, ys = final_norm(xp, xs, final_g[None], TM_WIDE)
    stack = lambda parts: jnp.stack(parts, axis=0)
    kv_p = [buf.reshape(depth, nb, win, 2, N_HEADS, GW) for buf, (win, _) in zip(kv_bufs, PATTERNS)]
    ps = stack(proj_s).reshape(depth, ns, SAMPLE_ROWS, N_IN)[:, :, :dec]
    kv_s = [ps[..., COL_B + (3 * gi + 1) * BR_W:COL_B + (3 * gi + 3) * BR_W].reshape(depth, ns, dec, 2, N_HEADS, GW)
            for gi in range(len(PATTERNS))]
    return (yp.reshape(nb, seq, d), ys.reshape(ns, SAMPLE_ROWS, d)[:, :dec],
            kv_p[0], kv_p[1], kv_p[2], stack(pool_p), stack(conv_p)[:, :, 8 - (CONV_W - 1):],
            kv_s[0], kv_s[1], kv_s[2], ps[..., COL_C:COL_C + BR_W], stack(conv_s)[:, :, :dec],
            stack(chunk_s).reshape(depth, ns, SAMPLE_ROWS, BR_W)[:, :, :dec])
```

```python
import functools
import math
from typing import NamedTuple

import jax
import jax.numpy as jnp
import numpy as np
from jax import lax
from jax.experimental import pallas as pl
from jax.experimental.pallas import tpu as pltpu

F32 = jnp.float32
BF16 = jnp.bfloat16

D_MODEL = 2048
BR_W = 512
GW = 128
N_HEADS = 4
CHUNK = 128
PATTERNS = ((128, 1), (512, 4), (2048, 16))
NK = 128
POOL_WINDOWS = (2, 4, 8, 16)
POOL_HIST = 15
CONV_W = 3
N_BUCKETS = 32
MAX_DIST = 2048
D_FF = 5632
N_IN = 7680
COL_A = 0
COL_B = 2 * BR_W
COL_C = COL_B + 9 * BR_W
COL_D = COL_C + BR_W
EPS = 1e-6
NEG = -1e30
SAMPLE_ROWS = 8
PAST_LEN = 16384
ATTN_BLOCKS_PER_STEP = 4

VMEM_LIMIT = 56 * 1024 * 1024


def _t5_bucket(dist):
    max_exact = N_BUCKETS // 2
    n = np.maximum(dist, 0)
    nf = np.maximum(n, 1).astype(np.float32)
    large = max_exact + (np.log(nf / np.float32(max_exact)) / np.float32(math.log(MAX_DIST / max_exact))
                         * np.float32(N_BUCKETS - max_exact)).astype(np.int32)
    large = np.minimum(large, N_BUCKETS - 1)
    return np.where(n < max_exact, n, large)


def _params(n_axes):
    return pltpu.CompilerParams(dimension_semantics=("arbitrary",) * n_axes,
                                vmem_limit_bytes=VMEM_LIMIT)


def _ada_kernel(c_ref, w_ref, b_ref, o_ref):
    a = jax.nn.silu(c_ref[...]).astype(BF16)
    o_ref[0] = jnp.dot(a, w_ref[0].astype(BF16), preferred_element_type=F32) + b_ref[0]


def ada_modulation(c_all, w_ada, b_ada):
    depth, d, n = w_ada.shape
    r = c_all.shape[0]
    tn = 1024
    return pl.pallas_call(
        _ada_kernel,
        grid=(depth, n // tn),
        in_specs=[pl.BlockSpec((r, d), lambda l, j: (0, 0)),
                  pl.BlockSpec((1, d, tn), lambda l, j: (l, 0, j)),
                  pl.BlockSpec((1, 1, tn), lambda l, j: (l, 0, j))],
        out_specs=pl.BlockSpec((1, r, tn), lambda l, j: (l, 0, j)),
        out_shape=jax.ShapeDtypeStruct((depth, r, n), F32),
        compiler_params=_params(2),
        name="ada_modulation",
    )(c_all, w_ada, b_ada.reshape(depth, 1, n))


def _split_tiles(axis, body):
    i = pl.program_id(axis)
    pl.when(i == 0)(functools.partial(body, True))
    pl.when(i > 0)(functools.partial(body, False))


def _joint_tile(i, nf):
    return (i + nf) % (nf + 1)


def _prompt_tile(i):
    return jnp.maximum(i - 1, 0)


def _norm_rows(x, g, sc, sh):
    y = x * lax.rsqrt(jnp.mean(x * x, axis=-1, keepdims=True) + EPS) * g
    return y * (1.0 + sc) + sh


class Mod(NamedTuple):
    prompt: jax.Array
    sample: jax.Array
    index: int
    nb: int


def _mod_specs(mod, width, per, grid_rank):
    ms = mod.sample.shape[1]
    base = mod.index * mod.nb
    if grid_rank == 1:
        return [pl.BlockSpec((1, 1, width), lambda i: (base + _prompt_tile(i) // per, 0, 0)),
                pl.BlockSpec((1, ms, width), lambda i: (mod.index, 0, 0))]
    return [pl.BlockSpec((1, 1, width), lambda j, i: (base + _prompt_tile(i) // per, 0, j)),
            pl.BlockSpec((1, ms, width), lambda j, i: (mod.index, 0, j))]


def _norm_in_kernel(xp_ref, xs_ref, g_ref, scp_ref, scs_ref, shp_ref, shs_ref, o_ref):
    ms = xs_ref.shape[0]

    def body(tail):
        if tail:
            o_ref[0:ms, :] = _norm_rows(xs_ref[...], g_ref[...], scs_ref[0], shs_ref[0]).astype(o_ref.dtype)
        else:
            o_ref[...] = _norm_rows(xp_ref[...], g_ref[...], scp_ref[0], shp_ref[0]).astype(o_ref.dtype)

    _split_tiles(0, body)


def norm_in(xp, xs, g, sc, sh, tm):
    mp, d = xp.shape
    ms = xs.shape[0]
    nf = mp // tm
    per = nf // sc.nb
    return pl.pallas_call(
        _norm_in_kernel,
        grid=(nf + 1,),
        in_specs=[pl.BlockSpec((tm, d), lambda i: (_prompt_tile(i), 0)),
                  pl.BlockSpec((ms, d), lambda i: (0, 0)),
                  pl.BlockSpec((1, d), lambda i: (0, 0)),
                  *_mod_specs(sc, d, per, 1), *_mod_specs(sh, d, per, 1)],
        out_specs=pl.BlockSpec((tm, d), lambda i: (_joint_tile(i, nf), 0)),
        out_shape=jax.ShapeDtypeStruct((mp + ms, d), BF16),
        compiler_params=_params(1),
        name="norm_in",
    )(xp, xs, g, sc.prompt, sc.sample, sh.prompt, sh.sample)


def _final_norm_kernel(xp_ref, xs_ref, g_ref, yp_ref, ys_ref):
    def body(tail):
        src, dst = (xs_ref, ys_ref) if tail else (xp_ref, yp_ref)
        x = src[...]
        dst[...] = x * lax.rsqrt(jnp.mean(x * x, axis=-1, keepdims=True) + EPS) * g_ref[...]

    _split_tiles(0, body)


def final_norm(xp, xs, g, tm):
    mp, d = xp.shape
    ms = xs.shape[0]
    nf = mp // tm
    p_spec = pl.BlockSpec((tm, d), lambda i: (_prompt_tile(i), 0))
    s_spec = pl.BlockSpec((ms, d), lambda i: (0, 0))
    return pl.pallas_call(
        _final_norm_kernel,
        grid=(nf + 1,),
        in_specs=[p_spec, s_spec, pl.BlockSpec((1, d), lambda i: (0, 0))],
        out_specs=[p_spec, s_spec],
        out_shape=[jax.ShapeDtypeStruct((mp, d), F32), jax.ShapeDtypeStruct((ms, d), F32)],
        compiler_params=_params(1),
        name="final_norm",
    )(xp, xs, g)


IN_TN = 1536


def _in_proj_kernel(a_ref, w_ref, lg_ref, lb_ref, ws_ref, bst_ref, pw_ref, pb_ref, ps_ref, cw_ref,
                    o_ref, ya_ref, yc_ref, yd_ref, zt_ref, wbf_ref, cbuf_ref, zbuf_ref, *, ms, per_seq):
    j = pl.program_id(0)
    i = pl.program_id(1)
    tm = a_ref.shape[0]
    tile = i - 1
    first = (tile % per_seq) == 0

    def idle(skip=None):
        for ref in (ya_ref, yc_ref, yd_ref):
            if ref is not skip:
                ref[...] = jnp.zeros(ref.shape, ref.dtype)
        if skip is not yd_ref:
            zt_ref[...] = jnp.zeros(zt_ref.shape, zt_ref.dtype)

    def project():
        acc = jnp.dot(a_ref[...], wbf_ref[...], preferred_element_type=F32)
        o_ref[...] = acc
        return acc

    @pl.when(i == 0)
    def _():
        wbf_ref[...] = w_ref[0].astype(BF16)
        o_ref[0:ms, :] = jnp.dot(a_ref[0:ms, :], wbf_ref[...], preferred_element_type=F32)
        idle()

    @pl.when((i > 0) & first & (j == COL_C // IN_TN))
    def _():
        cbuf_ref[0:POOL_HALO, :] = jnp.zeros((POOL_HALO, BR_W), F32)

    @pl.when((i > 0) & first & (j == COL_D // IN_TN))
    def _():
        zbuf_ref[0:CONV_HALO, :] = jnp.zeros((CONV_HALO, BR_W), F32)

    @pl.when((i > 0) & (j == COL_A // IN_TN))
    def _():
        acc = project()
        _gmlp_tile(acc[:, 0:BR_W], acc[:, BR_W:2 * BR_W], lg_ref, lb_ref, ws_ref, bst_ref, ya_ref, CHUNK)
        idle(ya_ref)

    @pl.when((i > 0) & (j == COL_C // IN_TN))
    def _():
        acc = project()
        c0 = COL_C % IN_TN
        _pool_tile(acc[:, c0:c0 + BR_W], (tile % per_seq) * tm, cbuf_ref, pw_ref, pb_ref, ps_ref, yc_ref)
        idle(yc_ref)

    @pl.when((i > 0) & (j == COL_D // IN_TN))
    def _():
        acc = project()
        _conv_tile(acc[:, 0:BR_W], acc[:, BR_W:2 * BR_W], acc[:, 2 * BR_W:3 * BR_W], zbuf_ref, cw_ref, yd_ref, zt_ref)
        idle(yd_ref)

    @pl.when((i > 0) & (j != COL_A // IN_TN) & (j != COL_C // IN_TN) & (j != COL_D // IN_TN))
    def _():
        project()
        idle()


def in_proj_mixers(a, w, layer, ms, tm, nseq, gm, pc):
    m, k = a.shape
    n = w.shape[2]
    nf = (m - ms) // tm
    per_seq = nf // nseq
    ln_g, ln_b, ws, bst = gm
    pool_w, pool_b, pool_scale, conv_wt, _ = pc
    ga, gc, gd = COL_A // IN_TN, COL_C // IN_TN, COL_D // IN_TN

    def parked(j, i, tile_j, live, n_live):
        before = (j < tile_j) | ((j == tile_j) & (i == 0))
        return jnp.where((j == tile_j) & (i > 0), live, jnp.where(before, n_live, n_live + 1))

    def n_parked(tile_j):
        return 1 if tile_j == n // IN_TN - 1 else 2

    def mixer_spec(tile_j):
        return pl.BlockSpec((tm, BR_W), lambda j, i: (parked(j, i, tile_j, i - 1, nf), 0))

    vec = pl.BlockSpec((1, BR_W), lambda j, i: (0, 0))
    return pl.pallas_call(
        functools.partial(_in_proj_kernel, ms=ms, per_seq=per_seq),
        grid=(n // IN_TN, nf + 1),
        in_specs=[pl.BlockSpec((tm, k), lambda j, i: (_joint_tile(i, nf), 0)),
                  pl.BlockSpec((1, k, IN_TN), lambda j, i: (layer, 0, j)),
                  vec, vec,
                  pl.BlockSpec((1, 4, CHUNK, CHUNK), lambda j, i: (0, 0, 0, 0)),
                  pl.BlockSpec((1, CHUNK, 4), lambda j, i: (0, 0, 0)),
                  pl.BlockSpec((1, 4, GW, GW), lambda j, i: (layer, 0, 0, 0)),
                  vec, vec,
                  pl.BlockSpec((CONV_W, BR_W), lambda j, i: (0, 0))],
        out_specs=[pl.BlockSpec((tm, IN_TN), lambda j, i: (_joint_tile(i, nf), j)),
                   mixer_spec(ga), mixer_spec(gc), mixer_spec(gd),
                   pl.BlockSpec((1, CONV_HALO, BR_W),
                                lambda j, i: (parked(j, i, gd, (i - 1) // per_seq, nseq), 0, 0))],
        out_shape=[jax.ShapeDtypeStruct((m, n), F32)]
        + [jax.ShapeDtypeStruct(((nf + n_parked(t)) * tm, BR_W), BF16) for t in (ga, gc, gd)]
        + [jax.ShapeDtypeStruct((nseq + n_parked(gd), CONV_HALO, BR_W), F32)],
        scratch_shapes=[pltpu.VMEM((k, IN_TN), BF16),
                        pltpu.VMEM((POOL_HALO + tm, BR_W), F32), pltpu.VMEM((CONV_HALO + tm, BR_W), F32)],
        compiler_params=_params(2),
        name="in_proj_mixers",
    )(a, w, ln_g, ln_b, ws, bst, pool_w, pool_b, pool_scale, conv_wt)


def _mm_residual_kernel(a_ref, w_ref, xp_ref, xs_ref, gp_ref, gs_ref, op_ref, os_ref, wbf_ref):
    ms = xs_ref.shape[0]

    @pl.when(pl.program_id(1) == 0)
    def _():
        wbf_ref[...] = w_ref[0].astype(BF16)

    def body(tail):
        if tail:
            y = jnp.dot(a_ref[0:ms, :], wbf_ref[...], preferred_element_type=F32)
            os_ref[...] = xs_ref[...] + gs_ref[0] * y
        else:
            y = jnp.dot(a_ref[...], wbf_ref[...], preferred_element_type=F32)
            op_ref[...] = xp_ref[...] + gp_ref[0] * y

    _split_tiles(1, body)


def matmul_residual(a, w, layer, xp, xs, gate, tm, tn):
    m, k = a.shape
    n = w.shape[2]
    mp, ms = xp.shape[0], xs.shape[0]
    nf = mp // tm
    per = nf // gate.nb
    p_spec = pl.BlockSpec((tm, tn), lambda j, i: (_prompt_tile(i), j))
    s_spec = pl.BlockSpec((ms, tn), lambda j, i: (0, j))
    return pl.pallas_call(
        _mm_residual_kernel,
        grid=(n // tn, nf + 1),
        in_specs=[pl.BlockSpec((tm, k), lambda j, i: (_joint_tile(i, nf), 0)),
                  pl.BlockSpec((1, k, tn), lambda j, i: (layer, 0, j)),
                  p_spec, s_spec, *_mod_specs(gate, tn, per, 2)],
        out_specs=[p_spec, s_spec],
        out_shape=[jax.ShapeDtypeStruct((mp, n), F32), jax.ShapeDtypeStruct((ms, n), F32)],
        scratch_shapes=[pltpu.VMEM((k, tn), BF16)],
        compiler_params=_params(2),
        name="matmul_residual",
    )(a, w, xp, xs, gate.prompt, gate.sample)


def _out_proj_norm_kernel(a_ref, w_ref, xp_ref, xs_ref, gp_ref, gs_ref, ln_ref, scp_ref, scs_ref,
                          shp_ref, shs_ref, op_ref, os_ref, hn_ref, wbf_ref):
    ms = xs_ref.shape[0]

    @pl.when(pl.program_id(0) == 0)
    def _():
        wbf_ref[...] = w_ref[0].astype(BF16)

    def body(tail):
        if tail:
            x = xs_ref[...] + gs_ref[0] * jnp.dot(a_ref[0:ms, :], wbf_ref[...], preferred_element_type=F32)
            os_ref[...] = x
            hn_ref[0:ms, :] = _norm_rows(x, ln_ref[...], scs_ref[0], shs_ref[0]).astype(hn_ref.dtype)
        else:
            x = xp_ref[...] + gp_ref[0] * jnp.dot(a_ref[...], wbf_ref[...], preferred_element_type=F32)
            op_ref[...] = x
            hn_ref[...] = _norm_rows(x, ln_ref[...], scp_ref[0], shp_ref[0]).astype(hn_ref.dtype)

    _split_tiles(0, body)


def out_proj_norm(a, w, layer, xp, xs, gate, ln_g, sc, sh, tm):
    m, k = a.shape
    n = w.shape[2]
    mp, ms = xp.shape[0], xs.shape[0]
    nf = mp // tm
    per = nf // gate.nb
    p_spec = pl.BlockSpec((tm, n), lambda i: (_prompt_tile(i), 0))
    s_spec = pl.BlockSpec((ms, n), lambda i: (0, 0))
    return pl.pallas_call(
        _out_proj_norm_kernel,
        grid=(nf + 1,),
        in_specs=[pl.BlockSpec((tm, k), lambda i: (_joint_tile(i, nf), 0)),
                  pl.BlockSpec((1, k, n), lambda i: (layer, 0, 0), pipeline_mode=pl.Buffered(1)),
                  p_spec, s_spec, *_mod_specs(gate, n, per, 1),
                  pl.BlockSpec((1, n), lambda i: (0, 0)),
                  *_mod_specs(sc, n, per, 1), *_mod_specs(sh, n, per, 1)],
        out_specs=[p_spec, s_spec, pl.BlockSpec((tm, n), lambda i: (_joint_tile(i, nf), 0))],
        out_shape=[jax.ShapeDtypeStruct((mp, n), F32), jax.ShapeDtypeStruct((ms, n), F32),
                   jax.ShapeDtypeStruct((m, n), BF16)],
        scratch_shapes=[pltpu.VMEM((k, n), BF16)],
        compiler_params=_params(1),
        name="out_proj_norm",
    )(a, w, xp, xs, gate.prompt, gate.sample, ln_g, sc.prompt, sc.sample, sh.prompt, sh.sample)


def _swiglu_kernel(a_ref, w1_ref, w3_ref, o_ref, w1bf_ref, w3bf_ref, *, ms):
    @pl.when(pl.program_id(1) == 0)
    def _():
        w1bf_ref[...] = w1_ref[0].astype(BF16)
        w3bf_ref[...] = w3_ref[0].astype(BF16)

    def body(tail):
        rows = slice(0, ms) if tail else slice(None)
        a = a_ref[rows, :]
        h1 = jnp.dot(a, w1bf_ref[...], preferred_element_type=F32)
        h3 = jnp.dot(a, w3bf_ref[...], preferred_element_type=F32)
        o_ref[rows, :] = (jax.nn.silu(h1) * h3).astype(o_ref.dtype)

    _split_tiles(1, body)


def swiglu_up(a, w1, w3, layer, ms, tm, tn):
    m, k = a.shape
    n = w1.shape[2]
    nf = (m - ms) // tm
    w_spec = pl.BlockSpec((1, k, tn), lambda j, i: (layer, 0, j))
    return pl.pallas_call(
        functools.partial(_swiglu_kernel, ms=ms),
        grid=(n // tn, nf + 1),
        in_specs=[pl.BlockSpec((tm, k), lambda j, i: (_joint_tile(i, nf), 0)), w_spec, w_spec],
        out_specs=pl.BlockSpec((tm, tn), lambda j, i: (_joint_tile(i, nf), j)),
        out_shape=jax.ShapeDtypeStruct((m, n), BF16),
        scratch_shapes=[pltpu.VMEM((k, tn), BF16), pltpu.VMEM((k, tn), BF16)],
        compiler_params=_params(2),
        name="swiglu_up",
    )(a, w1, w3)


def _gate_merge_kernel(hn_ref, yap_ref, ybp_ref, ycp_ref, ydp_ref, yas_ref, ybs_ref, ycs_ref, yds_ref,
                       wg0_ref, wg1_ref, wg2_ref, wg3_ref, wb_ref, bg_ref, o_ref, wgbf_ref, wbbf_ref):
    ms = yas_ref.shape[0]

    @pl.when(pl.program_id(1) == 0)
    def _():
        for g, wg_ref in enumerate((wg0_ref, wg1_ref, wg2_ref, wg3_ref)):
            wgbf_ref[g] = wg_ref[0].astype(BF16)
            wbbf_ref[g] = wb_ref[0, g].astype(BF16)

    def body(tail):
        rows = slice(0, ms) if tail else slice(None)
        y_refs = (yas_ref, ybs_ref, ycs_ref, yds_ref) if tail else (yap_ref, ybp_ref, ycp_ref, ydp_ref)
        hn = hn_ref[rows, :]
        acc = None
        for g, y_ref in enumerate(y_refs):
            gate = jax.nn.sigmoid(jnp.dot(hn, wgbf_ref[g], preferred_element_type=F32) + bg_ref[0, g:g + 1, :])
            br = jnp.dot(y_ref[...], wbbf_ref[g], preferred_element_type=F32)
            acc = gate * br if acc is None else acc + gate * br
        o_ref[rows, :] = acc.astype(o_ref.dtype)

    _split_tiles(1, body)


def gate_merge(hn, ys_p, ys_s, w_gate, b_gate, w_branch, layer, tm, tn):
    m, k = hn.shape
    ms = ys_s[0].shape[0]
    nf = (m - ms) // tm
    depth = w_gate.shape[0]
    d = w_branch.shape[3]
    nj = d // tn
    wg_specs = [pl.BlockSpec((1, k, tn), functools.partial(lambda j, i, g: (layer, 0, g * nj + j), g=g))
                for g in range(4)]
    yp_spec = pl.BlockSpec((tm, BR_W), lambda j, i: (_prompt_tile(i), 0))
    ys_spec = pl.BlockSpec((ms, BR_W), lambda j, i: (0, 0))
    return pl.pallas_call(
        _gate_merge_kernel,
        grid=(nj, nf + 1),
        in_specs=[pl.BlockSpec((tm, k), lambda j, i: (_joint_tile(i, nf), 0)), *[yp_spec] * 4, *[ys_spec] * 4,
                  *wg_specs,
                  pl.BlockSpec((1, 4, BR_W, tn), lambda j, i: (layer, 0, 0, j)),
                  pl.BlockSpec((1, 4, tn), lambda j, i: (layer, 0, j))],
        out_specs=pl.BlockSpec((tm, tn), lambda j, i: (_joint_tile(i, nf), j)),
        out_shape=jax.ShapeDtypeStruct((m, d), BF16),
        scratch_shapes=[pltpu.VMEM((4, k, tn), BF16), pltpu.VMEM((4, BR_W, tn), BF16)],
        compiler_params=_params(2),
        name="gate_merge",
    )(hn, *ys_p, *ys_s, w_gate, w_gate, w_gate, w_gate, w_branch, b_gate.reshape(depth, 4, d))


def _gmlp_kernel(pu_ref, pv_ref, lg_ref, lb_ref, ws_ref, bst_ref, ya_ref, va_ref, *, chunk):
    va_ref[...] = _gmlp_tile(pu_ref[...], pv_ref[...], lg_ref, lb_ref, ws_ref, bst_ref, ya_ref, chunk)


def _gmlp_tile(pu, pv, lg_ref, lb_ref, ws_ref, bst_ref, ya_ref, chunk):
    rows = pu.shape[0]
    u = jax.nn.gelu(pu)
    v = jax.nn.gelu(pv)
    mu = jnp.mean(v, axis=-1, keepdims=True)
    var = jnp.mean(jnp.square(v - mu), axis=-1, keepdims=True)
    vn = (v - mu) * lax.rsqrt(var + 1e-5) * lg_ref[...] + lb_ref[...]
    causal = (lax.broadcasted_iota(jnp.int32, (CHUNK, CHUNK), 0)
              >= lax.broadcasted_iota(jnp.int32, (CHUNK, CHUNK), 1))
    for g in range(4):
        wm = jnp.where(causal, ws_ref[0, g], 0.0).astype(BF16)
        bias = bst_ref[0, :, g:g + 1]
        for c in range(rows // chunk):
            r0 = c * chunk
            vc = vn[r0:r0 + chunk, g * GW:(g + 1) * GW]
            if chunk < CHUNK:
                vc = jnp.concatenate([vc, jnp.zeros((CHUNK - chunk, GW), F32)], axis=0)
            mix = (jnp.dot(wm, vc.astype(BF16), preferred_element_type=F32) + bias)[:chunk]
            ya_ref[r0:r0 + chunk, g * GW:(g + 1) * GW] = (
                u[r0:r0 + chunk, g * GW:(g + 1) * GW] * mix).astype(ya_ref.dtype)
    return vn


def gmlp(p, ln_g, ln_b, ws, bst, row0, m, tr, chunk):
    blk0 = row0 // tr
    in_spec = lambda col: pl.BlockSpec((tr, BR_W), lambda i: (blk0 + i, col))
    out_spec = pl.BlockSpec((tr, BR_W), lambda i: (i, 0))
    return pl.pallas_call(
        functools.partial(_gmlp_kernel, chunk=chunk),
        grid=(m // tr,),
        in_specs=[in_spec(0), in_spec(1),
                  pl.BlockSpec((1, BR_W), lambda i: (0, 0)),
                  pl.BlockSpec((1, BR_W), lambda i: (0, 0)),
                  pl.BlockSpec((1, 4, CHUNK, CHUNK), lambda i: (0, 0, 0, 0)),
                  pl.BlockSpec((1, CHUNK, 4), lambda i: (0, 0, 0))],
        out_specs=[out_spec, out_spec],
        out_shape=[jax.ShapeDtypeStruct((m, BR_W), BF16), jax.ShapeDtypeStruct((m, BR_W), F32)],
        compiler_params=_params(1),
        name="gmlp",
    )(p, p, ln_g, ln_b, ws, bst)


def _bias_from_buckets(idx, rel_ref, col, shape):
    bias = jnp.full(shape, NEG, F32)
    for b in range(N_BUCKETS):
        bias = jnp.where(idx == b, rel_ref[b, col], bias)
    return bias


def _bias_mask_kernel(rel_ref, idx_ref, o_ref):
    o_ref[0, 0] = _bias_from_buckets(idx_ref[0], rel_ref, pl.program_id(0) * N_HEADS + pl.program_id(1),
                                     (NK, 2 * NK))


def prompt_bias_mask(rel_bias):
    return pl.pallas_call(
        _bias_mask_kernel,
        grid=(len(PATTERNS), N_HEADS),
        in_specs=[pl.BlockSpec(memory_space=pltpu.SMEM),
                  pl.BlockSpec((1, NK, 2 * NK), lambda g, h: (g, 0, 0))],
        out_specs=pl.BlockSpec((1, 1, NK, 2 * NK), lambda g, h: (g, h, 0, 0)),
        out_shape=jax.ShapeDtypeStruct((len(PATTERNS), N_HEADS, NK, 2 * NK), F32),
        compiler_params=_params(2),
        name="prompt_bias_mask",
    )(rel_bias, jnp.asarray(_prompt_buckets()))


def _attn_prompt_kernel(q_ref, k_ref, v_ref, bm_ref, o_ref,
                        o0_ref, o1_ref, o2_ref, e0_ref, e1_ref, e2_ref):
    t = q_ref.shape[0]
    group = pl.program_id(2)
    scale = GW ** -0.5
    outs = (o0_ref, o1_ref, o2_ref)
    lses = (e0_ref, e1_ref, e2_ref)

    def rows(start, n, dil):
        return pl.ds(start, n) if dil == 1 else pl.ds(start, n, stride=dil)

    def blocks(gi, specs):
        dil = PATTERNS[gi][1]
        staged = []
        for first_block, q0, k0 in specs:
            nkeys = NK if first_block else 2 * NK
            qi = rows(q0, NK, dil)
            ki = rows(k0, nkeys, dil)
            q = q_ref[qi, :].astype(BF16)
            kk = k_ref[ki, :].astype(BF16)
            bm = bm_ref[0, 0, :, NK:] if first_block else bm_ref[0, 0]
            s = lax.dot_general(q, kk, (((1,), (1,)), ((), ())), preferred_element_type=F32) * scale + bm
            staged.append((qi, ki, s))
        probs = []
        for qi, ki, s in staged:
            m = jnp.max(s, axis=-1, keepdims=True)
            p = jnp.exp(s - m)
            l = jnp.sum(p, axis=-1, keepdims=True)
            probs.append((qi, ki, p.astype(BF16), m, l))
        for qi, ki, p, m, l in probs:
            acc = jnp.dot(p, v_ref[ki, :].astype(BF16), preferred_element_type=F32)
            outs[gi][qi, :] = acc * (1.0 / l)
            lses[gi][qi, :] = jnp.broadcast_to(m + jnp.log(l), (NK, GW))

    def run_group(gi):
        dil = PATTERNS[gi][1]
        span = NK * dil
        nb = t // span
        u = ATTN_BLOCKS_PER_STEP
        if dil == 1:
            blocks(gi, [(True, 0, 0)] + [(False, n * span, (n - 1) * span) for n in range(1, u)])

            def step(i, c):
                q0 = pl.multiple_of(u * i * span, span)
                blocks(gi, [(False, q0 + n * span, q0 + (n - 1) * span) for n in range(u)])
                return c

            lax.fori_loop(1, nb // u, step, 0)
        else:
            def bunch(i, c):
                res = [u * i + r for r in range(u)]
                blocks(gi, [(True, r, r) for r in res])

                def later(n, cc):
                    blocks(gi, [(False, n * span + r, (n - 1) * span + r) for r in res])
                    return cc

                return lax.fori_loop(1, nb, later, c)

            if dil == u:
                bunch(0, 0)
            else:
                lax.fori_loop(0, dil // u, bunch, 0)

    for gi in range(len(PATTERNS)):
        pl.when(group == gi)(functools.partial(run_group, gi))

    @pl.when(group == len(PATTERNS) - 1)
    def _():
        step = 256

        def merge(c, carry):
            sl = pl.ds(pl.multiple_of(c * step, step), step)
            e = [r[sl, :] for r in lses]
            top = jnp.maximum(jnp.maximum(e[0], e[1]), e[2])
            w = [jnp.exp(x - top) for x in e]
            num = w[0] * o0_ref[sl, :] + w[1] * o1_ref[sl, :] + w[2] * o2_ref[sl, :]
            o_ref[sl, :] = (num / (w[0] + w[1] + w[2])).astype(o_ref.dtype)
            return carry

        lax.fori_loop(0, t // step, merge, 0)


def _prompt_buckets():
    i = np.arange(NK)[:, None]
    j = np.arange(2 * NK)[None, :]
    diff = NK + i - j
    valid = (diff >= 0) & (diff <= NK)
    return np.stack([np.where(valid, _t5_bucket(diff * dil), -1) for _, dil in PATTERNS]).astype(np.int32)


def attention_prompt(p, bias_mask, nseq, t):
    def qkv_spec(which):
        return pl.BlockSpec((t, GW), lambda b, h, g: (b, COL_B // GW + g * 12 + which * 4 + h))

    return pl.pallas_call(
        _attn_prompt_kernel,
        grid=(nseq, N_HEADS, len(PATTERNS)),
        in_specs=[qkv_spec(0), qkv_spec(1), qkv_spec(2),
                  pl.BlockSpec((1, 1, NK, 2 * NK), lambda b, h, g: (g, h, 0, 0))],
        out_specs=pl.BlockSpec((t, GW), lambda b, h, g: (b, h)),
        out_shape=jax.ShapeDtypeStruct((nseq * t, BR_W), BF16),
        scratch_shapes=[pltpu.VMEM((t, GW), F32)] * 6,
        compiler_params=_params(3),
        name="attention_prompt",
    )(p, p, p, bias_mask)


N_NEW = 4
TILE_ROWS = 2 * N_HEADS


def _key_to_value_rows(x):
    n, r, w = x.shape
    return pltpu.roll(x.reshape(n * r, w), N_HEADS, axis=0).reshape(n, r, w)


def _attend(x, xn, q, bias, bias_n, scale):
    s = jnp.sum(x * q[None], axis=-1, keepdims=True) * scale + bias
    sn = jnp.sum(xn * q[None], axis=-1, keepdims=True) * scale + bias_n
    m = jnp.maximum(jnp.max(s, axis=0, keepdims=True), jnp.max(sn, axis=0, keepdims=True))
    p = jnp.exp(s - m)
    pn = jnp.exp(sn - m)
    l = jnp.sum(p, axis=0, keepdims=True) + jnp.sum(pn, axis=0, keepdims=True)
    acc = jnp.sum(_key_to_value_rows(p) * x, axis=0) + jnp.sum(_key_to_value_rows(pn) * xn, axis=0)
    return acc, m, l


def _attn_sample_kernel(rel_ref, q_ref, xn_ref, c1_ref, c2_ref, c3_ref, o_ref,
                        b1_ref, b1n_ref, b23_ref, b23n_ref):
    scale = GW ** -0.5
    rows = N_NEW * TILE_ROWS

    @pl.when(pl.program_id(0) == 0)
    def _():
        head = lax.broadcasted_iota(jnp.int32, (rows, GW), 0) % N_HEADS
        neg = jnp.full((rows, GW), NEG, F32)

        def tiles_for(gi):
            out = []
            for b in range(N_BUCKETS):
                v = [rel_ref[b, gi * N_HEADS + h] for h in range(N_HEADS)]
                out.append(jnp.where(head == 0, v[0], jnp.where(head == 1, v[1], jnp.where(head == 2, v[2], v[3]))))
            return out

        t1 = tiles_for(0)
        for t in range(N_NEW):
            for pos in range(NK):
                step = NK + t - pos
                tile = t1[int(_t5_bucket(np.int64(step)))] if step <= NK else neg
                b1_ref[t, pos] = tile[:TILE_ROWS]
            for j in range(N_NEW):
                tile = t1[int(_t5_bucket(np.int64(t - j)))] if j <= t else neg
                b1n_ref[t, j] = tile[:TILE_ROWS]
        for gi in (1, 2):
            dil = PATTERNS[gi][1]
            tg = tiles_for(gi)
            for jj in range(NK):
                b23_ref[gi - 1, jj] = tg[int(_t5_bucket(np.int64((NK - jj) * dil)))]
            b23n_ref[gi - 1] = tg[0]

    outs, lses = [], []
    x1 = c1_ref[0, 0]
    xn1 = xn_ref[0, 0].reshape(N_NEW, TILE_ROWS, GW)
    o1, e1 = [], []
    for t in range(N_NEW):
        q = q_ref[0, 0, t * TILE_ROWS:(t + 1) * TILE_ROWS, :]
        acc, m, l = _attend(x1, xn1, q, b1_ref[t], b1n_ref[t], scale)
        o1.append(acc)
        e1.append((m, l))
    outs.append(jnp.concatenate(o1, axis=0))
    lses.append((jnp.concatenate([m[0] for m, _ in e1], axis=0), jnp.concatenate([l[0] for _, l in e1], axis=0)))
    for gi, c_ref in ((1, c2_ref), (2, c3_ref)):
        acc, m, l = _attend(c_ref[0, 0], xn_ref[gi, 0][None], q_ref[gi, 0], b23_ref[gi - 1],
                            b23n_ref[gi - 1][None], scale)
        outs.append(acc)
        lses.append((m[0], l[0]))

    lse = [_key_to_value_rows((m + jnp.log(l))[None])[0] for m, l in lses]
    den = [_key_to_value_rows(l[None])[0] for _, l in lses]
    top = jnp.maximum(jnp.maximum(lse[0], lse[1]), lse[2])
    w = [jnp.exp(e - top) for e in lse]
    num = w[0] * outs[0] / den[0] + w[1] * outs[1] / den[1] + w[2] * outs[2] / den[2]
    o_ref[0] = num / (w[0] + w[1] + w[2])


def attention_sample(p, cache_b1, cache_b2, cache_b3, rel_bias, layer, row0, nseq):
    depth = cache_b1.shape[0]
    rows = N_NEW * TILE_ROWS
    n_groups = len(PATTERNS)
    qkv = p[row0:, COL_B:COL_B + n_groups * 3 * BR_W].reshape(nseq, SAMPLE_ROWS, n_groups, 3, N_HEADS, GW)
    qkv = qkv[:, :N_NEW].transpose(2, 0, 1, 3, 4, 5)
    q = qkv[:, :, :, 0:1]
    q_all = jnp.concatenate([q, jnp.zeros_like(q)], axis=3).reshape(n_groups, nseq, rows, GW)
    xn_all = qkv[:, :, :, 1:3].reshape(n_groups, nseq, rows, GW)
    c1 = cache_b1.reshape(depth, nseq, NK, TILE_ROWS, GW)
    c2 = cache_b2.reshape(depth, nseq, NK, 4 * TILE_ROWS, GW)
    c3 = cache_b3.reshape(depth, nseq, NK, 16 * TILE_ROWS, GW)
    cache_spec = lambda r: pl.BlockSpec((1, 1, NK, r, GW), lambda b: (layer, b, 0, 0, 0))
    tok_spec = pl.BlockSpec((len(PATTERNS), 1, rows, GW), lambda b: (0, b, 0, 0))
    y = pl.pallas_call(
        _attn_sample_kernel,
        grid=(nseq,),
        in_specs=[pl.BlockSpec(memory_space=pltpu.SMEM), tok_spec, tok_spec,
                  cache_spec(TILE_ROWS), cache_spec(rows), cache_spec(rows)],
        out_specs=pl.BlockSpec((1, rows, GW), lambda b: (b, 0, 0)),
        out_shape=jax.ShapeDtypeStruct((nseq, rows, GW), F32),
        scratch_shapes=[pltpu.VMEM((N_NEW, NK, TILE_ROWS, GW), F32), pltpu.VMEM((N_NEW, N_NEW, TILE_ROWS, GW), F32),
                        pltpu.VMEM((2, NK, rows, GW), F32), pltpu.VMEM((2, rows, GW), F32)],
        compiler_params=_params(1),
        name="attention_sample",
    )(rel_bias, q_all, xn_all, c1, c2, c3)
    y = y.reshape(nseq, N_NEW, 2, BR_W)[:, :, 1]
    y = jnp.pad(y, ((0, 0), (0, SAMPLE_ROWS - N_NEW), (0, 0)))
    return y.reshape(nseq * SAMPLE_ROWS, BR_W).astype(BF16)


def _pool_conv_kernel(pc_ref, pbg_ref, pcg_ref, phs_ref, hc_ref, hd_ref, pw_ref, pb_ref, ps_ref, cw_ref,
                      yc_ref, yd_ref, zt_ref, cbuf_ref, zbuf_ref, *, start):
    tr = pc_ref.shape[0]
    j = pl.program_id(1)

    @pl.when(j == 0)
    def _():
        cbuf_ref[0:POOL_HALO, :] = hc_ref[0]
        zbuf_ref[0:CONV_HALO, :] = hd_ref[0]

    _pool_tile(pc_ref[...], start + j * tr, cbuf_ref, pw_ref, pb_ref, ps_ref, yc_ref)
    _conv_tile(pbg_ref[...], pcg_ref[...], phs_ref[...], zbuf_ref, cw_ref, yd_ref, zt_ref)


POOL_HALO = 16
CONV_HALO = 8


def _pool_tile(x, pos0, cbuf_ref, pw_ref, pb_ref, ps_ref, yc_ref):
    tr = x.shape[0]
    hc = POOL_HALO
    cbuf_ref[hc:hc + tr, :] = x
    pos = pos0 + lax.broadcasted_iota(jnp.int32, (tr, 1), 0)
    for gi, win in enumerate(POOL_WINDOWS):
        cols = slice(gi * GW, (gi + 1) * GW)
        total = x[:, cols]
        for back in range(1, win):
            total = total + cbuf_ref[hc - back:hc - back + tr, cols]
        cnt = jnp.minimum(pos + 1, win).astype(F32)
        pooled = total / cnt - x[:, cols]
        y = jnp.dot(pooled.astype(BF16), pw_ref[0, gi].astype(BF16), preferred_element_type=F32)
        yc_ref[:, cols] = ((y + pb_ref[:, cols]) * ps_ref[:, cols]).astype(yc_ref.dtype)
    cbuf_ref[0:hc, :] = cbuf_ref[tr:tr + hc, :]


def _conv_tile(bg, cg, hs, zbuf_ref, cw_ref, yd_ref, zt_ref):
    tr = bg.shape[0]
    hz = CONV_HALO
    z = cg * hs
    zbuf_ref[hz:hz + tr, :] = z
    y = (zbuf_ref[hz - 2:hz - 2 + tr, :] * cw_ref[0:1, :] + zbuf_ref[hz - 1:hz - 1 + tr, :] * cw_ref[1:2, :]
         + z * cw_ref[2:3, :])
    yd_ref[...] = (bg * y).astype(yd_ref.dtype)
    zt_ref[0] = z[tr - hz:, :]
    zbuf_ref[0:hz, :] = zbuf_ref[tr:tr + hz, :]


def pool_conv(p, hist_c, hist_d, pool_w, pool_b, pool_scale, conv_wt, layer, row0, nseq, t, tr, start):
    per = t // tr
    blk0 = row0 // tr
    col = lambda c: pl.BlockSpec((tr, BR_W), lambda b, j: (b * per + j, c))
    pcol = lambda c: pl.BlockSpec((tr, BR_W), lambda b, j: (blk0 + b * per + j, c))
    vec = pl.BlockSpec((1, BR_W), lambda b, j: (0, 0))
    return pl.pallas_call(
        functools.partial(_pool_conv_kernel, start=start),
        grid=(nseq, per),
        in_specs=[pcol(COL_C // BR_W), pcol(COL_D // BR_W), pcol(COL_D // BR_W + 1), pcol(COL_D // BR_W + 2),
                  pl.BlockSpec((1, 16, BR_W), lambda b, j: (b, 0, 0)),
                  pl.BlockSpec((1, 8, BR_W), lambda b, j: (b, 0, 0)),
                  pl.BlockSpec((1, 4, GW, GW), lambda b, j: (layer, 0, 0, 0)),
                  vec, vec,
                  pl.BlockSpec((CONV_W, BR_W), lambda b, j: (0, 0))],
        out_specs=[col(0), col(0), pl.BlockSpec((1, 8, BR_W), lambda b, j: (b, 0, 0))],
        out_shape=[jax.ShapeDtypeStruct((nseq * t, BR_W), BF16),
                   jax.ShapeDtypeStruct((nseq * t, BR_W), BF16),
                   jax.ShapeDtypeStruct((nseq, 8, BR_W), F32)],
        scratch_shapes=[pltpu.VMEM((16 + tr, BR_W), F32), pltpu.VMEM((8 + tr, BR_W), F32)],
        compiler_params=_params(2),
        name="pool_conv",
    )(p, p, p, p, hist_c, hist_d, pool_w, pool_b, pool_scale, conv_wt)


CACHE_ROWS = 1024


def _cache_writer_kernel(k1_ref, v1_ref, k2_ref, v2_ref, k3_ref, v3_ref, b1_any, b2_any, b3_any,
                         o1_ref, o2_ref, o3_ref):
    def scatter(o_ref, k_ref, v_ref, row0, n):
        for kv, src in enumerate((k_ref, v_ref)):
            for h in range(N_HEADS):
                o_ref[0, 0, pl.ds(kv * N_HEADS + h, n, stride=TILE_ROWS), :] = src[row0:row0 + n, h * GW:(h + 1) * GW]

    scatter(o3_ref, k3_ref, v3_ref, 0, CACHE_ROWS)

    @pl.when(pl.program_id(1) == pl.num_programs(1) - 1)
    def _():
        scatter(o2_ref, k2_ref, v2_ref, CACHE_ROWS - PATTERNS[1][0], PATTERNS[1][0])
        scatter(o1_ref, k1_ref, v1_ref, CACHE_ROWS - PATTERNS[0][0], PATTERNS[0][0])


def cache_writer(p, bufs, layer, nseq, t):
    b1, b2, b3 = bufs
    per = t // CACHE_ROWS
    steps = PATTERNS[2][0] // CACHE_ROWS

    def slab(gi, which, whole_window):
        col = (COL_B + gi * 3 * BR_W + (1 + which) * BR_W) // BR_W
        if whole_window:
            return pl.BlockSpec((CACHE_ROWS, BR_W), lambda b, j: ((b + 1) * per - steps + j, col))
        return pl.BlockSpec((CACHE_ROWS, BR_W), lambda b, j: ((b + 1) * per - 1, col))

    any_spec = pl.BlockSpec(memory_space=pl.ANY)
    tile = lambda rows: (1, 1, rows * TILE_ROWS, GW)
    return pl.pallas_call(
        _cache_writer_kernel,
        grid=(nseq, steps),
        in_specs=[slab(0, 0, False), slab(0, 1, False), slab(1, 0, False), slab(1, 1, False),
                  slab(2, 0, True), slab(2, 1, True),
                  any_spec, any_spec, any_spec],
        out_specs=[pl.BlockSpec(tile(PATTERNS[0][0]), lambda b, j: (layer, b, 0, 0)),
                   pl.BlockSpec(tile(PATTERNS[1][0]), lambda b, j: (layer, b, 0, 0)),
                   pl.BlockSpec(tile(CACHE_ROWS), lambda b, j: (layer, b, j, 0))],
        out_shape=[jax.ShapeDtypeStruct(x.shape, x.dtype) for x in bufs],
        input_output_aliases={6: 0, 7: 1, 8: 2},
        compiler_params=_params(2),
        name="cache_writer",
    )(p, p, p, p, p, p, b1, b2, b3)


TM_WIDE = 1024
TM_ROWS = 512


def kernel(x_prompt, x_sample, c_prompt, c_sample, cache_b1, cache_b2, cache_b3, cache_pool, cache_conv,
           ln1_g, ln2_g, w_ada, b_ada, w_in, w_gate, b_gate, a_ln_g, a_ln_b, a_ws, a_bs,
           rel_bias, pool_w, pool_b, pool_scale, conv_w, w_branch, w_out, w1, w3, w2, final_g):
    nb, seq, d = x_prompt.shape
    ns, dec = x_sample.shape[:2]
    depth = w_in.shape[0]
    mp = nb * seq
    ms = ns * SAMPLE_ROWS

    c_all = jnp.concatenate([c_prompt, c_sample], axis=0)
    c_rows = -(-c_all.shape[0] // 8) * 8
    c_all = jnp.pad(c_all, ((0, c_rows - c_all.shape[0]), (0, 0)))
    mod = ada_modulation(c_all, w_ada, b_ada).reshape(depth, c_rows, 6, d)
    mod_p = mod[:, :nb].transpose(0, 2, 1, 3).reshape(depth * 6 * nb, 1, d)
    mod_s = jnp.repeat(mod[:, nb:nb + ns], SAMPLE_ROWS, axis=1).transpose(0, 2, 1, 3).reshape(depth * 6, ms, d)
    mods = [[Mod(mod_p, mod_s, l * 6 + i, nb) for i in range(6)] for l in range(depth)]

    xp = x_prompt.reshape(mp, d)
    xs = jnp.pad(x_sample, ((0, 0), (0, SAMPLE_ROWS - dec), (0, 0))).reshape(ms, d)

    bias_mask = prompt_bias_mask(rel_bias)
    kv_bufs = tuple(jnp.zeros((depth, nb, win * TILE_ROWS, GW), F32) for win, _ in PATTERNS)
    pool_p, conv_p, proj_s, conv_s, chunk_s = [], [], [], [], []
    a_bst = a_bs.transpose(0, 2, 1)
    conv_wt = conv_w.transpose(0, 2, 1)
    hist_c_all = jnp.pad(cache_pool, ((0, 0), (0, 0), (1, 0), (0, 0)))
    hist_d_all = jnp.pad(cache_conv, ((0, 0), (0, 0), (8 - (CONV_W - 1), 0), (0, 0)))
    for l in range(depth):
        sh1, sc1, g1, sh2, sc2, g2 = mods[l]
        hn = norm_in(xp, xs, ln1_g[l][None], sc1, sh1, TM_WIDE)
        gm = (a_ln_g[l][None], a_ln_b[l][None], a_ws[l][None], a_bst[l][None])
        pc = (pool_w, pool_b[l][None], pool_scale[l][None], conv_wt[l], l)
        p, ya_p, yc_p, yd_p, zt_p = in_proj_mixers(hn, w_in, l, ms, TM_ROWS, nb, gm, pc)
        ya_s, va_s = gmlp(p, *gm, mp, ms, ms, SAMPLE_ROWS)
        yb_p = attention_prompt(p, bias_mask, nb, seq)
        yb_s = attention_sample(p, cache_b1, cache_b2, cache_b3, rel_bias, l, mp, ns)
        yc_s, yd_s, zt_s = pool_conv(p, hist_c_all[l], hist_d_all[l], *pc, mp, ns, SAMPLE_ROWS, SAMPLE_ROWS,
                                     PAST_LEN)

        merged = gate_merge(hn, (ya_p, yb_p, yc_p, yd_p), (ya_s, yb_s, yc_s, yd_s),
                            w_gate, b_gate, w_branch, l, TM_WIDE, 256)
        xp, xs, hn2 = out_proj_norm(merged, w_out, l, xp, xs, g1, ln2_g[l][None], sc2, sh2, TM_ROWS)
        hmid = swiglu_up(hn2, w1, w3, l, ms, TM_WIDE, 512)
        xp, xs = matmul_residual(hmid, w2, l, xp, xs, g2, TM_ROWS, 512)

        kv_bufs = cache_writer(p, kv_bufs, l, nb, seq)
        pool_p.append(jnp.stack([p[(b + 1) * seq - POOL_HIST:(b + 1) * seq, COL_C:COL_C + BR_W]
                                 for b in range(nb)]))
        conv_p.append(zt_p[:nb])
        proj_s.append(p[mp:])
        conv_s.append(zt_s)
        chunk_s.append(va_s)

    yp, ys = final_norm(xp, xs, final_g[None], TM_WIDE)
    stack = lambda parts: jnp.stack(parts, axis=0)
    kv_p = [buf.reshape(depth, nb, win, 2, N_HEADS, GW) for buf, (win, _) in zip(kv_bufs, PATTERNS)]
    ps = stack(proj_s).reshape(depth, ns, SAMPLE_ROWS, N_IN)[:, :, :dec]
    kv_s = [ps[..., COL_B + (3 * gi + 1) * BR_W:COL_B + (3 * gi + 3) * BR_W].reshape(depth, ns, dec, 2, N_HEADS, GW)
            for gi in range(len(PATTERNS))]
    return (yp.reshape(nb, seq, d), ys.reshape(ns, SAMPLE_ROWS, d)[:, :dec],
            kv_p[0], kv_p[1], kv_p[2], stack(pool_p), stack(conv_p)[:, :, 8 - (CONV_W - 1):],
            kv_s[0], kv_s[1], kv_s[2], ps[..., COL_C:COL_C + BR_W], stack(conv_s)[:, :, :dec],
            stack(chunk_s).reshape(depth, ns, SAMPLE_ROWS, BR_W)[:, :, :dec])
```

```python
import functools
import math
from typing import NamedTuple

import jax
import jax.numpy as jnp
import numpy as np
from jax import lax
from jax.experimental import pallas as pl
from jax.experimental.pallas import tpu as pltpu

F32 = jnp.float32
BF16 = jnp.bfloat16

D_MODEL = 2048
BR_W = 512
GW = 128
N_HEADS = 4
CHUNK = 128
PATTERNS = ((128, 1), (512, 4), (2048, 16))
NK = 128
POOL_WINDOWS = (2, 4, 8, 16)
POOL_HIST = 15
CONV_W = 3
N_BUCKETS = 32
MAX_DIST = 2048
D_FF = 5632
N_IN = 7680
COL_A = 0
COL_B = 2 * BR_W
COL_C = COL_B + 9 * BR_W
COL_D = COL_C + BR_W
EPS = 1e-6
NEG = -1e30
SAMPLE_ROWS = 8
PAST_LEN = 16384
ATTN_BLOCKS_PER_STEP = 4

VMEM_LIMIT = 56 * 1024 * 1024


def _t5_bucket(dist):
    max_exact = N_BUCKETS // 2
    n = np.maximum(dist, 0)
    nf = np.maximum(n, 1).astype(np.float32)
    large = max_exact + (np.log(nf / np.float32(max_exact)) / np.float32(math.log(MAX_DIST / max_exact))
                         * np.float32(N_BUCKETS - max_exact)).astype(np.int32)
    large = np.minimum(large, N_BUCKETS - 1)
    return np.where(n < max_exact, n, large)


def _params(n_axes):
    return pltpu.CompilerParams(dimension_semantics=("arbitrary",) * n_axes,
                                vmem_limit_bytes=VMEM_LIMIT)


def _ada_kernel(c_ref, w_ref, b_ref, o_ref):
    a = jax.nn.silu(c_ref[...]).astype(BF16)
    o_ref[0] = jnp.dot(a, w_ref[0].astype(BF16), preferred_element_type=F32) + b_ref[0]


def ada_modulation(c_all, w_ada, b_ada):
    depth, d, n = w_ada.shape
    r = c_all.shape[0]
    tn = 1024
    return pl.pallas_call(
        _ada_kernel,
        grid=(depth, n // tn),
        in_specs=[pl.BlockSpec((r, d), lambda l, j: (0, 0)),
                  pl.BlockSpec((1, d, tn), lambda l, j: (l, 0, j)),
                  pl.BlockSpec((1, 1, tn), lambda l, j: (l, 0, j))],
        out_specs=pl.BlockSpec((1, r, tn), lambda l, j: (l, 0, j)),
        out_shape=jax.ShapeDtypeStruct((depth, r, n), F32),
        compiler_params=_params(2),
        name="ada_modulation",
    )(c_all, w_ada, b_ada.reshape(depth, 1, n))


def _split_tiles(axis, body):
    i = pl.program_id(axis)
    pl.when(i == 0)(functools.partial(body, True))
    pl.when(i > 0)(functools.partial(body, False))


def _joint_tile(i, nf):
    return (i + nf) % (nf + 1)


def _prompt_tile(i):
    return jnp.maximum(i - 1, 0)


def _norm_rows(x, g, sc, sh):
    y = x * lax.rsqrt(jnp.mean(x * x, axis=-1, keepdims=True) + EPS) * g
    return y * (1.0 + sc) + sh


class Mod(NamedTuple):
    prompt: jax.Array
    sample: jax.Array
    index: int
    nb: int


def _mod_specs(mod, width, per, grid_rank):
    ms = mod.sample.shape[1]
    base = mod.index * mod.nb
    if grid_rank == 1:
        return [pl.BlockSpec((1, 1, width), lambda i: (base + _prompt_tile(i) // per, 0, 0)),
                pl.BlockSpec((1, ms, width), lambda i: (mod.index, 0, 0))]
    return [pl.BlockSpec((1, 1, width), lambda j, i: (base + _prompt_tile(i) // per, 0, j)),
            pl.BlockSpec((1, ms, width), lambda j, i: (mod.index, 0, j))]


def _norm_in_kernel(xp_ref, xs_ref, g_ref, scp_ref, scs_ref, shp_ref, shs_ref, o_ref):
    ms = xs_ref.shape[0]

    def body(tail):
        if tail:
            o_ref[0:ms, :] = _norm_rows(xs_ref[...], g_ref[...], scs_ref[0], shs_ref[0]).astype(o_ref.dtype)
        else:
            o_ref[...] = _norm_rows(xp_ref[...], g_ref[...], scp_ref[0], shp_ref[0]).astype(o_ref.dtype)

    _split_tiles(0, body)


def norm_in(xp, xs, g, sc, sh, tm):
    mp, d = xp.shape
    ms = xs.shape[0]
    nf = mp // tm
    per = nf // sc.nb
    return pl.pallas_call(
        _norm_in_kernel,
        grid=(nf + 1,),
        in_specs=[pl.BlockSpec((tm, d), lambda i: (_prompt_tile(i), 0)),
                  pl.BlockSpec((ms, d), lambda i: (0, 0)),
                  pl.BlockSpec((1, d), lambda i: (0, 0)),
                  *_mod_specs(sc, d, per, 1), *_mod_specs(sh, d, per, 1)],
        out_specs=pl.BlockSpec((tm, d), lambda i: (_joint_tile(i, nf), 0)),
        out_shape=jax.ShapeDtypeStruct((mp + ms, d), BF16),
        compiler_params=_params(1),
        name="norm_in",
    )(xp, xs, g, sc.prompt, sc.sample, sh.prompt, sh.sample)


def _final_norm_kernel(xp_ref, xs_ref, g_ref, yp_ref, ys_ref):
    def body(tail):
        src, dst = (xs_ref, ys_ref) if tail else (xp_ref, yp_ref)
        x = src[...]
        dst[...] = x * lax.rsqrt(jnp.mean(x * x, axis=-1, keepdims=True) + EPS) * g_ref[...]

    _split_tiles(0, body)


def final_norm(xp, xs, g, tm):
    mp, d = xp.shape
    ms = xs.shape[0]
    nf = mp // tm
    p_spec = pl.BlockSpec((tm, d), lambda i: (_prompt_tile(i), 0))
    s_spec = pl.BlockSpec((ms, d), lambda i: (0, 0))
    return pl.pallas_call(
        _final_norm_kernel,
        grid=(nf + 1,),
        in_specs=[p_spec, s_spec, pl.BlockSpec((1, d), lambda i: (0, 0))],
        out_specs=[p_spec, s_spec],
        out_shape=[jax.ShapeDtypeStruct((mp, d), F32), jax.ShapeDtypeStruct((ms, d), F32)],
        compiler_params=_params(1),
        name="final_norm",
    )(xp, xs, g)


IN_TN = 1536


def _in_proj_kernel(a_ref, w_ref, lg_ref, lb_ref, ws_ref, bst_ref, pw_ref, pb_ref, ps_ref, cw_ref,
                    o_ref, ya_ref, yc_ref, yd_ref, zt_ref, wbf_ref, cbuf_ref, zbuf_ref, *, ms, per_seq):
    j = pl.program_id(0)
    i = pl.program_id(1)
    tm = a_ref.shape[0]
    tile = i - 1
    first = (tile % per_seq) == 0

    def project():
        acc = jnp.dot(a_ref[...], wbf_ref[...], preferred_element_type=F32)
        o_ref[...] = acc
        return acc

    @pl.when(i == 0)
    def _():
        wbf_ref[...] = w_ref[0].astype(BF16)
        o_ref[0:ms, :] = jnp.dot(a_ref[0:ms, :], wbf_ref[...], preferred_element_type=F32)
        for ref in (ya_ref, yc_ref, yd_ref, zt_ref):
            ref[...] = jnp.zeros(ref.shape, ref.dtype)

    @pl.when((i > 0) & first & (j == COL_C // IN_TN))
    def _():
        cbuf_ref[0:POOL_HALO, :] = jnp.zeros((POOL_HALO, BR_W), F32)

    @pl.when((i > 0) & first & (j == COL_D // IN_TN))
    def _():
        zbuf_ref[0:CONV_HALO, :] = jnp.zeros((CONV_HALO, BR_W), F32)

    @pl.when((i > 0) & (j == COL_A // IN_TN))
    def _():
        acc = project()
        _gmlp_tile(acc[:, 0:BR_W], acc[:, BR_W:2 * BR_W], lg_ref, lb_ref, ws_ref, bst_ref, ya_ref, CHUNK)

    @pl.when((i > 0) & (j == COL_C // IN_TN))
    def _():
        acc = project()
        c0 = COL_C % IN_TN
        _pool_tile(acc[:, c0:c0 + BR_W], (tile % per_seq) * tm, cbuf_ref, pw_ref, pb_ref, ps_ref, yc_ref)

    @pl.when((i > 0) & (j == COL_D // IN_TN))
    def _():
        acc = project()
        _conv_tile(acc[:, 0:BR_W], acc[:, BR_W:2 * BR_W], acc[:, 2 * BR_W:3 * BR_W], zbuf_ref, cw_ref, yd_ref, zt_ref)

    @pl.when((i > 0) & (j != COL_A // IN_TN) & (j != COL_C // IN_TN) & (j != COL_D // IN_TN))
    def _():
        project()


def in_proj_mixers(a, w, layer, ms, tm, nseq, gm, pc):
    m, k = a.shape
    n = w.shape[2]
    nf = (m - ms) // tm
    per_seq = nf // nseq
    ln_g, ln_b, ws, bst = gm
    pool_w, pool_b, pool_scale, conv_wt, _ = pc
    ga, gc, gd = COL_A // IN_TN, COL_C // IN_TN, COL_D // IN_TN

    def parked(j, i, tile_j, live, n_live):
        before = (j < tile_j) | ((j == tile_j) & (i == 0))
        return jnp.where((j == tile_j) & (i > 0), live, jnp.where(before, n_live, n_live + 1))

    def n_parked(tile_j):
        return 1 if tile_j == n // IN_TN - 1 else 2

    def mixer_spec(tile_j):
        return pl.BlockSpec((tm, BR_W), lambda j, i: (parked(j, i, tile_j, i - 1, nf), 0))

    vec = pl.BlockSpec((1, BR_W), lambda j, i: (0, 0))
    return pl.pallas_call(
        functools.partial(_in_proj_kernel, ms=ms, per_seq=per_seq),
        grid=(n // IN_TN, nf + 1),
        in_specs=[pl.BlockSpec((tm, k), lambda j, i: (_joint_tile(i, nf), 0)),
                  pl.BlockSpec((1, k, IN_TN), lambda j, i: (layer, 0, j)),
                  vec, vec,
                  pl.BlockSpec((1, 4, CHUNK, CHUNK), lambda j, i: (0, 0, 0, 0)),
                  pl.BlockSpec((1, CHUNK, 4), lambda j, i: (0, 0, 0)),
                  pl.BlockSpec((1, 4, GW, GW), lambda j, i: (layer, 0, 0, 0)),
                  vec, vec,
                  pl.BlockSpec((CONV_W, BR_W), lambda j, i: (0, 0))],
        out_specs=[pl.BlockSpec((tm, IN_TN), lambda j, i: (_joint_tile(i, nf), j)),
                   mixer_spec(ga), mixer_spec(gc), mixer_spec(gd),
                   pl.BlockSpec((1, CONV_HALO, BR_W),
                                lambda j, i: (parked(j, i, gd, (i - 1) // per_seq, nseq), 0, 0))],
        out_shape=[jax.ShapeDtypeStruct((m, n), F32)]
        + [jax.ShapeDtypeStruct(((nf + n_parked(t)) * tm, BR_W), BF16) for t in (ga, gc, gd)]
        + [jax.ShapeDtypeStruct((nseq + n_parked(gd), CONV_HALO, BR_W), F32)],
        scratch_shapes=[pltpu.VMEM((k, IN_TN), BF16),
                        pltpu.VMEM((POOL_HALO + tm, BR_W), F32), pltpu.VMEM((CONV_HALO + tm, BR_W), F32)],
        compiler_params=_params(2),
        name="in_proj_mixers",
    )(a, w, ln_g, ln_b, ws, bst, pool_w, pool_b, pool_scale, conv_wt)


def _mm_residual_kernel(a_ref, w_ref, xp_ref, xs_ref, gp_ref, gs_ref, op_ref, os_ref, wbf_ref):
    ms = xs_ref.shape[0]

    @pl.when(pl.program_id(1) == 0)
    def _():
        wbf_ref[...] = w_ref[0].astype(BF16)

    def body(tail):
        if tail:
            y = jnp.dot(a_ref[0:ms, :], wbf_ref[...], preferred_element_type=F32)
            os_ref[...] = xs_ref[...] + gs_ref[0] * y
        else:
            y = jnp.dot(a_ref[...], wbf_ref[...], preferred_element_type=F32)
            op_ref[...] = xp_ref[...] + gp_ref[0] * y

    _split_tiles(1, body)


def matmul_residual(a, w, layer, xp, xs, gate, tm, tn):
    m, k = a.shape
    n = w.shape[2]
    mp, ms = xp.shape[0], xs.shape[0]
    nf = mp // tm
    per = nf // gate.nb
    p_spec = pl.BlockSpec((tm, tn), lambda j, i: (_prompt_tile(i), j))
    s_spec = pl.BlockSpec((ms, tn), lambda j, i: (0, j))
    return pl.pallas_call(
        _mm_residual_kernel,
        grid=(n // tn, nf + 1),
        in_specs=[pl.BlockSpec((tm, k), lambda j, i: (_joint_tile(i, nf), 0)),
                  pl.BlockSpec((1, k, tn), lambda j, i: (layer, 0, j)),
                  p_spec, s_spec, *_mod_specs(gate, tn, per, 2)],
        out_specs=[p_spec, s_spec],
        out_shape=[jax.ShapeDtypeStruct((mp, n), F32), jax.ShapeDtypeStruct((ms, n), F32)],
        scratch_shapes=[pltpu.VMEM((k, tn), BF16)],
        compiler_params=_params(2),
        name="matmul_residual",
    )(a, w, xp, xs, gate.prompt, gate.sample)


def _out_proj_norm_kernel(a_ref, w_ref, xp_ref, xs_ref, gp_ref, gs_ref, ln_ref, scp_ref, scs_ref,
                          shp_ref, shs_ref, op_ref, os_ref, hn_ref, wbf_ref):
    ms = xs_ref.shape[0]

    @pl.when(pl.program_id(0) == 0)
    def _():
        wbf_ref[...] = w_ref[0].astype(BF16)

    def body(tail):
        if tail:
            x = xs_ref[...] + gs_ref[0] * jnp.dot(a_ref[0:ms, :], wbf_ref[...], preferred_element_type=F32)
            os_ref[...] = x
            hn_ref[0:ms, :] = _norm_rows(x, ln_ref[...], scs_ref[0], shs_ref[0]).astype(hn_ref.dtype)
        else:
            x = xp_ref[...] + gp_ref[0] * jnp.dot(a_ref[...], wbf_ref[...], preferred_element_type=F32)
            op_ref[...] = x
            hn_ref[...] = _norm_rows(x, ln_ref[...], scp_ref[0], shp_ref[0]).astype(hn_ref.dtype)

    _split_tiles(0, body)


def out_proj_norm(a, w, layer, xp, xs, gate, ln_g, sc, sh, tm):
    m, k = a.shape
    n = w.shape[2]
    mp, ms = xp.shape[0], xs.shape[0]
    nf = mp // tm
    per = nf // gate.nb
    p_spec = pl.BlockSpec((tm, n), lambda i: (_prompt_tile(i), 0))
    s_spec = pl.BlockSpec((ms, n), lambda i: (0, 0))
    return pl.pallas_call(
        _out_proj_norm_kernel,
        grid=(nf + 1,),
        in_specs=[pl.BlockSpec((tm, k), lambda i: (_joint_tile(i, nf), 0)),
                  pl.BlockSpec((1, k, n), lambda i: (layer, 0, 0), pipeline_mode=pl.Buffered(1)),
                  p_spec, s_spec, *_mod_specs(gate, n, per, 1),
                  pl.BlockSpec((1, n), lambda i: (0, 0)),
                  *_mod_specs(sc, n, per, 1), *_mod_specs(sh, n, per, 1)],
        out_specs=[p_spec, s_spec, pl.BlockSpec((tm, n), lambda i: (_joint_tile(i, nf), 0))],
        out_shape=[jax.ShapeDtypeStruct((mp, n), F32), jax.ShapeDtypeStruct((ms, n), F32),
                   jax.ShapeDtypeStruct((m, n), BF16)],
        scratch_shapes=[pltpu.VMEM((k, n), BF16)],
        compiler_params=_params(1),
        name="out_proj_norm",
    )(a, w, xp, xs, gate.prompt, gate.sample, ln_g, sc.prompt, sc.sample, sh.prompt, sh.sample)


def _swiglu_kernel(a_ref, w1_ref, w3_ref, o_ref, w1bf_ref, w3bf_ref, *, ms):
    @pl.when(pl.program_id(1) == 0)
    def _():
        w1bf_ref[...] = w1_ref[0].astype(BF16)
        w3bf_ref[...] = w3_ref[0].astype(BF16)

    def body(tail):
        rows = slice(0, ms) if tail else slice(None)
        a = a_ref[rows, :]
        h1 = jnp.dot(a, w1bf_ref[...], preferred_element_type=F32)
        h3 = jnp.dot(a, w3bf_ref[...], preferred_element_type=F32)
        o_ref[rows, :] = (jax.nn.silu(h1) * h3).astype(o_ref.dtype)

    _split_tiles(1, body)


def swiglu_up(a, w1, w3, layer, ms, tm, tn):
    m, k = a.shape
    n = w1.shape[2]
    nf = (m - ms) // tm
    w_spec = pl.BlockSpec((1, k, tn), lambda j, i: (layer, 0, j))
    return pl.pallas_call(
        functools.partial(_swiglu_kernel, ms=ms),
        grid=(n // tn, nf + 1),
        in_specs=[pl.BlockSpec((tm, k), lambda j, i: (_joint_tile(i, nf), 0)), w_spec, w_spec],
        out_specs=pl.BlockSpec((tm, tn), lambda j, i: (_joint_tile(i, nf), j)),
        out_shape=jax.ShapeDtypeStruct((m, n), BF16),
        scratch_shapes=[pltpu.VMEM((k, tn), BF16), pltpu.VMEM((k, tn), BF16)],
        compiler_params=_params(2),
        name="swiglu_up",
    )(a, w1, w3)


def _gate_merge_kernel(hn_ref, yap_ref, ybp_ref, ycp_ref, ydp_ref, yas_ref, ybs_ref, ycs_ref, yds_ref,
                       wg0_ref, wg1_ref, wg2_ref, wg3_ref, wb_ref, bg_ref, o_ref, wgbf_ref, wbbf_ref):
    ms = yas_ref.shape[0]

    @pl.when(pl.program_id(1) == 0)
    def _():
        for g, wg_ref in enumerate((wg0_ref, wg1_ref, wg2_ref, wg3_ref)):
            wgbf_ref[g] = wg_ref[0].astype(BF16)
            wbbf_ref[g] = wb_ref[0, g].astype(BF16)

    def body(tail):
        rows = slice(0, ms) if tail else slice(None)
        y_refs = (yas_ref, ybs_ref, ycs_ref, yds_ref) if tail else (yap_ref, ybp_ref, ycp_ref, ydp_ref)
        hn = hn_ref[rows, :]
        acc = None
        for g, y_ref in enumerate(y_refs):
            gate = jax.nn.sigmoid(jnp.dot(hn, wgbf_ref[g], preferred_element_type=F32) + bg_ref[0, g:g + 1, :])
            br = jnp.dot(y_ref[...], wbbf_ref[g], preferred_element_type=F32)
            acc = gate * br if acc is None else acc + gate * br
        o_ref[rows, :] = acc.astype(o_ref.dtype)

    _split_tiles(1, body)


def gate_merge(hn, ys_p, ys_s, w_gate, b_gate, w_branch, layer, tm, tn):
    m, k = hn.shape
    ms = ys_s[0].shape[0]
    nf = (m - ms) // tm
    depth = w_gate.shape[0]
    d = w_branch.shape[3]
    nj = d // tn
    wg_specs = [pl.BlockSpec((1, k, tn), functools.partial(lambda j, i, g: (layer, 0, g * nj + j), g=g))
                for g in range(4)]
    yp_spec = pl.BlockSpec((tm, BR_W), lambda j, i: (_prompt_tile(i), 0))
    ys_spec = pl.BlockSpec((ms, BR_W), lambda j, i: (0, 0))
    return pl.pallas_call(
        _gate_merge_kernel,
        grid=(nj, nf + 1),
        in_specs=[pl.BlockSpec((tm, k), lambda j, i: (_joint_tile(i, nf), 0)), *[yp_spec] * 4, *[ys_spec] * 4,
                  *wg_specs,
                  pl.BlockSpec((1, 4, BR_W, tn), lambda j, i: (layer, 0, 0, j)),
                  pl.BlockSpec((1, 4, tn), lambda j, i: (layer, 0, j))],
        out_specs=pl.BlockSpec((tm, tn), lambda j, i: (_joint_tile(i, nf), j)),
        out_shape=jax.ShapeDtypeStruct((m, d), BF16),
        scratch_shapes=[pltpu.VMEM((4, k, tn), BF16), pltpu.VMEM((4, BR_W, tn), BF16)],
        compiler_params=_params(2),
        name="gate_merge",
    )(hn, *ys_p, *ys_s, w_gate, w_gate, w_gate, w_gate, w_branch, b_gate.reshape(depth, 4, d))


def _gmlp_kernel(pu_ref, pv_ref, lg_ref, lb_ref, ws_ref, bst_ref, ya_ref, va_ref, *, chunk):
    va_ref[...] = _gmlp_tile(pu_ref[...], pv_ref[...], lg_ref, lb_ref, ws_ref, bst_ref, ya_ref, chunk)


def _gmlp_tile(pu, pv, lg_ref, lb_ref, ws_ref, bst_ref, ya_ref, chunk):
    rows = pu.shape[0]
    u = jax.nn.gelu(pu)
    v = jax.nn.gelu(pv)
    mu = jnp.mean(v, axis=-1, keepdims=True)
    var = jnp.mean(jnp.square(v - mu), axis=-1, keepdims=True)
    vn = (v - mu) * lax.rsqrt(var + 1e-5) * lg_ref[...] + lb_ref[...]
    causal = (lax.broadcasted_iota(jnp.int32, (CHUNK, CHUNK), 0)
              >= lax.broadcasted_iota(jnp.int32, (CHUNK, CHUNK), 1))
    for g in range(4):
        wm = jnp.where(causal, ws_ref[0, g], 0.0).astype(BF16)
        bias = bst_ref[0, :, g:g + 1]
        for c in range(rows // chunk):
            r0 = c * chunk
            vc = vn[r0:r0 + chunk, g * GW:(g + 1) * GW]
            if chunk < CHUNK:
                vc = jnp.concatenate([vc, jnp.zeros((CHUNK - chunk, GW), F32)], axis=0)
            mix = (jnp.dot(wm, vc.astype(BF16), preferred_element_type=F32) + bias)[:chunk]
            ya_ref[r0:r0 + chunk, g * GW:(g + 1) * GW] = (
                u[r0:r0 + chunk, g * GW:(g + 1) * GW] * mix).astype(ya_ref.dtype)
    return vn


def gmlp(p, ln_g, ln_b, ws, bst, row0, m, tr, chunk):
    blk0 = row0 // tr
    in_spec = lambda col: pl.BlockSpec((tr, BR_W), lambda i: (blk0 + i, col))
    out_spec = pl.BlockSpec((tr, BR_W), lambda i: (i, 0))
    return pl.pallas_call(
        functools.partial(_gmlp_kernel, chunk=chunk),
        grid=(m // tr,),
        in_specs=[in_spec(0), in_spec(1),
                  pl.BlockSpec((1, BR_W), lambda i: (0, 0)),
                  pl.BlockSpec((1, BR_W), lambda i: (0, 0)),
                  pl.BlockSpec((1, 4, CHUNK, CHUNK), lambda i: (0, 0, 0, 0)),
                  pl.BlockSpec((1, CHUNK, 4), lambda i: (0, 0, 0))],
        out_specs=[out_spec, out_spec],
        out_shape=[jax.ShapeDtypeStruct((m, BR_W), BF16), jax.ShapeDtypeStruct((m, BR_W), F32)],
        compiler_params=_params(1),
        name="gmlp",
    )(p, p, ln_g, ln_b, ws, bst)


def _bias_from_buckets(idx, rel_ref, col, shape):
    bias = jnp.full(shape, NEG, F32)
    for b in range(N_BUCKETS):
        bias = jnp.where(idx == b, rel_ref[b, col], bias)
    return bias


def _bias_mask_kernel(rel_ref, idx_ref, o_ref):
    o_ref[0, 0] = _bias_from_buckets(idx_ref[0], rel_ref, pl.program_id(0) * N_HEADS + pl.program_id(1),
                                     (NK, 2 * NK))


def prompt_bias_mask(rel_bias):
    return pl.pallas_call(
        _bias_mask_kernel,
        grid=(len(PATTERNS), N_HEADS),
        in_specs=[pl.BlockSpec(memory_space=pltpu.SMEM),
                  pl.BlockSpec((1, NK, 2 * NK), lambda g, h: (g, 0, 0))],
        out_specs=pl.BlockSpec((1, 1, NK, 2 * NK), lambda g, h: (g, h, 0, 0)),
        out_shape=jax.ShapeDtypeStruct((len(PATTERNS), N_HEADS, NK, 2 * NK), F32),
        compiler_params=_params(2),
        name="prompt_bias_mask",
    )(rel_bias, jnp.asarray(_prompt_buckets()))


def _attn_prompt_kernel(q_ref, k_ref, v_ref, bm_ref, o_ref,
                        o0_ref, o1_ref, o2_ref, e0_ref, e1_ref, e2_ref):
    t = q_ref.shape[0]
    group = pl.program_id(2)
    scale = GW ** -0.5
    outs = (o0_ref, o1_ref, o2_ref)
    lses = (e0_ref, e1_ref, e2_ref)

    def rows(start, n, dil):
        return pl.ds(start, n) if dil == 1 else pl.ds(start, n, stride=dil)

    def blocks(gi, specs):
        dil = PATTERNS[gi][1]
        staged = []
        for first_block, q0, k0 in specs:
            nkeys = NK if first_block else 2 * NK
            qi = rows(q0, NK, dil)
            ki = rows(k0, nkeys, dil)
            q = q_ref[qi, :].astype(BF16)
            kk = k_ref[ki, :].astype(BF16)
            bm = bm_ref[0, 0, :, NK:] if first_block else bm_ref[0, 0]
            s = lax.dot_general(q, kk, (((1,), (1,)), ((), ())), preferred_element_type=F32) * scale + bm
            staged.append((qi, ki, s))
        probs = []
        for qi, ki, s in staged:
            m = jnp.max(s, axis=-1, keepdims=True)
            p = jnp.exp(s - m)
            l = jnp.sum(p, axis=-1, keepdims=True)
            probs.append((qi, ki, p.astype(BF16), m, l))
        for qi, ki, p, m, l in probs:
            acc = jnp.dot(p, v_ref[ki, :].astype(BF16), preferred_element_type=F32)
            outs[gi][qi, :] = acc * (1.0 / l)
            lses[gi][qi, :] = jnp.broadcast_to(m + jnp.log(l), (NK, GW))

    def run_group(gi):
        dil = PATTERNS[gi][1]
        span = NK * dil
        nb = t // span
        u = ATTN_BLOCKS_PER_STEP
        if dil == 1:
            blocks(gi, [(True, 0, 0)] + [(False, n * span, (n - 1) * span) for n in range(1, u)])

            def step(i, c):
                q0 = pl.multiple_of(u * i * span, span)
                blocks(gi, [(False, q0 + n * span, q0 + (n - 1) * span) for n in range(u)])
                return c

            lax.fori_loop(1, nb // u, step, 0)
        else:
            def bunch(i, c):
                res = [u * i + r for r in range(u)]
                blocks(gi, [(True, r, r) for r in res])

                def later(n, cc):
                    blocks(gi, [(False, n * span + r, (n - 1) * span + r) for r in res])
                    return cc

                return lax.fori_loop(1, nb, later, c)

            if dil == u:
                bunch(0, 0)
            else:
                lax.fori_loop(0, dil // u, bunch, 0)

    for gi in range(len(PATTERNS)):
        pl.when(group == gi)(functools.partial(run_group, gi))

    @pl.when(group == len(PATTERNS) - 1)
    def _():
        step = 256

        def merge(c, carry):
            sl = pl.ds(pl.multiple_of(c * step, step), step)
            e = [r[sl, :] for r in lses]
            top = jnp.maximum(jnp.maximum(e[0], e[1]), e[2])
            w = [jnp.exp(x - top) for x in e]
            num = w[0] * o0_ref[sl, :] + w[1] * o1_ref[sl, :] + w[2] * o2_ref[sl, :]
            o_ref[sl, :] = (num / (w[0] + w[1] + w[2])).astype(o_ref.dtype)
            return carry

        lax.fori_loop(0, t // step, merge, 0)


def _prompt_buckets():
    i = np.arange(NK)[:, None]
    j = np.arange(2 * NK)[None, :]
    diff = NK + i - j
    valid = (diff >= 0) & (diff <= NK)
    return np.stack([np.where(valid, _t5_bucket(diff * dil), -1) for _, dil in PATTERNS]).astype(np.int32)


def attention_prompt(p, bias_mask, nseq, t):
    def qkv_spec(which):
        return pl.BlockSpec((t, GW), lambda b, h, g: (b, COL_B // GW + g * 12 + which * 4 + h))

    return pl.pallas_call(
        _attn_prompt_kernel,
        grid=(nseq, N_HEADS, len(PATTERNS)),
        in_specs=[qkv_spec(0), qkv_spec(1), qkv_spec(2),
                  pl.BlockSpec((1, 1, NK, 2 * NK), lambda b, h, g: (g, h, 0, 0))],
        out_specs=pl.BlockSpec((t, GW), lambda b, h, g: (b, h)),
        out_shape=jax.ShapeDtypeStruct((nseq * t, BR_W), BF16),
        scratch_shapes=[pltpu.VMEM((t, GW), F32)] * 6,
        compiler_params=_params(3),
        name="attention_prompt",
    )(p, p, p, bias_mask)


N_NEW = 4
TILE_ROWS = 2 * N_HEADS


def _key_to_value_rows(x):
    n, r, w = x.shape
    return pltpu.roll(x.reshape(n * r, w), N_HEADS, axis=0).reshape(n, r, w)


def _attend(x, xn, q, bias, bias_n):
    s = jnp.sum(x * q[None], axis=-1, keepdims=True) + bias
    sn = jnp.sum(xn * q[None], axis=-1, keepdims=True) + bias_n
    m = jnp.maximum(jnp.max(s, axis=0, keepdims=True), jnp.max(sn, axis=0, keepdims=True))
    p = jnp.exp(s - m)
    pn = jnp.exp(sn - m)
    l = jnp.sum(p, axis=0, keepdims=True) + jnp.sum(pn, axis=0, keepdims=True)
    acc = jnp.sum(_key_to_value_rows(p) * x, axis=0) + jnp.sum(_key_to_value_rows(pn) * xn, axis=0)
    return acc, m, l


def _attn_sample_kernel(rel_ref, q_ref, xn_ref, c1_ref, c2_ref, c3_ref, o_ref,
                        b1_ref, b1n_ref, b23_ref, b23n_ref):
    scale = GW ** -0.5
    rows = N_NEW * TILE_ROWS

    @pl.when(pl.program_id(0) == 0)
    def _():
        head = lax.broadcasted_iota(jnp.int32, (rows, GW), 0) % N_HEADS
        neg = jnp.full((rows, GW), NEG, F32)

        def tiles_for(gi):
            out = []
            for b in range(N_BUCKETS):
                v = [rel_ref[b, gi * N_HEADS + h] for h in range(N_HEADS)]
                out.append(jnp.where(head == 0, v[0], jnp.where(head == 1, v[1], jnp.where(head == 2, v[2], v[3]))))
            return out

        t1 = tiles_for(0)
        for t in range(N_NEW):
            for pos in range(NK):
                step = NK + t - pos
                tile = t1[int(_t5_bucket(np.int64(step)))] if step <= NK else neg
                b1_ref[t, pos] = tile[:TILE_ROWS]
            for j in range(N_NEW):
                tile = t1[int(_t5_bucket(np.int64(t - j)))] if j <= t else neg
                b1n_ref[t, j] = tile[:TILE_ROWS]
        for gi in (1, 2):
            dil = PATTERNS[gi][1]
            tg = tiles_for(gi)
            for jj in range(NK):
                b23_ref[gi - 1, jj] = tg[int(_t5_bucket(np.int64((NK - jj) * dil)))]
            b23n_ref[gi - 1] = tg[0]

    outs, lses = [], []
    x1 = c1_ref[0, 0]
    xn1 = xn_ref[0, 0].reshape(N_NEW, TILE_ROWS, GW)
    o1, e1 = [], []
    for t in range(N_NEW):
        q = q_ref[0, 0, t * TILE_ROWS:(t + 1) * TILE_ROWS, :] * scale
        acc, m, l = _attend(x1, xn1, q, b1_ref[t], b1n_ref[t])
        o1.append(acc)
        e1.append((m, l))
    outs.append(jnp.concatenate(o1, axis=0))
    lses.append((jnp.concatenate([m[0] for m, _ in e1], axis=0), jnp.concatenate([l[0] for _, l in e1], axis=0)))
    for gi, c_ref in ((1, c2_ref), (2, c3_ref)):
        acc, m, l = _attend(c_ref[0, 0], xn_ref[gi, 0][None], q_ref[gi, 0] * scale, b23_ref[gi - 1],
                            b23n_ref[gi - 1][None])
        outs.append(acc)
        lses.append((m[0], l[0]))

    lse = [_key_to_value_rows((m + jnp.log(l))[None])[0] for m, l in lses]
    den = [_key_to_value_rows(l[None])[0] for _, l in lses]
    top = jnp.maximum(jnp.maximum(lse[0], lse[1]), lse[2])
    w = [jnp.exp(e - top) for e in lse]
    num = w[0] * outs[0] / den[0] + w[1] * outs[1] / den[1] + w[2] * outs[2] / den[2]
    o_ref[0] = num / (w[0] + w[1] + w[2])


def attention_sample(p, cache_b1, cache_b2, cache_b3, rel_bias, layer, row0, nseq):
    depth = cache_b1.shape[0]
    rows = N_NEW * TILE_ROWS
    n_groups = len(PATTERNS)
    qkv = p[row0:, COL_B:COL_B + n_groups * 3 * BR_W].reshape(nseq, SAMPLE_ROWS, n_groups, 3, N_HEADS, GW)
    qkv = qkv[:, :N_NEW].transpose(2, 0, 1, 3, 4, 5)
    q = qkv[:, :, :, 0:1]
    q_all = jnp.concatenate([q, jnp.zeros_like(q)], axis=3).reshape(n_groups, nseq, rows, GW)
    xn_all = qkv[:, :, :, 1:3].reshape(n_groups, nseq, rows, GW)
    c1 = cache_b1.reshape(depth, nseq, NK, TILE_ROWS, GW)
    c2 = cache_b2.reshape(depth, nseq, NK, 4 * TILE_ROWS, GW)
    c3 = cache_b3.reshape(depth, nseq, NK, 16 * TILE_ROWS, GW)
    cache_spec = lambda r: pl.BlockSpec((1, 1, NK, r, GW), lambda b: (layer, b, 0, 0, 0))
    tok_spec = pl.BlockSpec((len(PATTERNS), 1, rows, GW), lambda b: (0, b, 0, 0))
    y = pl.pallas_call(
        _attn_sample_kernel,
        grid=(nseq,),
        in_specs=[pl.BlockSpec(memory_space=pltpu.SMEM), tok_spec, tok_spec,
                  cache_spec(TILE_ROWS), cache_spec(rows), cache_spec(rows)],
        out_specs=pl.BlockSpec((1, rows, GW), lambda b: (b, 0, 0)),
        out_shape=jax.ShapeDtypeStruct((nseq, rows, GW), F32),
        scratch_shapes=[pltpu.VMEM((N_NEW, NK, TILE_ROWS, GW), F32), pltpu.VMEM((N_NEW, N_NEW, TILE_ROWS, GW), F32),
                        pltpu.VMEM((2, NK, rows, GW), F32), pltpu.VMEM((2, rows, GW), F32)],
        compiler_params=_params(1),
        name="attention_sample",
    )(rel_bias, q_all, xn_all, c1, c2, c3)
    y = y.reshape(nseq, N_NEW, 2, BR_W)[:, :, 1]
    y = jnp.pad(y, ((0, 0), (0, SAMPLE_ROWS - N_NEW), (0, 0)))
    return y.reshape(nseq * SAMPLE_ROWS, BR_W).astype(BF16)


def _pool_conv_kernel(pc_ref, pbg_ref, pcg_ref, phs_ref, hc_ref, hd_ref, pw_ref, pb_ref, ps_ref, cw_ref,
                      yc_ref, yd_ref, zt_ref, cbuf_ref, zbuf_ref, *, start):
    tr = pc_ref.shape[0]
    j = pl.program_id(1)

    @pl.when(j == 0)
    def _():
        cbuf_ref[0:POOL_HALO, :] = hc_ref[0]
        zbuf_ref[0:CONV_HALO, :] = hd_ref[0]

    _pool_tile(pc_ref[...], start + j * tr, cbuf_ref, pw_ref, pb_ref, ps_ref, yc_ref)
    _conv_tile(pbg_ref[...], pcg_ref[...], phs_ref[...], zbuf_ref, cw_ref, yd_ref, zt_ref)


POOL_HALO = 16
CONV_HALO = 8


def _pool_tile(x, pos0, cbuf_ref, pw_ref, pb_ref, ps_ref, yc_ref):
    tr = x.shape[0]
    hc = POOL_HALO
    cbuf_ref[hc:hc + tr, :] = x
    pos = pos0 + lax.broadcasted_iota(jnp.int32, (tr, 1), 0)
    for gi, win in enumerate(POOL_WINDOWS):
        cols = slice(gi * GW, (gi + 1) * GW)
        total = x[:, cols]
        for back in range(1, win):
            total = total + cbuf_ref[hc - back:hc - back + tr, cols]
        cnt = jnp.minimum(pos + 1, win).astype(F32)
        pooled = total / cnt - x[:, cols]
        y = jnp.dot(pooled.astype(BF16), pw_ref[0, gi].astype(BF16), preferred_element_type=F32)
        yc_ref[:, cols] = ((y + pb_ref[:, cols]) * ps_ref[:, cols]).astype(yc_ref.dtype)
    cbuf_ref[0:hc, :] = cbuf_ref[tr:tr + hc, :]


def _conv_tile(bg, cg, hs, zbuf_ref, cw_ref, yd_ref, zt_ref):
    tr = bg.shape[0]
    hz = CONV_HALO
    z = cg * hs
    zbuf_ref[hz:hz + tr, :] = z
    y = (zbuf_ref[hz - 2:hz - 2 + tr, :] * cw_ref[0:1, :] + zbuf_ref[hz - 1:hz - 1 + tr, :] * cw_ref[1:2, :]
         + z * cw_ref[2:3, :])
    yd_ref[...] = (bg * y).astype(yd_ref.dtype)
    zt_ref[0] = z[tr - hz:, :]
    zbuf_ref[0:hz, :] = zbuf_ref[tr:tr + hz, :]


def pool_conv(p, hist_c, hist_d, pool_w, pool_b, pool_scale, conv_wt, layer, row0, nseq, t, tr, start):
    per = t // tr
    blk0 = row0 // tr
    col = lambda c: pl.BlockSpec((tr, BR_W), lambda b, j: (b * per + j, c))
    pcol = lambda c: pl.BlockSpec((tr, BR_W), lambda b, j: (blk0 + b * per + j, c))
    vec = pl.BlockSpec((1, BR_W), lambda b, j: (0, 0))
    return pl.pallas_call(
        functools.partial(_pool_conv_kernel, start=start),
        grid=(nseq, per),
        in_specs=[pcol(COL_C // BR_W), pcol(COL_D // BR_W), pcol(COL_D // BR_W + 1), pcol(COL_D // BR_W + 2),
                  pl.BlockSpec((1, 16, BR_W), lambda b, j: (b, 0, 0)),
                  pl.BlockSpec((1, 8, BR_W), lambda b, j: (b, 0, 0)),
                  pl.BlockSpec((1, 4, GW, GW), lambda b, j: (layer, 0, 0, 0)),
                  vec, vec,
                  pl.BlockSpec((CONV_W, BR_W), lambda b, j: (0, 0))],
        out_specs=[col(0), col(0), pl.BlockSpec((1, 8, BR_W), lambda b, j: (b, 0, 0))],
        out_shape=[jax.ShapeDtypeStruct((nseq * t, BR_W), BF16),
                   jax.ShapeDtypeStruct((nseq * t, BR_W), BF16),
                   jax.ShapeDtypeStruct((nseq, 8, BR_W), F32)],
        scratch_shapes=[pltpu.VMEM((16 + tr, BR_W), F32), pltpu.VMEM((8 + tr, BR_W), F32)],
        compiler_params=_params(2),
        name="pool_conv",
    )(p, p, p, p, hist_c, hist_d, pool_w, pool_b, pool_scale, conv_wt)


CACHE_ROWS = 1024


def _cache_writer_kernel(k1_ref, v1_ref, k2_ref, v2_ref, k3_ref, v3_ref, b1_any, b2_any, b3_any,
                         o1_ref, o2_ref, o3_ref):
    def scatter(o_ref, k_ref, v_ref, row0, n):
        for kv, src in enumerate((k_ref, v_ref)):
            for h in range(N_HEADS):
                o_ref[0, 0, pl.ds(kv * N_HEADS + h, n, stride=TILE_ROWS), :] = src[row0:row0 + n, h * GW:(h + 1) * GW]

    scatter(o3_ref, k3_ref, v3_ref, 0, CACHE_ROWS)

    @pl.when(pl.program_id(1) == pl.num_programs(1) - 1)
    def _():
        scatter(o2_ref, k2_ref, v2_ref, CACHE_ROWS - PATTERNS[1][0], PATTERNS[1][0])
        scatter(o1_ref, k1_ref, v1_ref, CACHE_ROWS - PATTERNS[0][0], PATTERNS[0][0])


def cache_writer(p, bufs, layer, nseq, t):
    b1, b2, b3 = bufs
    per = t // CACHE_ROWS
    steps = PATTERNS[2][0] // CACHE_ROWS

    def slab(gi, which, whole_window):
        col = (COL_B + gi * 3 * BR_W + (1 + which) * BR_W) // BR_W
        if whole_window:
            return pl.BlockSpec((CACHE_ROWS, BR_W), lambda b, j: ((b + 1) * per - steps + j, col))
        return pl.BlockSpec((CACHE_ROWS, BR_W), lambda b, j: ((b + 1) * per - 1, col))

    any_spec = pl.BlockSpec(memory_space=pl.ANY)
    tile = lambda rows: (1, 1, rows * TILE_ROWS, GW)
    return pl.pallas_call(
        _cache_writer_kernel,
        grid=(nseq, steps),
        in_specs=[slab(0, 0, False), slab(0, 1, False), slab(1, 0, False), slab(1, 1, False),
                  slab(2, 0, True), slab(2, 1, True),
                  any_spec, any_spec, any_spec],
        out_specs=[pl.BlockSpec(tile(PATTERNS[0][0]), lambda b, j: (layer, b, 0, 0)),
                   pl.BlockSpec(tile(PATTERNS[1][0]), lambda b, j: (layer, b, 0, 0)),
                   pl.BlockSpec(tile(CACHE_ROWS), lambda b, j: (layer, b, j, 0))],
        out_shape=[jax.ShapeDtypeStruct(x.shape, x.dtype) for x in bufs],
        input_output_aliases={6: 0, 7: 1, 8: 2},
        compiler_params=_params(2),
        name="cache_writer",
    )(p, p, p, p, p, p, b1, b2, b3)


TM_WIDE = 1024
TM_ROWS = 512


def kernel(x_prompt, x_sample, c_prompt, c_sample, cache_b1, cache_b2, cache_b3, cache_pool, cache_conv,
           ln1_g, ln2_g, w_ada, b_ada, w_in, w_gate, b_gate, a_ln_g, a_ln_b, a_ws, a_bs,
           rel_bias, pool_w, pool_b, pool_scale, conv_w, w_branch, w_out, w1, w3, w2, final_g):
    nb, seq, d = x_prompt.shape
    ns, dec = x_sample.shape[:2]
    depth = w_in.shape[0]
    mp = nb * seq
    ms = ns * SAMPLE_ROWS

    c_all = jnp.concatenate([c_prompt, c_sample], axis=0)
    c_rows = -(-c_all.shape[0] // 8) * 8
    c_all = jnp.pad(c_all, ((0, c_rows - c_all.shape[0]), (0, 0)))
    mod = ada_modulation(c_all, w_ada, b_ada).reshape(depth, c_rows, 6, d)
    mod_p = mod[:, :nb].transpose(0, 2, 1, 3).reshape(depth * 6 * nb, 1, d)
    mod_s = jnp.repeat(mod[:, nb:nb + ns], SAMPLE_ROWS, axis=1).transpose(0, 2, 1, 3).reshape(depth * 6, ms, d)
    mods = [[Mod(mod_p, mod_s, l * 6 + i, nb) for i in range(6)] for l in range(depth)]

    xp = x_prompt.reshape(mp, d)
    xs = jnp.pad(x_sample, ((0, 0), (0, SAMPLE_ROWS - dec), (0, 0))).reshape(ms, d)

    bias_mask = prompt_bias_mask(rel_bias)
    kv_bufs = tuple(jnp.zeros((depth, nb, win * TILE_ROWS, GW), F32) for win, _ in PATTERNS)
    pool_p, conv_p, proj_s, conv_s, chunk_s = [], [], [], [], []
    a_bst = a_bs.transpose(0, 2, 1)
    conv_wt = conv_w.transpose(0, 2, 1)
    hist_c_all = jnp.pad(cache_pool, ((0, 0), (0, 0), (1, 0), (0, 0)))
    hist_d_all = jnp.pad(cache_conv, ((0, 0), (0, 0), (8 - (CONV_W - 1), 0), (0, 0)))
    for l in range(depth):
        sh1, sc1, g1, sh2, sc2, g2 = mods[l]
        hn = norm_in(xp, xs, ln1_g[l][None], sc1, sh1, TM_WIDE)
        gm = (a_ln_g[l][None], a_ln_b[l][None], a_ws[l][None], a_bst[l][None])
        pc = (pool_w, pool_b[l][None], pool_scale[l][None], conv_wt[l], l)
        p, ya_p, yc_p, yd_p, zt_p = in_proj_mixers(hn, w_in, l, ms, TM_ROWS, nb, gm, pc)
        ya_s, va_s = gmlp(p, *gm, mp, ms, ms, SAMPLE_ROWS)
        yb_p = attention_prompt(p, bias_mask, nb, seq)
        yb_s = attention_sample(p, cache_b1, cache_b2, cache_b3, rel_bias, l, mp, ns)
        yc_s, yd_s, zt_s = pool_conv(p, hist_c_all[l], hist_d_all[l], *pc, mp, ns, SAMPLE_ROWS, SAMPLE_ROWS,
                                     PAST_LEN)

        merged = gate_merge(hn, (ya_p, yb_p, yc_p, yd_p), (ya_s, yb_s, yc_s, yd_s),
                            w_gate, b_gate, w_branch, l, TM_WIDE, 256)
        xp, xs, hn2 = out_proj_norm(merged, w_out, l, xp, xs, g1, ln2_g[l][None], sc2, sh2, TM_ROWS)
        hmid = swiglu_up(hn2, w1, w3, l, ms, TM_WIDE, 512)
        xp, xs = matmul_residual(hmid, w2, l, xp, xs, g2, TM_ROWS, 512)

        kv_bufs = cache_writer(p, kv_bufs, l, nb, seq)
        pool_p.append(jnp.stack([p[(b + 1) * seq - POOL_HIST:(b + 1) * seq, COL_C:COL_C + BR_W]
                                 for b in range(nb)]))
        conv_p.append(zt_p[:nb])
        proj_s.append(p[mp:])
        conv_s.append(zt_s)
        chunk_s.append(va_s)

    yp, ys = final_norm(xp, xs, final_g[None], TM_WIDE)
    stack = lambda parts: jnp.stack(parts, axis=0)
    kv_p = [buf.reshape(depth, nb, win, 2, N_HEADS, GW) for buf, (win, _) in zip(kv_bufs, PATTERNS)]
    ps = stack(proj_s).reshape(depth, ns, SAMPLE_ROWS, N_IN)[:, :, :dec]
    kv_s = [ps[..., COL_B + (3 * gi + 1) * BR_W:COL_B + (3 * gi + 3) * BR_W].reshape(depth, ns, dec, 2, N_HEADS, GW)
            for gi in range(len(PATTERNS))]
    return (yp.reshape(nb, seq, d), ys.reshape(ns, SAMPLE_ROWS, d)[:, :dec],
            kv_p[0], kv_p[1], kv_p[2], stack(pool_p), stack(conv_p)[:, :, 8 - (CONV_W - 1):],
            kv_s[0], kv_s[1], kv_s[2], ps[..., COL_C:COL_C + BR_W], stack(conv_s)[:, :, :dec],
            stack(chunk_s).reshape(depth, ns, SAMPLE_ROWS, BR_W)[:, :, :dec])
```

```python
import functools
import math
from typing import NamedTuple

import jax
import jax.numpy as jnp
import numpy as np
from jax import lax
from jax.experimental import pallas as pl
from jax.experimental.pallas import tpu as pltpu

F32 = jnp.float32
BF16 = jnp.bfloat16

D_MODEL = 2048
BR_W = 512
GW = 128
N_HEADS = 4
CHUNK = 128
PATTERNS = ((128, 1), (512, 4), (2048, 16))
NK = 128
POOL_WINDOWS = (2, 4, 8, 16)
POOL_HIST = 15
CONV_W = 3
N_BUCKETS = 32
MAX_DIST = 2048
D_FF = 5632
N_IN = 7680
COL_A = 0
COL_B = 2 * BR_W
COL_C = COL_B + 9 * BR_W
COL_D = COL_C + BR_W
EPS = 1e-6
NEG = -1e30
SAMPLE_ROWS = 8
PAST_LEN = 16384
ATTN_BLOCKS_PER_STEP = 4

VMEM_LIMIT = 56 * 1024 * 1024
VMEM_LIMIT_WIDE = 60 * 1024 * 1024


def _t5_bucket(dist):
    max_exact = N_BUCKETS // 2
    n = np.maximum(dist, 0)
    nf = np.maximum(n, 1).astype(np.float32)
    large = max_exact + (np.log(nf / np.float32(max_exact)) / np.float32(math.log(MAX_DIST / max_exact))
                         * np.float32(N_BUCKETS - max_exact)).astype(np.int32)
    large = np.minimum(large, N_BUCKETS - 1)
    return np.where(n < max_exact, n, large)


def _params(n_axes, vmem_limit=VMEM_LIMIT):
    return pltpu.CompilerParams(dimension_semantics=("arbitrary",) * n_axes,
                                vmem_limit_bytes=vmem_limit)


def _ada_kernel(c_ref, w_ref, b_ref, o_ref):
    a = jax.nn.silu(c_ref[...]).astype(BF16)
    o_ref[0] = jnp.dot(a, w_ref[0].astype(BF16), preferred_element_type=F32) + b_ref[0]


def ada_modulation(c_all, w_ada, b_ada):
    depth, d, n = w_ada.shape
    r = c_all.shape[0]
    tn = 1024
    return pl.pallas_call(
        _ada_kernel,
        grid=(depth, n // tn),
        in_specs=[pl.BlockSpec((r, d), lambda l, j: (0, 0)),
                  pl.BlockSpec((1, d, tn), lambda l, j: (l, 0, j)),
                  pl.BlockSpec((1, 1, tn), lambda l, j: (l, 0, j))],
        out_specs=pl.BlockSpec((1, r, tn), lambda l, j: (l, 0, j)),
        out_shape=jax.ShapeDtypeStruct((depth, r, n), F32),
        compiler_params=_params(2),
        name="ada_modulation",
    )(c_all, w_ada, b_ada.reshape(depth, 1, n))


def _split_tiles(axis, body):
    i = pl.program_id(axis)
    pl.when(i == 0)(functools.partial(body, True))
    pl.when(i > 0)(functools.partial(body, False))


def _joint_tile(i, nf):
    return (i + nf) % (nf + 1)


def _prompt_tile(i):
    return jnp.maximum(i - 1, 0)


def _norm_rows(x, g, sc, sh):
    y = x * lax.rsqrt(jnp.mean(x * x, axis=-1, keepdims=True) + EPS) * g
    return y * (1.0 + sc) + sh


class Mod(NamedTuple):
    prompt: jax.Array
    sample: jax.Array
    index: int
    nb: int


def _mod_specs(mod, width, per, grid_rank):
    ms = mod.sample.shape[1]
    base = mod.index * mod.nb
    if grid_rank == 1:
        return [pl.BlockSpec((1, 1, width), lambda i: (base + _prompt_tile(i) // per, 0, 0)),
                pl.BlockSpec((1, ms, width), lambda i: (mod.index, 0, 0))]
    return [pl.BlockSpec((1, 1, width), lambda j, i: (base + _prompt_tile(i) // per, 0, j)),
            pl.BlockSpec((1, ms, width), lambda j, i: (mod.index, 0, j))]


def _final_norm_kernel(xp_ref, xs_ref, g_ref, yp_ref, ys_ref):
    def body(tail):
        src, dst = (xs_ref, ys_ref) if tail else (xp_ref, yp_ref)
        x = src[...]
        dst[...] = x * lax.rsqrt(jnp.mean(x * x, axis=-1, keepdims=True) + EPS) * g_ref[...]

    _split_tiles(0, body)


def final_norm(xp, xs, g, tm):
    mp, d = xp.shape
    ms = xs.shape[0]
    nf = mp // tm
    p_spec = pl.BlockSpec((tm, d), lambda i: (_prompt_tile(i), 0))
    s_spec = pl.BlockSpec((ms, d), lambda i: (0, 0))
    return pl.pallas_call(
        _final_norm_kernel,
        grid=(nf + 1,),
        in_specs=[p_spec, s_spec, pl.BlockSpec((1, d), lambda i: (0, 0))],
        out_specs=[p_spec, s_spec],
        out_shape=[jax.ShapeDtypeStruct((mp, d), F32), jax.ShapeDtypeStruct((ms, d), F32)],
        compiler_params=_params(1),
        name="final_norm",
    )(xp, xs, g)


IN_TN = 1536


def _in_proj_kernel(xp_ref, xs_ref, g_ref, scp_ref, scs_ref, shp_ref, shs_ref, w_ref,
                    lg_ref, lb_ref, ws_ref, bst_ref, pw_ref, pb_ref, ps_ref, cw_ref,
                    o_ref, hn_ref, ya_ref, yc_ref, yd_ref, zt_ref, wbf_ref, cbuf_ref, zbuf_ref, *, ms, per_seq):
    j = pl.program_id(0)
    i = pl.program_id(1)
    tm = xp_ref.shape[0]
    tile = i - 1
    first = (tile % per_seq) == 0

    def project(keep_hn=False):
        hn = _norm_rows(xp_ref[...], g_ref[...], scp_ref[0], shp_ref[0]).astype(BF16)
        if keep_hn:
            hn_ref[...] = hn
        acc = jnp.dot(hn, wbf_ref[...], preferred_element_type=F32)
        o_ref[...] = acc
        return acc

    @pl.when(i == 0)
    def _():
        wbf_ref[...] = w_ref[0].astype(BF16)
        hn = _norm_rows(xs_ref[...], g_ref[...], scs_ref[0], shs_ref[0]).astype(BF16)
        o_ref[0:ms, :] = jnp.dot(hn, wbf_ref[...], preferred_element_type=F32)
        hn_ref[...] = jnp.zeros(hn_ref.shape, hn_ref.dtype)

        @pl.when(j == 0)
        def _():
            hn_ref[0:ms, :] = hn

        for ref in (ya_ref, yc_ref, yd_ref, zt_ref):
            ref[...] = jnp.zeros(ref.shape, ref.dtype)

    @pl.when((i > 0) & first & (j == COL_C // IN_TN))
    def _():
        cbuf_ref[0:POOL_HALO, :] = jnp.zeros((POOL_HALO, BR_W), F32)

    @pl.when((i > 0) & first & (j == COL_D // IN_TN))
    def _():
        zbuf_ref[0:CONV_HALO, :] = jnp.zeros((CONV_HALO, BR_W), F32)

    @pl.when((i > 0) & (j == COL_A // IN_TN))
    def _():
        acc = project(keep_hn=COL_A // IN_TN == 0)
        _gmlp_tile(acc[:, 0:BR_W], acc[:, BR_W:2 * BR_W], lg_ref, lb_ref, ws_ref, bst_ref, ya_ref, CHUNK)

    @pl.when((i > 0) & (j == COL_C // IN_TN))
    def _():
        acc = project()
        c0 = COL_C % IN_TN
        _pool_tile(acc[:, c0:c0 + BR_W], (tile % per_seq) * tm, cbuf_ref, pw_ref, pb_ref, ps_ref, yc_ref)

    @pl.when((i > 0) & (j == COL_D // IN_TN))
    def _():
        acc = project()
        _conv_tile(acc[:, 0:BR_W], acc[:, BR_W:2 * BR_W], acc[:, 2 * BR_W:3 * BR_W], zbuf_ref, cw_ref, yd_ref, zt_ref)

    @pl.when((i > 0) & (j != COL_A // IN_TN) & (j != COL_C // IN_TN) & (j != COL_D // IN_TN))
    def _():
        project()


def in_proj_mixers(xp, xs, ln_g, sc, sh, w, layer, tm, nseq, gm, pc):
    mp, k = xp.shape
    ms = xs.shape[0]
    m = mp + ms
    n = w.shape[2]
    nf = mp // tm
    per_seq = nf // nseq
    mod_p = lambda mod: pl.BlockSpec((1, 1, k), lambda j, i: (mod.index * mod.nb + _prompt_tile(i) // per_seq, 0, 0))
    mod_s = lambda mod: pl.BlockSpec((1, ms, k), lambda j, i: (mod.index, 0, 0))
    a_ln_g, a_ln_b, ws, bst = gm
    pool_w, pool_b, pool_scale, conv_wt, _ = pc
    ga, gc, gd = COL_A // IN_TN, COL_C // IN_TN, COL_D // IN_TN

    def parked(j, i, tile_j, live, n_live):
        before = (j < tile_j) | ((j == tile_j) & (i == 0))
        return jnp.where((j == tile_j) & (i > 0), live, jnp.where(before, n_live, n_live + 1))

    def n_parked(tile_j):
        return 1 if tile_j == n // IN_TN - 1 else 2

    def mixer_spec(tile_j):
        return pl.BlockSpec((tm, BR_W), lambda j, i: (parked(j, i, tile_j, i - 1, nf), 0))

    vec = pl.BlockSpec((1, BR_W), lambda j, i: (0, 0))
    return pl.pallas_call(
        functools.partial(_in_proj_kernel, ms=ms, per_seq=per_seq),
        grid=(n // IN_TN, nf + 1),
        in_specs=[pl.BlockSpec((tm, k), lambda j, i: (_prompt_tile(i), 0)),
                  pl.BlockSpec((ms, k), lambda j, i: (0, 0)),
                  pl.BlockSpec((1, k), lambda j, i: (0, 0)),
                  mod_p(sc), mod_s(sc), mod_p(sh), mod_s(sh),
                  pl.BlockSpec((1, k, IN_TN), lambda j, i: (layer, 0, j)),
                  vec, vec,
                  pl.BlockSpec((1, 4, CHUNK, CHUNK), lambda j, i: (0, 0, 0, 0)),
                  pl.BlockSpec((1, CHUNK, 4), lambda j, i: (0, 0, 0)),
                  pl.BlockSpec((1, 4, GW, GW), lambda j, i: (layer, 0, 0, 0)),
                  vec, vec,
                  pl.BlockSpec((CONV_W, BR_W), lambda j, i: (0, 0))],
        out_specs=[pl.BlockSpec((tm, IN_TN), lambda j, i: (_joint_tile(i, nf), j)),
                   pl.BlockSpec((tm, k), lambda j, i: (jnp.where(j == 0, _joint_tile(i, nf), nf + 1), 0)),
                   mixer_spec(ga), mixer_spec(gc), mixer_spec(gd),
                   pl.BlockSpec((1, CONV_HALO, BR_W),
                                lambda j, i: (parked(j, i, gd, (i - 1) // per_seq, nseq), 0, 0))],
        out_shape=[jax.ShapeDtypeStruct((m, n), F32), jax.ShapeDtypeStruct(((nf + 2) * tm, k), BF16)]
        + [jax.ShapeDtypeStruct(((nf + n_parked(t)) * tm, BR_W), BF16) for t in (ga, gc, gd)]
        + [jax.ShapeDtypeStruct((nseq + n_parked(gd), CONV_HALO, BR_W), F32)],
        scratch_shapes=[pltpu.VMEM((k, IN_TN), BF16),
                        pltpu.VMEM((POOL_HALO + tm, BR_W), F32), pltpu.VMEM((CONV_HALO + tm, BR_W), F32)],
        compiler_params=_params(2, VMEM_LIMIT_WIDE),
        name="in_proj_mixers",
    )(xp, xs, ln_g, sc.prompt, sc.sample, sh.prompt, sh.sample, w,
      a_ln_g, a_ln_b, ws, bst, pool_w, pool_b, pool_scale, conv_wt)


def _mm_residual_kernel(a_ref, w_ref, xp_ref, xs_ref, gp_ref, gs_ref, op_ref, os_ref, wbf_ref):
    ms = xs_ref.shape[0]

    @pl.when(pl.program_id(1) == 0)
    def _():
        wbf_ref[...] = w_ref[0].astype(BF16)

    def body(tail):
        if tail:
            y = jnp.dot(a_ref[0:ms, :], wbf_ref[...], preferred_element_type=F32)
            os_ref[...] = xs_ref[...] + gs_ref[0] * y
        else:
            y = jnp.dot(a_ref[...], wbf_ref[...], preferred_element_type=F32)
            op_ref[...] = xp_ref[...] + gp_ref[0] * y

    _split_tiles(1, body)


def matmul_residual(a, w, layer, xp, xs, gate, tm, tn):
    m, k = a.shape
    n = w.shape[2]
    mp, ms = xp.shape[0], xs.shape[0]
    nf = mp // tm
    per = nf // gate.nb
    p_spec = pl.BlockSpec((tm, tn), lambda j, i: (_prompt_tile(i), j))
    s_spec = pl.BlockSpec((ms, tn), lambda j, i: (0, j))
    return pl.pallas_call(
        _mm_residual_kernel,
        grid=(n // tn, nf + 1),
        in_specs=[pl.BlockSpec((tm, k), lambda j, i: (_joint_tile(i, nf), 0)),
                  pl.BlockSpec((1, k, tn), lambda j, i: (layer, 0, j)),
                  p_spec, s_spec, *_mod_specs(gate, tn, per, 2)],
        out_specs=[p_spec, s_spec],
        out_shape=[jax.ShapeDtypeStruct((mp, n), F32), jax.ShapeDtypeStruct((ms, n), F32)],
        scratch_shapes=[pltpu.VMEM((k, tn), BF16)],
        compiler_params=_params(2),
        name="matmul_residual",
    )(a, w, xp, xs, gate.prompt, gate.sample)


def _out_proj_norm_kernel(a_ref, w_ref, xp_ref, xs_ref, gp_ref, gs_ref, ln_ref, scp_ref, scs_ref,
                          shp_ref, shs_ref, op_ref, os_ref, hn_ref, wbf_ref):
    ms = xs_ref.shape[0]

    @pl.when(pl.program_id(0) == 0)
    def _():
        wbf_ref[...] = w_ref[0].astype(BF16)

    def body(tail):
        if tail:
            x = xs_ref[...] + gs_ref[0] * jnp.dot(a_ref[0:ms, :], wbf_ref[...], preferred_element_type=F32)
            os_ref[...] = x
            hn_ref[0:ms, :] = _norm_rows(x, ln_ref[...], scs_ref[0], shs_ref[0]).astype(hn_ref.dtype)
        else:
            x = xp_ref[...] + gp_ref[0] * jnp.dot(a_ref[...], wbf_ref[...], preferred_element_type=F32)
            op_ref[...] = x
            hn_ref[...] = _norm_rows(x, ln_ref[...], scp_ref[0], shp_ref[0]).astype(hn_ref.dtype)

    _split_tiles(0, body)


def out_proj_norm(a, w, layer, xp, xs, gate, ln_g, sc, sh, tm):
    m, k = a.shape
    n = w.shape[2]
    mp, ms = xp.shape[0], xs.shape[0]
    nf = mp // tm
    per = nf // gate.nb
    p_spec = pl.BlockSpec((tm, n), lambda i: (_prompt_tile(i), 0))
    s_spec = pl.BlockSpec((ms, n), lambda i: (0, 0))
    return pl.pallas_call(
        _out_proj_norm_kernel,
        grid=(nf + 1,),
        in_specs=[pl.BlockSpec((tm, k), lambda i: (_joint_tile(i, nf), 0)),
                  pl.BlockSpec((1, k, n), lambda i: (layer, 0, 0), pipeline_mode=pl.Buffered(1)),
                  p_spec, s_spec, *_mod_specs(gate, n, per, 1),
                  pl.BlockSpec((1, n), lambda i: (0, 0)),
                  *_mod_specs(sc, n, per, 1), *_mod_specs(sh, n, per, 1)],
        out_specs=[p_spec, s_spec, pl.BlockSpec((tm, n), lambda i: (_joint_tile(i, nf), 0))],
        out_shape=[jax.ShapeDtypeStruct((mp, n), F32), jax.ShapeDtypeStruct((ms, n), F32),
                   jax.ShapeDtypeStruct((m, n), BF16)],
        scratch_shapes=[pltpu.VMEM((k, n), BF16)],
        compiler_params=_params(1),
        name="out_proj_norm",
    )(a, w, xp, xs, gate.prompt, gate.sample, ln_g, sc.prompt, sc.sample, sh.prompt, sh.sample)


def _swiglu_kernel(a_ref, w1_ref, w3_ref, o_ref, w1bf_ref, w3bf_ref, *, ms):
    @pl.when(pl.program_id(1) == 0)
    def _():
        w1bf_ref[...] = w1_ref[0].astype(BF16)
        w3bf_ref[...] = w3_ref[0].astype(BF16)

    def body(tail):
        rows = slice(0, ms) if tail else slice(None)
        a = a_ref[rows, :]
        h1 = jnp.dot(a, w1bf_ref[...], preferred_element_type=F32)
        h3 = jnp.dot(a, w3bf_ref[...], preferred_element_type=F32)
        o_ref[rows, :] = (jax.nn.silu(h1) * h3).astype(o_ref.dtype)

    _split_tiles(1, body)


def swiglu_up(a, w1, w3, layer, ms, tm, tn):
    m, k = a.shape
    n = w1.shape[2]
    nf = (m - ms) // tm
    w_spec = pl.BlockSpec((1, k, tn), lambda j, i: (layer, 0, j))
    return pl.pallas_call(
        functools.partial(_swiglu_kernel, ms=ms),
        grid=(n // tn, nf + 1),
        in_specs=[pl.BlockSpec((tm, k), lambda j, i: (_joint_tile(i, nf), 0)), w_spec, w_spec],
        out_specs=pl.BlockSpec((tm, tn), lambda j, i: (_joint_tile(i, nf), j)),
        out_shape=jax.ShapeDtypeStruct((m, n), BF16),
        scratch_shapes=[pltpu.VMEM((k, tn), BF16), pltpu.VMEM((k, tn), BF16)],
        compiler_params=_params(2),
        name="swiglu_up",
    )(a, w1, w3)


def _gate_merge_kernel(hn_ref, yap_ref, ybp_ref, ycp_ref, ydp_ref, yas_ref, ybs_ref, ycs_ref, yds_ref,
                       wg0_ref, wg1_ref, wg2_ref, wg3_ref, wb_ref, bg_ref, o_ref, wgbf_ref, wbbf_ref):
    ms = yas_ref.shape[0]

    @pl.when(pl.program_id(1) == 0)
    def _():
        for g, wg_ref in enumerate((wg0_ref, wg1_ref, wg2_ref, wg3_ref)):
            wgbf_ref[g] = wg_ref[0].astype(BF16)
            wbbf_ref[g] = wb_ref[0, g].astype(BF16)

    def body(tail):
        rows = slice(0, ms) if tail else slice(None)
        y_refs = (yas_ref, ybs_ref, ycs_ref, yds_ref) if tail else (yap_ref, ybp_ref, ycp_ref, ydp_ref)
        hn = hn_ref[rows, :]
        acc = None
        for g, y_ref in enumerate(y_refs):
            gate = jax.nn.sigmoid(jnp.dot(hn, wgbf_ref[g], preferred_element_type=F32) + bg_ref[0, g:g + 1, :])
            br = jnp.dot(y_ref[...], wbbf_ref[g], preferred_element_type=F32)
            acc = gate * br if acc is None else acc + gate * br
        o_ref[rows, :] = acc.astype(o_ref.dtype)

    _split_tiles(1, body)


def gate_merge(hn, m, ys_p, ys_s, w_gate, b_gate, w_branch, layer, tm, tn):
    k = hn.shape[1]
    ms = ys_s[0].shape[0]
    nf = (m - ms) // tm
    depth = w_gate.shape[0]
    d = w_branch.shape[3]
    nj = d // tn
    wg_specs = [pl.BlockSpec((1, k, tn), functools.partial(lambda j, i, g: (layer, 0, g * nj + j), g=g))
                for g in range(4)]
    yp_spec = pl.BlockSpec((tm, BR_W), lambda j, i: (_prompt_tile(i), 0))
    ys_spec = pl.BlockSpec((ms, BR_W), lambda j, i: (0, 0))
    return pl.pallas_call(
        _gate_merge_kernel,
        grid=(nj, nf + 1),
        in_specs=[pl.BlockSpec((tm, k), lambda j, i: (_joint_tile(i, nf), 0)), *[yp_spec] * 4, *[ys_spec] * 4,
                  *wg_specs,
                  pl.BlockSpec((1, 4, BR_W, tn), lambda j, i: (layer, 0, 0, j)),
                  pl.BlockSpec((1, 4, tn), lambda j, i: (layer, 0, j))],
        out_specs=pl.BlockSpec((tm, tn), lambda j, i: (_joint_tile(i, nf), j)),
        out_shape=jax.ShapeDtypeStruct((m, d), BF16),
        scratch_shapes=[pltpu.VMEM((4, k, tn), BF16), pltpu.VMEM((4, BR_W, tn), BF16)],
        compiler_params=_params(2),
        name="gate_merge",
    )(hn, *ys_p, *ys_s, w_gate, w_gate, w_gate, w_gate, w_branch, b_gate.reshape(depth, 4, d))


def _gmlp_kernel(pu_ref, pv_ref, lg_ref, lb_ref, ws_ref, bst_ref, ya_ref, va_ref, *, chunk):
    va_ref[...] = _gmlp_tile(pu_ref[...], pv_ref[...], lg_ref, lb_ref, ws_ref, bst_ref, ya_ref, chunk)


def _gmlp_tile(pu, pv, lg_ref, lb_ref, ws_ref, bst_ref, ya_ref, chunk):
    rows = pu.shape[0]
    u = jax.nn.gelu(pu)
    v = jax.nn.gelu(pv)
    mu = jnp.mean(v, axis=-1, keepdims=True)
    var = jnp.mean(jnp.square(v - mu), axis=-1, keepdims=True)
    vn = (v - mu) * lax.rsqrt(var + 1e-5) * lg_ref[...] + lb_ref[...]
    causal = (lax.broadcasted_iota(jnp.int32, (CHUNK, CHUNK), 0)
              >= lax.broadcasted_iota(jnp.int32, (CHUNK, CHUNK), 1))
    for g in range(4):
        wm = jnp.where(causal, ws_ref[0, g], 0.0).astype(BF16)
        bias = bst_ref[0, :, g:g + 1]
        for c in range(rows // chunk):
            r0 = c * chunk
            vc = vn[r0:r0 + chunk, g * GW:(g + 1) * GW]
            if chunk < CHUNK:
                vc = jnp.concatenate([vc, jnp.zeros((CHUNK - chunk, GW), F32)], axis=0)
            mix = (jnp.dot(wm, vc.astype(BF16), preferred_element_type=F32) + bias)[:chunk]
            ya_ref[r0:r0 + chunk, g * GW:(g + 1) * GW] = (
                u[r0:r0 + chunk, g * GW:(g + 1) * GW] * mix).astype(ya_ref.dtype)
    return vn


def gmlp(p, ln_g, ln_b, ws, bst, row0, m, tr, chunk):
    blk0 = row0 // tr
    in_spec = lambda col: pl.BlockSpec((tr, BR_W), lambda i: (blk0 + i, col))
    out_spec = pl.BlockSpec((tr, BR_W), lambda i: (i, 0))
    return pl.pallas_call(
        functools.partial(_gmlp_kernel, chunk=chunk),
        grid=(m // tr,),
        in_specs=[in_spec(0), in_spec(1),
                  pl.BlockSpec((1, BR_W), lambda i: (0, 0)),
                  pl.BlockSpec((1, BR_W), lambda i: (0, 0)),
                  pl.BlockSpec((1, 4, CHUNK, CHUNK), lambda i: (0, 0, 0, 0)),
                  pl.BlockSpec((1, CHUNK, 4), lambda i: (0, 0, 0))],
        out_specs=[out_spec, out_spec],
        out_shape=[jax.ShapeDtypeStruct((m, BR_W), BF16), jax.ShapeDtypeStruct((m, BR_W), F32)],
        compiler_params=_params(1),
        name="gmlp",
    )(p, p, ln_g, ln_b, ws, bst)


def _bias_from_buckets(idx, rel_ref, col, shape):
    bias = jnp.full(shape, NEG, F32)
    for b in range(N_BUCKETS):
        bias = jnp.where(idx == b, rel_ref[b, col], bias)
    return bias


def _bias_mask_kernel(rel_ref, idx_ref, o_ref):
    o_ref[0, 0] = _bias_from_buckets(idx_ref[0], rel_ref, pl.program_id(0) * N_HEADS + pl.program_id(1),
                                     (NK, 2 * NK))


def prompt_bias_mask(rel_bias):
    return pl.pallas_call(
        _bias_mask_kernel,
        grid=(len(PATTERNS), N_HEADS),
        in_specs=[pl.BlockSpec(memory_space=pltpu.SMEM),
                  pl.BlockSpec((1, NK, 2 * NK), lambda g, h: (g, 0, 0))],
        out_specs=pl.BlockSpec((1, 1, NK, 2 * NK), lambda g, h: (g, h, 0, 0)),
        out_shape=jax.ShapeDtypeStruct((len(PATTERNS), N_HEADS, NK, 2 * NK), F32),
        compiler_params=_params(2),
        name="prompt_bias_mask",
    )(rel_bias, jnp.asarray(_prompt_buckets()))


def _attn_prompt_kernel(q_ref, k_ref, v_ref, bm_ref, o_ref,
                        o0_ref, o1_ref, o2_ref, e0_ref, e1_ref, e2_ref):
    t = q_ref.shape[0]
    group = pl.program_id(2)
    scale = GW ** -0.5
    outs = (o0_ref, o1_ref, o2_ref)
    lses = (e0_ref, e1_ref, e2_ref)

    def rows(start, n, dil):
        return pl.ds(start, n) if dil == 1 else pl.ds(start, n, stride=dil)

    def blocks(gi, specs):
        dil = PATTERNS[gi][1]
        staged = []
        for first_block, q0, k0 in specs:
            nkeys = NK if first_block else 2 * NK
            qi = rows(q0, NK, dil)
            ki = rows(k0, nkeys, dil)
            q = q_ref[qi, :].astype(BF16)
            kk = k_ref[ki, :].astype(BF16)
            bm = bm_ref[0, 0, :, NK:] if first_block else bm_ref[0, 0]
            s = lax.dot_general(q, kk, (((1,), (1,)), ((), ())), preferred_element_type=F32) * scale + bm
            staged.append((qi, ki, s))
        probs = []
        for qi, ki, s in staged:
            m = jnp.max(s, axis=-1, keepdims=True)
            p = jnp.exp(s - m)
            l = jnp.sum(p, axis=-1, keepdims=True)
            probs.append((qi, ki, p.astype(BF16), m, l))
        for qi, ki, p, m, l in probs:
            acc = jnp.dot(p, v_ref[ki, :].astype(BF16), preferred_element_type=F32)
            outs[gi][qi, :] = acc * (1.0 / l)
            lses[gi][qi, :] = jnp.broadcast_to(m + jnp.log(l), (NK, GW))

    def run_group(gi):
        dil = PATTERNS[gi][1]
        span = NK * dil
        nb = t // span
        u = ATTN_BLOCKS_PER_STEP
        if dil == 1:
            blocks(gi, [(True, 0, 0)] + [(False, n * span, (n - 1) * span) for n in range(1, u)])

            def step(i, c):
                q0 = pl.multiple_of(u * i * span, span)
                blocks(gi, [(False, q0 + n * span, q0 + (n - 1) * span) for n in range(u)])
                return c

            lax.fori_loop(1, nb // u, step, 0)
        else:
            def bunch(i, c):
                res = [u * i + r for r in range(u)]
                blocks(gi, [(True, r, r) for r in res])

                def later(n, cc):
                    blocks(gi, [(False, n * span + r, (n - 1) * span + r) for r in res])
                    return cc

                return lax.fori_loop(1, nb, later, c)

            if dil == u:
                bunch(0, 0)
            else:
                lax.fori_loop(0, dil // u, bunch, 0)

    for gi in range(len(PATTERNS)):
        pl.when(group == gi)(functools.partial(run_group, gi))

    @pl.when(group == len(PATTERNS) - 1)
    def _():
        step = 256

        def merge(c, carry):
            sl = pl.ds(pl.multiple_of(c * step, step), step)
            e = [r[sl, :] for r in lses]
            top = jnp.maximum(jnp.maximum(e[0], e[1]), e[2])
            w = [jnp.exp(x - top) for x in e]
            num = w[0] * o0_ref[sl, :] + w[1] * o1_ref[sl, :] + w[2] * o2_ref[sl, :]
            o_ref[sl, :] = (num / (w[0] + w[1] + w[2])).astype(o_ref.dtype)
            return carry

        lax.fori_loop(0, t // step, merge, 0)


def _prompt_buckets():
    i = np.arange(NK)[:, None]
    j = np.arange(2 * NK)[None, :]
    diff = NK + i - j
    valid = (diff >= 0) & (diff <= NK)
    return np.stack([np.where(valid, _t5_bucket(diff * dil), -1) for _, dil in PATTERNS]).astype(np.int32)


def attention_prompt(p, bias_mask, nseq, t):
    def qkv_spec(which):
        return pl.BlockSpec((t, GW), lambda b, h, g: (b, COL_B // GW + g * 12 + which * 4 + h))

    return pl.pallas_call(
        _attn_prompt_kernel,
        grid=(nseq, N_HEADS, len(PATTERNS)),
        in_specs=[qkv_spec(0), qkv_spec(1), qkv_spec(2),
                  pl.BlockSpec((1, 1, NK, 2 * NK), lambda b, h, g: (g, h, 0, 0))],
        out_specs=pl.BlockSpec((t, GW), lambda b, h, g: (b, h)),
        out_shape=jax.ShapeDtypeStruct((nseq * t, BR_W), BF16),
        scratch_shapes=[pltpu.VMEM((t, GW), F32)] * 6,
        compiler_params=_params(3),
        name="attention_prompt",
    )(p, p, p, bias_mask)


N_NEW = 4
TILE_ROWS = 2 * N_HEADS


def _key_to_value_rows(x):
    n, r, w = x.shape
    return pltpu.roll(x.reshape(n * r, w), N_HEADS, axis=0).reshape(n, r, w)


def _attend(x, xn, q, bias, bias_n):
    s = jnp.sum(x * q[None], axis=-1, keepdims=True) + bias
    sn = jnp.sum(xn * q[None], axis=-1, keepdims=True) + bias_n
    m = jnp.maximum(jnp.max(s, axis=0, keepdims=True), jnp.max(sn, axis=0, keepdims=True))
    p = jnp.exp(s - m)
    pn = jnp.exp(sn - m)
    l = jnp.sum(p, axis=0, keepdims=True) + jnp.sum(pn, axis=0, keepdims=True)
    acc = jnp.sum(_key_to_value_rows(p) * x, axis=0) + jnp.sum(_key_to_value_rows(pn) * xn, axis=0)
    return acc, m, l


def _attn_sample_kernel(rel_ref, q_ref, xn_ref, c1_ref, c2_ref, c3_ref, o_ref,
                        b1_ref, b1n_ref, b23_ref, b23n_ref):
    scale = GW ** -0.5
    rows = N_NEW * TILE_ROWS

    @pl.when(pl.program_id(0) == 0)
    def _():
        head = lax.broadcasted_iota(jnp.int32, (rows, GW), 0) % N_HEADS
        neg = jnp.full((rows, GW), NEG, F32)

        def tiles_for(gi):
            out = []
            for b in range(N_BUCKETS):
                v = [rel_ref[b, gi * N_HEADS + h] for h in range(N_HEADS)]
                out.append(jnp.where(head == 0, v[0], jnp.where(head == 1, v[1], jnp.where(head == 2, v[2], v[3]))))
            return out

        t1 = tiles_for(0)
        for t in range(N_NEW):
            for pos in range(NK):
                step = NK + t - pos
                tile = t1[int(_t5_bucket(np.int64(step)))] if step <= NK else neg
                b1_ref[t, pos] = tile[:TILE_ROWS]
            for j in range(N_NEW):
                tile = t1[int(_t5_bucket(np.int64(t - j)))] if j <= t else neg
                b1n_ref[t, j] = tile[:TILE_ROWS]
        for gi in (1, 2):
            dil = PATTERNS[gi][1]
            tg = tiles_for(gi)
            for jj in range(NK):
                b23_ref[gi - 1, jj] = tg[int(_t5_bucket(np.int64((NK - jj) * dil)))]
            b23n_ref[gi - 1] = tg[0]

    outs, lses = [], []
    x1 = c1_ref[0, 0]
    xn1 = xn_ref[0, 0].reshape(N_NEW, TILE_ROWS, GW)
    o1, e1 = [], []
    for t in range(N_NEW):
        q = q_ref[0, 0, t * TILE_ROWS:(t + 1) * TILE_ROWS, :] * scale
        acc, m, l = _attend(x1, xn1, q, b1_ref[t], b1n_ref[t])
        o1.append(acc)
        e1.append((m, l))
    outs.append(jnp.concatenate(o1, axis=0))
    lses.append((jnp.concatenate([m[0] for m, _ in e1], axis=0), jnp.concatenate([l[0] for _, l in e1], axis=0)))
    for gi, c_ref in ((1, c2_ref), (2, c3_ref)):
        acc, m, l = _attend(c_ref[0, 0], xn_ref[gi, 0][None], q_ref[gi, 0] * scale, b23_ref[gi - 1],
                            b23n_ref[gi - 1][None])
        outs.append(acc)
        lses.append((m[0], l[0]))

    lse = [_key_to_value_rows((m + jnp.log(l))[None])[0] for m, l in lses]
    den = [_key_to_value_rows(l[None])[0] for _, l in lses]
    top = jnp.maximum(jnp.maximum(lse[0], lse[1]), lse[2])
    w = [jnp.exp(e - top) for e in lse]
    num = w[0] * outs[0] / den[0] + w[1] * outs[1] / den[1] + w[2] * outs[2] / den[2]
    o_ref[0] = num / (w[0] + w[1] + w[2])


def attention_sample(p, cache_b1, cache_b2, cache_b3, rel_bias, layer, row0, nseq):
    depth = cache_b1.shape[0]
    rows = N_NEW * TILE_ROWS
    n_groups = len(PATTERNS)
    qkv = p[row0:, COL_B:COL_B + n_groups * 3 * BR_W].reshape(nseq, SAMPLE_ROWS, n_groups, 3, N_HEADS, GW)
    qkv = qkv[:, :N_NEW].transpose(2, 0, 1, 3, 4, 5)
    q = qkv[:, :, :, 0:1]
    q_all = jnp.concatenate([q, jnp.zeros_like(q)], axis=3).reshape(n_groups, nseq, rows, GW)
    xn_all = qkv[:, :, :, 1:3].reshape(n_groups, nseq, rows, GW)
    c1 = cache_b1.reshape(depth, nseq, NK, TILE_ROWS, GW)
    c2 = cache_b2.reshape(depth, nseq, NK, 4 * TILE_ROWS, GW)
    c3 = cache_b3.reshape(depth, nseq, NK, 16 * TILE_ROWS, GW)
    cache_spec = lambda r: pl.BlockSpec((1, 1, NK, r, GW), lambda b: (layer, b, 0, 0, 0))
    tok_spec = pl.BlockSpec((len(PATTERNS), 1, rows, GW), lambda b: (0, b, 0, 0))
    y = pl.pallas_call(
        _attn_sample_kernel,
        grid=(nseq,),
        in_specs=[pl.BlockSpec(memory_space=pltpu.SMEM), tok_spec, tok_spec,
                  cache_spec(TILE_ROWS), cache_spec(rows), cache_spec(rows)],
        out_specs=pl.BlockSpec((1, rows, GW), lambda b: (b, 0, 0)),
        out_shape=jax.ShapeDtypeStruct((nseq, rows, GW), F32),
        scratch_shapes=[pltpu.VMEM((N_NEW, NK, TILE_ROWS, GW), F32), pltpu.VMEM((N_NEW, N_NEW, TILE_ROWS, GW), F32),
                        pltpu.VMEM((2, NK, rows, GW), F32), pltpu.VMEM((2, rows, GW), F32)],
        compiler_params=_params(1),
        name="attention_sample",
    )(rel_bias, q_all, xn_all, c1, c2, c3)
    y = y.reshape(nseq, N_NEW, 2, BR_W)[:, :, 1]
    y = jnp.pad(y, ((0, 0), (0, SAMPLE_ROWS - N_NEW), (0, 0)))
    return y.reshape(nseq * SAMPLE_ROWS, BR_W).astype(BF16)


def _pool_conv_kernel(pc_ref, pbg_ref, pcg_ref, phs_ref, hc_ref, hd_ref, pw_ref, pb_ref, ps_ref, cw_ref,
                      yc_ref, yd_ref, zt_ref, cbuf_ref, zbuf_ref, *, start):
    tr = pc_ref.shape[0]
    j = pl.program_id(1)

    @pl.when(j == 0)
    def _():
        cbuf_ref[0:POOL_HALO, :] = hc_ref[0]
        zbuf_ref[0:CONV_HALO, :] = hd_ref[0]

    _pool_tile(pc_ref[...], start + j * tr, cbuf_ref, pw_ref, pb_ref, ps_ref, yc_ref)
    _conv_tile(pbg_ref[...], pcg_ref[...], phs_ref[...], zbuf_ref, cw_ref, yd_ref, zt_ref)


POOL_HALO = 16
CONV_HALO = 8


def _pool_tile(x, pos0, cbuf_ref, pw_ref, pb_ref, ps_ref, yc_ref):
    tr = x.shape[0]
    hc = POOL_HALO
    cbuf_ref[hc:hc + tr, :] = x
    pos = pos0 + lax.broadcasted_iota(jnp.int32, (tr, 1), 0)
    for gi, win in enumerate(POOL_WINDOWS):
        cols = slice(gi * GW, (gi + 1) * GW)
        total = x[:, cols]
        for back in range(1, win):
            total = total + cbuf_ref[hc - back:hc - back + tr, cols]
        cnt = jnp.minimum(pos + 1, win).astype(F32)
        pooled = total / cnt - x[:, cols]
        y = jnp.dot(pooled.astype(BF16), pw_ref[0, gi].astype(BF16), preferred_element_type=F32)
        yc_ref[:, cols] = ((y + pb_ref[:, cols]) * ps_ref[:, cols]).astype(yc_ref.dtype)
    cbuf_ref[0:hc, :] = cbuf_ref[tr:tr + hc, :]


def _conv_tile(bg, cg, hs, zbuf_ref, cw_ref, yd_ref, zt_ref):
    tr = bg.shape[0]
    hz = CONV_HALO
    z = cg * hs
    zbuf_ref[hz:hz + tr, :] = z
    y = (zbuf_ref[hz - 2:hz - 2 + tr, :] * cw_ref[0:1, :] + zbuf_ref[hz - 1:hz - 1 + tr, :] * cw_ref[1:2, :]
         + z * cw_ref[2:3, :])
    yd_ref[...] = (bg * y).astype(yd_ref.dtype)
    zt_ref[0] = z[tr - hz:, :]
    zbuf_ref[0:hz, :] = zbuf_ref[tr:tr + hz, :]


def pool_conv(p, hist_c, hist_d, pool_w, pool_b, pool_scale, conv_wt, layer, row0, nseq, t, tr, start):
    per = t // tr
    blk0 = row0 // tr
    col = lambda c: pl.BlockSpec((tr, BR_W), lambda b, j: (b * per + j, c))
    pcol = lambda c: pl.BlockSpec((tr, BR_W), lambda b, j: (blk0 + b * per + j, c))
    vec = pl.BlockSpec((1, BR_W), lambda b, j: (0, 0))
    return pl.pallas_call(
        functools.partial(_pool_conv_kernel, start=start),
        grid=(nseq, per),
        in_specs=[pcol(COL_C // BR_W), pcol(COL_D // BR_W), pcol(COL_D // BR_W + 1), pcol(COL_D // BR_W + 2),
                  pl.BlockSpec((1, 16, BR_W), lambda b, j: (b, 0, 0)),
                  pl.BlockSpec((1, 8, BR_W), lambda b, j: (b, 0, 0)),
                  pl.BlockSpec((1, 4, GW, GW), lambda b, j: (layer, 0, 0, 0)),
                  vec, vec,
                  pl.BlockSpec((CONV_W, BR_W), lambda b, j: (0, 0))],
        out_specs=[col(0), col(0), pl.BlockSpec((1, 8, BR_W), lambda b, j: (b, 0, 0))],
        out_shape=[jax.ShapeDtypeStruct((nseq * t, BR_W), BF16),
                   jax.ShapeDtypeStruct((nseq * t, BR_W), BF16),
                   jax.ShapeDtypeStruct((nseq, 8, BR_W), F32)],
        scratch_shapes=[pltpu.VMEM((16 + tr, BR_W), F32), pltpu.VMEM((8 + tr, BR_W), F32)],
        compiler_params=_params(2),
        name="pool_conv",
    )(p, p, p, p, hist_c, hist_d, pool_w, pool_b, pool_scale, conv_wt)


CACHE_ROWS = 1024


def _cache_writer_kernel(k1_ref, v1_ref, k2_ref, v2_ref, k3_ref, v3_ref, b1_any, b2_any, b3_any,
                         o1_ref, o2_ref, o3_ref):
    def scatter(o_ref, k_ref, v_ref, row0, n):
        for kv, src in enumerate((k_ref, v_ref)):
            for h in range(N_HEADS):
                o_ref[0, 0, pl.ds(kv * N_HEADS + h, n, stride=TILE_ROWS), :] = src[row0:row0 + n, h * GW:(h + 1) * GW]

    scatter(o3_ref, k3_ref, v3_ref, 0, CACHE_ROWS)

    @pl.when(pl.program_id(1) == pl.num_programs(1) - 1)
    def _():
        scatter(o2_ref, k2_ref, v2_ref, CACHE_ROWS - PATTERNS[1][0], PATTERNS[1][0])
        scatter(o1_ref, k1_ref, v1_ref, CACHE_ROWS - PATTERNS[0][0], PATTERNS[0][0])


def cache_writer(p, bufs, layer, nseq, t):
    b1, b2, b3 = bufs
    per = t // CACHE_ROWS
    steps = PATTERNS[2][0] // CACHE_ROWS

    def slab(gi, which, whole_window):
        col = (COL_B + gi * 3 * BR_W + (1 + which) * BR_W) // BR_W
        if whole_window:
            return pl.BlockSpec((CACHE_ROWS, BR_W), lambda b, j: ((b + 1) * per - steps + j, col))
        return pl.BlockSpec((CACHE_ROWS, BR_W), lambda b, j: ((b + 1) * per - 1, col))

    any_spec = pl.BlockSpec(memory_space=pl.ANY)
    tile = lambda rows: (1, 1, rows * TILE_ROWS, GW)
    return pl.pallas_call(
        _cache_writer_kernel,
        grid=(nseq, steps),
        in_specs=[slab(0, 0, False), slab(0, 1, False), slab(1, 0, False), slab(1, 1, False),
                  slab(2, 0, True), slab(2, 1, True),
                  any_spec, any_spec, any_spec],
        out_specs=[pl.BlockSpec(tile(PATTERNS[0][0]), lambda b, j: (layer, b, 0, 0)),
                   pl.BlockSpec(tile(PATTERNS[1][0]), lambda b, j: (layer, b, 0, 0)),
                   pl.BlockSpec(tile(CACHE_ROWS), lambda b, j: (layer, b, j, 0))],
        out_shape=[jax.ShapeDtypeStruct(x.shape, x.dtype) for x in bufs],
        input_output_aliases={6: 0, 7: 1, 8: 2},
        compiler_params=_params(2),
        name="cache_writer",
    )(p, p, p, p, p, p, b1, b2, b3)


TM_WIDE = 1024
TM_ROWS = 512


def kernel(x_prompt, x_sample, c_prompt, c_sample, cache_b1, cache_b2, cache_b3, cache_pool, cache_conv,
           ln1_g, ln2_g, w_ada, b_ada, w_in, w_gate, b_gate, a_ln_g, a_ln_b, a_ws, a_bs,
           rel_bias, pool_w, pool_b, pool_scale, conv_w, w_branch, w_out, w1, w3, w2, final_g):
    nb, seq, d = x_prompt.shape
    ns, dec = x_sample.shape[:2]
    depth = w_in.shape[0]
    mp = nb * seq
    ms = ns * SAMPLE_ROWS

    c_all = jnp.concatenate([c_prompt, c_sample], axis=0)
    c_rows = -(-c_all.shape[0] // 8) * 8
    c_all = jnp.pad(c_all, ((0, c_rows - c_all.shape[0]), (0, 0)))
    mod = ada_modulation(c_all, w_ada, b_ada).reshape(depth, c_rows, 6, d)
    mod_p = mod[:, :nb].transpose(0, 2, 1, 3).reshape(depth * 6 * nb, 1, d)
    mod_s = jnp.repeat(mod[:, nb:nb + ns], SAMPLE_ROWS, axis=1).transpose(0, 2, 1, 3).reshape(depth * 6, ms, d)
    mods = [[Mod(mod_p, mod_s, l * 6 + i, nb) for i in range(6)] for l in range(depth)]

    xp = x_prompt.reshape(mp, d)
    xs = jnp.pad(x_sample, ((0, 0), (0, SAMPLE_ROWS - dec), (0, 0))).reshape(ms, d)

    bias_mask = prompt_bias_mask(rel_bias)
    kv_bufs = tuple(jnp.zeros((depth, nb, win * TILE_ROWS, GW), F32) for win, _ in PATTERNS)
    pool_p, conv_p, proj_s, conv_s, chunk_s = [], [], [], [], []
    a_bst = a_bs.transpose(0, 2, 1)
    conv_wt = conv_w.transpose(0, 2, 1)
    hist_c_all = jnp.pad(cache_pool, ((0, 0), (0, 0), (1, 0), (0, 0)))
    hist_d_all = jnp.pad(cache_conv, ((0, 0), (0, 0), (8 - (CONV_W - 1), 0), (0, 0)))
    for l in range(depth):
        sh1, sc1, g1, sh2, sc2, g2 = mods[l]
        gm = (a_ln_g[l][None], a_ln_b[l][None], a_ws[l][None], a_bst[l][None])
        pc = (pool_w, pool_b[l][None], pool_scale[l][None], conv_wt[l], l)
        p, hn, ya_p, yc_p, yd_p, zt_p = in_proj_mixers(xp, xs, ln1_g[l][None], sc1, sh1, w_in, l, TM_ROWS, nb,
                                                       gm, pc)
        ya_s, va_s = gmlp(p, *gm, mp, ms, ms, SAMPLE_ROWS)
        yb_p = attention_prompt(p, bias_mask, nb, seq)
        yb_s = attention_sample(p, cache_b1, cache_b2, cache_b3, rel_bias, l, mp, ns)
        yc_s, yd_s, zt_s = pool_conv(p, hist_c_all[l], hist_d_all[l], *pc, mp, ns, SAMPLE_ROWS, SAMPLE_ROWS,
                                     PAST_LEN)

        merged = gate_merge(hn, mp + ms, (ya_p, yb_p, yc_p, yd_p), (ya_s, yb_s, yc_s, yd_s),
                            w_gate, b_gate, w_branch, l, TM_WIDE, 256)
        xp, xs, hn2 = out_proj_norm(merged, w_out, l, xp, xs, g1, ln2_g[l][None], sc2, sh2, TM_ROWS)
        hmid = swiglu_up(hn2, w1, w3, l, ms, TM_WIDE, 512)
        xp, xs = matmul_residual(hmid, w2, l, xp, xs, g2, TM_ROWS, 512)

        kv_bufs = cache_writer(p, kv_bufs, l, nb, seq)
        pool_p.append(jnp.stack([p[(b + 1) * seq - POOL_HIST:(b + 1) * seq, COL_C:COL_C + BR_W]
                                 for b in range(nb)]))
        conv_p.append(zt_p[:nb])
        proj_s.append(p[mp:])
        conv_s.append(zt_s)
        chunk_s.append(va_s)

    yp, ys = final_norm(xp, xs, final_g[None], TM_WIDE)
    stack = lambda parts: jnp.stack(parts, axis=0)
    kv_p = [buf.reshape(depth, nb, win, 2, N_HEADS, GW) for buf, (win, _) in zip(kv_bufs, PATTERNS)]
    ps = stack(proj_s).reshape(depth, ns, SAMPLE_ROWS, N_IN)[:, :, :dec]
    kv_s = [ps[..., COL_B + (3 * gi + 1) * BR_W:COL_B + (3 * gi + 3) * BR_W].reshape(depth, ns, dec, 2, N_HEADS, GW)
            for gi in range(len(PATTERNS))]
    return (yp.reshape(nb, seq, d), ys.reshape(ns, SAMPLE_ROWS, d)[:, :dec],
            kv_p[0], kv_p[1], kv_p[2], stack(pool_p), stack(conv_p)[:, :, 8 - (CONV_W - 1):],
            kv_s[0], kv_s[1], kv_s[2], ps[..., COL_C:COL_C + BR_W], stack(conv_s)[:, :, :dec],
            stack(chunk_s).reshape(depth, ns, SAMPLE_ROWS, BR_W)[:, :, :dec])
```

```python
import functools
import math
from typing import NamedTuple

import jax
import jax.numpy as jnp
import numpy as np
from jax import lax
from jax.experimental import pallas as pl
from jax.experimental.pallas import tpu as pltpu

F32 = jnp.float32
BF16 = jnp.bfloat16

D_MODEL = 2048
BR_W = 512
GW = 128
N_HEADS = 4
CHUNK = 128
PATTERNS = ((128, 1), (512, 4), (2048, 16))
NK = 128
POOL_WINDOWS = (2, 4, 8, 16)
POOL_HIST = 15
CONV_W = 3
N_BUCKETS = 32
MAX_DIST = 2048
D_FF = 5632
N_IN = 7680
COL_A = 0
COL_B = 2 * BR_W
COL_C = COL_B + 9 * BR_W
COL_D = COL_C + BR_W
EPS = 1e-6
NEG = -1e30
SAMPLE_ROWS = 8
PAST_LEN = 16384
ATTN_BLOCKS_PER_STEP = 4

VMEM_LIMIT = 56 * 1024 * 1024


def _t5_bucket(dist):
    max_exact = N_BUCKETS // 2
    n = np.maximum(dist, 0)
    nf = np.maximum(n, 1).astype(np.float32)
    large = max_exact + (np.log(nf / np.float32(max_exact)) / np.float32(math.log(MAX_DIST / max_exact))
                         * np.float32(N_BUCKETS - max_exact)).astype(np.int32)
    large = np.minimum(large, N_BUCKETS - 1)
    return np.where(n < max_exact, n, large)


def _params(n_axes):
    return pltpu.CompilerParams(dimension_semantics=("arbitrary",) * n_axes,
                                vmem_limit_bytes=VMEM_LIMIT)


def _ada_kernel(c_ref, w_ref, b_ref, o_ref):
    a = jax.nn.silu(c_ref[...]).astype(BF16)
    o_ref[0] = jnp.dot(a, w_ref[0].astype(BF16), preferred_element_type=F32) + b_ref[0]


def ada_modulation(c_all, w_ada, b_ada):
    depth, d, n = w_ada.shape
    r = c_all.shape[0]
    tn = 1024
    return pl.pallas_call(
        _ada_kernel,
        grid=(depth, n // tn),
        in_specs=[pl.BlockSpec((r, d), lambda l, j: (0, 0)),
                  pl.BlockSpec((1, d, tn), lambda l, j: (l, 0, j)),
                  pl.BlockSpec((1, 1, tn), lambda l, j: (l, 0, j))],
        out_specs=pl.BlockSpec((1, r, tn), lambda l, j: (l, 0, j)),
        out_shape=jax.ShapeDtypeStruct((depth, r, n), F32),
        compiler_params=_params(2),
        name="ada_modulation",
    )(c_all, w_ada, b_ada.reshape(depth, 1, n))


def _split_tiles(axis, body):
    i = pl.program_id(axis)
    pl.when(i == 0)(functools.partial(body, True))
    pl.when(i > 0)(functools.partial(body, False))


def _joint_tile(i, nf):
    return (i + nf) % (nf + 1)


def _prompt_tile(i):
    return jnp.maximum(i - 1, 0)


def _norm_rows(x, g, sc, sh):
    y = x * lax.rsqrt(jnp.mean(x * x, axis=-1, keepdims=True) + EPS) * g
    return y * (1.0 + sc) + sh


class Mod(NamedTuple):
    prompt: jax.Array
    sample: jax.Array
    index: int
    nb: int


def _mod_specs(mod, width, per, grid_rank):
    ms = mod.sample.shape[1]
    base = mod.index * mod.nb
    if grid_rank == 1:
        return [pl.BlockSpec((1, 1, width), lambda i: (base + _prompt_tile(i) // per, 0, 0)),
                pl.BlockSpec((1, ms, width), lambda i: (mod.index, 0, 0))]
    return [pl.BlockSpec((1, 1, width), lambda j, i: (base + _prompt_tile(i) // per, 0, j)),
            pl.BlockSpec((1, ms, width), lambda j, i: (mod.index, 0, j))]


def _norm_in_kernel(xp_ref, xs_ref, g_ref, scp_ref, scs_ref, shp_ref, shs_ref, o_ref):
    ms = xs_ref.shape[0]

    def body(tail):
        if tail:
            o_ref[0:ms, :] = _norm_rows(xs_ref[...], g_ref[...], scs_ref[0], shs_ref[0]).astype(o_ref.dtype)
        else:
            o_ref[...] = _norm_rows(xp_ref[...], g_ref[...], scp_ref[0], shp_ref[0]).astype(o_ref.dtype)

    _split_tiles(0, body)


def norm_in(xp, xs, g, sc, sh, tm):
    mp, d = xp.shape
    ms = xs.shape[0]
    nf = mp // tm
    per = nf // sc.nb
    return pl.pallas_call(
        _norm_in_kernel,
        grid=(nf + 1,),
        in_specs=[pl.BlockSpec((tm, d), lambda i: (_prompt_tile(i), 0)),
                  pl.BlockSpec((ms, d), lambda i: (0, 0)),
                  pl.BlockSpec((1, d), lambda i: (0, 0)),
                  *_mod_specs(sc, d, per, 1), *_mod_specs(sh, d, per, 1)],
        out_specs=pl.BlockSpec((tm, d), lambda i: (_joint_tile(i, nf), 0)),
        out_shape=jax.ShapeDtypeStruct((mp + ms, d), BF16),
        compiler_params=_params(1),
        name="norm_in",
    )(xp, xs, g, sc.prompt, sc.sample, sh.prompt, sh.sample)


def _final_norm_kernel(xp_ref, xs_ref, g_ref, yp_ref, ys_ref):
    def body(tail):
        src, dst = (xs_ref, ys_ref) if tail else (xp_ref, yp_ref)
        x = src[...]
        dst[...] = x * lax.rsqrt(jnp.mean(x * x, axis=-1, keepdims=True) + EPS) * g_ref[...]

    _split_tiles(0, body)


def final_norm(xp, xs, g, tm):
    mp, d = xp.shape
    ms = xs.shape[0]
    nf = mp // tm
    p_spec = pl.BlockSpec((tm, d), lambda i: (_prompt_tile(i), 0))
    s_spec = pl.BlockSpec((ms, d), lambda i: (0, 0))
    return pl.pallas_call(
        _final_norm_kernel,
        grid=(nf + 1,),
        in_specs=[p_spec, s_spec, pl.BlockSpec((1, d), lambda i: (0, 0))],
        out_specs=[p_spec, s_spec],
        out_shape=[jax.ShapeDtypeStruct((mp, d), F32), jax.ShapeDtypeStruct((ms, d), F32)],
        compiler_params=_params(1),
        name="final_norm",
    )(xp, xs, g)


IN_TN = 1536


def _in_proj_kernel(a_ref, w_ref, lg_ref, lb_ref, ws_ref, bst_ref, pw_ref, pb_ref, ps_ref, cw_ref,
                    o_ref, ya_ref, yc_ref, yd_ref, zt_ref, wbf_ref, cbuf_ref, zbuf_ref, *, ms, per_seq):
    j = pl.program_id(0)
    i = pl.program_id(1)
    tm = a_ref.shape[0]
    tile = i - 1
    first = (tile % per_seq) == 0

    def project():
        acc = jnp.dot(a_ref[...], wbf_ref[...], preferred_element_type=F32)
        o_ref[...] = acc
        return acc

    @pl.when(i == 0)
    def _():
        wbf_ref[...] = w_ref[0].astype(BF16)
        o_ref[0:ms, :] = jnp.dot(a_ref[0:ms, :], wbf_ref[...], preferred_element_type=F32)
        for ref in (ya_ref, yc_ref, yd_ref, zt_ref):
            ref[...] = jnp.zeros(ref.shape, ref.dtype)

    @pl.when((i > 0) & first & (j == COL_C // IN_TN))
    def _():
        cbuf_ref[0:POOL_HALO, :] = jnp.zeros((POOL_HALO, BR_W), F32)

    @pl.when((i > 0) & first & (j == COL_D // IN_TN))
    def _():
        zbuf_ref[0:CONV_HALO, :] = jnp.zeros((CONV_HALO, BR_W), F32)

    @pl.when((i > 0) & (j == COL_A // IN_TN))
    def _():
        acc = project()
        _gmlp_tile(acc[:, 0:BR_W], acc[:, BR_W:2 * BR_W], lg_ref, lb_ref, ws_ref, bst_ref, ya_ref, CHUNK)

    @pl.when((i > 0) & (j == COL_C // IN_TN))
    def _():
        acc = project()
        c0 = COL_C % IN_TN
        _pool_tile(acc[:, c0:c0 + BR_W], (tile % per_seq) * tm, cbuf_ref, pw_ref, pb_ref, ps_ref, yc_ref)

    @pl.when((i > 0) & (j == COL_D // IN_TN))
    def _():
        acc = project()
        _conv_tile(acc[:, 0:BR_W], acc[:, BR_W:2 * BR_W], acc[:, 2 * BR_W:3 * BR_W], zbuf_ref, cw_ref, yd_ref, zt_ref)

    @pl.when((i > 0) & (j != COL_A // IN_TN) & (j != COL_C // IN_TN) & (j != COL_D // IN_TN))
    def _():
        project()


def in_proj_mixers(a, w, layer, ms, tm, nseq, gm, pc):
    m, k = a.shape
    n = w.shape[2]
    nf = (m - ms) // tm
    per_seq = nf // nseq
    ln_g, ln_b, ws, bst = gm
    pool_w, pool_b, pool_scale, conv_wt, _ = pc
    ga, gc, gd = COL_A // IN_TN, COL_C // IN_TN, COL_D // IN_TN

    def parked(j, i, tile_j, live, n_live):
        before = (j < tile_j) | ((j == tile_j) & (i == 0))
        return jnp.where((j == tile_j) & (i > 0), live, jnp.where(before, n_live, n_live + 1))

    def n_parked(tile_j):
        return 1 if tile_j == n // IN_TN - 1 else 2

    def mixer_spec(tile_j):
        return pl.BlockSpec((tm, BR_W), lambda j, i: (parked(j, i, tile_j, i - 1, nf), 0))

    vec = pl.BlockSpec((1, BR_W), lambda j, i: (0, 0))
    return pl.pallas_call(
        functools.partial(_in_proj_kernel, ms=ms, per_seq=per_seq),
        grid=(n // IN_TN, nf + 1),
        in_specs=[pl.BlockSpec((tm, k), lambda j, i: (_joint_tile(i, nf), 0)),
                  pl.BlockSpec((1, k, IN_TN), lambda j, i: (layer, 0, j)),
                  vec, vec,
                  pl.BlockSpec((1, 4, CHUNK, CHUNK), lambda j, i: (0, 0, 0, 0)),
                  pl.BlockSpec((1, CHUNK, 4), lambda j, i: (0, 0, 0)),
                  pl.BlockSpec((1, 4, GW, GW), lambda j, i: (layer, 0, 0, 0)),
                  vec, vec,
                  pl.BlockSpec((CONV_W, BR_W), lambda j, i: (0, 0))],
        out_specs=[pl.BlockSpec((tm, IN_TN), lambda j, i: (_joint_tile(i, nf), j)),
                   mixer_spec(ga), mixer_spec(gc), mixer_spec(gd),
                   pl.BlockSpec((1, CONV_HALO, BR_W),
                                lambda j, i: (parked(j, i, gd, (i - 1) // per_seq, nseq), 0, 0))],
        out_shape=[jax.ShapeDtypeStruct((m, n), F32)]
        + [jax.ShapeDtypeStruct(((nf + n_parked(t)) * tm, BR_W), BF16) for t in (ga, gc, gd)]
        + [jax.ShapeDtypeStruct((nseq + n_parked(gd), CONV_HALO, BR_W), F32)],
        scratch_shapes=[pltpu.VMEM((k, IN_TN), BF16),
                        pltpu.VMEM((POOL_HALO + tm, BR_W), F32), pltpu.VMEM((CONV_HALO + tm, BR_W), F32)],
        compiler_params=_params(2),
        name="in_proj_mixers",
    )(a, w, ln_g, ln_b, ws, bst, pool_w, pool_b, pool_scale, conv_wt)


def _mm_residual_kernel(a_ref, w_ref, xp_ref, xs_ref, gp_ref, gs_ref, op_ref, os_ref, wbf_ref):
    ms = xs_ref.shape[0]

    @pl.when(pl.program_id(1) == 0)
    def _():
        wbf_ref[...] = w_ref[0].astype(BF16)

    def body(tail):
        if tail:
            y = jnp.dot(a_ref[0:ms, :], wbf_ref[...], preferred_element_type=F32)
            os_ref[...] = xs_ref[...] + gs_ref[0] * y
        else:
            y = jnp.dot(a_ref[...], wbf_ref[...], preferred_element_type=F32)
            op_ref[...] = xp_ref[...] + gp_ref[0] * y

    _split_tiles(1, body)


def matmul_residual(a, w, layer, xp, xs, gate, tm, tn):
    m, k = a.shape
    n = w.shape[2]
    mp, ms = xp.shape[0], xs.shape[0]
    nf = mp // tm
    per = nf // gate.nb
    p_spec = pl.BlockSpec((tm, tn), lambda j, i: (_prompt_tile(i), j))
    s_spec = pl.BlockSpec((ms, tn), lambda j, i: (0, j))
    return pl.pallas_call(
        _mm_residual_kernel,
        grid=(n // tn, nf + 1),
        in_specs=[pl.BlockSpec((tm, k), lambda j, i: (_joint_tile(i, nf), 0)),
                  pl.BlockSpec((1, k, tn), lambda j, i: (layer, 0, j)),
                  p_spec, s_spec, *_mod_specs(gate, tn, per, 2)],
        out_specs=[p_spec, s_spec],
        out_shape=[jax.ShapeDtypeStruct((mp, n), F32), jax.ShapeDtypeStruct((ms, n), F32)],
        scratch_shapes=[pltpu.VMEM((k, tn), BF16)],
        compiler_params=_params(2),
        name="matmul_residual",
    )(a, w, xp, xs, gate.prompt, gate.sample)


def _out_proj_norm_kernel(a_ref, w_ref, xp_ref, xs_ref, gp_ref, gs_ref, ln_ref, scp_ref, scs_ref,
                          shp_ref, shs_ref, op_ref, os_ref, hn_ref, wbf_ref):
    ms = xs_ref.shape[0]

    @pl.when(pl.program_id(0) == 0)
    def _():
        wbf_ref[...] = w_ref[0].astype(BF16)

    def body(tail):
        if tail:
            x = xs_ref[...] + gs_ref[0] * jnp.dot(a_ref[0:ms, :], wbf_ref[...], preferred_element_type=F32)
            os_ref[...] = x
            hn_ref[0:ms, :] = _norm_rows(x, ln_ref[...], scs_ref[0], shs_ref[0]).astype(hn_ref.dtype)
        else:
            x = xp_ref[...] + gp_ref[0] * jnp.dot(a_ref[...], wbf_ref[...], preferred_element_type=F32)
            op_ref[...] = x
            hn_ref[...] = _norm_rows(x, ln_ref[...], scp_ref[0], shp_ref[0]).astype(hn_ref.dtype)

    _split_tiles(0, body)


def out_proj_norm(a, w, layer, xp, xs, gate, ln_g, sc, sh, tm):
    m, k = a.shape
    n = w.shape[2]
    mp, ms = xp.shape[0], xs.shape[0]
    nf = mp // tm
    per = nf // gate.nb
    p_spec = pl.BlockSpec((tm, n), lambda i: (_prompt_tile(i), 0))
    s_spec = pl.BlockSpec((ms, n), lambda i: (0, 0))
    return pl.pallas_call(
        _out_proj_norm_kernel,
        grid=(nf + 1,),
        in_specs=[pl.BlockSpec((tm, k), lambda i: (_joint_tile(i, nf), 0)),
                  pl.BlockSpec((1, k, n), lambda i: (layer, 0, 0), pipeline_mode=pl.Buffered(1)),
                  p_spec, s_spec, *_mod_specs(gate, n, per, 1),
                  pl.BlockSpec((1, n), lambda i: (0, 0)),
                  *_mod_specs(sc, n, per, 1), *_mod_specs(sh, n, per, 1)],
        out_specs=[p_spec, s_spec, pl.BlockSpec((tm, n), lambda i: (_joint_tile(i, nf), 0))],
        out_shape=[jax.ShapeDtypeStruct((mp, n), F32), jax.ShapeDtypeStruct((ms, n), F32),
                   jax.ShapeDtypeStruct((m, n), BF16)],
        scratch_shapes=[pltpu.VMEM((k, n), BF16)],
        compiler_params=_params(1),
        name="out_proj_norm",
    )(a, w, xp, xs, gate.prompt, gate.sample, ln_g, sc.prompt, sc.sample, sh.prompt, sh.sample)


def _swiglu_kernel(a_ref, w1_ref, w3_ref, o_ref, wbf_ref, *, ms):
    tn = o_ref.shape[1]

    @pl.when(pl.program_id(1) == 0)
    def _():
        wbf_ref[:, 0:tn] = w1_ref[0].astype(BF16)
        wbf_ref[:, tn:2 * tn] = w3_ref[0].astype(BF16)

    def body(tail):
        rows = slice(0, ms) if tail else slice(None)
        h = jnp.dot(a_ref[rows, :], wbf_ref[...], preferred_element_type=F32)
        o_ref[rows, :] = (jax.nn.silu(h[:, 0:tn]) * h[:, tn:2 * tn]).astype(o_ref.dtype)

    _split_tiles(1, body)


def swiglu_up(a, w1, w3, layer, ms, tm, tn):
    m, k = a.shape
    n = w1.shape[2]
    nf = (m - ms) // tm
    w_spec = pl.BlockSpec((1, k, tn), lambda j, i: (layer, 0, j))
    return pl.pallas_call(
        functools.partial(_swiglu_kernel, ms=ms),
        grid=(n // tn, nf + 1),
        in_specs=[pl.BlockSpec((tm, k), lambda j, i: (_joint_tile(i, nf), 0)), w_spec, w_spec],
        out_specs=pl.BlockSpec((tm, tn), lambda j, i: (_joint_tile(i, nf), j)),
        out_shape=jax.ShapeDtypeStruct((m, n), BF16),
        scratch_shapes=[pltpu.VMEM((k, 2 * tn), BF16)],
        compiler_params=_params(2),
        name="swiglu_up",
    )(a, w1, w3)


def _gate_merge_kernel(hn_ref, yap_ref, ybp_ref, ycp_ref, ydp_ref, yas_ref, ybs_ref, ycs_ref, yds_ref,
                       wg0_ref, wg1_ref, wg2_ref, wg3_ref, wb_ref, bg_ref, o_ref, wgbf_ref, wbbf_ref):
    ms = yas_ref.shape[0]

    @pl.when(pl.program_id(1) == 0)
    def _():
        for g, wg_ref in enumerate((wg0_ref, wg1_ref, wg2_ref, wg3_ref)):
            wgbf_ref[g] = wg_ref[0].astype(BF16)
            wbbf_ref[g] = wb_ref[0, g].astype(BF16)

    def body(tail):
        rows = slice(0, ms) if tail else slice(None)
        y_refs = (yas_ref, ybs_ref, ycs_ref, yds_ref) if tail else (yap_ref, ybp_ref, ycp_ref, ydp_ref)
        hn = hn_ref[rows, :]
        acc = None
        for g, y_ref in enumerate(y_refs):
            gate = jax.nn.sigmoid(jnp.dot(hn, wgbf_ref[g], preferred_element_type=F32) + bg_ref[0, g:g + 1, :])
            br = jnp.dot(y_ref[...], wbbf_ref[g], preferred_element_type=F32)
            acc = gate * br if acc is None else acc + gate * br
        o_ref[rows, :] = acc.astype(o_ref.dtype)

    _split_tiles(1, body)


def gate_merge(hn, ys_p, ys_s, w_gate, b_gate, w_branch, layer, tm, tn):
    m, k = hn.shape
    ms = ys_s[0].shape[0]
    nf = (m - ms) // tm
    depth = w_gate.shape[0]
    d = w_branch.shape[3]
    nj = d // tn
    wg_specs = [pl.BlockSpec((1, k, tn), functools.partial(lambda j, i, g: (layer, 0, g * nj + j), g=g))
                for g in range(4)]
    yp_spec = pl.BlockSpec((tm, BR_W), lambda j, i: (_prompt_tile(i), 0))
    ys_spec = pl.BlockSpec((ms, BR_W), lambda j, i: (0, 0))
    return pl.pallas_call(
        _gate_merge_kernel,
        grid=(nj, nf + 1),
        in_specs=[pl.BlockSpec((tm, k), lambda j, i: (_joint_tile(i, nf), 0)), *[yp_spec] * 4, *[ys_spec] * 4,
                  *wg_specs,
                  pl.BlockSpec((1, 4, BR_W, tn), lambda j, i: (layer, 0, 0, j)),
                  pl.BlockSpec((1, 4, tn), lambda j, i: (layer, 0, j))],
        out_specs=pl.BlockSpec((tm, tn), lambda j, i: (_joint_tile(i, nf), j)),
        out_shape=jax.ShapeDtypeStruct((m, d), BF16),
        scratch_shapes=[pltpu.VMEM((4, k, tn), BF16), pltpu.VMEM((4, BR_W, tn), BF16)],
        compiler_params=_params(2),
        name="gate_merge",
    )(hn, *ys_p, *ys_s, w_gate, w_gate, w_gate, w_gate, w_branch, b_gate.reshape(depth, 4, d))


def _gmlp_kernel(pu_ref, pv_ref, lg_ref, lb_ref, ws_ref, bst_ref, ya_ref, va_ref, *, chunk):
    va_ref[...] = _gmlp_tile(pu_ref[...], pv_ref[...], lg_ref, lb_ref, ws_ref, bst_ref, ya_ref, chunk)


def _gmlp_tile(pu, pv, lg_ref, lb_ref, ws_ref, bst_ref, ya_ref, chunk):
    rows = pu.shape[0]
    u = jax.nn.gelu(pu)
    v = jax.nn.gelu(pv)
    mu = jnp.mean(v, axis=-1, keepdims=True)
    var = jnp.mean(jnp.square(v - mu), axis=-1, keepdims=True)
    vn = (v - mu) * lax.rsqrt(var + 1e-5) * lg_ref[...] + lb_ref[...]
    causal = (lax.broadcasted_iota(jnp.int32, (CHUNK, CHUNK), 0)
              >= lax.broadcasted_iota(jnp.int32, (CHUNK, CHUNK), 1))
    for g in range(4):
        wm = jnp.where(causal, ws_ref[0, g], 0.0).astype(BF16)
        bias = bst_ref[0, :, g:g + 1]
        for c in range(rows // chunk):
            r0 = c * chunk
            vc = vn[r0:r0 + chunk, g * GW:(g + 1) * GW]
            if chunk < CHUNK:
                vc = jnp.concatenate([vc, jnp.zeros((CHUNK - chunk, GW), F32)], axis=0)
            mix = (jnp.dot(wm, vc.astype(BF16), preferred_element_type=F32) + bias)[:chunk]
            ya_ref[r0:r0 + chunk, g * GW:(g + 1) * GW] = (
                u[r0:r0 + chunk, g * GW:(g + 1) * GW] * mix).astype(ya_ref.dtype)
    return vn


def gmlp(p, ln_g, ln_b, ws, bst, row0, m, tr, chunk):
    blk0 = row0 // tr
    in_spec = lambda col: pl.BlockSpec((tr, BR_W), lambda i: (blk0 + i, col))
    out_spec = pl.BlockSpec((tr, BR_W), lambda i: (i, 0))
    return pl.pallas_call(
        functools.partial(_gmlp_kernel, chunk=chunk),
        grid=(m // tr,),
        in_specs=[in_spec(0), in_spec(1),
                  pl.BlockSpec((1, BR_W), lambda i: (0, 0)),
                  pl.BlockSpec((1, BR_W), lambda i: (0, 0)),
                  pl.BlockSpec((1, 4, CHUNK, CHUNK), lambda i: (0, 0, 0, 0)),
                  pl.BlockSpec((1, CHUNK, 4), lambda i: (0, 0, 0))],
        out_specs=[out_spec, out_spec],
        out_shape=[jax.ShapeDtypeStruct((m, BR_W), BF16), jax.ShapeDtypeStruct((m, BR_W), F32)],
        compiler_params=_params(1),
        name="gmlp",
    )(p, p, ln_g, ln_b, ws, bst)


def _bias_from_buckets(idx, rel_ref, col, shape):
    bias = jnp.full(shape, NEG, F32)
    for b in range(N_BUCKETS):
        bias = jnp.where(idx == b, rel_ref[b, col], bias)
    return bias


def _bias_mask_kernel(rel_ref, idx_ref, o_ref):
    o_ref[0, 0] = _bias_from_buckets(idx_ref[0], rel_ref, pl.program_id(0) * N_HEADS + pl.program_id(1),
                                     (NK, 2 * NK))


def prompt_bias_mask(rel_bias):
    return pl.pallas_call(
        _bias_mask_kernel,
        grid=(len(PATTERNS), N_HEADS),
        in_specs=[pl.BlockSpec(memory_space=pltpu.SMEM),
                  pl.BlockSpec((1, NK, 2 * NK), lambda g, h: (g, 0, 0))],
        out_specs=pl.BlockSpec((1, 1, NK, 2 * NK), lambda g, h: (g, h, 0, 0)),
        out_shape=jax.ShapeDtypeStruct((len(PATTERNS), N_HEADS, NK, 2 * NK), F32),
        compiler_params=_params(2),
        name="prompt_bias_mask",
    )(rel_bias, jnp.asarray(_prompt_buckets()))


def _attn_prompt_kernel(q_ref, k_ref, v_ref, bm_ref, o_ref,
                        o0_ref, o1_ref, o2_ref, e0_ref, e1_ref, e2_ref):
    t = q_ref.shape[0]
    group = pl.program_id(2)
    scale = GW ** -0.5
    outs = (o0_ref, o1_ref, o2_ref)
    lses = (e0_ref, e1_ref, e2_ref)

    def rows(start, n, dil):
        return pl.ds(start, n) if dil == 1 else pl.ds(start, n, stride=dil)

    def blocks(gi, specs):
        dil = PATTERNS[gi][1]
        staged = []
        for first_block, q0, k0 in specs:
            nkeys = NK if first_block else 2 * NK
            qi = rows(q0, NK, dil)
            ki = rows(k0, nkeys, dil)
            q = q_ref[qi, :].astype(BF16)
            kk = k_ref[ki, :].astype(BF16)
            bm = bm_ref[0, 0, :, NK:] if first_block else bm_ref[0, 0]
            s = lax.dot_general(q, kk, (((1,), (1,)), ((), ())), preferred_element_type=F32) * scale + bm
            staged.append((qi, ki, s))
        probs = []
        for qi, ki, s in staged:
            m = jnp.max(s, axis=-1, keepdims=True)
            p = jnp.exp(s - m)
            l = jnp.sum(p, axis=-1, keepdims=True)
            probs.append((qi, ki, p.astype(BF16), m, l))
        for qi, ki, p, m, l in probs:
            acc = jnp.dot(p, v_ref[ki, :].astype(BF16), preferred_element_type=F32)
            outs[gi][qi, :] = acc * (1.0 / l)
            lses[gi][qi, :] = jnp.broadcast_to(m + jnp.log(l), (NK, GW))

    def run_group(gi):
        dil = PATTERNS[gi][1]
        span = NK * dil
        nb = t // span
        u = ATTN_BLOCKS_PER_STEP
        if dil == 1:
            blocks(gi, [(True, 0, 0)] + [(False, n * span, (n - 1) * span) for n in range(1, u)])

            def step(i, c):
                q0 = pl.multiple_of(u * i * span, span)
                blocks(gi, [(False, q0 + n * span, q0 + (n - 1) * span) for n in range(u)])
                return c

            lax.fori_loop(1, nb // u, step, 0)
        else:
            def bunch(i, c):
                res = [u * i + r for r in range(u)]
                blocks(gi, [(True, r, r) for r in res])

                def later(n, cc):
                    blocks(gi, [(False, n * span + r, (n - 1) * span + r) for r in res])
                    return cc

                return lax.fori_loop(1, nb, later, c)

            if dil == u:
                bunch(0, 0)
            else:
                lax.fori_loop(0, dil // u, bunch, 0)

    for gi in range(len(PATTERNS)):
        pl.when(group == gi)(functools.partial(run_group, gi))

    @pl.when(group == len(PATTERNS) - 1)
    def _():
        step = 256

        def merge(c, carry):
            sl = pl.ds(pl.multiple_of(c * step, step), step)
            e = [r[sl, :] for r in lses]
            top = jnp.maximum(jnp.maximum(e[0], e[1]), e[2])
            w = [jnp.exp(x - top) for x in e]
            num = w[0] * o0_ref[sl, :] + w[1] * o1_ref[sl, :] + w[2] * o2_ref[sl, :]
            o_ref[sl, :] = (num / (w[0] + w[1] + w[2])).astype(o_ref.dtype)
            return carry

        lax.fori_loop(0, t // step, merge, 0)


def _prompt_buckets():
    i = np.arange(NK)[:, None]
    j = np.arange(2 * NK)[None, :]
    diff = NK + i - j
    valid = (diff >= 0) & (diff <= NK)
    return np.stack([np.where(valid, _t5_bucket(diff * dil), -1) for _, dil in PATTERNS]).astype(np.int32)


def attention_prompt(p, bias_mask, nseq, t):
    def qkv_spec(which):
        return pl.BlockSpec((t, GW), lambda b, h, g: (b, COL_B // GW + g * 12 + which * 4 + h))

    return pl.pallas_call(
        _attn_prompt_kernel,
        grid=(nseq, N_HEADS, len(PATTERNS)),
        in_specs=[qkv_spec(0), qkv_spec(1), qkv_spec(2),
                  pl.BlockSpec((1, 1, NK, 2 * NK), lambda b, h, g: (g, h, 0, 0))],
        out_specs=pl.BlockSpec((t, GW), lambda b, h, g: (b, h)),
        out_shape=jax.ShapeDtypeStruct((nseq * t, BR_W), BF16),
        scratch_shapes=[pltpu.VMEM((t, GW), F32)] * 6,
        compiler_params=_params(3),
        name="attention_prompt",
    )(p, p, p, bias_mask)


N_NEW = 4
TILE_ROWS = 2 * N_HEADS


def _key_to_value_rows(x):
    n, r, w = x.shape
    return pltpu.roll(x.reshape(n * r, w), N_HEADS, axis=0).reshape(n, r, w)


def _attend(x, xn, q, bias, bias_n):
    s = jnp.sum(x * q[None], axis=-1, keepdims=True) + bias
    sn = jnp.sum(xn * q[None], axis=-1, keepdims=True) + bias_n
    m = jnp.maximum(jnp.max(s, axis=0, keepdims=True), jnp.max(sn, axis=0, keepdims=True))
    p = jnp.exp(s - m)
    pn = jnp.exp(sn - m)
    l = jnp.sum(p, axis=0, keepdims=True) + jnp.sum(pn, axis=0, keepdims=True)
    acc = jnp.sum(_key_to_value_rows(p) * x, axis=0) + jnp.sum(_key_to_value_rows(pn) * xn, axis=0)
    return acc, m, l


def _attn_sample_kernel(rel_ref, q_ref, xn_ref, c1_ref, c2_ref, c3_ref, o_ref,
                        b1_ref, b1n_ref, b23_ref, b23n_ref):
    scale = GW ** -0.5
    rows = N_NEW * TILE_ROWS

    @pl.when(pl.program_id(0) == 0)
    def _():
        head = lax.broadcasted_iota(jnp.int32, (rows, GW), 0) % N_HEADS
        neg = jnp.full((rows, GW), NEG, F32)

        def tiles_for(gi):
            out = []
            for b in range(N_BUCKETS):
                v = [rel_ref[b, gi * N_HEADS + h] for h in range(N_HEADS)]
                out.append(jnp.where(head == 0, v[0], jnp.where(head == 1, v[1], jnp.where(head == 2, v[2], v[3]))))
            return out

        t1 = tiles_for(0)
        for t in range(N_NEW):
            for pos in range(NK):
                step = NK + t - pos
                tile = t1[int(_t5_bucket(np.int64(step)))] if step <= NK else neg
                b1_ref[t, pos] = tile[:TILE_ROWS]
            for j in range(N_NEW):
                tile = t1[int(_t5_bucket(np.int64(t - j)))] if j <= t else neg
                b1n_ref[t, j] = tile[:TILE_ROWS]
        for gi in (1, 2):
            dil = PATTERNS[gi][1]
            tg = tiles_for(gi)
            for jj in range(NK):
                b23_ref[gi - 1, jj] = tg[int(_t5_bucket(np.int64((NK - jj) * dil)))]
            b23n_ref[gi - 1] = tg[0]

    outs, lses = [], []
    x1 = c1_ref[0, 0]
    xn1 = xn_ref[0, 0].reshape(N_NEW, TILE_ROWS, GW)
    o1, e1 = [], []
    for t in range(N_NEW):
        q = q_ref[0, 0, t * TILE_ROWS:(t + 1) * TILE_ROWS, :] * scale
        acc, m, l = _attend(x1, xn1, q, b1_ref[t], b1n_ref[t])
        o1.append(acc)
        e1.append((m, l))
    outs.append(jnp.concatenate(o1, axis=0))
    lses.append((jnp.concatenate([m[0] for m, _ in e1], axis=0), jnp.concatenate([l[0] for _, l in e1], axis=0)))
    for gi, c_ref in ((1, c2_ref), (2, c3_ref)):
        acc, m, l = _attend(c_ref[0, 0], xn_ref[gi, 0][None], q_ref[gi, 0] * scale, b23_ref[gi - 1],
                            b23n_ref[gi - 1][None])
        outs.append(acc)
        lses.append((m[0], l[0]))

    lse = [_key_to_value_rows((m + jnp.log(l))[None])[0] for m, l in lses]
    den = [_key_to_value_rows(l[None])[0] for _, l in lses]
    top = jnp.maximum(jnp.maximum(lse[0], lse[1]), lse[2])
    w = [jnp.exp(e - top) for e in lse]
    num = w[0] * outs[0] / den[0] + w[1] * outs[1] / den[1] + w[2] * outs[2] / den[2]
    o_ref[0] = num / (w[0] + w[1] + w[2])


def attention_sample(p, cache_b1, cache_b2, cache_b3, rel_bias, layer, row0, nseq):
    depth = cache_b1.shape[0]
    rows = N_NEW * TILE_ROWS
    n_groups = len(PATTERNS)
    qkv = p[row0:, COL_B:COL_B + n_groups * 3 * BR_W].reshape(nseq, SAMPLE_ROWS, n_groups, 3, N_HEADS, GW)
    qkv = qkv[:, :N_NEW].transpose(2, 0, 1, 3, 4, 5)
    q = qkv[:, :, :, 0:1]
    q_all = jnp.concatenate([q, jnp.zeros_like(q)], axis=3).reshape(n_groups, nseq, rows, GW)
    xn_all = qkv[:, :, :, 1:3].reshape(n_groups, nseq, rows, GW)
    c1 = cache_b1.reshape(depth, nseq, NK, TILE_ROWS, GW)
    c2 = cache_b2.reshape(depth, nseq, NK, 4 * TILE_ROWS, GW)
    c3 = cache_b3.reshape(depth, nseq, NK, 16 * TILE_ROWS, GW)
    cache_spec = lambda r: pl.BlockSpec((1, 1, NK, r, GW), lambda b: (layer, b, 0, 0, 0))
    tok_spec = pl.BlockSpec((len(PATTERNS), 1, rows, GW), lambda b: (0, b, 0, 0))
    y = pl.pallas_call(
        _attn_sample_kernel,
        grid=(nseq,),
        in_specs=[pl.BlockSpec(memory_space=pltpu.SMEM), tok_spec, tok_spec,
                  cache_spec(TILE_ROWS), cache_spec(rows), cache_spec(rows)],
        out_specs=pl.BlockSpec((1, rows, GW), lambda b: (b, 0, 0)),
        out_shape=jax.ShapeDtypeStruct((nseq, rows, GW), F32),
        scratch_shapes=[pltpu.VMEM((N_NEW, NK, TILE_ROWS, GW), F32), pltpu.VMEM((N_NEW, N_NEW, TILE_ROWS, GW), F32),
                        pltpu.VMEM((2, NK, rows, GW), F32), pltpu.VMEM((2, rows, GW), F32)],
        compiler_params=_params(1),
        name="attention_sample",
    )(rel_bias, q_all, xn_all, c1, c2, c3)
    y = y.reshape(nseq, N_NEW, 2, BR_W)[:, :, 1]
    y = jnp.pad(y, ((0, 0), (0, SAMPLE_ROWS - N_NEW), (0, 0)))
    return y.reshape(nseq * SAMPLE_ROWS, BR_W).astype(BF16)


def _pool_conv_kernel(pc_ref, pbg_ref, pcg_ref, phs_ref, hc_ref, hd_ref, pw_ref, pb_ref, ps_ref, cw_ref,
                      yc_ref, yd_ref, zt_ref, cbuf_ref, zbuf_ref, *, start):
    tr = pc_ref.shape[0]
    j = pl.program_id(1)

    @pl.when(j == 0)
    def _():
        cbuf_ref[0:POOL_HALO, :] = hc_ref[0]
        zbuf_ref[0:CONV_HALO, :] = hd_ref[0]

    _pool_tile(pc_ref[...], start + j * tr, cbuf_ref, pw_ref, pb_ref, ps_ref, yc_ref)
    _conv_tile(pbg_ref[...], pcg_ref[...], phs_ref[...], zbuf_ref, cw_ref, yd_ref, zt_ref)


POOL_HALO = 16
CONV_HALO = 8


def _pool_tile(x, pos0, cbuf_ref, pw_ref, pb_ref, ps_ref, yc_ref):
    tr = x.shape[0]
    hc = POOL_HALO
    cbuf_ref[hc:hc + tr, :] = x
    pos = pos0 + lax.broadcasted_iota(jnp.int32, (tr, 1), 0)
    for gi, win in enumerate(POOL_WINDOWS):
        cols = slice(gi * GW, (gi + 1) * GW)
        total = x[:, cols]
        for back in range(1, win):
            total = total + cbuf_ref[hc - back:hc - back + tr, cols]
        cnt = jnp.minimum(pos + 1, win).astype(F32)
        pooled = total / cnt - x[:, cols]
        y = jnp.dot(pooled.astype(BF16), pw_ref[0, gi].astype(BF16), preferred_element_type=F32)
        yc_ref[:, cols] = ((y + pb_ref[:, cols]) * ps_ref[:, cols]).astype(yc_ref.dtype)
    cbuf_ref[0:hc, :] = cbuf_ref[tr:tr + hc, :]


def _conv_tile(bg, cg, hs, zbuf_ref, cw_ref, yd_ref, zt_ref):
    tr = bg.shape[0]
    hz = CONV_HALO
    z = cg * hs
    zbuf_ref[hz:hz + tr, :] = z
    y = (zbuf_ref[hz - 2:hz - 2 + tr, :] * cw_ref[0:1, :] + zbuf_ref[hz - 1:hz - 1 + tr, :] * cw_ref[1:2, :]
         + z * cw_ref[2:3, :])
    yd_ref[...] = (bg * y).astype(yd_ref.dtype)
    zt_ref[0] = z[tr - hz:, :]
    zbuf_ref[0:hz, :] = zbuf_ref[tr:tr + hz, :]


def pool_conv(p, hist_c, hist_d, pool_w, pool_b, pool_scale, conv_wt, layer, row0, nseq, t, tr, start):
    per = t // tr
    blk0 = row0 // tr
    col = lambda c: pl.BlockSpec((tr, BR_W), lambda b, j: (b * per + j, c))
    pcol = lambda c: pl.BlockSpec((tr, BR_W), lambda b, j: (blk0 + b * per + j, c))
    vec = pl.BlockSpec((1, BR_W), lambda b, j: (0, 0))
    return pl.pallas_call(
        functools.partial(_pool_conv_kernel, start=start),
        grid=(nseq, per),
        in_specs=[pcol(COL_C // BR_W), pcol(COL_D // BR_W), pcol(COL_D // BR_W + 1), pcol(COL_D // BR_W + 2),
                  pl.BlockSpec((1, 16, BR_W), lambda b, j: (b, 0, 0)),
                  pl.BlockSpec((1, 8, BR_W), lambda b, j: (b, 0, 0)),
                  pl.BlockSpec((1, 4, GW, GW), lambda b, j: (layer, 0, 0, 0)),
                  vec, vec,
                  pl.BlockSpec((CONV_W, BR_W), lambda b, j: (0, 0))],
        out_specs=[col(0), col(0), pl.BlockSpec((1, 8, BR_W), lambda b, j: (b, 0, 0))],
        out_shape=[jax.ShapeDtypeStruct((nseq * t, BR_W), BF16),
                   jax.ShapeDtypeStruct((nseq * t, BR_W), BF16),
                   jax.ShapeDtypeStruct((nseq, 8, BR_W), F32)],
        scratch_shapes=[pltpu.VMEM((16 + tr, BR_W), F32), pltpu.VMEM((8 + tr, BR_W), F32)],
        compiler_params=_params(2),
        name="pool_conv",
    )(p, p, p, p, hist_c, hist_d, pool_w, pool_b, pool_scale, conv_wt)


CACHE_ROWS = 1024


def _cache_writer_kernel(k1_ref, v1_ref, k2_ref, v2_ref, k3_ref, v3_ref, b1_any, b2_any, b3_any,
                         o1_ref, o2_ref, o3_ref):
    def scatter(o_ref, k_ref, v_ref, row0, n):
        for kv, src in enumerate((k_ref, v_ref)):
            for h in range(N_HEADS):
                o_ref[0, 0, pl.ds(kv * N_HEADS + h, n, stride=TILE_ROWS), :] = src[row0:row0 + n, h * GW:(h + 1) * GW]

    scatter(o3_ref, k3_ref, v3_ref, 0, CACHE_ROWS)

    @pl.when(pl.program_id(1) == pl.num_programs(1) - 1)
    def _():
        scatter(o2_ref, k2_ref, v2_ref, CACHE_ROWS - PATTERNS[1][0], PATTERNS[1][0])
        scatter(o1_ref, k1_ref, v1_ref, CACHE_ROWS - PATTERNS[0][0], PATTERNS[0][0])


def cache_writer(p, bufs, layer, nseq, t):
    b1, b2, b3 = bufs
    per = t // CACHE_ROWS
    steps = PATTERNS[2][0] // CACHE_ROWS

    def slab(gi, which, whole_window):
        col = (COL_B + gi * 3 * BR_W + (1 + which) * BR_W) // BR_W
        if whole_window:
            return pl.BlockSpec((CACHE_ROWS, BR_W), lambda b, j: ((b + 1) * per - steps + j, col))
        return pl.BlockSpec((CACHE_ROWS, BR_W), lambda b, j: ((b + 1) * per - 1, col))

    any_spec = pl.BlockSpec(memory_space=pl.ANY)
    tile = lambda rows: (1, 1, rows * TILE_ROWS, GW)
    return pl.pallas_call(
        _cache_writer_kernel,
        grid=(nseq, steps),
        in_specs=[slab(0, 0, False), slab(0, 1, False), slab(1, 0, False), slab(1, 1, False),
                  slab(2, 0, True), slab(2, 1, True),
                  any_spec, any_spec, any_spec],
        out_specs=[pl.BlockSpec(tile(PATTERNS[0][0]), lambda b, j: (layer, b, 0, 0)),
                   pl.BlockSpec(tile(PATTERNS[1][0]), lambda b, j: (layer, b, 0, 0)),
                   pl.BlockSpec(tile(CACHE_ROWS), lambda b, j: (layer, b, j, 0))],
        out_shape=[jax.ShapeDtypeStruct(x.shape, x.dtype) for x in bufs],
        input_output_aliases={6: 0, 7: 1, 8: 2},
        compiler_params=_params(2),
        name="cache_writer",
    )(p, p, p, p, p, p, b1, b2, b3)


TM_WIDE = 1024
TM_ROWS = 512


def kernel(x_prompt, x_sample, c_prompt, c_sample, cache_b1, cache_b2, cache_b3, cache_pool, cache_conv,
           ln1_g, ln2_g, w_ada, b_ada, w_in, w_gate, b_gate, a_ln_g, a_ln_b, a_ws, a_bs,
           rel_bias, pool_w, pool_b, pool_scale, conv_w, w_branch, w_out, w1, w3, w2, final_g):
    nb, seq, d = x_prompt.shape
    ns, dec = x_sample.shape[:2]
    depth = w_in.shape[0]
    mp = nb * seq
    ms = ns * SAMPLE_ROWS

    c_all = jnp.concatenate([c_prompt, c_sample], axis=0)
    c_rows = -(-c_all.shape[0] // 8) * 8
    c_all = jnp.pad(c_all, ((0, c_rows - c_all.shape[0]), (0, 0)))
    mod = ada_modulation(c_all, w_ada, b_ada).reshape(depth, c_rows, 6, d)
    mod_p = mod[:, :nb].transpose(0, 2, 1, 3).reshape(depth * 6 * nb, 1, d)
    mod_s = jnp.repeat(mod[:, nb:nb + ns], SAMPLE_ROWS, axis=1).transpose(0, 2, 1, 3).reshape(depth * 6, ms, d)
    mods = [[Mod(mod_p, mod_s, l * 6 + i, nb) for i in range(6)] for l in range(depth)]

    xp = x_prompt.reshape(mp, d)
    xs = jnp.pad(x_sample, ((0, 0), (0, SAMPLE_ROWS - dec), (0, 0))).reshape(ms, d)

    bias_mask = prompt_bias_mask(rel_bias)
    kv_bufs = tuple(jnp.zeros((depth, nb, win * TILE_ROWS, GW), F32) for win, _ in PATTERNS)
    pool_p, conv_p, proj_s, conv_s, chunk_s = [], [], [], [], []
    a_bst = a_bs.transpose(0, 2, 1)
    conv_wt = conv_w.transpose(0, 2, 1)
    hist_c_all = jnp.pad(cache_pool, ((0, 0), (0, 0), (1, 0), (0, 0)))
    hist_d_all = jnp.pad(cache_conv, ((0, 0), (0, 0), (8 - (CONV_W - 1), 0), (0, 0)))
    for l in range(depth):
        sh1, sc1, g1, sh2, sc2, g2 = mods[l]
        hn = norm_in(xp, xs, ln1_g[l][None], sc1, sh1, TM_WIDE)
        gm = (a_ln_g[l][None], a_ln_b[l][None], a_ws[l][None], a_bst[l][None])
        pc = (pool_w, pool_b[l][None], pool_scale[l][None], conv_wt[l], l)
        p, ya_p, yc_p, yd_p, zt_p = in_proj_mixers(hn, w_in, l, ms, TM_ROWS, nb, gm, pc)
        ya_s, va_s = gmlp(p, *gm, mp, ms, ms, SAMPLE_ROWS)
        yb_p = attention_prompt(p, bias_mask, nb, seq)
        yb_s = attention_sample(p, cache_b1, cache_b2, cache_b3, rel_bias, l, mp, ns)
        yc_s, yd_s, zt_s = pool_conv(p, hist_c_all[l], hist_d_all[l], *pc, mp, ns, SAMPLE_ROWS, SAMPLE_ROWS,
                                     PAST_LEN)

        merged = gate_merge(hn, (ya_p, yb_p, yc_p, yd_p), (ya_s, yb_s, yc_s, yd_s),
                            w_gate, b_gate, w_branch, l, TM_WIDE, 256)
        xp, xs, hn2 = out_proj_norm(merged, w_out, l, xp, xs, g1, ln2_g[l][None], sc2, sh2, TM_ROWS)
        hmid = swiglu_up(hn2, w1, w3, l, ms, TM_WIDE, 512)
        xp, xs = matmul_residual(hmid, w2, l, xp, xs, g2, TM_ROWS, 512)

        kv_bufs = cache_writer(p, kv_bufs, l, nb, seq)
        pool_p.append(jnp.stack([p[(b + 1) * seq - POOL_HIST:(b + 1) * seq, COL_C:COL_C + BR_W]
                                 for b in range(nb)]))
        conv_p.append(zt_p[:nb])
        proj_s.append(p[mp:])
        conv_s.append(zt_s)
        chunk_s.append(va_s)

    yp, ys = final_norm(xp, xs, final_g[None], TM_WIDE)
    stack = lambda parts: jnp.stack(parts, axis=0)
    kv_p = [buf.reshape(depth, nb, win, 2, N_HEADS, GW) for buf, (win, _) in zip(kv_bufs, PATTERNS)]
    ps = stack(proj_s).reshape(depth, ns, SAMPLE_ROWS, N_IN)[:, :, :dec]
    kv_s = [ps[..., COL_B + (3 * gi + 1) * BR_W:COL_B + (3 * gi + 3) * BR_W].reshape(depth, ns, dec, 2, N_HEADS, GW)
            for gi in range(len(PATTERNS))]
    return (yp.reshape(nb, seq, d), ys.reshape(ns, SAMPLE_ROWS, d)[:, :dec],
            kv_p[0], kv_p[1], kv_p[2], stack(pool_p), stack(conv_p)[:, :, 8 - (CONV_W - 1):],
            kv_s[0], kv_s[1], kv_s[2], ps[..., COL_C:COL_C + BR_W], stack(conv_s)[:, :, :dec],
            stack(chunk_s).reshape(depth, ns, SAMPLE_ROWS, BR_W)[:, :, :dec])
```
